```python
import jax, jax.numpy as jnp
from jax import lax
import numpy as np

D_MODEL = 1024
BATCH = 4
SEQ = 8192
DEPTH = 1
DEC_BATCH = 128
DEC_SEQ = 4
PAST_LEN = 8192
PAGE_SIZE = 128

HEAD_DIM = 64
HEADS_PER_GROUP = 4
ATTN_WINDOWS = (128, 512, 2048)
ATTN_DILATIONS = (1, 4, 16)
N_ATTN_GROUPS = len(ATTN_WINDOWS)
N_HEADS = N_ATTN_GROUPS * HEADS_PER_GROUP
ATTN_WIDTH = N_HEADS * HEAD_DIM
ATTN_OUT_WIDTH = HEADS_PER_GROUP * HEAD_DIM
ROPE_THETA = 10000.0
POOL_WINDOWS = (2, 4, 8, 16)
N_POOL_GROUPS = len(POOL_WINDOWS)
POOL_GROUP_WIDTH = D_MODEL // 8
POOL_WIDTH = N_POOL_GROUPS * POOL_GROUP_WIDTH
POOL_STATE_LEN = max(POOL_WINDOWS) - 1
MOE_GROUPS = 4
EXPERTS_PER_GROUP = 8
N_EXPERTS = MOE_GROUPS * EXPERTS_PER_GROUP
TOP_K_IN_GROUP = 2
D_EXPERT = D_MODEL // 4
MOE_TOKEN_BLOCK = 512
N_COND = 6
IN_WIDTH = 3 * ATTN_WIDTH + POOL_WIDTH + 2 * D_MODEL
IN_SPLITS = (ATTN_WIDTH, 2 * ATTN_WIDTH, 3 * ATTN_WIDTH, 3 * ATTN_WIDTH + POOL_WIDTH,
             3 * ATTN_WIDTH + POOL_WIDTH + D_MODEL)
EPS = 1e-6

kernel_name = "hybrid_dilated_attn_pool_hmoe_step"


def rms_norm(x, g):
    xf = x.astype(jnp.float32)
    y = xf * lax.rsqrt(jnp.mean(xf * xf, axis=-1, keepdims=True) + EPS)
    return (y * g.astype(jnp.float32)).astype(x.dtype)


def rope(x, pos):
    half = HEAD_DIM // 2
    inv = ROPE_THETA ** (-jnp.arange(half, dtype=jnp.float32) * 2.0 / HEAD_DIM)
    ang = pos.astype(jnp.float32)[:, None] * inv[None, :]
    cos = jnp.cos(ang)[:, None, :]
    sin = jnp.sin(ang)[:, None, :]
    xf = x.astype(jnp.float32)
    x1, x2 = xf[..., :half], xf[..., half:]
    return jnp.concatenate([x1 * cos - x2 * sin, x2 * cos + x1 * sin], axis=-1).astype(x.dtype)


def dilated_attn_prompt(q, k, v, dil, span):
    B, S, H, E = q.shape
    M = S // dil
    nb = -(-M // span)
    Mp = nb * span

    def to_sub(a):
        a = a.reshape(B, M, dil, H, E)
        return jnp.pad(a, ((0, 0), (0, Mp - M), (0, 0), (0, 0), (0, 0)))

    def key_blocks(a):
        a = jnp.pad(to_sub(a), ((0, 0), (span, 0), (0, 0), (0, 0), (0, 0)))
        prev = a[:, :Mp].reshape(B, nb, span, dil, H, E)
        cur = a[:, span:].reshape(B, nb, span, dil, H, E)
        return jnp.concatenate([prev, cur], axis=2)

    qs = to_sub(q).reshape(B, nb, span, dil, H, E)
    kb = key_blocks(k)
    vb = key_blocks(v)
    s = jnp.einsum('bnqrhe,bnkrhe->bnqrhk', qs, kb, preferred_element_type=jnp.float32) * (E ** -0.5)
    kk = jnp.arange(2 * span)[None, :]
    diff = jnp.arange(span)[:, None] + span - kk
    band = (diff >= 0) & (diff <= span)
    has_prev = (jnp.arange(nb)[:, None, None] > 0) | (kk[None] >= span)
    mask = (band[None] & has_prev)[None, :, :, None, None, :]
    s = jnp.where(mask, s, -jnp.inf)
    mx = jnp.max(s, axis=-1, keepdims=True)
    p = jnp.exp(s - mx)
    den = jnp.sum(p, axis=-1)
    o = jnp.einsum('bnqrhk,bnkrhe->bnqrhe', p, vb.astype(jnp.float32)) / den[..., None]
    lse = mx[..., 0] + jnp.log(den)
    o = o.reshape(B, Mp, dil, H, E)[:, :M].reshape(B, S, H, E)
    lse = lse.reshape(B, Mp, dil, H)[:, :M].reshape(B, S, H)
    return o, lse


def dilated_attn_sample(q, k_new, v_new, kv_buf, dil, span):
    T = q.shape[1]
    L = kv_buf.shape[1]
    kv_all = jnp.concatenate([kv_buf, jnp.stack([k_new, v_new], axis=2).astype(kv_buf.dtype)], axis=1)
    idx = L + jnp.arange(T)[:, None] - dil * jnp.arange(span + 1)[None, :]
    valid = idx >= 0
    g = kv_all[:, jnp.maximum(idx, 0)]
    s = jnp.einsum('nthe,ntjhe->nthj', q, g[:, :, :, 0], preferred_element_type=jnp.float32) * (q.shape[-1] ** -0.5)
    s = jnp.where(valid[None, :, None, :], s, -jnp.inf)
    mx = jnp.max(s, axis=-1, keepdims=True)
    p = jnp.exp(s - mx)
    den = jnp.sum(p, axis=-1)
    o = jnp.einsum('nthj,ntjhe->nthe', p, g[:, :, :, 1].astype(jnp.float32)) / den[..., None]
    lse = mx[..., 0] + jnp.log(den)
    return o, lse, kv_all[:, -L:]


def multiscale_pool(u_ext, pos_ext, w_pool, pool_scale):
    N, R, _ = u_ext.shape
    ug = u_ext.astype(jnp.float32).reshape(N, R, N_POOL_GROUPS, POOL_GROUP_WIDTH)
    cs = jnp.cumsum(ug, axis=1)
    means = []
    for gi, w in enumerate(POOL_WINDOWS):
        c = cs[:, :, gi]
        prev = jnp.pad(c, ((0, 0), (w, 0), (0, 0)))[:, :R]
        cnt = jnp.minimum(w, pos_ext + 1).astype(jnp.float32)
        means.append((c - prev) / cnt[None, :, None])
    mean = jnp.stack(means, axis=2)
    z = jnp.einsum('nrgc,gcd->nrgd', mean - ug, w_pool.astype(jnp.float32))
    return (z.reshape(N, R, POOL_WIDTH) * pool_scale).astype(u_ext.dtype)


def hier_moe(h, w_grp, b_grp, w_exp_router, b_exp_router, w_gate_up, w_down):
    N, T, D = h.shape
    hf = h.reshape(N * T, D)
    grp_logits = (jnp.dot(hf, w_grp) + b_grp).astype(jnp.float32)
    grp = jnp.argmax(grp_logits, axis=-1)
    grp_w = jnp.max(jax.nn.softmax(grp_logits, axis=-1), axis=-1)
    exp_logits = (jnp.dot(hf, w_exp_router) + b_exp_router).astype(jnp.float32)
    exp_logits = exp_logits.reshape(-1, MOE_GROUPS, EXPERTS_PER_GROUP)
    in_grp = jnp.take_along_axis(exp_logits, grp[:, None, None], axis=1)[:, 0]
    top_v, top_i = lax.top_k(in_grp, TOP_K_IN_GROUP)
    top_w = jax.nn.softmax(top_v, axis=-1) * grp_w[:, None]
    eid = grp[:, None] * EXPERTS_PER_GROUP + top_i
    combine = jnp.sum(jax.nn.one_hot(eid, N_EXPERTS, dtype=jnp.float32) * top_w[..., None], axis=1)
    M = hf.shape[0]
    nblk = -(-M // MOE_TOKEN_BLOCK)
    pad = nblk * MOE_TOKEN_BLOCK - M
    hb = jnp.pad(hf, ((0, pad), (0, 0))).reshape(nblk, MOE_TOKEN_BLOCK, D)
    cb = jnp.pad(combine, ((0, pad), (0, 0))).reshape(nblk, MOE_TOKEN_BLOCK, N_EXPERTS).astype(h.dtype)

    def expert_block(args):
        hx, cx = args
        gu = jnp.einsum('md,edf->mef', hx, w_gate_up)
        gate, up = jnp.split(gu, 2, axis=-1)
        act = jax.nn.silu(gate) * up * cx[..., None]
        return jnp.einsum('mef,efd->md', act, w_down)

    out = lax.map(expert_block, (hb, cb)).reshape(-1, D)[:M]
    return out.reshape(N, T, D)


def decoder_layer(x, c, pos, kv_bufs, pool_buf, lw):
    (norm1_g, w_ada, b_ada, w_in, w_attn_out, w_pool, pool_scale, w_pool_out, w_o,
     norm2_g, w_grp, b_grp, w_exp_router, b_exp_router, w_gate_up, w_down) = lw
    N, T, _ = x.shape
    mod = (jnp.dot(jax.nn.silu(c), w_ada) + b_ada)[:, None, :]
    sh1, sc1, g1, sh2, sc2, g2 = jnp.split(mod, N_COND, axis=-1)
    h = rms_norm(x, norm1_g) * (1 + sc1) + sh1
    q, k, v, u, gate_a, gate_b = jnp.split(jnp.dot(h, w_in), IN_SPLITS, axis=-1)
    q = rope(q.reshape(N, T, N_HEADS, HEAD_DIM), pos)
    k = rope(k.reshape(N, T, N_HEADS, HEAD_DIM), pos)
    v = v.reshape(N, T, N_HEADS, HEAD_DIM)
    outs, lses, kv_new = [], [], []
    for g, (win, dil) in enumerate(zip(ATTN_WINDOWS, ATTN_DILATIONS)):
        hs = slice(g * HEADS_PER_GROUP, (g + 1) * HEADS_PER_GROUP)
        span = win // dil
        qg, kg, vg = q[:, :, hs], k[:, :, hs], v[:, :, hs]
        if kv_bufs is None:
            o, lse = dilated_attn_prompt(qg, kg, vg, dil, span)
            kv_new.append(jnp.stack([kg, vg], axis=2)[:, -min(win, T):])
        else:
            o, lse, nbuf = dilated_attn_sample(qg, kg, vg, kv_bufs[g], dil, span)
            kv_new.append(nbuf)
        outs.append(o)
        lses.append(lse)
    alpha = jax.nn.softmax(jnp.stack(lses, axis=0), axis=0)
    attn = jnp.sum(alpha[..., None] * jnp.stack(outs, axis=0), axis=0)
    attn = attn.reshape(N, T, ATTN_OUT_WIDTH).astype(x.dtype)
    if pool_buf is None:
        u_ext, pos_ext = u, pos
    else:
        u_ext = jnp.concatenate([pool_buf, u.astype(pool_buf.dtype)], axis=1)
        pos_ext = pos[0] - pool_buf.shape[1] + jnp.arange(pool_buf.shape[1] + T, dtype=jnp.int32)
    pooled = multiscale_pool(u_ext, pos_ext, w_pool, pool_scale)[:, -T:]
    pool_new = u_ext[:, -POOL_STATE_LEN:]
    merged = (jax.nn.sigmoid(gate_a) * jnp.dot(attn, w_attn_out)
              + jax.nn.sigmoid(gate_b) * jnp.dot(pooled, w_pool_out))
    x = x + g1 * jnp.dot(merged, w_o)
    h2 = rms_norm(x, norm2_g) * (1 + sc2) + sh2
    x = x + g2 * hier_moe(h2, w_grp, b_grp, w_exp_router, b_exp_router, w_gate_up, w_down)
    return x, kv_new, pool_new


def setup_inputs(seed: int = 0) -> dict:
    key = jax.random.key(seed)
    ks = jax.random.split(key, 32)
    f32 = jnp.float32
    D = D_MODEL
    L = DEPTH

    def nrm(k, shape, scale):
        return jax.random.normal(k, shape, f32) * scale

    kv_shape = lambda w: (L, DEC_BATCH, min(w, PAST_LEN), 2, HEADS_PER_GROUP, HEAD_DIM)
    return {
        "x_prompt": nrm(ks[0], (BATCH, SEQ, D), 1.0),
        "x_sample": nrm(ks[1], (DEC_BATCH, DEC_SEQ, D), 1.0),
        "cache_kv_w128": nrm(ks[2], kv_shape(ATTN_WINDOWS[0]), 1.0),
        "cache_kv_w512": nrm(ks[3], kv_shape(ATTN_WINDOWS[1]), 1.0),
        "cache_kv_w2048": nrm(ks[4], kv_shape(ATTN_WINDOWS[2]), 1.0),
        "state_pool": nrm(ks[5], (L, DEC_BATCH, POOL_STATE_LEN, POOL_WIDTH), 1.0),
        "c_prompt": nrm(ks[6], (BATCH, D), 1.0),
        "c_sample": nrm(ks[7], (DEC_BATCH, D), 1.0),
        "norm1_g": 1.0 + nrm(ks[8], (L, D), 0.05),
        "w_ada": nrm(ks[9], (L, D, N_COND * D), 0.5 * D ** -0.5),
        "b_ada": nrm(ks[10], (L, N_COND * D), 0.01),
        "w_in": nrm(ks[11], (L, D, IN_WIDTH), D ** -0.5),
        "w_attn_out": nrm(ks[12], (L, ATTN_OUT_WIDTH, D), ATTN_OUT_WIDTH ** -0.5),
        "w_pool": nrm(ks[13], (L, N_POOL_GROUPS, POOL_GROUP_WIDTH, POOL_GROUP_WIDTH), POOL_GROUP_WIDTH ** -0.5),
        "pool_scale": 1.0 + nrm(ks[14], (L, POOL_WIDTH), 0.1),
        "w_pool_out": nrm(ks[15], (L, POOL_WIDTH, D), POOL_WIDTH ** -0.5),
        "w_o": nrm(ks[16], (L, D, D), D ** -0.5),
        "norm2_g": 1.0 + nrm(ks[17], (L, D), 0.05),
        "w_grp": nrm(ks[18], (L, D, MOE_GROUPS), D ** -0.5),
        "b_grp": nrm(ks[19], (L, MOE_GROUPS), 0.01),
        "w_exp_router": nrm(ks[20], (L, D, N_EXPERTS), D ** -0.5),
        "b_exp_router": nrm(ks[21], (L, N_EXPERTS), 0.01),
        "w_gate_up": nrm(ks[22], (L, N_EXPERTS, D, 2 * D_EXPERT), D ** -0.5),
        "w_down": nrm(ks[23], (L, N_EXPERTS, D_EXPERT, D), D_EXPERT ** -0.5),
        "final_norm_g": 1.0 + nrm(ks[24], (D,), 0.05),
    }


def reference(x_prompt, x_sample, cache_kv_w128, cache_kv_w512, cache_kv_w2048, state_pool,
              c_prompt, c_sample, norm1_g, w_ada, b_ada, w_in, w_attn_out, w_pool, pool_scale,
              w_pool_out, w_o, norm2_g, w_grp, b_grp, w_exp_router, b_exp_router, w_gate_up,
              w_down, final_norm_g):
    pos_p = jnp.arange(x_prompt.shape[1], dtype=jnp.int32)
    pos_s = PAST_LEN + jnp.arange(x_sample.shape[1], dtype=jnp.int32)
    xp, xs = x_prompt, x_sample
    kvp = [[], [], []]
    kvs = [[], [], []]
    poolp, pools = [], []
    for l in range(DEPTH):
        lw = (norm1_g[l], w_ada[l], b_ada[l], w_in[l], w_attn_out[l], w_pool[l], pool_scale[l],
              w_pool_out[l], w_o[l], norm2_g[l], w_grp[l], b_grp[l], w_exp_router[l],
              b_exp_router[l], w_gate_up[l], w_down[l])
        xp, kv_p, pool_p = decoder_layer(xp, c_prompt, pos_p, None, None, lw)
        xs, kv_s, pool_s = decoder_layer(
            xs, c_sample, pos_s, (cache_kv_w128[l], cache_kv_w512[l], cache_kv_w2048[l]), state_pool[l], lw)
        for g in range(N_ATTN_GROUPS):
            kvp[g].append(kv_p[g])
            kvs[g].append(kv_s[g])
        poolp.append(pool_p)
        pools.append(pool_s)
    y_prompt = rms_norm(xp, final_norm_g)
    y_sample = rms_norm(xs, final_norm_g)
    kv128_prompt = jnp.stack(kvp[0], axis=0)
    kv512_prompt = jnp.stack(kvp[1], axis=0)
    kv2048_prompt = jnp.stack(kvp[2], axis=0)
    pool_prompt = jnp.stack(poolp, axis=0)
    kv128_sample = jnp.stack(kvs[0], axis=0)
    kv512_sample = jnp.stack(kvs[1], axis=0)
    kv2048_sample = jnp.stack(kvs[2], axis=0)
    pool_sample = jnp.stack(pools, axis=0)
    return (y_prompt, y_sample, kv128_prompt, kv512_prompt, kv2048_prompt, pool_prompt,
            kv128_sample, kv512_sample, kv2048_sample, pool_sample)
```

```python
import functools

import jax
import jax.numpy as jnp
from jax import lax
from jax.experimental import pallas as pl
from jax.experimental.pallas import tpu as pltpu

F32 = jnp.float32
BF16 = jnp.bfloat16

HEAD_DIM = 64
HEADS_PER_GROUP = 4
HEAD_SHIFT = 2
LANE_HEAD_SHIFT = 6
GROUP_WIDTH = HEADS_PER_GROUP * HEAD_DIM
ATTN_WINDOWS = (128, 512, 2048)
ATTN_DILATIONS = (1, 4, 16)
N_GROUPS = 3
SPAN = 128
ATTN_WIDTH = N_GROUPS * GROUP_WIDTH
ROPE_THETA = 10000.0
PAST_LEN = 8192
POOL_WINDOWS = (2, 4, 8, 16)
POOL_GROUP_WIDTH = 128
POOL_WIDTH = 512
POOL_STATE_LEN = 15
POOL_HALO = 16
MOE_GROUPS = 4
EXPERTS_PER_GROUP = 8
N_EXPERTS = 32
N_COND = 6
EPS = 1e-6

LANE = 128
VMEM_LIMIT_BYTES = 56 * 1024 * 1024

NEG = -1e30
BIG_LANE = 1e9

ROUTER_LANES = LANE
GROUP_LANE0 = N_EXPERTS


def _cparams(n_axes):
    return pltpu.CompilerParams(dimension_semantics=("arbitrary",) * n_axes,
                                vmem_limit_bytes=VMEM_LIMIT_BYTES)


def _full(shape):
    nd = len(shape)
    return pl.BlockSpec(tuple(shape), lambda *_: (0,) * nd)


def _norm_mod(x, g, scale, shift):
    var = jnp.mean(x * x, axis=-1, keepdims=True)
    return (x * lax.rsqrt(var + EPS) * g) * (1.0 + scale) + shift


def _rope(x, cos, sin):
    lane = lax.broadcasted_iota(jnp.int32, (x.shape[0], LANE), 1)
    first_half = (lane & (HEAD_DIM - 1)) < (HEAD_DIM // 2)
    outs = []
    for c in range(x.shape[1] // LANE):
        xc = x[:, c * LANE:(c + 1) * LANE]
        partner = jnp.where(first_half, pltpu.roll(xc, LANE - HEAD_DIM // 2, 1),
                            pltpu.roll(xc, HEAD_DIM // 2, 1))
        outs.append(xc * cos + partner * sin)
    return jnp.concatenate(outs, axis=1)


def _silu(x):
    return x * jax.nn.sigmoid(x)


def _ada_kernel(c_ref, w_ref, b_ref, o_ref):
    s = _silu(c_ref[...]).astype(BF16)
    o_ref[...] = jnp.dot(s, w_ref[...].astype(BF16), preferred_element_type=F32) + b_ref[...]


def _ada(c_all, w_ada, b_ada):
    rows, d = c_all.shape
    n_out = w_ada.shape[1]
    tn = 1024
    return pl.pallas_call(
        _ada_kernel,
        grid=(n_out // tn,),
        in_specs=[_full((rows, d)),
                  pl.BlockSpec((d, tn), lambda j: (0, j)),
                  pl.BlockSpec((1, tn), lambda j: (0, j))],
        out_specs=pl.BlockSpec((rows, tn), lambda j: (0, j)),
        out_shape=jax.ShapeDtypeStruct((rows, n_out), F32),
        compiler_params=_cparams(1),
        name="ada",
    )(c_all, w_ada, b_ada.reshape(1, n_out))


def _in_proj_prompt_kernel(x_ref, mod_ref, g1_ref, w_ref, cos_ref, sin_ref, wpool_ref, pscale_ref,
                           q_ref, k_ref, v_ref, pooled_ref, kv0_ref, kv1_ref, kv2_ref, ptail_ref,
                           ue_ref, *, tm, n_tiles, tails):
    i = pl.program_id(1)
    x = x_ref[0]
    h = _norm_mod(x, g1_ref[...], mod_ref[0, 1], mod_ref[0, 0]).astype(BF16)
    y = jnp.dot(h, w_ref[...], preferred_element_type=F32)
    cos = cos_ref[...]
    sin = sin_ref[...]
    q = _rope(y[:, :ATTN_WIDTH], cos, sin) * (HEAD_DIM ** -0.5)
    k = _rope(y[:, ATTN_WIDTH:2 * ATTN_WIDTH], cos, sin)
    v = y[:, 2 * ATTN_WIDTH:3 * ATTN_WIDTH]
    u = y[:, 3 * ATTN_WIDTH:3 * ATTN_WIDTH + POOL_WIDTH]
    q_ref[0] = q.astype(BF16)
    k_ref[0] = k.astype(BF16)
    v_ref[0] = v.astype(BF16)

    for g, kv_ref in enumerate((kv0_ref, kv1_ref, kv2_ref)):
        first_tile, rows = tails[g]
        lo = tm - rows

        @pl.when(i >= first_tile)
        def _(kv_ref=kv_ref, g=g, lo=lo):
            kv_ref[0, :, 0:GROUP_WIDTH] = k[lo:, g * GROUP_WIDTH:(g + 1) * GROUP_WIDTH]
            kv_ref[0, :, GROUP_WIDTH:2 * GROUP_WIDTH] = v[lo:, g * GROUP_WIDTH:(g + 1) * GROUP_WIDTH]

    @pl.when(i == 0)
    def _():
        ue_ref[0:POOL_HALO, :] = jnp.zeros((POOL_HALO, POOL_WIDTH), F32)

    ue_ref[POOL_HALO:, :] = u
    pos1 = (i * tm + 1 + lax.broadcasted_iota(jnp.int32, (tm, 1), 0)).astype(F32)
    for gi, w in enumerate(POOL_WINDOWS):
        lanes = slice(gi * POOL_GROUP_WIDTH, (gi + 1) * POOL_GROUP_WIDTH)
        acc = ue_ref[pl.ds(POOL_HALO, tm), lanes]
        for j in range(1, w):
            acc = acc + ue_ref[pl.ds(POOL_HALO - j, tm), lanes]
        mean = acc / jnp.minimum(float(w), pos1)
        z = jnp.dot((mean - u[:, lanes]).astype(BF16), wpool_ref[gi], preferred_element_type=F32)
        pooled_ref[0, :, lanes] = (z * pscale_ref[:, lanes]).astype(BF16)
    ue_ref[0:POOL_HALO, :] = u[tm - POOL_HALO:, :]

    @pl.when(i == n_tiles - 1)
    def _():
        ptail_ref[0] = u[tm - POOL_HALO:, :]


def _in_proj_prompt(x, mod, g1, w_qkvu, cos, sin, w_pool, pool_scale, tm):
    B, S, D = x.shape
    n_tiles = S // tm
    n_w = w_qkvu.shape[1]
    tails, kv_shapes, kv_specs = [], [], []
    for W in ATTN_WINDOWS:
        Wg = min(W, S)
        if Wg >= tm:
            first = n_tiles - Wg // tm
            rows = tm
        else:
            first = n_tiles - 1
            rows = Wg
        tails.append((first, rows))
        kv_shapes.append(jax.ShapeDtypeStruct((B, Wg, 2 * GROUP_WIDTH), F32))
        kv_specs.append(pl.BlockSpec((1, rows, 2 * GROUP_WIDTH),
                                     lambda b, i, first=first: (b, jnp.maximum(i - first, 0), 0)))
    tok = lambda width: pl.BlockSpec((1, tm, width), lambda b, i: (b, i, 0))
    outs = pl.pallas_call(
        functools.partial(_in_proj_prompt_kernel, tm=tm, n_tiles=n_tiles, tails=tuple(tails)),
        grid=(B, n_tiles),
        in_specs=[tok(D),
                  pl.BlockSpec((1, N_COND, 1, D), lambda b, i: (b, 0, 0, 0)),
                  _full((1, D)),
                  _full((D, n_w)),
                  pl.BlockSpec((tm, LANE), lambda b, i: (i, 0)),
                  pl.BlockSpec((tm, LANE), lambda b, i: (i, 0)),
                  _full(w_pool.shape),
                  _full((1, POOL_WIDTH))],
        out_specs=[tok(ATTN_WIDTH), tok(ATTN_WIDTH), tok(ATTN_WIDTH), tok(POOL_WIDTH)] + kv_specs
                  + [pl.BlockSpec((1, POOL_HALO, POOL_WIDTH), lambda b, i: (b, 0, 0))],
        out_shape=[jax.ShapeDtypeStruct((B, S, ATTN_WIDTH), BF16)] * 3
                  + [jax.ShapeDtypeStruct((B, S, POOL_WIDTH), BF16)] + kv_shapes
                  + [jax.ShapeDtypeStruct((B, POOL_HALO, POOL_WIDTH), F32)],
        scratch_shapes=[pltpu.VMEM((tm + POOL_HALO, POOL_WIDTH), F32)],
        compiler_params=_cparams(2),
        name="in_proj_prompt",
    )(x, mod, g1, w_qkvu, cos, sin, w_pool, pool_scale)
    return outs


def _in_proj_sample_kernel(x_ref, mod_ref, g1_ref, w_ref, cos_ref, sin_ref, q_ref, kvn_ref, u_ref):
    h = _norm_mod(x_ref[0], g1_ref[...], mod_ref[0, 1], mod_ref[0, 0]).astype(BF16)
    y = jnp.dot(h, w_ref[...], preferred_element_type=F32)
    cos = cos_ref[...]
    sin = sin_ref[...]
    q = _rope(y[:, :ATTN_WIDTH], cos, sin) * (HEAD_DIM ** -0.5)
    k = _rope(y[:, ATTN_WIDTH:2 * ATTN_WIDTH], cos, sin)
    v = y[:, 2 * ATTN_WIDTH:3 * ATTN_WIDTH]
    q_ref[...] = q.astype(BF16)
    for g in range(N_GROUPS):
        gl = slice(g * GROUP_WIDTH, (g + 1) * GROUP_WIDTH)
        kvn_ref[:, 2 * g * GROUP_WIDTH:(2 * g + 1) * GROUP_WIDTH] = k[:, gl]
        kvn_ref[:, (2 * g + 1) * GROUP_WIDTH:(2 * g + 2) * GROUP_WIDTH] = v[:, gl]
    u_ref[...] = y[:, 3 * ATTN_WIDTH:3 * ATTN_WIDTH + POOL_WIDTH]


def _in_proj_sample(x, mod, g1, w_qkvu, cos, sin):
    _, T, D = x.shape
    n_w = w_qkvu.shape[1]
    return pl.pallas_call(
        _in_proj_sample_kernel,
        grid=(1,),
        in_specs=[_full((1, T, D)), _full((1, N_COND, T, D)), _full((1, D)), _full((D, n_w)),
                  _full((T, LANE)), _full((T, LANE))],
        out_specs=[_full((T, ATTN_WIDTH)), _full((T, 2 * ATTN_WIDTH)), _full((T, POOL_WIDTH))],
        out_shape=[jax.ShapeDtypeStruct((T, ATTN_WIDTH), BF16),
                   jax.ShapeDtypeStruct((T, 2 * ATTN_WIDTH), F32),
                   jax.ShapeDtypeStruct((T, POOL_WIDTH), F32)],
        compiler_params=_cparams(1),
        name="in_proj_sample",
    )(x, mod, g1, w_qkvu, cos, sin)


def _attn_kernel(q_ref, kc_ref, kp_ref, vc_ref, vp_ref, o_ref, lse_ref, kbuf, vbuf, *, nq):
    n0 = pl.program_id(1)
    kbuf[0:SPAN, :] = kp_ref[0]
    kbuf[SPAN:, :] = kc_ref[0]
    vbuf[0:SPAN, :] = vp_ref[0]
    vbuf[SPAN:, :] = vc_ref[0]
    qi = lax.broadcasted_iota(jnp.int32, (SPAN, 2 * SPAN), 0)
    kj = lax.broadcasted_iota(jnp.int32, (SPAN, 2 * SPAN), 1)
    band = (kj >= qi) & (kj <= qi + SPAN)
    band_first = band & (kj >= jnp.where(n0 > 0, 0, SPAN))
    lane = lax.broadcasted_iota(jnp.int32, (SPAN, LANE), 1)
    low_head = lane < HEAD_DIM
    for j in range(nq):
        valid = band_first if j == 0 else band
        rows = slice(j * SPAN, (j + 1) * SPAN)
        krows = slice(j * SPAN, (j + 2) * SPAN)
        for c in range(GROUP_WIDTH // LANE):
            cl = slice(c * LANE, (c + 1) * LANE)
            q = q_ref[0, rows, cl]
            kk = kbuf[krows, cl]
            vv = vbuf[krows, cl]
            o_pair, lse_pair = [], []
            for hh in range(2):
                mask_h = low_head if hh == 0 else jnp.logical_not(low_head)
                qm = jnp.where(mask_h, q, jnp.zeros_like(q))
                s = lax.dot_general(qm, kk, (((1,), (1,)), ((), ())), preferred_element_type=F32)
                s = jnp.where(valid, s, NEG)
                m = jnp.max(s, axis=-1, keepdims=True)
                p = jnp.exp(s - m)
                den = jnp.sum(p, axis=-1, keepdims=True)
                o_pair.append(jnp.dot(p.astype(BF16), vv, preferred_element_type=F32) / den)
                lse_pair.append(m + jnp.log(den))
            o_ref[0, rows, cl] = jnp.where(low_head, o_pair[0], o_pair[1]).astype(BF16)
            lse_ref[0, rows, cl] = jnp.where(low_head, lse_pair[0], lse_pair[1])


def _attn_prompt(q, k, v, g, nq):
    B, S, _ = q.shape
    d = ATTN_DILATIONS[g]
    M = S // d
    nq = min(nq, M // SPAN)
    ncol = ATTN_WIDTH // GROUP_WIDTH
    qv, kv, vv = (a.reshape(B, M, d * ATTN_WIDTH) for a in (q, k, v))
    cur = pl.BlockSpec((1, nq * SPAN, GROUP_WIDTH), lambda b, n, r: (b, n, r * ncol + g))
    prev = pl.BlockSpec((1, SPAN, GROUP_WIDTH), lambda b, n, r: (b, jnp.maximum(n * nq - 1, 0), r * ncol + g))
    out = pl.BlockSpec((1, nq * SPAN, GROUP_WIDTH), lambda b, n, r: (b, n, r))
    o, lse = pl.pallas_call(
        functools.partial(_attn_kernel, nq=nq),
        grid=(B, M // (nq * SPAN), d),
        in_specs=[cur, cur, prev, cur, prev],
        out_specs=[out, out],
        out_shape=[jax.ShapeDtypeStruct((B, M, d * GROUP_WIDTH), BF16),
                   jax.ShapeDtypeStruct((B, M, d * GROUP_WIDTH), F32)],
        scratch_shapes=[pltpu.VMEM(((nq + 1) * SPAN, GROUP_WIDTH), BF16)] * 2,
        compiler_params=_cparams(3),
        name=f"attn_prompt_g{g}",
    )(qv, kv, kv, vv, vv)
    return o.reshape(B, S, GROUP_WIDTH), lse.reshape(B, S, GROUP_WIDTH)


def _sample_kernel(qbd_ref, kvn_ref, c0_ref, c1_ref, c2_ref, sp_ref, u_ref, wpool_ref, pscale_ref,
                   attn_ref, pooled_ref, o0_ref, o1_ref, o2_ref, po_ref, ue_ref, diff_ref, *, t_new):
    rows = HEADS_PER_GROUP * t_new
    caches = (c0_ref, c1_ref, c2_ref)
    outs = (o0_ref, o1_ref, o2_ref)

    s_cache, s_new, valid_new = [], [], []
    m = jnp.full((rows, 1), NEG, F32)
    for g in range(N_GROUPS):
        d = ATTN_DILATIONS[g]
        c_ref = caches[g]
        L = c_ref.shape[1]
        qb = qbd_ref[0, g]
        kc = c_ref[0, :, 0:GROUP_WIDTH].astype(BF16)
        s = lax.dot_general(qb, kc, (((1,), (1,)), ((), ())), preferred_element_type=F32)
        t_row = lax.broadcasted_iota(jnp.int32, (rows, L), 0) >> HEAD_SHIFT
        delta = L + t_row - lax.broadcasted_iota(jnp.int32, (rows, L), 1)
        ok = ((delta & (d - 1)) == 0) & (delta <= SPAN * d)
        s = jnp.where(ok, s, NEG)
        s_cache.append(s)
        m = jnp.maximum(m, jnp.max(s, axis=-1, keepdims=True))
        qf = qb.astype(F32)
        t_col = lax.broadcasted_iota(jnp.int32, (rows, 1), 0) >> HEAD_SHIFT
        sn, okn = [], []
        for tn in range(t_new):
            kn = kvn_ref[0, tn:tn + 1, 2 * g * GROUP_WIDTH:(2 * g + 1) * GROUP_WIDTH]
            kn = kn.astype(BF16).astype(F32)
            dn = t_col - tn
            ok_n = (dn >= 0) & ((dn & (d - 1)) == 0)
            s1 = jnp.where(ok_n, jnp.sum(qf * kn, axis=-1, keepdims=True), NEG)
            sn.append(s1)
            okn.append(ok_n)
            m = jnp.maximum(m, s1)
        s_new.append(sn)
        valid_new.append(okn)

    den = jnp.zeros((rows, 1), F32)
    acc = jnp.zeros((rows, GROUP_WIDTH), F32)
    for g in range(N_GROUPS):
        c_ref = caches[g]
        p = jnp.exp(s_cache[g] - m)
        den = den + jnp.sum(p, axis=-1, keepdims=True)
        vc = c_ref[0, :, GROUP_WIDTH:2 * GROUP_WIDTH].astype(BF16)
        acc = acc + jnp.dot(p.astype(BF16), vc, preferred_element_type=F32)
        for tn in range(t_new):
            pn = jnp.exp(s_new[g][tn] - m)
            den = den + pn
            vn = kvn_ref[0, tn:tn + 1, (2 * g + 1) * GROUP_WIDTH:(2 * g + 2) * GROUP_WIDTH]
            acc = acc + pn * vn
    row_head = lax.broadcasted_iota(jnp.int32, (rows, GROUP_WIDTH), 0) & (HEADS_PER_GROUP - 1)
    lane_head = lax.broadcasted_iota(jnp.int32, (rows, GROUP_WIDTH), 1) >> LANE_HEAD_SHIFT
    o_diag = jnp.where(row_head == lane_head, acc / den, 0.0).astype(BF16)
    sel = ((lax.broadcasted_iota(jnp.int32, (rows, rows), 1) >> HEAD_SHIFT)
           == lax.broadcasted_iota(jnp.int32, (rows, rows), 0)).astype(BF16)
    attn = jnp.dot(sel, o_diag, preferred_element_type=F32)
    attn_ref[0] = attn[0:t_new].astype(BF16)

    for g in range(N_GROUPS):
        c_ref, o_ref = caches[g], outs[g]
        L = c_ref.shape[1]
        o_ref[0, 0:L - t_new, :] = c_ref[0, t_new:L, :]
        o_ref[0, L - t_new:L, :] = kvn_ref[0, :, 2 * g * GROUP_WIDTH:(2 * g + 2) * GROUP_WIDTH]

    ue_ref[0:POOL_STATE_LEN, :] = sp_ref[0]
    ue_ref[POOL_STATE_LEN:POOL_STATE_LEN + t_new, :] = u_ref[0]
    diff_ref[...] = jnp.zeros(diff_ref.shape, F32)
    for tn in range(t_new):
        r = POOL_STATE_LEN + tn
        for gi, w in enumerate(POOL_WINDOWS):
            lanes = slice(gi * POOL_GROUP_WIDTH, (gi + 1) * POOL_GROUP_WIDTH)
            win = jnp.sum(ue_ref[r - w + 1:r + 1, lanes], axis=0, keepdims=True)
            diff_ref[tn:tn + 1, lanes] = win / float(w) - ue_ref[r:r + 1, lanes]
    for gi in range(len(POOL_WINDOWS)):
        lanes = slice(gi * POOL_GROUP_WIDTH, (gi + 1) * POOL_GROUP_WIDTH)
        z = jnp.dot(diff_ref[:, lanes].astype(BF16), wpool_ref[gi], preferred_element_type=F32)
        pooled_ref[0, :, lanes] = (z[0:t_new] * pscale_ref[:, lanes]).astype(BF16)
    po_ref[0] = ue_ref[t_new:t_new + POOL_STATE_LEN, :]


def _sample_step(qbd, kvn, caches, state_pool, u, w_pool, pool_scale):
    N, t_new, _ = u.shape
    rows = HEADS_PER_GROUP * t_new
    per_n = lambda shape: pl.BlockSpec((1,) + tuple(shape), lambda n: (n,) + (0,) * len(shape))
    cache_specs = [per_n(c.shape[1:]) for c in caches]
    return pl.pallas_call(
        functools.partial(_sample_kernel, t_new=t_new),
        grid=(N,),
        in_specs=[per_n(qbd.shape[1:]), per_n(kvn.shape[1:])] + cache_specs
                 + [per_n(state_pool.shape[1:]), per_n(u.shape[1:]), _full(w_pool.shape), _full((1, POOL_WIDTH))],
        out_specs=[per_n((t_new, GROUP_WIDTH)), per_n((t_new, POOL_WIDTH))] + cache_specs
                  + [per_n(state_pool.shape[1:])],
        out_shape=[jax.ShapeDtypeStruct((N, t_new, GROUP_WIDTH), BF16),
                   jax.ShapeDtypeStruct((N, t_new, POOL_WIDTH), BF16)]
                  + [jax.ShapeDtypeStruct(c.shape, F32) for c in caches]
                  + [jax.ShapeDtypeStruct(state_pool.shape, F32)],
        scratch_shapes=[pltpu.VMEM((POOL_STATE_LEN + t_new + 5, POOL_WIDTH), F32),
                        pltpu.VMEM((rows, POOL_WIDTH), F32)],
        compiler_params=_cparams(1),
        name="sample_step",
    )(qbd, kvn, *caches, state_pool, u, w_pool, pool_scale)


def _route(logits):
    lane = lax.broadcasted_iota(jnp.int32, logits.shape, 1).astype(F32)
    is_grp = (lane >= GROUP_LANE0) & (lane < GROUP_LANE0 + MOE_GROUPS)
    gl = jnp.where(is_grp, logits, NEG)
    gmax = jnp.max(gl, axis=-1, keepdims=True)
    gidx = jnp.min(jnp.where(gl == gmax, lane, BIG_LANE), axis=-1, keepdims=True) - GROUP_LANE0
    gsum = jnp.sum(jnp.where(is_grp, jnp.exp(gl - gmax), 0.0), axis=-1, keepdims=True)
    grp_w = 1.0 / gsum
    lo = gidx * EXPERTS_PER_GROUP
    in_grp = (lane >= lo) & (lane < lo + EXPERTS_PER_GROUP)
    el = jnp.where(in_grp, logits, NEG)
    v1 = jnp.max(el, axis=-1, keepdims=True)
    i1 = jnp.min(jnp.where(el == v1, lane, BIG_LANE), axis=-1, keepdims=True)
    el2 = jnp.where(lane == i1, NEG, el)
    v2 = jnp.max(el2, axis=-1, keepdims=True)
    i2 = jnp.min(jnp.where(el2 == v2, lane, BIG_LANE), axis=-1, keepdims=True)
    e = jnp.exp(v2 - v1)
    w1 = grp_w / (1.0 + e)
    w2 = grp_w * e / (1.0 + e)
    return jnp.where(lane == i1, w1, jnp.where(lane == i2, w2, 0.0))


def _merge_kernel(*refs, n_groups):
    x_ref, mod_ref, g1_ref, g2_ref = refs[0:4]
    attn_refs = refs[4:4 + 2 * n_groups] if n_groups > 1 else refs[4:5]
    rest = refs[4 + (2 * n_groups if n_groups > 1 else 1):]
    (pooled_ref, wg_ref, wa_ref, wp_ref, wo_ref, wrh_ref, wrl_ref, br_ref,
     x1_ref, h2_ref, comb_ref) = rest

    x = x_ref[0]
    if n_groups > 1:
        os_ = [r[0].astype(F32) for r in attn_refs[:n_groups]]
        ls = [r[0] for r in attn_refs[n_groups:]]
        lmax = functools.reduce(jnp.maximum, ls)
        es = [jnp.exp(l - lmax) for l in ls]
        attn = sum(e * o for e, o in zip(es, os_)) / sum(es)
    else:
        attn = attn_refs[0][0]
    a = jnp.dot(attn.astype(BF16), wa_ref[...], preferred_element_type=F32)
    p = jnp.dot(pooled_ref[0], wp_ref[...], preferred_element_type=F32)
    h = _norm_mod(x, g1_ref[...], mod_ref[0, 1], mod_ref[0, 0]).astype(BF16)
    gates = jnp.dot(h, wg_ref[...], preferred_element_type=F32)
    D = x.shape[1]
    merged = jax.nn.sigmoid(gates[:, :D]) * a + jax.nn.sigmoid(gates[:, D:]) * p
    y = jnp.dot(merged.astype(BF16), wo_ref[...], preferred_element_type=F32)
    x1 = x + mod_ref[0, 2] * y
    x1_ref[0] = x1
    h2 = _norm_mod(x1, g2_ref[...], mod_ref[0, 4], mod_ref[0, 3])
    h2_hi = h2.astype(BF16)
    h2_ref[0] = h2_hi
    h2_lo = (h2 - h2_hi.astype(F32)).astype(BF16)
    logits = (jnp.dot(h2_hi, wrh_ref[...], preferred_element_type=F32)
              + jnp.dot(h2_lo, wrh_ref[...], preferred_element_type=F32)
              + jnp.dot(h2_hi, wrl_ref[...], preferred_element_type=F32)) + br_ref[...]
    comb_ref[0] = _route(logits)


def _merge(x, mod, g1, g2, attn_inputs, pooled, w_gates, w_attn_out, w_pool_out, w_o, wr_hi, wr_lo, b_r, tm):
    B, S, D = x.shape
    R = mod.shape[2]
    n_groups = len(attn_inputs) // 2 if len(attn_inputs) > 1 else 1
    tok = lambda width: pl.BlockSpec((1, tm, width), lambda b, i: (b, i, 0))
    if R == 1:
        mod_spec = pl.BlockSpec((1, N_COND, 1, D), lambda b, i: (b, 0, 0, 0))
    else:
        mod_spec = pl.BlockSpec((1, N_COND, tm, D), lambda b, i: (b, 0, i, 0))
    weights = (w_gates, w_attn_out, w_pool_out, w_o, wr_hi, wr_lo, b_r)
    return pl.pallas_call(
        functools.partial(_merge_kernel, n_groups=n_groups),
        grid=(B, S // tm),
        in_specs=[tok(D), mod_spec, _full((1, D)), _full((1, D))]
                 + [tok(GROUP_WIDTH)] * len(attn_inputs) + [tok(POOL_WIDTH)]
                 + [_full(w.shape) for w in weights],
        out_specs=[tok(D), tok(D), tok(ROUTER_LANES)],
        out_shape=[jax.ShapeDtypeStruct((B, S, D), F32), jax.ShapeDtypeStruct((B, S, D), BF16),
                   jax.ShapeDtypeStruct((B, S, ROUTER_LANES), F32)],
        compiler_params=_cparams(2),
        name="merge",
    )(x, mod, g1, g2, *attn_inputs, pooled, *weights)


def _moe_kernel(h2_ref, comb_ref, x1_ref, mod_ref, gf_ref, wgu_ref, wd_ref, y_ref, acc_ref, *, d_expert):
    e = pl.program_id(2)

    @pl.when(e == 0)
    def _():
        acc_ref[...] = jnp.zeros(acc_ref.shape, F32)

    gu = jnp.dot(h2_ref[0], wgu_ref[0], preferred_element_type=F32)
    comb = comb_ref[0]
    lane = lax.broadcasted_iota(jnp.int32, comb.shape, 1)
    c_e = jnp.sum(jnp.where(lane == e, comb, 0.0), axis=-1, keepdims=True)
    act = _silu(gu[:, :d_expert]) * gu[:, d_expert:] * c_e
    acc_ref[...] += jnp.dot(act.astype(BF16), wd_ref[0], preferred_element_type=F32)

    @pl.when(e == pl.num_programs(2) - 1)
    def _():
        x2 = x1_ref[0] + mod_ref[0, 5] * acc_ref[...]
        var = jnp.mean(x2 * x2, axis=-1, keepdims=True)
        y_ref[0] = x2 * lax.rsqrt(var + EPS) * gf_ref[...]


def _moe(h2, comb, x1, mod, gf, w_gate_up, w_down, tm):
    B, S, D = x1.shape
    R = mod.shape[2]
    n_exp, _, two_f = w_gate_up.shape
    tok = lambda width: pl.BlockSpec((1, tm, width), lambda b, i, e: (b, i, 0))
    if R == 1:
        mod_spec = pl.BlockSpec((1, N_COND, 1, D), lambda b, i, e: (b, 0, 0, 0))
    else:
        mod_spec = pl.BlockSpec((1, N_COND, tm, D), lambda b, i, e: (b, 0, i, 0))
    return pl.pallas_call(
        functools.partial(_moe_kernel, d_expert=two_f // 2),
        grid=(B, S // tm, n_exp),
        in_specs=[tok(D), tok(ROUTER_LANES), tok(D), mod_spec, _full((1, D)),
                  pl.BlockSpec((1, D, two_f), lambda b, i, e: (e, 0, 0)),
                  pl.BlockSpec((1, two_f // 2, D), lambda b, i, e: (e, 0, 0))],
        out_specs=tok(D),
        out_shape=jax.ShapeDtypeStruct((B, S, D), F32),
        scratch_shapes=[pltpu.VMEM((tm, D), F32)],
        compiler_params=_cparams(3),
        name="moe",
    )(h2, comb, x1, mod, gf, w_gate_up, w_down)


def _rope_tables(pos):
    half = HEAD_DIM // 2
    inv = ROPE_THETA ** (-jnp.arange(half, dtype=F32) * 2.0 / HEAD_DIM)
    ang = pos.astype(F32)[:, None] * inv[None, :]
    cos, sin = jnp.cos(ang), jnp.sin(ang)
    reps = LANE // HEAD_DIM
    return (jnp.tile(jnp.concatenate([cos, cos], axis=-1), (1, reps)),
            jnp.tile(jnp.concatenate([-sin, sin], axis=-1), (1, reps)))


def kernel(x_prompt, x_sample, cache_kv_w128, cache_kv_w512, cache_kv_w2048, state_pool, c_prompt, c_sample, norm1_g, w_ada, b_ada, w_in, w_attn_out, w_pool, pool_scale, w_pool_out, w_o, norm2_g, w_grp, b_grp, w_exp_router, b_exp_router, w_gate_up, w_down, final_norm_g):
    B, S, D = x_prompt.shape
    N, T, _ = x_sample.shape
    depth = norm1_g.shape[0]
    assert depth == 1, "single trunk layer"
    tm = min(512, S)
    assert S % tm == 0 and all(S % (SPAN * d) == 0 for d in ATTN_DILATIONS)

    n_qkvu = 3 * ATTN_WIDTH + POOL_WIDTH
    w_qkvu = w_in[0, :, :n_qkvu].astype(BF16)
    w_gates = w_in[0, :, n_qkvu:].astype(BF16)
    wa, wpo, wo = w_attn_out[0].astype(BF16), w_pool_out[0].astype(BF16), w_o[0].astype(BF16)
    wpool = w_pool[0].astype(BF16)
    pscale = pool_scale[0].reshape(1, POOL_WIDTH)
    w_r = jnp.concatenate([w_exp_router[0], w_grp[0]], axis=1)
    w_r = jnp.pad(w_r, ((0, 0), (0, ROUTER_LANES - w_r.shape[1])))
    wr_hi = w_r.astype(BF16)
    wr_lo = (w_r - wr_hi.astype(F32)).astype(BF16)
    b_r = jnp.pad(jnp.concatenate([b_exp_router[0], b_grp[0]]), (0, ROUTER_LANES - N_EXPERTS - MOE_GROUPS))
    b_r = b_r.reshape(1, ROUTER_LANES)
    wgu, wd = w_gate_up[0].astype(BF16), w_down[0].astype(BF16)
    g1, g2, gf = norm1_g[0].reshape(1, D), norm2_g[0].reshape(1, D), final_norm_g.reshape(1, D)

    mod = _ada(jnp.concatenate([c_prompt, c_sample], axis=0), w_ada[0], b_ada[0])
    mod_p = mod[:B].reshape(B, N_COND, 1, D)
    mod_s = jnp.repeat(mod[B:].reshape(N, N_COND, D), T, axis=0)
    mod_s = jnp.transpose(mod_s, (1, 0, 2)).reshape(1, N_COND, N * T, D)

    cos_p, sin_p = _rope_tables(jnp.arange(S, dtype=jnp.int32))
    q, k, v, pooled, kv0, kv1, kv2, ptail = _in_proj_prompt(x_prompt, mod_p, g1, w_qkvu, cos_p, sin_p,
                                                            wpool, pscale, tm)
    attn_parts = [_attn_prompt(q, k, v, g, nq=4) for g in range(N_GROUPS)]
    attn_inputs = [o for o, _ in attn_parts] + [l for _, l in attn_parts]
    x1, h2, comb = _merge(x_prompt, mod_p, g1, g2, attn_inputs, pooled, w_gates, wa, wpo, wo,
                          wr_hi, wr_lo, b_r, tm)
    y_prompt = _moe(h2, comb, x1, mod_p, gf, wgu, wd, tm=min(1024, S))

    kv_shape = lambda a: a.reshape(1, B, a.shape[1], 2, HEADS_PER_GROUP, HEAD_DIM)
    pool_prompt = ptail[:, POOL_HALO - POOL_STATE_LEN:, :][None]

    TS = N * T
    pos_s = PAST_LEN + jnp.arange(T, dtype=jnp.int32)
    cos_s, sin_s = _rope_tables(pos_s)
    cos_s, sin_s = jnp.tile(cos_s, (N, 1)), jnp.tile(sin_s, (N, 1))
    xs = x_sample.reshape(1, TS, D)
    q_s, kvn, u_s = _in_proj_sample(xs, mod_s, g1, w_qkvu, cos_s, sin_s)
    eye = jnp.eye(HEADS_PER_GROUP, dtype=BF16)
    qbd = jnp.einsum('ntghe,hk->ngthke', q_s.reshape(N, T, N_GROUPS, HEADS_PER_GROUP, HEAD_DIM), eye)
    qbd = qbd.reshape(N, N_GROUPS, T * HEADS_PER_GROUP, GROUP_WIDTH)
    caches = [c[0].reshape(N, c.shape[2], 2 * GROUP_WIDTH) for c in (cache_kv_w128, cache_kv_w512, cache_kv_w2048)]
    attn_s, pooled_s, ko0, ko1, ko2, pool_s = _sample_step(
        qbd, kvn.reshape(N, T, 2 * ATTN_WIDTH), caches, state_pool[0], u_s.reshape(N, T, POOL_WIDTH), wpool, pscale)
    x1s, h2s, comb_s = _merge(xs, mod_s, g1, g2, [attn_s.reshape(1, TS, GROUP_WIDTH)],
                              pooled_s.reshape(1, TS, POOL_WIDTH), w_gates, wa, wpo, wo, wr_hi, wr_lo, b_r, TS)
    y_sample = _moe(h2s, comb_s, x1s, mod_s, gf, wgu, wd, tm=TS).reshape(N, T, D)

    kvs_shape = lambda a: a.reshape(1, N, a.shape[1], 2, HEADS_PER_GROUP, HEAD_DIM)
    return (y_prompt, y_sample, kv_shape(kv0), kv_shape(kv1), kv_shape(kv2), pool_prompt,
            kvs_shape(ko0), kvs_shape(ko1), kvs_shape(ko2), pool_s[None])
```

```python
import functools

import jax
import jax.numpy as jnp
from jax import lax
from jax.experimental import pallas as pl
from jax.experimental.pallas import tpu as pltpu

F32 = jnp.float32
BF16 = jnp.bfloat16

HEAD_DIM = 64
HEADS_PER_GROUP = 4
HEAD_SHIFT = 2
LANE_HEAD_SHIFT = 6
GROUP_WIDTH = HEADS_PER_GROUP * HEAD_DIM
ATTN_WINDOWS = (128, 512, 2048)
ATTN_DILATIONS = (1, 4, 16)
N_GROUPS = 3
SPAN = 128
ATTN_WIDTH = N_GROUPS * GROUP_WIDTH
ROPE_THETA = 10000.0
PAST_LEN = 8192
POOL_WINDOWS = (2, 4, 8, 16)
POOL_GROUP_WIDTH = 128
POOL_WIDTH = 512
POOL_STATE_LEN = 15
POOL_HALO = 16
MOE_GROUPS = 4
EXPERTS_PER_GROUP = 8
N_EXPERTS = 32
N_COND = 6
EPS = 1e-6

LANE = 128
VMEM_LIMIT_BYTES = 56 * 1024 * 1024

NEG = -1e30
BIG_LANE = 1e9

ROUTER_LANES = LANE
GROUP_LANE0 = N_EXPERTS


def _cparams(n_axes):
    return pltpu.CompilerParams(dimension_semantics=("arbitrary",) * n_axes,
                                vmem_limit_bytes=VMEM_LIMIT_BYTES)


def _full(shape):
    nd = len(shape)
    return pl.BlockSpec(tuple(shape), lambda *_: (0,) * nd)


def _norm_mod(x, g, scale, shift):
    var = jnp.mean(x * x, axis=-1, keepdims=True)
    return (x * lax.rsqrt(var + EPS) * g) * (1.0 + scale) + shift


def _rope(x, cos, sin):
    lane = lax.broadcasted_iota(jnp.int32, (x.shape[0], LANE), 1)
    first_half = (lane & (HEAD_DIM - 1)) < (HEAD_DIM // 2)
    outs = []
    for c in range(x.shape[1] // LANE):
        xc = x[:, c * LANE:(c + 1) * LANE]
        partner = jnp.where(first_half, pltpu.roll(xc, LANE - HEAD_DIM // 2, 1),
                            pltpu.roll(xc, HEAD_DIM // 2, 1))
        outs.append(xc * cos + partner * sin)
    return jnp.concatenate(outs, axis=1)


def _silu(x):
    return x * jax.nn.sigmoid(x)


def _ada_kernel(c_ref, w_ref, b_ref, o_ref):
    s = _silu(c_ref[...]).astype(BF16)
    o_ref[...] = jnp.dot(s, w_ref[...].astype(BF16), preferred_element_type=F32) + b_ref[...]


def _ada(c_all, w_ada, b_ada):
    rows, d = c_all.shape
    n_out = w_ada.shape[1]
    tn = 1024
    return pl.pallas_call(
        _ada_kernel,
        grid=(n_out // tn,),
        in_specs=[_full((rows, d)),
                  pl.BlockSpec((d, tn), lambda j: (0, j)),
                  pl.BlockSpec((1, tn), lambda j: (0, j))],
        out_specs=pl.BlockSpec((rows, tn), lambda j: (0, j)),
        out_shape=jax.ShapeDtypeStruct((rows, n_out), F32),
        compiler_params=_cparams(1),
        name="ada",
    )(c_all, w_ada, b_ada.reshape(1, n_out))


def _in_proj_prompt_kernel(x_ref, mod_ref, g1_ref, w_ref, cos_ref, sin_ref, wpool_ref, pscale_ref, *rest,
                           tm, n_tiles, tails):
    qkv_refs = (rest[0:3], rest[3:6], rest[6:9])
    pooled_ref, kv0_ref, kv1_ref, kv2_ref, ptail_ref = rest[9:14]
    stage_refs = rest[14:17]
    ue_ref = rest[17]
    i = pl.program_id(1)
    x = x_ref[0]
    h = _norm_mod(x, g1_ref[...], mod_ref[0, 1], mod_ref[0, 0]).astype(BF16)
    y = jnp.dot(h, w_ref[...], preferred_element_type=F32)
    cos = cos_ref[...]
    sin = sin_ref[...]
    q = _rope(y[:, :ATTN_WIDTH], cos, sin) * (HEAD_DIM ** -0.5)
    k = _rope(y[:, ATTN_WIDTH:2 * ATTN_WIDTH], cos, sin)
    v = y[:, 2 * ATTN_WIDTH:3 * ATTN_WIDTH]
    u = y[:, 3 * ATTN_WIDTH:3 * ATTN_WIDTH + POOL_WIDTH]

    chunks = GROUP_WIDTH // LANE
    for val, out_refs, stage_ref in zip((q, k, v), qkv_refs, stage_refs):
        for g, d in enumerate(ATTN_DILATIONS):
            gl = slice(g * GROUP_WIDTH, (g + 1) * GROUP_WIDTH)
            if d == 1:
                out_refs[g][0] = val[:, gl].astype(BF16)
                continue
            for c in range(chunks):
                stage_ref[g * chunks + c] = val[:, g * GROUP_WIDTH + c * LANE:g * GROUP_WIDTH + (c + 1) * LANE]
            for r in range(d):
                for c in range(chunks):
                    out_refs[g][0, :, r * GROUP_WIDTH + c * LANE:r * GROUP_WIDTH + (c + 1) * LANE] = (
                        stage_ref[g * chunks + c, pl.ds(r, tm // d, stride=d), :].astype(BF16))

    for g, kv_ref in enumerate((kv0_ref, kv1_ref, kv2_ref)):
        first_tile, rows = tails[g]
        lo = tm - rows

        @pl.when(i >= first_tile)
        def _(kv_ref=kv_ref, g=g, lo=lo):
            kv_ref[0, 0] = k[lo:, g * GROUP_WIDTH:(g + 1) * GROUP_WIDTH].T
            kv_ref[0, 1] = v[lo:, g * GROUP_WIDTH:(g + 1) * GROUP_WIDTH].T

    @pl.when(i == 0)
    def _():
        ue_ref[0:POOL_HALO, :] = jnp.zeros((POOL_HALO, POOL_WIDTH), F32)

    ue_ref[POOL_HALO:, :] = u
    pos1 = (i * tm + 1 + lax.broadcasted_iota(jnp.int32, (tm, 1), 0)).astype(F32)
    for gi, w in enumerate(POOL_WINDOWS):
        lanes = slice(gi * POOL_GROUP_WIDTH, (gi + 1) * POOL_GROUP_WIDTH)
        acc = ue_ref[pl.ds(POOL_HALO, tm), lanes]
        for j in range(1, w):
            acc = acc + ue_ref[pl.ds(POOL_HALO - j, tm), lanes]
        mean = acc / jnp.minimum(float(w), pos1)
        z = jnp.dot((mean - u[:, lanes]).astype(BF16), wpool_ref[gi], preferred_element_type=F32)
        pooled_ref[0, :, lanes] = (z * pscale_ref[:, lanes]).astype(BF16)
    ue_ref[0:POOL_HALO, :] = u[tm - POOL_HALO:, :]

    @pl.when(i == n_tiles - 1)
    def _():
        ptail_ref[0] = u[tm - POOL_HALO:, :]


def _in_proj_prompt(x, mod, g1, w_qkvu, cos, sin, w_pool, pool_scale, tm):
    B, S, D = x.shape
    n_tiles = S // tm
    n_w = w_qkvu.shape[1]
    tails, kv_shapes, kv_specs = [], [], []
    for W in ATTN_WINDOWS:
        Wg = min(W, S)
        if Wg >= tm:
            first = n_tiles - Wg // tm
            rows = tm
        else:
            first = n_tiles - 1
            rows = Wg
        tails.append((first, rows))
        kv_shapes.append(jax.ShapeDtypeStruct((B, 2, GROUP_WIDTH, Wg), F32))
        kv_specs.append(pl.BlockSpec((1, 2, GROUP_WIDTH, rows),
                                     lambda b, i, first=first: (b, 0, 0, jnp.maximum(i - first, 0))))
    tok = lambda width: pl.BlockSpec((1, tm, width), lambda b, i: (b, i, 0))
    dil_specs = [pl.BlockSpec((1, tm // d, d * GROUP_WIDTH), lambda b, i: (b, i, 0)) for d in ATTN_DILATIONS]
    dil_shapes = [jax.ShapeDtypeStruct((B, S // d, d * GROUP_WIDTH), BF16) for d in ATTN_DILATIONS]
    outs = pl.pallas_call(
        functools.partial(_in_proj_prompt_kernel, tm=tm, n_tiles=n_tiles, tails=tuple(tails)),
        grid=(B, n_tiles),
        in_specs=[tok(D),
                  pl.BlockSpec((1, N_COND, 1, D), lambda b, i: (b, 0, 0, 0)),
                  _full((1, D)),
                  _full((D, n_w)),
                  pl.BlockSpec((tm, LANE), lambda b, i: (i, 0)),
                  pl.BlockSpec((tm, LANE), lambda b, i: (i, 0)),
                  _full(w_pool.shape),
                  _full((1, POOL_WIDTH))],
        out_specs=dil_specs * 3 + [tok(POOL_WIDTH)] + kv_specs
                  + [pl.BlockSpec((1, POOL_HALO, POOL_WIDTH), lambda b, i: (b, 0, 0))],
        out_shape=dil_shapes * 3
                  + [jax.ShapeDtypeStruct((B, S, POOL_WIDTH), BF16)] + kv_shapes
                  + [jax.ShapeDtypeStruct((B, POOL_HALO, POOL_WIDTH), F32)],
        scratch_shapes=[pltpu.VMEM((ATTN_WIDTH // LANE, tm, LANE), F32)] * 3
                       + [pltpu.VMEM((tm + POOL_HALO, POOL_WIDTH), F32)],
        compiler_params=_cparams(2),
        name="in_proj_prompt",
    )(x, mod, g1, w_qkvu, cos, sin, w_pool, pool_scale)
    return outs


def _in_proj_sample_kernel(x_ref, mod_ref, g1_ref, w_ref, cos_ref, sin_ref, q_ref, kvn_ref, u_ref):
    h = _norm_mod(x_ref[0], g1_ref[...], mod_ref[0, 1], mod_ref[0, 0]).astype(BF16)
    y = jnp.dot(h, w_ref[...], preferred_element_type=F32)
    cos = cos_ref[...]
    sin = sin_ref[...]
    q = _rope(y[:, :ATTN_WIDTH], cos, sin) * (HEAD_DIM ** -0.5)
    k = _rope(y[:, ATTN_WIDTH:2 * ATTN_WIDTH], cos, sin)
    v = y[:, 2 * ATTN_WIDTH:3 * ATTN_WIDTH]
    q_ref[...] = q.astype(BF16)
    for g in range(N_GROUPS):
        gl = slice(g * GROUP_WIDTH, (g + 1) * GROUP_WIDTH)
        kvn_ref[:, 2 * g * GROUP_WIDTH:(2 * g + 1) * GROUP_WIDTH] = k[:, gl]
        kvn_ref[:, (2 * g + 1) * GROUP_WIDTH:(2 * g + 2) * GROUP_WIDTH] = v[:, gl]
    u_ref[...] = y[:, 3 * ATTN_WIDTH:3 * ATTN_WIDTH + POOL_WIDTH]


def _in_proj_sample(x, mod, g1, w_qkvu, cos, sin):
    _, T, D = x.shape
    n_w = w_qkvu.shape[1]
    return pl.pallas_call(
        _in_proj_sample_kernel,
        grid=(1,),
        in_specs=[_full((1, T, D)), _full((1, N_COND, T, D)), _full((1, D)), _full((D, n_w)),
                  _full((T, LANE)), _full((T, LANE))],
        out_specs=[_full((T, ATTN_WIDTH)), _full((T, 2 * ATTN_WIDTH)), _full((T, POOL_WIDTH))],
        out_shape=[jax.ShapeDtypeStruct((T, ATTN_WIDTH), BF16),
                   jax.ShapeDtypeStruct((T, 2 * ATTN_WIDTH), F32),
                   jax.ShapeDtypeStruct((T, POOL_WIDTH), F32)],
        compiler_params=_cparams(1),
        name="in_proj_sample",
    )(x, mod, g1, w_qkvu, cos, sin)


def _attn_kernel(q_ref, kc_ref, kp_ref, vc_ref, vp_ref, o_ref, lse_ref, kbuf, vbuf, *, nq):
    n0 = pl.program_id(1)
    kbuf[0:SPAN, :] = kp_ref[0]
    kbuf[SPAN:, :] = kc_ref[0]
    vbuf[0:SPAN, :] = vp_ref[0]
    vbuf[SPAN:, :] = vc_ref[0]
    qi = lax.broadcasted_iota(jnp.int32, (SPAN, 2 * SPAN), 0)
    kj = lax.broadcasted_iota(jnp.int32, (SPAN, 2 * SPAN), 1)
    band = (kj >= qi) & (kj <= qi + SPAN)
    band_first = band & (kj >= jnp.where(n0 > 0, 0, SPAN))
    lane = lax.broadcasted_iota(jnp.int32, (SPAN, LANE), 1)
    low_head = lane < HEAD_DIM
    for j in range(nq):
        valid = band_first if j == 0 else band
        rows = slice(j * SPAN, (j + 1) * SPAN)
        krows = slice(j * SPAN, (j + 2) * SPAN)
        for c in range(GROUP_WIDTH // LANE):
            cl = slice(c * LANE, (c + 1) * LANE)
            q = q_ref[0, rows, cl]
            kk = kbuf[krows, cl]
            vv = vbuf[krows, cl]
            o_pair, lse_pair = [], []
            for hh in range(2):
                mask_h = low_head if hh == 0 else jnp.logical_not(low_head)
                qm = jnp.where(mask_h, q, jnp.zeros_like(q))
                s = lax.dot_general(qm, kk, (((1,), (1,)), ((), ())), preferred_element_type=F32)
                s = jnp.where(valid, s, NEG)
                m = jnp.max(s, axis=-1, keepdims=True)
                p = jnp.exp(s - m)
                den = jnp.sum(p, axis=-1, keepdims=True)
                o_pair.append(jnp.dot(p.astype(BF16), vv, preferred_element_type=F32) / den)
                lse_pair.append(m + jnp.log(den))
            o_ref[0, rows, cl] = jnp.where(low_head, o_pair[0], o_pair[1]).astype(BF16)
            lse_ref[0, rows, cl] = jnp.where(low_head, lse_pair[0], lse_pair[1])


def _attn_prompt(qv, kv, vv, g, nq):
    B, M, width = qv.shape
    d = width // GROUP_WIDTH
    nq = min(nq, M // SPAN)
    cur = pl.BlockSpec((1, nq * SPAN, GROUP_WIDTH), lambda b, n, r: (b, n, r))
    prev = pl.BlockSpec((1, SPAN, GROUP_WIDTH), lambda b, n, r: (b, jnp.maximum(n * nq - 1, 0), r))
    out = cur
    return pl.pallas_call(
        functools.partial(_attn_kernel, nq=nq),
        grid=(B, M // (nq * SPAN), d),
        in_specs=[cur, cur, prev, cur, prev],
        out_specs=[out, out],
        out_shape=[jax.ShapeDtypeStruct((B, M, d * GROUP_WIDTH), BF16),
                   jax.ShapeDtypeStruct((B, M, d * GROUP_WIDTH), F32)],
        scratch_shapes=[pltpu.VMEM(((nq + 1) * SPAN, GROUP_WIDTH), BF16)] * 2,
        compiler_params=_cparams(3),
        name=f"attn_prompt_g{g}",
    )(qv, kv, kv, vv, vv)


def _sample_kernel(qbd_ref, kvn_ref, c0_ref, c1_ref, c2_ref, sp_ref, u_ref, wpool_ref, pscale_ref,
                   attn_ref, pooled_ref, o0_ref, o1_ref, o2_ref, po_ref, ue_ref, diff_ref, nt_ref, *, t_new):
    rows = HEADS_PER_GROUP * t_new
    caches = (c0_ref, c1_ref, c2_ref)
    outs = (o0_ref, o1_ref, o2_ref)

    s_cache, s_new, valid_new = [], [], []
    m = jnp.full((rows, 1), NEG, F32)
    for g in range(N_GROUPS):
        d = ATTN_DILATIONS[g]
        c_ref = caches[g]
        L = c_ref.shape[3]
        qb = qbd_ref[0, g]
        s = jnp.dot(qb, c_ref[0, 0].astype(BF16), preferred_element_type=F32)
        t_row = lax.broadcasted_iota(jnp.int32, (rows, L), 0) >> HEAD_SHIFT
        delta = L + t_row - lax.broadcasted_iota(jnp.int32, (rows, L), 1)
        ok = ((delta & (d - 1)) == 0) & (delta <= SPAN * d)
        s = jnp.where(ok, s, NEG)
        s_cache.append(s)
        m = jnp.maximum(m, jnp.max(s, axis=-1, keepdims=True))
        qf = qb.astype(F32)
        t_col = lax.broadcasted_iota(jnp.int32, (rows, 1), 0) >> HEAD_SHIFT
        sn, okn = [], []
        for tn in range(t_new):
            kn = kvn_ref[0, tn:tn + 1, 2 * g * GROUP_WIDTH:(2 * g + 1) * GROUP_WIDTH]
            kn = kn.astype(BF16).astype(F32)
            dn = t_col - tn
            ok_n = (dn >= 0) & ((dn & (d - 1)) == 0)
            s1 = jnp.where(ok_n, jnp.sum(qf * kn, axis=-1, keepdims=True), NEG)
            sn.append(s1)
            okn.append(ok_n)
            m = jnp.maximum(m, s1)
        s_new.append(sn)
        valid_new.append(okn)

    den = jnp.zeros((rows, 1), F32)
    acc = jnp.zeros((rows, GROUP_WIDTH), F32)
    for g in range(N_GROUPS):
        c_ref = caches[g]
        p = jnp.exp(s_cache[g] - m)
        den = den + jnp.sum(p, axis=-1, keepdims=True)
        acc = acc + lax.dot_general(p.astype(BF16), c_ref[0, 1].astype(BF16), (((1,), (1,)), ((), ())),
                                    preferred_element_type=F32)
        for tn in range(t_new):
            pn = jnp.exp(s_new[g][tn] - m)
            den = den + pn
            vn = kvn_ref[0, tn:tn + 1, (2 * g + 1) * GROUP_WIDTH:(2 * g + 2) * GROUP_WIDTH]
            acc = acc + pn * vn
    row_head = lax.broadcasted_iota(jnp.int32, (rows, GROUP_WIDTH), 0) & (HEADS_PER_GROUP - 1)
    lane_head = lax.broadcasted_iota(jnp.int32, (rows, GROUP_WIDTH), 1) >> LANE_HEAD_SHIFT
    o_diag = jnp.where(row_head == lane_head, acc / den, 0.0).astype(BF16)
    sel = ((lax.broadcasted_iota(jnp.int32, (rows, rows), 1) >> HEAD_SHIFT)
           == lax.broadcasted_iota(jnp.int32, (rows, rows), 0)).astype(BF16)
    attn = jnp.dot(sel, o_diag, preferred_element_type=F32)
    attn_ref[0] = attn[0:t_new].astype(BF16)

    tail_lane = lax.broadcasted_iota(jnp.int32, (GROUP_WIDTH, LANE), 1)
    nt_ref[...] = jnp.zeros(nt_ref.shape, F32)
    for g in range(N_GROUPS):
        c_ref, o_ref = caches[g], outs[g]
        L = c_ref.shape[3]
        for kv in range(2):
            col0 = (2 * g + kv) * GROUP_WIDTH
            nt_ref[LANE - t_new:LANE, :] = kvn_ref[0, :, col0:col0 + GROUP_WIDTH]
            new_t = nt_ref[...].T
            rolled = pltpu.roll(c_ref[0, kv], L - t_new, 1)
            if L > LANE:
                o_ref[0, kv, :, 0:L - LANE] = rolled[:, 0:L - LANE]
            o_ref[0, kv, :, L - LANE:L] = jnp.where(tail_lane >= LANE - t_new, new_t, rolled[:, L - LANE:L])

    ue_ref[0:POOL_STATE_LEN, :] = sp_ref[0]
    ue_ref[POOL_STATE_LEN:POOL_STATE_LEN + t_new, :] = u_ref[0]
    diff_ref[...] = jnp.zeros(diff_ref.shape, F32)
    for tn in range(t_new):
        r = POOL_STATE_LEN + tn
        for gi, w in enumerate(POOL_WINDOWS):
            lanes = slice(gi * POOL_GROUP_WIDTH, (gi + 1) * POOL_GROUP_WIDTH)
            win = jnp.sum(ue_ref[r - w + 1:r + 1, lanes], axis=0, keepdims=True)
            diff_ref[tn:tn + 1, lanes] = win / float(w) - ue_ref[r:r + 1, lanes]
    for gi in range(len(POOL_WINDOWS)):
        lanes = slice(gi * POOL_GROUP_WIDTH, (gi + 1) * POOL_GROUP_WIDTH)
        z = jnp.dot(diff_ref[:, lanes].astype(BF16), wpool_ref[gi], preferred_element_type=F32)
        pooled_ref[0, :, lanes] = (z[0:t_new] * pscale_ref[:, lanes]).astype(BF16)
    po_ref[0] = ue_ref[t_new:t_new + POOL_STATE_LEN, :]


def _sample_step(qbd, kvn, caches, state_pool, u, w_pool, pool_scale):
    N, t_new, _ = u.shape
    rows = HEADS_PER_GROUP * t_new
    per_n = lambda shape: pl.BlockSpec((1,) + tuple(shape), lambda n: (n,) + (0,) * len(shape))
    cache_specs = [per_n(c.shape[1:]) for c in caches]
    return pl.pallas_call(
        functools.partial(_sample_kernel, t_new=t_new),
        grid=(N,),
        in_specs=[per_n(qbd.shape[1:]), per_n(kvn.shape[1:])] + cache_specs
                 + [per_n(state_pool.shape[1:]), per_n(u.shape[1:]), _full(w_pool.shape), _full((1, POOL_WIDTH))],
        out_specs=[per_n((t_new, GROUP_WIDTH)), per_n((t_new, POOL_WIDTH))] + cache_specs
                  + [per_n(state_pool.shape[1:])],
        out_shape=[jax.ShapeDtypeStruct((N, t_new, GROUP_WIDTH), BF16),
                   jax.ShapeDtypeStruct((N, t_new, POOL_WIDTH), BF16)]
                  + [jax.ShapeDtypeStruct(c.shape, F32) for c in caches]
                  + [jax.ShapeDtypeStruct(state_pool.shape, F32)],
        scratch_shapes=[pltpu.VMEM((POOL_STATE_LEN + t_new + 5, POOL_WIDTH), F32),
                        pltpu.VMEM((rows, POOL_WIDTH), F32),
                        pltpu.VMEM((LANE, GROUP_WIDTH), F32)],
        compiler_params=_cparams(1),
        name="sample_step",
    )(qbd, kvn, *caches, state_pool, u, w_pool, pool_scale)


def _route(logits):
    lane = lax.broadcasted_iota(jnp.int32, logits.shape, 1).astype(F32)
    is_grp = (lane >= GROUP_LANE0) & (lane < GROUP_LANE0 + MOE_GROUPS)
    gl = jnp.where(is_grp, logits, NEG)
    gmax = jnp.max(gl, axis=-1, keepdims=True)
    gidx = jnp.min(jnp.where(gl == gmax, lane, BIG_LANE), axis=-1, keepdims=True) - GROUP_LANE0
    gsum = jnp.sum(jnp.where(is_grp, jnp.exp(gl - gmax), 0.0), axis=-1, keepdims=True)
    grp_w = 1.0 / gsum
    lo = gidx * EXPERTS_PER_GROUP
    in_grp = (lane >= lo) & (lane < lo + EXPERTS_PER_GROUP)
    el = jnp.where(in_grp, logits, NEG)
    v1 = jnp.max(el, axis=-1, keepdims=True)
    i1 = jnp.min(jnp.where(el == v1, lane, BIG_LANE), axis=-1, keepdims=True)
    el2 = jnp.where(lane == i1, NEG, el)
    v2 = jnp.max(el2, axis=-1, keepdims=True)
    i2 = jnp.min(jnp.where(el2 == v2, lane, BIG_LANE), axis=-1, keepdims=True)
    e = jnp.exp(v2 - v1)
    w1 = grp_w / (1.0 + e)
    w2 = grp_w * e / (1.0 + e)
    return jnp.where(lane == i1, w1, jnp.where(lane == i2, w2, 0.0))


def _merge_kernel(*refs, n_groups):
    x_ref, mod_ref, g1_ref, g2_ref = refs[0:4]
    attn_refs = refs[4:4 + 2 * n_groups] if n_groups > 1 else refs[4:5]
    rest = refs[4 + (2 * n_groups if n_groups > 1 else 1):]
    (pooled_ref, wg_ref, wa_ref, wp_ref, wo_ref, wrh_ref, wrl_ref, br_ref,
     x1_ref, h2_ref, comb_ref) = rest[:11]
    stage_refs = rest[11:]

    x = x_ref[0]
    tm = x.shape[0]
    if n_groups > 1:
        vals = []
        for idx, ref in enumerate(attn_refs):
            d = ATTN_DILATIONS[idx % n_groups]
            if d == 1:
                vals.append(ref[0].astype(F32))
                continue
            stage_ref = stage_refs[idx]
            chunks = GROUP_WIDTH // LANE
            for r in range(d):
                for c in range(chunks):
                    stage_ref[c, pl.ds(r, tm // d, stride=d), :] = (
                        ref[0, :, r * GROUP_WIDTH + c * LANE:r * GROUP_WIDTH + (c + 1) * LANE].astype(F32))
            vals.append(jnp.concatenate([stage_ref[c] for c in range(chunks)], axis=1))
        os_, ls = vals[:n_groups], vals[n_groups:]
        lmax = functools.reduce(jnp.maximum, ls)
        es = [jnp.exp(l - lmax) for l in ls]
        attn = sum(e * o for e, o in zip(es, os_)) / sum(es)
    else:
        attn = attn_refs[0][0]
    a = jnp.dot(attn.astype(BF16), wa_ref[...], preferred_element_type=F32)
    p = jnp.dot(pooled_ref[0], wp_ref[...], preferred_element_type=F32)
    h = _norm_mod(x, g1_ref[...], mod_ref[0, 1], mod_ref[0, 0]).astype(BF16)
    gates = jnp.dot(h, wg_ref[...], preferred_element_type=F32)
    D = x.shape[1]
    merged = jax.nn.sigmoid(gates[:, :D]) * a + jax.nn.sigmoid(gates[:, D:]) * p
    y = jnp.dot(merged.astype(BF16), wo_ref[...], preferred_element_type=F32)
    x1 = x + mod_ref[0, 2] * y
    x1_ref[0] = x1
    h2 = _norm_mod(x1, g2_ref[...], mod_ref[0, 4], mod_ref[0, 3])
    h2_hi = h2.astype(BF16)
    h2_ref[0] = h2_hi
    h2_lo = (h2 - h2_hi.astype(F32)).astype(BF16)
    logits = (jnp.dot(h2_hi, wrh_ref[...], preferred_element_type=F32)
              + jnp.dot(h2_lo, wrh_ref[...], preferred_element_type=F32)
              + jnp.dot(h2_hi, wrl_ref[...], preferred_element_type=F32)) + br_ref[...]
    comb_ref[0] = _route(logits)


def _merge(x, mod, g1, g2, attn_inputs, pooled, w_gates, w_attn_out, w_pool_out, w_o, wr_hi, wr_lo, b_r, tm):
    B, S, D = x.shape
    R = mod.shape[2]
    n_groups = len(attn_inputs) // 2 if len(attn_inputs) > 1 else 1
    tok = lambda width: pl.BlockSpec((1, tm, width), lambda b, i: (b, i, 0))
    if R == 1:
        mod_spec = pl.BlockSpec((1, N_COND, 1, D), lambda b, i: (b, 0, 0, 0))
    else:
        mod_spec = pl.BlockSpec((1, N_COND, tm, D), lambda b, i: (b, 0, i, 0))
    weights = (w_gates, w_attn_out, w_pool_out, w_o, wr_hi, wr_lo, b_r)
    if n_groups > 1:
        attn_specs = [pl.BlockSpec((1, tm // (a.shape[2] // GROUP_WIDTH), a.shape[2]), lambda b, i: (b, i, 0))
                      for a in attn_inputs]
        scratch = [pltpu.VMEM((GROUP_WIDTH // LANE, tm, LANE), F32)] * len(attn_inputs)
    else:
        attn_specs = [tok(GROUP_WIDTH)]
        scratch = []
    return pl.pallas_call(
        functools.partial(_merge_kernel, n_groups=n_groups),
        grid=(B, S // tm),
        in_specs=[tok(D), mod_spec, _full((1, D)), _full((1, D))]
                 + attn_specs + [tok(POOL_WIDTH)]
                 + [_full(w.shape) for w in weights],
        out_specs=[tok(D), tok(D), tok(ROUTER_LANES)],
        out_shape=[jax.ShapeDtypeStruct((B, S, D), F32), jax.ShapeDtypeStruct((B, S, D), BF16),
                   jax.ShapeDtypeStruct((B, S, ROUTER_LANES), F32)],
        scratch_shapes=scratch,
        compiler_params=_cparams(2),
        name="merge",
    )(x, mod, g1, g2, *attn_inputs, pooled, *weights)


def _moe_kernel(h2_ref, comb_ref, x1_ref, mod_ref, gf_ref, wgu_ref, wd_ref, y_ref, acc_ref, *, d_expert):
    e = pl.program_id(2)

    @pl.when(e == 0)
    def _():
        acc_ref[...] = jnp.zeros(acc_ref.shape, F32)

    gu = jnp.dot(h2_ref[0], wgu_ref[0], preferred_element_type=F32)
    comb = comb_ref[0]
    lane = lax.broadcasted_iota(jnp.int32, comb.shape, 1)
    c_e = jnp.sum(jnp.where(lane == e, comb, 0.0), axis=-1, keepdims=True)
    act = _silu(gu[:, :d_expert]) * gu[:, d_expert:] * c_e
    acc_ref[...] += jnp.dot(act.astype(BF16), wd_ref[0], preferred_element_type=F32)

    @pl.when(e == pl.num_programs(2) - 1)
    def _():
        x2 = x1_ref[0] + mod_ref[0, 5] * acc_ref[...]
        var = jnp.mean(x2 * x2, axis=-1, keepdims=True)
        y_ref[0] = x2 * lax.rsqrt(var + EPS) * gf_ref[...]


def _moe(h2, comb, x1, mod, gf, w_gate_up, w_down, tm):
    B, S, D = x1.shape
    R = mod.shape[2]
    n_exp, _, two_f = w_gate_up.shape
    tok = lambda width: pl.BlockSpec((1, tm, width), lambda b, i, e: (b, i, 0))
    if R == 1:
        mod_spec = pl.BlockSpec((1, N_COND, 1, D), lambda b, i, e: (b, 0, 0, 0))
    else:
        mod_spec = pl.BlockSpec((1, N_COND, tm, D), lambda b, i, e: (b, 0, i, 0))
    return pl.pallas_call(
        functools.partial(_moe_kernel, d_expert=two_f // 2),
        grid=(B, S // tm, n_exp),
        in_specs=[tok(D), tok(ROUTER_LANES), tok(D), mod_spec, _full((1, D)),
                  pl.BlockSpec((1, D, two_f), lambda b, i, e: (e, 0, 0)),
                  pl.BlockSpec((1, two_f // 2, D), lambda b, i, e: (e, 0, 0))],
        out_specs=tok(D),
        out_shape=jax.ShapeDtypeStruct((B, S, D), F32),
        scratch_shapes=[pltpu.VMEM((tm, D), F32)],
        compiler_params=_cparams(3),
        name="moe",
    )(h2, comb, x1, mod, gf, w_gate_up, w_down)


def _rope_tables(pos):
    half = HEAD_DIM // 2
    inv = ROPE_THETA ** (-jnp.arange(half, dtype=F32) * 2.0 / HEAD_DIM)
    ang = pos.astype(F32)[:, None] * inv[None, :]
    cos, sin = jnp.cos(ang), jnp.sin(ang)
    reps = LANE // HEAD_DIM
    return (jnp.tile(jnp.concatenate([cos, cos], axis=-1), (1, reps)),
            jnp.tile(jnp.concatenate([-sin, sin], axis=-1), (1, reps)))


def kernel(x_prompt, x_sample, cache_kv_w128, cache_kv_w512, cache_kv_w2048, state_pool, c_prompt, c_sample, norm1_g, w_ada, b_ada, w_in, w_attn_out, w_pool, pool_scale, w_pool_out, w_o, norm2_g, w_grp, b_grp, w_exp_router, b_exp_router, w_gate_up, w_down, final_norm_g):
    B, S, D = x_prompt.shape
    N, T, _ = x_sample.shape
    depth = norm1_g.shape[0]
    assert depth == 1, "single trunk layer"
    tm = min(512, S)
    assert S % tm == 0 and all(S % (SPAN * d) == 0 for d in ATTN_DILATIONS)

    n_qkvu = 3 * ATTN_WIDTH + POOL_WIDTH
    w_qkvu = w_in[0, :, :n_qkvu].astype(BF16)
    w_gates = w_in[0, :, n_qkvu:].astype(BF16)
    wa, wpo, wo = w_attn_out[0].astype(BF16), w_pool_out[0].astype(BF16), w_o[0].astype(BF16)
    wpool = w_pool[0].astype(BF16)
    pscale = pool_scale[0].reshape(1, POOL_WIDTH)
    w_r = jnp.concatenate([w_exp_router[0], w_grp[0]], axis=1)
    w_r = jnp.pad(w_r, ((0, 0), (0, ROUTER_LANES - w_r.shape[1])))
    wr_hi = w_r.astype(BF16)
    wr_lo = (w_r - wr_hi.astype(F32)).astype(BF16)
    b_r = jnp.pad(jnp.concatenate([b_exp_router[0], b_grp[0]]), (0, ROUTER_LANES - N_EXPERTS - MOE_GROUPS))
    b_r = b_r.reshape(1, ROUTER_LANES)
    wgu, wd = w_gate_up[0].astype(BF16), w_down[0].astype(BF16)
    g1, g2, gf = norm1_g[0].reshape(1, D), norm2_g[0].reshape(1, D), final_norm_g.reshape(1, D)

    mod = _ada(jnp.concatenate([c_prompt, c_sample], axis=0), w_ada[0], b_ada[0])
    mod_p = mod[:B].reshape(B, N_COND, 1, D)
    mod_s = jnp.repeat(mod[B:].reshape(N, N_COND, D), T, axis=0)
    mod_s = jnp.transpose(mod_s, (1, 0, 2)).reshape(1, N_COND, N * T, D)

    cos_p, sin_p = _rope_tables(jnp.arange(S, dtype=jnp.int32))
    outs = _in_proj_prompt(x_prompt, mod_p, g1, w_qkvu, cos_p, sin_p, wpool, pscale, tm)
    qs, ks, vs = outs[0:3], outs[3:6], outs[6:9]
    pooled, kv0, kv1, kv2, ptail = outs[9:14]
    attn_parts = [_attn_prompt(qs[g], ks[g], vs[g], g, nq=4) for g in range(N_GROUPS)]
    attn_inputs = [o for o, _ in attn_parts] + [l for _, l in attn_parts]
    x1, h2, comb = _merge(x_prompt, mod_p, g1, g2, attn_inputs, pooled, w_gates, wa, wpo, wo,
                          wr_hi, wr_lo, b_r, tm)
    y_prompt = _moe(h2, comb, x1, mod_p, gf, wgu, wd, tm=min(1024, S))

    def kv_shape(a):
        a = a.reshape(a.shape[0], 2, HEADS_PER_GROUP, HEAD_DIM, a.shape[3])
        return jnp.transpose(a, (0, 4, 1, 2, 3))[None]

    pool_prompt = ptail[:, POOL_HALO - POOL_STATE_LEN:, :][None]

    TS = N * T
    pos_s = PAST_LEN + jnp.arange(T, dtype=jnp.int32)
    cos_s, sin_s = _rope_tables(pos_s)
    cos_s, sin_s = jnp.tile(cos_s, (N, 1)), jnp.tile(sin_s, (N, 1))
    xs = x_sample.reshape(1, TS, D)
    q_s, kvn, u_s = _in_proj_sample(xs, mod_s, g1, w_qkvu, cos_s, sin_s)
    eye = jnp.eye(HEADS_PER_GROUP, dtype=BF16)
    qbd = jnp.einsum('ntghe,hk->ngthke', q_s.reshape(N, T, N_GROUPS, HEADS_PER_GROUP, HEAD_DIM), eye)
    qbd = qbd.reshape(N, N_GROUPS, T * HEADS_PER_GROUP, GROUP_WIDTH)
    caches = [jnp.transpose(c[0], (0, 2, 3, 4, 1)).reshape(N, 2, GROUP_WIDTH, c.shape[2])
              for c in (cache_kv_w128, cache_kv_w512, cache_kv_w2048)]
    attn_s, pooled_s, ko0, ko1, ko2, pool_s = _sample_step(
        qbd, kvn.reshape(N, T, 2 * ATTN_WIDTH), caches, state_pool[0], u_s.reshape(N, T, POOL_WIDTH), wpool, pscale)
    x1s, h2s, comb_s = _merge(xs, mod_s, g1, g2, [attn_s.reshape(1, TS, GROUP_WIDTH)],
                              pooled_s.reshape(1, TS, POOL_WIDTH), w_gates, wa, wpo, wo, wr_hi, wr_lo, b_r, TS)
    y_sample = _moe(h2s, comb_s, x1s, mod_s, gf, wgu, wd, tm=TS).reshape(N, T, D)

    return (y_prompt, y_sample, kv_shape(kv0), kv_shape(kv1), kv_shape(kv2), pool_prompt,
            kv_shape(ko0), kv_shape(ko1), kv_shape(ko2), pool_s[None])
```

```python
import functools

import jax
import jax.numpy as jnp
from jax import lax
from jax.experimental import pallas as pl
from jax.experimental.pallas import tpu as pltpu

F32 = jnp.float32
BF16 = jnp.bfloat16

HEAD_DIM = 64
HEADS_PER_GROUP = 4
HEAD_SHIFT = 2
LANE_HEAD_SHIFT = 6
GROUP_WIDTH = HEADS_PER_GROUP * HEAD_DIM
ATTN_WINDOWS = (128, 512, 2048)
ATTN_DILATIONS = (1, 4, 16)
N_GROUPS = 3
SPAN = 128
ATTN_WIDTH = N_GROUPS * GROUP_WIDTH
ROPE_THETA = 10000.0
PAST_LEN = 8192
POOL_WINDOWS = (2, 4, 8, 16)
POOL_GROUP_WIDTH = 128
POOL_WIDTH = 512
POOL_STATE_LEN = 15
POOL_HALO = 16
MOE_GROUPS = 4
EXPERTS_PER_GROUP = 8
N_EXPERTS = 32
N_COND = 6
EPS = 1e-6

LANE = 128
SUBLANE = 8
VMEM_LIMIT_BYTES = 56 * 1024 * 1024

NEG = -1e30
BIG_LANE = 1e9

ROUTER_LANES = LANE
GROUP_LANE0 = N_EXPERTS


def _cparams(n_axes):
    return pltpu.CompilerParams(dimension_semantics=("arbitrary",) * n_axes,
                                vmem_limit_bytes=VMEM_LIMIT_BYTES)


def _full(shape):
    nd = len(shape)
    return pl.BlockSpec(tuple(shape), lambda *_: (0,) * nd)


def _norm_mod(x, g, scale, shift):
    var = jnp.mean(x * x, axis=-1, keepdims=True)
    return (x * lax.rsqrt(var + EPS) * g) * (1.0 + scale) + shift


def _rope(x, cos, sin):
    lane = lax.broadcasted_iota(jnp.int32, (x.shape[0], LANE), 1)
    first_half = (lane & (HEAD_DIM - 1)) < (HEAD_DIM // 2)
    outs = []
    for c in range(x.shape[1] // LANE):
        xc = x[:, c * LANE:(c + 1) * LANE]
        partner = jnp.where(first_half, pltpu.roll(xc, LANE - HEAD_DIM // 2, 1),
                            pltpu.roll(xc, HEAD_DIM // 2, 1))
        outs.append(xc * cos + partner * sin)
    return jnp.concatenate(outs, axis=1)


def _silu(x):
    return x * jax.nn.sigmoid(x)


def _ada_kernel(c_ref, w_ref, b_ref, o_ref):
    s = _silu(c_ref[...]).astype(BF16)
    o_ref[...] = jnp.dot(s, w_ref[...].astype(BF16), preferred_element_type=F32) + b_ref[...]


def _ada(c_all, w_ada, b_ada):
    rows, d = c_all.shape
    n_out = w_ada.shape[1]
    tn = 1024
    return pl.pallas_call(
        _ada_kernel,
        grid=(n_out // tn,),
        in_specs=[_full((rows, d)),
                  pl.BlockSpec((d, tn), lambda j: (0, j)),
                  pl.BlockSpec((1, tn), lambda j: (0, j))],
        out_specs=pl.BlockSpec((rows, tn), lambda j: (0, j)),
        out_shape=jax.ShapeDtypeStruct((rows, n_out), F32),
        compiler_params=_cparams(1),
        name="ada",
    )(c_all, w_ada, b_ada.reshape(1, n_out))


def _in_proj_prompt_kernel(x_ref, mod_ref, g1_ref, w_ref, cos_ref, sin_ref, wpool_ref, pscale_ref, *rest,
                           tm, n_tiles, tails):
    qkv_refs = (rest[0:3], rest[3:6], rest[6:9])
    pooled_ref, kv0_ref, kv1_ref, kv2_ref, ptail_ref = rest[9:14]
    stage_refs = rest[14:17]
    ue_ref = rest[17]
    i = pl.program_id(1)
    x = x_ref[0]
    h = _norm_mod(x, g1_ref[...], mod_ref[0, 1], mod_ref[0, 0]).astype(BF16)
    y = jnp.dot(h, w_ref[...], preferred_element_type=F32)
    cos = cos_ref[...]
    sin = sin_ref[...]
    q = _rope(y[:, :ATTN_WIDTH], cos, sin) * (HEAD_DIM ** -0.5)
    k = _rope(y[:, ATTN_WIDTH:2 * ATTN_WIDTH], cos, sin)
    v = y[:, 2 * ATTN_WIDTH:3 * ATTN_WIDTH]
    u = y[:, 3 * ATTN_WIDTH:3 * ATTN_WIDTH + POOL_WIDTH]

    chunks = GROUP_WIDTH // LANE
    for val, out_refs, stage_ref in zip((q, k, v), qkv_refs, stage_refs):
        for g, d in enumerate(ATTN_DILATIONS):
            gl = slice(g * GROUP_WIDTH, (g + 1) * GROUP_WIDTH)
            if d == 1:
                out_refs[g][0] = val[:, gl].astype(BF16)
                continue
            for c in range(chunks):
                stage_ref[g * chunks + c] = val[:, g * GROUP_WIDTH + c * LANE:g * GROUP_WIDTH + (c + 1) * LANE]
            for r in range(d):
                for c in range(chunks):
                    out_refs[g][0, :, r * GROUP_WIDTH + c * LANE:r * GROUP_WIDTH + (c + 1) * LANE] = (
                        stage_ref[g * chunks + c, pl.ds(r, tm // d, stride=d), :].astype(BF16))

    for g, kv_ref in enumerate((kv0_ref, kv1_ref, kv2_ref)):
        first_tile, rows = tails[g]
        lo = tm - rows

        @pl.when(i >= first_tile)
        def _(kv_ref=kv_ref, g=g, lo=lo):
            kv_ref[0, 0] = k[lo:, g * GROUP_WIDTH:(g + 1) * GROUP_WIDTH].T
            kv_ref[0, 1] = v[lo:, g * GROUP_WIDTH:(g + 1) * GROUP_WIDTH].T

    @pl.when(i == 0)
    def _():
        ue_ref[0:POOL_HALO, :] = jnp.zeros((POOL_HALO, POOL_WIDTH), F32)

    ue_ref[POOL_HALO:, :] = u
    pos1 = (i * tm + 1 + lax.broadcasted_iota(jnp.int32, (tm, 1), 0)).astype(F32)
    for gi, w in enumerate(POOL_WINDOWS):
        lanes = slice(gi * POOL_GROUP_WIDTH, (gi + 1) * POOL_GROUP_WIDTH)
        acc = ue_ref[pl.ds(POOL_HALO, tm), lanes]
        for j in range(1, w):
            acc = acc + ue_ref[pl.ds(POOL_HALO - j, tm), lanes]
        mean = acc / jnp.minimum(float(w), pos1)
        z = jnp.dot((mean - u[:, lanes]).astype(BF16), wpool_ref[gi], preferred_element_type=F32)
        pooled_ref[0, :, lanes] = (z * pscale_ref[:, lanes]).astype(BF16)
    ue_ref[0:POOL_HALO, :] = u[tm - POOL_HALO:, :]

    @pl.when(i == n_tiles - 1)
    def _():
        ptail_ref[0] = u[tm - POOL_HALO:, :]


def _in_proj_prompt(x, mod, g1, w_qkvu, cos, sin, w_pool, pool_scale, tm):
    B, S, D = x.shape
    n_tiles = S // tm
    n_w = w_qkvu.shape[1]
    tails, kv_shapes, kv_specs = [], [], []
    for W in ATTN_WINDOWS:
        Wg = min(W, S)
        if Wg >= tm:
            first = n_tiles - Wg // tm
            rows = tm
        else:
            first = n_tiles - 1
            rows = Wg
        tails.append((first, rows))
        kv_shapes.append(jax.ShapeDtypeStruct((B, 2, GROUP_WIDTH, Wg), F32))
        kv_specs.append(pl.BlockSpec((1, 2, GROUP_WIDTH, rows),
                                     lambda b, i, first=first: (b, 0, 0, jnp.maximum(i - first, 0))))
    tok = lambda width: pl.BlockSpec((1, tm, width), lambda b, i: (b, i, 0))
    dil_specs = [pl.BlockSpec((1, tm // d, d * GROUP_WIDTH), lambda b, i: (b, i, 0)) for d in ATTN_DILATIONS]
    dil_shapes = [jax.ShapeDtypeStruct((B, S // d, d * GROUP_WIDTH), BF16) for d in ATTN_DILATIONS]
    outs = pl.pallas_call(
        functools.partial(_in_proj_prompt_kernel, tm=tm, n_tiles=n_tiles, tails=tuple(tails)),
        grid=(B, n_tiles),
        in_specs=[tok(D),
                  pl.BlockSpec((1, N_COND, 1, D), lambda b, i: (b, 0, 0, 0)),
                  _full((1, D)),
                  _full((D, n_w)),
                  pl.BlockSpec((tm, LANE), lambda b, i: (i, 0)),
                  pl.BlockSpec((tm, LANE), lambda b, i: (i, 0)),
                  _full(w_pool.shape),
                  _full((1, POOL_WIDTH))],
        out_specs=dil_specs * 3 + [tok(POOL_WIDTH)] + kv_specs
                  + [pl.BlockSpec((1, POOL_HALO, POOL_WIDTH), lambda b, i: (b, 0, 0))],
        out_shape=dil_shapes * 3
                  + [jax.ShapeDtypeStruct((B, S, POOL_WIDTH), BF16)] + kv_shapes
                  + [jax.ShapeDtypeStruct((B, POOL_HALO, POOL_WIDTH), F32)],
        scratch_shapes=[pltpu.VMEM((ATTN_WIDTH // LANE, tm, LANE), F32)] * 3
                       + [pltpu.VMEM((tm + POOL_HALO, POOL_WIDTH), F32)],
        compiler_params=_cparams(2),
        name="in_proj_prompt",
    )(x, mod, g1, w_qkvu, cos, sin, w_pool, pool_scale)
    return outs


def _in_proj_sample_kernel(x_ref, mod_ref, g1_ref, w_ref, cos_ref, sin_ref, q_ref, kvn_ref, u_ref):
    h = _norm_mod(x_ref[0], g1_ref[...], mod_ref[0, 1], mod_ref[0, 0]).astype(BF16)
    y = jnp.dot(h, w_ref[...], preferred_element_type=F32)
    cos = cos_ref[...]
    sin = sin_ref[...]
    q = _rope(y[:, :ATTN_WIDTH], cos, sin) * (HEAD_DIM ** -0.5)
    k = _rope(y[:, ATTN_WIDTH:2 * ATTN_WIDTH], cos, sin)
    v = y[:, 2 * ATTN_WIDTH:3 * ATTN_WIDTH]
    q_ref[...] = q.astype(BF16)
    for g in range(N_GROUPS):
        gl = slice(g * GROUP_WIDTH, (g + 1) * GROUP_WIDTH)
        kvn_ref[:, 2 * g * GROUP_WIDTH:(2 * g + 1) * GROUP_WIDTH] = k[:, gl]
        kvn_ref[:, (2 * g + 1) * GROUP_WIDTH:(2 * g + 2) * GROUP_WIDTH] = v[:, gl]
    u_ref[...] = y[:, 3 * ATTN_WIDTH:3 * ATTN_WIDTH + POOL_WIDTH]


def _in_proj_sample(x, mod, g1, w_qkvu, cos, sin):
    _, T, D = x.shape
    n_w = w_qkvu.shape[1]
    return pl.pallas_call(
        _in_proj_sample_kernel,
        grid=(1,),
        in_specs=[_full((1, T, D)), _full((1, N_COND, T, D)), _full((1, D)), _full((D, n_w)),
                  _full((T, LANE)), _full((T, LANE))],
        out_specs=[_full((T, ATTN_WIDTH)), _full((T, 2 * ATTN_WIDTH)), _full((T, POOL_WIDTH))],
        out_shape=[jax.ShapeDtypeStruct((T, ATTN_WIDTH), BF16),
                   jax.ShapeDtypeStruct((T, 2 * ATTN_WIDTH), F32),
                   jax.ShapeDtypeStruct((T, POOL_WIDTH), F32)],
        compiler_params=_cparams(1),
        name="in_proj_sample",
    )(x, mod, g1, w_qkvu, cos, sin)


def _attn_kernel(q_ref, kc_ref, kp_ref, vc_ref, vp_ref, o_ref, lse_ref, kbuf, vbuf, *, nq):
    n0 = pl.program_id(1)
    kbuf[0:SPAN, :] = kp_ref[0]
    kbuf[SPAN:, :] = kc_ref[0]
    vbuf[0:SPAN, :] = vp_ref[0]
    vbuf[SPAN:, :] = vc_ref[0]
    qi = lax.broadcasted_iota(jnp.int32, (SPAN, 2 * SPAN), 0)
    kj = lax.broadcasted_iota(jnp.int32, (SPAN, 2 * SPAN), 1)
    band = (kj >= qi) & (kj <= qi + SPAN)
    band_first = band & (kj >= jnp.where(n0 > 0, 0, SPAN))
    lane = lax.broadcasted_iota(jnp.int32, (SPAN, LANE), 1)
    low_head = lane < HEAD_DIM
    for j in range(nq):
        valid = band_first if j == 0 else band
        rows = slice(j * SPAN, (j + 1) * SPAN)
        krows = slice(j * SPAN, (j + 2) * SPAN)
        for c in range(GROUP_WIDTH // LANE):
            cl = slice(c * LANE, (c + 1) * LANE)
            q = q_ref[0, rows, cl]
            kk = kbuf[krows, cl]
            vv = vbuf[krows, cl]
            o_pair, lse_pair = [], []
            for hh in range(2):
                mask_h = low_head if hh == 0 else jnp.logical_not(low_head)
                qm = jnp.where(mask_h, q, jnp.zeros_like(q))
                s = lax.dot_general(qm, kk, (((1,), (1,)), ((), ())), preferred_element_type=F32)
                s = jnp.where(valid, s, NEG)
                m = jnp.max(s, axis=-1, keepdims=True)
                p = jnp.exp(s - m)
                den = jnp.sum(p, axis=-1, keepdims=True)
                o_pair.append(jnp.dot(p.astype(BF16), vv, preferred_element_type=F32) / den)
                lse_pair.append(m + jnp.log(den))
            o_ref[0, rows, cl] = jnp.where(low_head, o_pair[0], o_pair[1]).astype(BF16)
            lse_ref[0, rows, cl] = jnp.where(low_head, lse_pair[0], lse_pair[1])


def _attn_prompt(qv, kv, vv, g, nq):
    B, M, width = qv.shape
    d = width // GROUP_WIDTH
    nq = min(nq, M // SPAN)
    cur = pl.BlockSpec((1, nq * SPAN, GROUP_WIDTH), lambda b, n, r: (b, n, r))
    prev = pl.BlockSpec((1, SPAN, GROUP_WIDTH), lambda b, n, r: (b, jnp.maximum(n * nq - 1, 0), r))
    out = cur
    return pl.pallas_call(
        functools.partial(_attn_kernel, nq=nq),
        grid=(B, M // (nq * SPAN), d),
        in_specs=[cur, cur, prev, cur, prev],
        out_specs=[out, out],
        out_shape=[jax.ShapeDtypeStruct((B, M, d * GROUP_WIDTH), BF16),
                   jax.ShapeDtypeStruct((B, M, d * GROUP_WIDTH), F32)],
        scratch_shapes=[pltpu.VMEM(((nq + 1) * SPAN, GROUP_WIDTH), BF16)] * 2,
        compiler_params=_cparams(3),
        name=f"attn_prompt_g{g}",
    )(qv, kv, kv, vv, vv)


def _sample_kernel(qbd_ref, kvn_ref, c0_ref, c1_ref, c2_ref, sp_ref, u_ref, wpool_ref, pscale_ref,
                   attn_ref, pooled_ref, o0_ref, o1_ref, o2_ref, po_ref, ue_ref, diff_ref, nt_ref, *, t_new):
    rows = HEADS_PER_GROUP * t_new
    caches = (c0_ref, c1_ref, c2_ref)
    outs = (o0_ref, o1_ref, o2_ref)

    s_cache, s_new, valid_new = [], [], []
    m = jnp.full((rows, 1), NEG, F32)
    for g in range(N_GROUPS):
        d = ATTN_DILATIONS[g]
        c_ref = caches[g]
        L = c_ref.shape[3]
        qb = qbd_ref[0, g]
        s = jnp.dot(qb, c_ref[0, 0].astype(BF16), preferred_element_type=F32)
        t_row = lax.broadcasted_iota(jnp.int32, (rows, L), 0) >> HEAD_SHIFT
        delta = L + t_row - lax.broadcasted_iota(jnp.int32, (rows, L), 1)
        ok = ((delta & (d - 1)) == 0) & (delta <= SPAN * d)
        s = jnp.where(ok, s, NEG)
        s_cache.append(s)
        m = jnp.maximum(m, jnp.max(s, axis=-1, keepdims=True))
        qf = qb.astype(F32)
        t_col = lax.broadcasted_iota(jnp.int32, (rows, 1), 0) >> HEAD_SHIFT
        sn, okn = [], []
        for tn in range(t_new):
            kn = kvn_ref[0, tn:tn + 1, 2 * g * GROUP_WIDTH:(2 * g + 1) * GROUP_WIDTH]
            kn = kn.astype(BF16).astype(F32)
            dn = t_col - tn
            ok_n = (dn >= 0) & ((dn & (d - 1)) == 0)
            s1 = jnp.where(ok_n, jnp.sum(qf * kn, axis=-1, keepdims=True), NEG)
            sn.append(s1)
            okn.append(ok_n)
            m = jnp.maximum(m, s1)
        s_new.append(sn)
        valid_new.append(okn)

    den = jnp.zeros((rows, 1), F32)
    acc = jnp.zeros((rows, GROUP_WIDTH), F32)
    for g in range(N_GROUPS):
        c_ref = caches[g]
        p = jnp.exp(s_cache[g] - m)
        den = den + jnp.sum(p, axis=-1, keepdims=True)
        acc = acc + lax.dot_general(p.astype(BF16), c_ref[0, 1].astype(BF16), (((1,), (1,)), ((), ())),
                                    preferred_element_type=F32)
        for tn in range(t_new):
            pn = jnp.exp(s_new[g][tn] - m)
            den = den + pn
            vn = kvn_ref[0, tn:tn + 1, (2 * g + 1) * GROUP_WIDTH:(2 * g + 2) * GROUP_WIDTH]
            acc = acc + pn * vn
    row_head = lax.broadcasted_iota(jnp.int32, (rows, GROUP_WIDTH), 0) & (HEADS_PER_GROUP - 1)
    lane_head = lax.broadcasted_iota(jnp.int32, (rows, GROUP_WIDTH), 1) >> LANE_HEAD_SHIFT
    o_diag = jnp.where(row_head == lane_head, acc / den, 0.0).astype(BF16)
    sel = ((lax.broadcasted_iota(jnp.int32, (rows, rows), 1) >> HEAD_SHIFT)
           == lax.broadcasted_iota(jnp.int32, (rows, rows), 0)).astype(BF16)
    attn = jnp.dot(sel, o_diag, preferred_element_type=F32)
    attn_ref[0] = attn[0:t_new].astype(BF16)

    tail_lane = lax.broadcasted_iota(jnp.int32, (GROUP_WIDTH, LANE), 1)
    nt_ref[...] = jnp.zeros(nt_ref.shape, F32)
    for g in range(N_GROUPS):
        c_ref, o_ref = caches[g], outs[g]
        L = c_ref.shape[3]
        for kv in range(2):
            col0 = (2 * g + kv) * GROUP_WIDTH
            nt_ref[LANE - t_new:LANE, :] = kvn_ref[0, :, col0:col0 + GROUP_WIDTH]
            new_t = nt_ref[...].T
            rolled = pltpu.roll(c_ref[0, kv], L - t_new, 1)
            if L > LANE:
                o_ref[0, kv, :, 0:L - LANE] = rolled[:, 0:L - LANE]
            o_ref[0, kv, :, L - LANE:L] = jnp.where(tail_lane >= LANE - t_new, new_t, rolled[:, L - LANE:L])

    ue_ref[0:POOL_STATE_LEN, :] = sp_ref[0]
    ue_ref[POOL_STATE_LEN:POOL_STATE_LEN + t_new, :] = u_ref[0]
    diff_ref[...] = jnp.zeros(diff_ref.shape, F32)
    for tn in range(t_new):
        r = POOL_STATE_LEN + tn
        for gi, w in enumerate(POOL_WINDOWS):
            lanes = slice(gi * POOL_GROUP_WIDTH, (gi + 1) * POOL_GROUP_WIDTH)
            win = jnp.sum(ue_ref[r - w + 1:r + 1, lanes], axis=0, keepdims=True)
            diff_ref[tn:tn + 1, lanes] = win / float(w) - ue_ref[r:r + 1, lanes]
    for gi in range(len(POOL_WINDOWS)):
        lanes = slice(gi * POOL_GROUP_WIDTH, (gi + 1) * POOL_GROUP_WIDTH)
        z = jnp.dot(diff_ref[:, lanes].astype(BF16), wpool_ref[gi], preferred_element_type=F32)
        pooled_ref[0, :, lanes] = (z[0:t_new] * pscale_ref[:, lanes]).astype(BF16)
    po_ref[0] = ue_ref[t_new:t_new + POOL_STATE_LEN, :]


def _sample_step(qbd, kvn, caches, state_pool, u, w_pool, pool_scale):
    N, t_new, _ = u.shape
    rows = HEADS_PER_GROUP * t_new
    per_n = lambda shape: pl.BlockSpec((1,) + tuple(shape), lambda n: (n,) + (0,) * len(shape))
    cache_specs = [per_n(c.shape[1:]) for c in caches]
    return pl.pallas_call(
        functools.partial(_sample_kernel, t_new=t_new),
        grid=(N,),
        in_specs=[per_n(qbd.shape[1:]), per_n(kvn.shape[1:])] + cache_specs
                 + [per_n(state_pool.shape[1:]), per_n(u.shape[1:]), _full(w_pool.shape), _full((1, POOL_WIDTH))],
        out_specs=[per_n((t_new, GROUP_WIDTH)), per_n((t_new, POOL_WIDTH))] + cache_specs
                  + [per_n(state_pool.shape[1:])],
        out_shape=[jax.ShapeDtypeStruct((N, t_new, GROUP_WIDTH), BF16),
                   jax.ShapeDtypeStruct((N, t_new, POOL_WIDTH), BF16)]
                  + [jax.ShapeDtypeStruct(c.shape, F32) for c in caches]
                  + [jax.ShapeDtypeStruct(state_pool.shape, F32)],
        scratch_shapes=[pltpu.VMEM((POOL_STATE_LEN + t_new + 5, POOL_WIDTH), F32),
                        pltpu.VMEM((rows, POOL_WIDTH), F32),
                        pltpu.VMEM((LANE, GROUP_WIDTH), F32)],
        compiler_params=_cparams(1),
        name="sample_step",
    )(qbd, kvn, *caches, state_pool, u, w_pool, pool_scale)


def _route(logits):
    lane = lax.broadcasted_iota(jnp.int32, logits.shape, 1).astype(F32)
    is_grp = (lane >= GROUP_LANE0) & (lane < GROUP_LANE0 + MOE_GROUPS)
    gl = jnp.where(is_grp, logits, NEG)
    gmax = jnp.max(gl, axis=-1, keepdims=True)
    gidx = jnp.min(jnp.where(gl == gmax, lane, BIG_LANE), axis=-1, keepdims=True) - GROUP_LANE0
    gsum = jnp.sum(jnp.where(is_grp, jnp.exp(gl - gmax), 0.0), axis=-1, keepdims=True)
    grp_w = 1.0 / gsum
    lo = gidx * EXPERTS_PER_GROUP
    in_grp = (lane >= lo) & (lane < lo + EXPERTS_PER_GROUP)
    el = jnp.where(in_grp, logits, NEG)
    v1 = jnp.max(el, axis=-1, keepdims=True)
    i1 = jnp.min(jnp.where(el == v1, lane, BIG_LANE), axis=-1, keepdims=True)
    el2 = jnp.where(lane == i1, NEG, el)
    v2 = jnp.max(el2, axis=-1, keepdims=True)
    i2 = jnp.min(jnp.where(el2 == v2, lane, BIG_LANE), axis=-1, keepdims=True)
    e = jnp.exp(v2 - v1)
    w1 = grp_w / (1.0 + e)
    w2 = grp_w * e / (1.0 + e)
    return lane, i1, i2, w1, w2


def _pack_bf16_pairs(x):
    half = x.shape[1] // 2
    lo = lax.bitcast_convert_type(x[:, :half].astype(BF16).astype(F32), jnp.uint32)
    hi = lax.bitcast_convert_type(x[:, half:].astype(BF16).astype(F32), jnp.uint32)
    return (lo >> 16) | hi


def _unpack_bf16_pairs(w):
    lo = lax.bitcast_convert_type(w << 16, F32).astype(BF16)
    hi = lax.bitcast_convert_type(w & jnp.uint32(0xFFFF0000), F32).astype(BF16)
    return lo, hi


def _merge_kernel(*refs, n_groups, routed):
    x_ref, mod_ref, g1_ref, g2_ref = refs[0:4]
    n_attn = 2 * n_groups if n_groups > 1 else 1
    attn_refs = refs[4:4 + n_attn]
    rest = refs[4 + n_attn:]
    pooled_ref, wg_ref, wa_ref, wp_ref, wo_ref, wrh_ref, wrl_ref, br_ref = rest[:8]
    n_out = 4 if routed else 3
    out_refs = rest[8:8 + n_out]
    x1_ref, h2_ref, route_ref = out_refs[:3]
    scratch_refs = rest[8 + n_out:]
    stage_refs = scratch_refs[:n_attn] if n_groups > 1 else ()

    x = x_ref[0]
    tm = x.shape[0]
    if n_groups > 1:
        vals = []
        for idx, ref in enumerate(attn_refs):
            d = ATTN_DILATIONS[idx % n_groups]
            if d == 1:
                vals.append(ref[0].astype(F32))
                continue
            stage_ref = stage_refs[idx]
            chunks = GROUP_WIDTH // LANE
            for r in range(d):
                for c in range(chunks):
                    stage_ref[c, pl.ds(r, tm // d, stride=d), :] = (
                        ref[0, :, r * GROUP_WIDTH + c * LANE:r * GROUP_WIDTH + (c + 1) * LANE].astype(F32))
            vals.append(jnp.concatenate([stage_ref[c] for c in range(chunks)], axis=1))
        os_, ls = vals[:n_groups], vals[n_groups:]
        lmax = functools.reduce(jnp.maximum, ls)
        es = [jnp.exp(l - lmax) for l in ls]
        attn = sum(e * o for e, o in zip(es, os_)) / sum(es)
    else:
        attn = attn_refs[0][0]
    a = jnp.dot(attn.astype(BF16), wa_ref[...], preferred_element_type=F32)
    p = jnp.dot(pooled_ref[0], wp_ref[...], preferred_element_type=F32)
    h = _norm_mod(x, g1_ref[...], mod_ref[0, 1], mod_ref[0, 0]).astype(BF16)
    gates = jnp.dot(h, wg_ref[...], preferred_element_type=F32)
    D = x.shape[1]
    merged = jax.nn.sigmoid(gates[:, :D]) * a + jax.nn.sigmoid(gates[:, D:]) * p
    y = jnp.dot(merged.astype(BF16), wo_ref[...], preferred_element_type=F32)
    x1 = x + mod_ref[0, 2] * y
    x1_ref[0] = x1
    h2 = _norm_mod(x1, g2_ref[...], mod_ref[0, 4], mod_ref[0, 3])
    h2_hi = h2.astype(BF16)
    h2_lo = (h2 - h2_hi.astype(F32)).astype(BF16)
    logits = (jnp.dot(h2_hi, wrh_ref[...], preferred_element_type=F32)
              + jnp.dot(h2_lo, wrh_ref[...], preferred_element_type=F32)
              + jnp.dot(h2_hi, wrl_ref[...], preferred_element_type=F32)) + br_ref[...]
    lane, i1, i2, w1, w2 = _route(logits)
    if not routed:
        h2_ref[0] = h2_hi
        route_ref[0] = jnp.where(lane == i1, w1, jnp.where(lane == i2, w2, 0.0))
        return

    h2_ref[0] = _pack_bf16_pairs(h2)
    counts_ref, carry_ref = out_refs[3], scratch_refs[-1]

    @pl.when((pl.program_id(0) == 0) & (pl.program_id(1) == 0))
    def _():
        carry_ref[...] = jnp.zeros(carry_ref.shape, F32)

    hit = ((lane == i1) | (lane == i2)).astype(BF16)
    ltri = (lax.broadcasted_iota(jnp.int32, (tm, tm), 0) >= lax.broadcasted_iota(jnp.int32, (tm, tm), 1))
    prefix = jnp.dot(ltri.astype(BF16), hit, preferred_element_type=F32) + carry_ref[...]
    rank1 = jnp.sum(jnp.where(lane == i1, prefix, 0.0), axis=-1, keepdims=True) - 1.0
    rank2 = jnp.sum(jnp.where(lane == i2, prefix, 0.0), axis=-1, keepdims=True) - 1.0
    carry_ref[...] = prefix[tm - 1:tm, :]
    counts_ref[...] = jnp.broadcast_to(prefix[tm - 1:tm, :], counts_ref.shape)
    cols = (i1, i2, w1, w2, rank1, rank2)
    route = jnp.zeros(logits.shape, F32)
    for c, col in enumerate(cols):
        route = jnp.where(lane == float(c), col, route)
    route_ref[0] = route


def _merge(x, mod, g1, g2, attn_inputs, pooled, w_gates, w_attn_out, w_pool_out, w_o, wr_hi, wr_lo, b_r, tm,
           routed):
    B, S, D = x.shape
    R = mod.shape[2]
    n_groups = len(attn_inputs) // 2 if len(attn_inputs) > 1 else 1
    tok = lambda width: pl.BlockSpec((1, tm, width), lambda b, i: (b, i, 0))
    if R == 1:
        mod_spec = pl.BlockSpec((1, N_COND, 1, D), lambda b, i: (b, 0, 0, 0))
    else:
        mod_spec = pl.BlockSpec((1, N_COND, tm, D), lambda b, i: (b, 0, i, 0))
    weights = (w_gates, w_attn_out, w_pool_out, w_o, wr_hi, wr_lo, b_r)
    if n_groups > 1:
        attn_specs = [pl.BlockSpec((1, tm // (a.shape[2] // GROUP_WIDTH), a.shape[2]), lambda b, i: (b, i, 0))
                      for a in attn_inputs]
        scratch = [pltpu.VMEM((GROUP_WIDTH // LANE, tm, LANE), F32)] * len(attn_inputs)
    else:
        attn_specs = [tok(GROUP_WIDTH)]
        scratch = []
    out_specs = [tok(D), tok(D // 2 if routed else D), tok(ROUTER_LANES)]
    out_shape = [jax.ShapeDtypeStruct((B, S, D), F32),
                 jax.ShapeDtypeStruct((B, S, D // 2), jnp.uint32) if routed else jax.ShapeDtypeStruct((B, S, D), BF16),
                 jax.ShapeDtypeStruct((B, S, ROUTER_LANES), F32)]
    if routed:
        out_specs.append(_full((SUBLANE, ROUTER_LANES)))
        out_shape.append(jax.ShapeDtypeStruct((SUBLANE, ROUTER_LANES), F32))
        scratch = scratch + [pltpu.VMEM((1, ROUTER_LANES), F32)]
    return pl.pallas_call(
        functools.partial(_merge_kernel, n_groups=n_groups, routed=routed),
        grid=(B, S // tm),
        in_specs=[tok(D), mod_spec, _full((1, D)), _full((1, D))]
                 + attn_specs + [tok(POOL_WIDTH)]
                 + [_full(w.shape) for w in weights],
        out_specs=out_specs,
        out_shape=out_shape,
        scratch_shapes=scratch,
        compiler_params=_cparams(2),
        name="merge",
    )(x, mod, g1, g2, *attn_inputs, pooled, *weights)


def _moe_kernel(h2_ref, comb_ref, x1_ref, mod_ref, gf_ref, wgu_ref, wd_ref, y_ref, acc_ref, *, d_expert):
    e = pl.program_id(2)

    @pl.when(e == 0)
    def _():
        acc_ref[...] = jnp.zeros(acc_ref.shape, F32)

    gu = jnp.dot(h2_ref[0], wgu_ref[0], preferred_element_type=F32)
    comb = comb_ref[0]
    lane = lax.broadcasted_iota(jnp.int32, comb.shape, 1)
    c_e = jnp.sum(jnp.where(lane == e, comb, 0.0), axis=-1, keepdims=True)
    act = _silu(gu[:, :d_expert]) * gu[:, d_expert:] * c_e
    acc_ref[...] += jnp.dot(act.astype(BF16), wd_ref[0], preferred_element_type=F32)

    @pl.when(e == pl.num_programs(2) - 1)
    def _():
        x2 = x1_ref[0] + mod_ref[0, 5] * acc_ref[...]
        var = jnp.mean(x2 * x2, axis=-1, keepdims=True)
        y_ref[0] = x2 * lax.rsqrt(var + EPS) * gf_ref[...]


def _moe(h2, comb, x1, mod, gf, w_gate_up, w_down, tm):
    B, S, D = x1.shape
    R = mod.shape[2]
    n_exp, _, two_f = w_gate_up.shape
    tok = lambda width: pl.BlockSpec((1, tm, width), lambda b, i, e: (b, i, 0))
    if R == 1:
        mod_spec = pl.BlockSpec((1, N_COND, 1, D), lambda b, i, e: (b, 0, 0, 0))
    else:
        mod_spec = pl.BlockSpec((1, N_COND, tm, D), lambda b, i, e: (b, 0, i, 0))
    return pl.pallas_call(
        functools.partial(_moe_kernel, d_expert=two_f // 2),
        grid=(B, S // tm, n_exp),
        in_specs=[tok(D), tok(ROUTER_LANES), tok(D), mod_spec, _full((1, D)),
                  pl.BlockSpec((1, D, two_f), lambda b, i, e: (e, 0, 0)),
                  pl.BlockSpec((1, two_f // 2, D), lambda b, i, e: (e, 0, 0))],
        out_specs=tok(D),
        out_shape=jax.ShapeDtypeStruct((B, S, D), F32),
        scratch_shapes=[pltpu.VMEM((tm, D), F32)],
        compiler_params=_cparams(3),
        name="moe",
    )(h2, comb, x1, mod, gf, w_gate_up, w_down)


EXPERT_TILE = 256


def _routing_tables(route, counts, tile):
    T = route.shape[0]
    cnt = counts[0, :N_EXPERTS].astype(jnp.int32)
    padded = ((cnt + tile - 1) // tile) * tile
    ends = jnp.cumsum(padded)
    base = ends - padded
    e12 = route[:, 0:2].astype(jnp.int32)
    pos = jnp.take(base, e12, axis=0) + route[:, 4:6].astype(jnp.int32)
    n_tiles = 2 * T // tile + N_EXPERTS
    starts = jnp.arange(n_tiles, dtype=jnp.int32) * tile
    n_used = ends[-1] // tile
    tile_expert = jnp.sum(starts[:, None] >= ends[None, :], axis=1).astype(jnp.int32)
    last = jnp.take(tile_expert, n_used - 1)
    tile_expert = jnp.where(jnp.arange(n_tiles) < n_used, tile_expert, last)
    pads = jnp.stack([base + cnt, ends], axis=1).reshape(-1).astype(jnp.int32)
    return pos, tile_expert, n_used.reshape(1).astype(jnp.int32), pads, n_tiles


def _row_copy(src_ref, src_row, dst_ref, dst_row, sem):
    return pltpu.make_async_copy(src_ref.at[pl.ds(src_row, 1)], dst_ref.at[pl.ds(dst_row, 1)], sem)


def _dispatch_kernel(pads_ref, pos_ref, h_ref, xs_ref, zero_ref, sem, *, tm, tile):
    @pl.when(pl.program_id(0) == 0)
    def _():
        zero_ref[...] = jnp.zeros(zero_ref.shape, zero_ref.dtype)
        n_tiles = xs_ref.shape[0] // tile
        first_unused = pads_ref[2 * N_EXPERTS - 1] // tile

        def tile_copy(j):
            return pltpu.make_async_copy(zero_ref, xs_ref.at[pl.ds(pl.multiple_of(j * tile, tile), tile)], sem)

        def fill_tile(j, c):
            tile_copy(j).start()
            return c

        def drain_tile(j, c):
            tile_copy(j).wait()
            return c

        lax.fori_loop(first_unused, n_tiles, fill_tile, 0)
        lax.fori_loop(first_unused, n_tiles, drain_tile, 0)

        def per_expert(e, carry):
            lo, hi = pads_ref[2 * e], pads_ref[2 * e + 1]

            def fill(r, c):
                _row_copy(zero_ref, 0, xs_ref, r, sem).start()
                return c

            def drain(r, c):
                _row_copy(zero_ref, 0, xs_ref, r, sem).wait()
                return c

            lax.fori_loop(lo, hi, fill, 0)
            lax.fori_loop(lo, hi, drain, 0)
            return carry

        lax.fori_loop(0, N_EXPERTS, per_expert, 0)

    def issue(i, carry):
        _row_copy(h_ref, i, xs_ref, pos_ref[0, 0, 2 * i], sem).start()
        _row_copy(h_ref, i, xs_ref, pos_ref[0, 0, 2 * i + 1], sem).start()
        return carry

    lax.fori_loop(0, tm, issue, 0, unroll=8)
    for _ in range(2):
        pltpu.make_async_copy(h_ref, xs_ref.at[pl.ds(0, tm)], sem).wait()


def _dispatch(h2p, pos, pads, n_rows, tm, tile):
    T, width = h2p.shape
    grid_spec = pltpu.PrefetchScalarGridSpec(
        num_scalar_prefetch=1,
        grid=(T // tm,),
        in_specs=[pl.BlockSpec((1, 1, 2 * tm), lambda i, pads: (i, 0, 0), memory_space=pltpu.SMEM),
                  pl.BlockSpec((tm, width), lambda i, pads: (i, 0))],
        out_specs=pl.BlockSpec(memory_space=pl.ANY),
        scratch_shapes=[pltpu.VMEM((tile, width), h2p.dtype), pltpu.SemaphoreType.DMA(())],
    )
    return pl.pallas_call(
        functools.partial(_dispatch_kernel, tm=tm, tile=tile),
        grid_spec=grid_spec,
        out_shape=jax.ShapeDtypeStruct((n_rows, width), h2p.dtype),
        compiler_params=_cparams(1),
        name="moe_dispatch",
    )(pads, pos.reshape(T // tm, 1, 2 * tm), h2p)


def _experts_kernel(te_ref, nu_ref, xs_ref, wgu_ref, wd_ref, ys_ref, *, d_expert):
    @pl.when(pl.program_id(0) >= nu_ref[0])
    def _():
        ys_ref[...] = jnp.zeros(ys_ref.shape, F32)

    @pl.when(pl.program_id(0) < nu_ref[0])
    def _():
        lo, hi = _unpack_bf16_pairs(xs_ref[...])
        half = lo.shape[1]
        gu = (jnp.dot(lo, wgu_ref[0, :half, :], preferred_element_type=F32)
              + jnp.dot(hi, wgu_ref[0, half:, :], preferred_element_type=F32))
        act = _silu(gu[:, :d_expert]) * gu[:, d_expert:]
        ys_ref[...] = jnp.dot(act.astype(BF16), wd_ref[0], preferred_element_type=F32)


def _experts(xs, tile_expert, n_used, w_gate_up, w_down, tile):
    n_rows, width = xs.shape
    _, D, two_f = w_gate_up.shape
    grid_spec = pltpu.PrefetchScalarGridSpec(
        num_scalar_prefetch=2,
        grid=(n_rows // tile,),
        in_specs=[pl.BlockSpec((tile, width), lambda j, te, nu: (jnp.minimum(j, nu[0] - 1), 0)),
                  pl.BlockSpec((1, D, two_f), lambda j, te, nu: (te[j], 0, 0)),
                  pl.BlockSpec((1, two_f // 2, D), lambda j, te, nu: (te[j], 0, 0))],
        out_specs=pl.BlockSpec((tile, D), lambda j, te, nu: (j, 0)),
    )
    return pl.pallas_call(
        functools.partial(_experts_kernel, d_expert=two_f // 2),
        grid_spec=grid_spec,
        out_shape=jax.ShapeDtypeStruct((n_rows, D), F32),
        compiler_params=_cparams(1),
        name="moe_experts",
    )(tile_expert, n_used, xs, w_gate_up, w_down)


def _combine_kernel(pos_ref, ys_ref, route_ref, x1_ref, mod_ref, gf_ref, y_ref, ya_ref, yb_ref, sem, *, tm):
    def issue(i, carry):
        _row_copy(ys_ref, pos_ref[0, 0, 2 * i], ya_ref, i, sem).start()
        _row_copy(ys_ref, pos_ref[0, 0, 2 * i + 1], yb_ref, i, sem).start()
        return carry

    lax.fori_loop(0, tm, issue, 0, unroll=8)
    pltpu.make_async_copy(ys_ref.at[pl.ds(0, tm)], ya_ref, sem).wait()
    pltpu.make_async_copy(ys_ref.at[pl.ds(0, tm)], yb_ref, sem).wait()
    route = route_ref[0]
    lane = lax.broadcasted_iota(jnp.int32, route.shape, 1)
    w1 = jnp.sum(jnp.where(lane == 2, route, 0.0), axis=-1, keepdims=True)
    w2 = jnp.sum(jnp.where(lane == 3, route, 0.0), axis=-1, keepdims=True)
    x2 = x1_ref[0] + mod_ref[0, 5] * (w1 * ya_ref[...] + w2 * yb_ref[...])
    var = jnp.mean(x2 * x2, axis=-1, keepdims=True)
    y_ref[0] = x2 * lax.rsqrt(var + EPS) * gf_ref[...]


def _combine(ys, pos, route, x1, mod, gf, tm):
    B, S, D = x1.shape
    n_t = S // tm
    tok = lambda width: pl.BlockSpec((1, tm, width), lambda b, i: (b, i, 0))
    return pl.pallas_call(
        functools.partial(_combine_kernel, tm=tm),
        grid=(B, n_t),
        in_specs=[pl.BlockSpec((1, 1, 2 * tm), lambda b, i: (b * n_t + i, 0, 0), memory_space=pltpu.SMEM),
                  pl.BlockSpec(memory_space=pl.ANY),
                  tok(ROUTER_LANES), tok(D),
                  pl.BlockSpec((1, N_COND, 1, D), lambda b, i: (b, 0, 0, 0)),
                  _full((1, D))],
        out_specs=tok(D),
        out_shape=jax.ShapeDtypeStruct((B, S, D), F32),
        scratch_shapes=[pltpu.VMEM((tm, D), F32), pltpu.VMEM((tm, D), F32), pltpu.SemaphoreType.DMA(())],
        compiler_params=_cparams(2),
        name="moe_combine",
    )(pos.reshape(B * n_t, 1, 2 * tm), ys, route, x1, mod, gf)


def _rope_tables(pos):
    half = HEAD_DIM // 2
    inv = ROPE_THETA ** (-jnp.arange(half, dtype=F32) * 2.0 / HEAD_DIM)
    ang = pos.astype(F32)[:, None] * inv[None, :]
    cos, sin = jnp.cos(ang), jnp.sin(ang)
    reps = LANE // HEAD_DIM
    return (jnp.tile(jnp.concatenate([cos, cos], axis=-1), (1, reps)),
            jnp.tile(jnp.concatenate([-sin, sin], axis=-1), (1, reps)))


def kernel(x_prompt, x_sample, cache_kv_w128, cache_kv_w512, cache_kv_w2048, state_pool, c_prompt, c_sample, norm1_g, w_ada, b_ada, w_in, w_attn_out, w_pool, pool_scale, w_pool_out, w_o, norm2_g, w_grp, b_grp, w_exp_router, b_exp_router, w_gate_up, w_down, final_norm_g):
    B, S, D = x_prompt.shape
    N, T, _ = x_sample.shape
    depth = norm1_g.shape[0]
    assert depth == 1, "single trunk layer"
    tm = min(512, S)
    assert S % tm == 0 and all(S % (SPAN * d) == 0 for d in ATTN_DILATIONS)

    n_qkvu = 3 * ATTN_WIDTH + POOL_WIDTH
    w_qkvu = w_in[0, :, :n_qkvu].astype(BF16)
    w_gates = w_in[0, :, n_qkvu:].astype(BF16)
    wa, wpo, wo = w_attn_out[0].astype(BF16), w_pool_out[0].astype(BF16), w_o[0].astype(BF16)
    wpool = w_pool[0].astype(BF16)
    pscale = pool_scale[0].reshape(1, POOL_WIDTH)
    w_r = jnp.concatenate([w_exp_router[0], w_grp[0]], axis=1)
    w_r = jnp.pad(w_r, ((0, 0), (0, ROUTER_LANES - w_r.shape[1])))
    wr_hi = w_r.astype(BF16)
    wr_lo = (w_r - wr_hi.astype(F32)).astype(BF16)
    b_r = jnp.pad(jnp.concatenate([b_exp_router[0], b_grp[0]]), (0, ROUTER_LANES - N_EXPERTS - MOE_GROUPS))
    b_r = b_r.reshape(1, ROUTER_LANES)
    wgu, wd = w_gate_up[0].astype(BF16), w_down[0].astype(BF16)
    g1, g2, gf = norm1_g[0].reshape(1, D), norm2_g[0].reshape(1, D), final_norm_g.reshape(1, D)

    mod = _ada(jnp.concatenate([c_prompt, c_sample], axis=0), w_ada[0], b_ada[0])
    mod_p = mod[:B].reshape(B, N_COND, 1, D)
    mod_s = jnp.repeat(mod[B:].reshape(N, N_COND, D), T, axis=0)
    mod_s = jnp.transpose(mod_s, (1, 0, 2)).reshape(1, N_COND, N * T, D)

    cos_p, sin_p = _rope_tables(jnp.arange(S, dtype=jnp.int32))
    outs = _in_proj_prompt(x_prompt, mod_p, g1, w_qkvu, cos_p, sin_p, wpool, pscale, tm)
    qs, ks, vs = outs[0:3], outs[3:6], outs[6:9]
    pooled, kv0, kv1, kv2, ptail = outs[9:14]
    attn_parts = [_attn_prompt(qs[g], ks[g], vs[g], g, nq=4) for g in range(N_GROUPS)]
    attn_inputs = [o for o, _ in attn_parts] + [l for _, l in attn_parts]
    x1, h2p, route, counts = _merge(x_prompt, mod_p, g1, g2, attn_inputs, pooled, w_gates, wa, wpo, wo,
                                    wr_hi, wr_lo, b_r, tm, routed=True)
    pos, tile_expert, n_used, pads, n_tiles = _routing_tables(route.reshape(B * S, ROUTER_LANES), counts,
                                                              EXPERT_TILE)
    xs = _dispatch(h2p.reshape(B * S, D // 2), pos, pads, n_tiles * EXPERT_TILE, tm, EXPERT_TILE)
    ys = _experts(xs, tile_expert, n_used, wgu, wd, EXPERT_TILE)
    y_prompt = _combine(ys, pos, route, x1, mod_p, gf, tm=min(256, S))

    def kv_shape(a):
        a = a.reshape(a.shape[0], 2, HEADS_PER_GROUP, HEAD_DIM, a.shape[3])
        return jnp.transpose(a, (0, 4, 1, 2, 3))[None]

    pool_prompt = ptail[:, POOL_HALO - POOL_STATE_LEN:, :][None]

    TS = N * T
    pos_s = PAST_LEN + jnp.arange(T, dtype=jnp.int32)
    cos_s, sin_s = _rope_tables(pos_s)
    cos_s, sin_s = jnp.tile(cos_s, (N, 1)), jnp.tile(sin_s, (N, 1))
    xs = x_sample.reshape(1, TS, D)
    q_s, kvn, u_s = _in_proj_sample(xs, mod_s, g1, w_qkvu, cos_s, sin_s)
    eye = jnp.eye(HEADS_PER_GROUP, dtype=BF16)
    qbd = jnp.einsum('ntghe,hk->ngthke', q_s.reshape(N, T, N_GROUPS, HEADS_PER_GROUP, HEAD_DIM), eye)
    qbd = qbd.reshape(N, N_GROUPS, T * HEADS_PER_GROUP, GROUP_WIDTH)
    caches = [jnp.transpose(c[0], (0, 2, 3, 4, 1)).reshape(N, 2, GROUP_WIDTH, c.shape[2])
              for c in (cache_kv_w128, cache_kv_w512, cache_kv_w2048)]
    attn_s, pooled_s, ko0, ko1, ko2, pool_s = _sample_step(
        qbd, kvn.reshape(N, T, 2 * ATTN_WIDTH), caches, state_pool[0], u_s.reshape(N, T, POOL_WIDTH), wpool, pscale)
    x1s, h2s, comb_s = _merge(xs, mod_s, g1, g2, [attn_s.reshape(1, TS, GROUP_WIDTH)],
                              pooled_s.reshape(1, TS, POOL_WIDTH), w_gates, wa, wpo, wo, wr_hi, wr_lo, b_r, TS,
                              routed=False)
    y_sample = _moe(h2s, comb_s, x1s, mod_s, gf, wgu, wd, tm=TS).reshape(N, T, D)

    return (y_prompt, y_sample, kv_shape(kv0), kv_shape(kv1), kv_shape(kv2), pool_prompt,
            kv_shape(ko0), kv_shape(ko1), kv_shape(ko2), pool_s[None])
```

```python
import functools

import jax
import jax.numpy as jnp
from jax import lax
from jax.experimental import pallas as pl
from jax.experimental.pallas import tpu as pltpu

F32 = jnp.float32
BF16 = jnp.bfloat16

HEAD_DIM = 64
HEADS_PER_GROUP = 4
HEAD_SHIFT = 2
LANE_HEAD_SHIFT = 6
GROUP_WIDTH = HEADS_PER_GROUP * HEAD_DIM
ATTN_WINDOWS = (128, 512, 2048)
ATTN_DILATIONS = (1, 4, 16)
N_GROUPS = 3
SPAN = 128
ATTN_WIDTH = N_GROUPS * GROUP_WIDTH
ROPE_THETA = 10000.0
PAST_LEN = 8192
POOL_WINDOWS = (2, 4, 8, 16)
POOL_GROUP_WIDTH = 128
POOL_WIDTH = 512
POOL_STATE_LEN = 15
POOL_HALO = 16
MOE_GROUPS = 4
EXPERTS_PER_GROUP = 8
N_EXPERTS = 32
N_COND = 6
EPS = 1e-6

LANE = 128
SUBLANE = 8
VMEM_LIMIT_BYTES = 56 * 1024 * 1024

NEG = -1e30
BIG_LANE = 1e9

ROUTER_LANES = LANE
GROUP_LANE0 = N_EXPERTS


def _cparams(n_axes):
    return pltpu.CompilerParams(dimension_semantics=("arbitrary",) * n_axes,
                                vmem_limit_bytes=VMEM_LIMIT_BYTES)


def _full(shape):
    nd = len(shape)
    return pl.BlockSpec(tuple(shape), lambda *_: (0,) * nd)


def _norm_mod(x, g, scale, shift):
    var = jnp.mean(x * x, axis=-1, keepdims=True)
    return (x * lax.rsqrt(var + EPS) * g) * (1.0 + scale) + shift


def _rope(x, cos, sin):
    lane = lax.broadcasted_iota(jnp.int32, (x.shape[0], LANE), 1)
    first_half = (lane & (HEAD_DIM - 1)) < (HEAD_DIM // 2)
    outs = []
    for c in range(x.shape[1] // LANE):
        xc = x[:, c * LANE:(c + 1) * LANE]
        partner = jnp.where(first_half, pltpu.roll(xc, LANE - HEAD_DIM // 2, 1),
                            pltpu.roll(xc, HEAD_DIM // 2, 1))
        outs.append(xc * cos + partner * sin)
    return jnp.concatenate(outs, axis=1)


def _silu(x):
    return x * jax.nn.sigmoid(x)


def _ada_kernel(c_ref, w_ref, b_ref, o_ref):
    s = _silu(c_ref[...]).astype(BF16)
    o_ref[...] = jnp.dot(s, w_ref[...].astype(BF16), preferred_element_type=F32) + b_ref[...]


def _ada(c_all, w_ada, b_ada):
    rows, d = c_all.shape
    n_out = w_ada.shape[1]
    tn = 1024
    return pl.pallas_call(
        _ada_kernel,
        grid=(n_out // tn,),
        in_specs=[_full((rows, d)),
                  pl.BlockSpec((d, tn), lambda j: (0, j)),
                  pl.BlockSpec((1, tn), lambda j: (0, j))],
        out_specs=pl.BlockSpec((rows, tn), lambda j: (0, j)),
        out_shape=jax.ShapeDtypeStruct((rows, n_out), F32),
        compiler_params=_cparams(1),
        name="ada",
    )(c_all, w_ada, b_ada.reshape(1, n_out))


def _in_proj_prompt_kernel(x_ref, mod_ref, g1_ref, w_ref, cos_ref, sin_ref, wpool_ref, pscale_ref, *rest,
                           tm, n_tiles, tails):
    qkv_refs = (rest[0:3], rest[3:6], rest[6:9])
    pooled_ref, kv0_ref, kv1_ref, kv2_ref, ptail_ref = rest[9:14]
    stage_refs = rest[14:17]
    ue_ref = rest[17]
    i = pl.program_id(1)
    x = x_ref[0]
    h = _norm_mod(x, g1_ref[...], mod_ref[0, 1], mod_ref[0, 0]).astype(BF16)
    y = jnp.dot(h, w_ref[...], preferred_element_type=F32)
    cos = cos_ref[...]
    sin = sin_ref[...]
    q = _rope(y[:, :ATTN_WIDTH], cos, sin) * (HEAD_DIM ** -0.5)
    k = _rope(y[:, ATTN_WIDTH:2 * ATTN_WIDTH], cos, sin)
    v = y[:, 2 * ATTN_WIDTH:3 * ATTN_WIDTH]
    u = y[:, 3 * ATTN_WIDTH:3 * ATTN_WIDTH + POOL_WIDTH]

    chunks = GROUP_WIDTH // LANE
    for val, out_refs, stage_ref in zip((q, k, v), qkv_refs, stage_refs):
        for g, d in enumerate(ATTN_DILATIONS):
            gl = slice(g * GROUP_WIDTH, (g + 1) * GROUP_WIDTH)
            if d == 1:
                out_refs[g][0] = val[:, gl].astype(BF16)
                continue
            for c in range(chunks):
                stage_ref[g * chunks + c] = val[:, g * GROUP_WIDTH + c * LANE:g * GROUP_WIDTH + (c + 1) * LANE]
            for r in range(d):
                for c in range(chunks):
                    out_refs[g][0, :, r * GROUP_WIDTH + c * LANE:r * GROUP_WIDTH + (c + 1) * LANE] = (
                        stage_ref[g * chunks + c, pl.ds(r, tm // d, stride=d), :].astype(BF16))

    for g, kv_ref in enumerate((kv0_ref, kv1_ref, kv2_ref)):
        first_tile, rows = tails[g]
        lo = tm - rows

        @pl.when(i >= first_tile)
        def _(kv_ref=kv_ref, g=g, lo=lo):
            kv_ref[0, 0] = k[lo:, g * GROUP_WIDTH:(g + 1) * GROUP_WIDTH].T
            kv_ref[0, 1] = v[lo:, g * GROUP_WIDTH:(g + 1) * GROUP_WIDTH].T

    @pl.when(i == 0)
    def _():
        ue_ref[0:POOL_HALO, :] = jnp.zeros((POOL_HALO, POOL_WIDTH), F32)

    ue_ref[POOL_HALO:, :] = u
    pos1 = (i * tm + 1 + lax.broadcasted_iota(jnp.int32, (tm, 1), 0)).astype(F32)
    for gi, w in enumerate(POOL_WINDOWS):
        lanes = slice(gi * POOL_GROUP_WIDTH, (gi + 1) * POOL_GROUP_WIDTH)
        acc = ue_ref[pl.ds(POOL_HALO, tm), lanes]
        for j in range(1, w):
            acc = acc + ue_ref[pl.ds(POOL_HALO - j, tm), lanes]
        mean = acc / jnp.minimum(float(w), pos1)
        z = jnp.dot((mean - u[:, lanes]).astype(BF16), wpool_ref[gi], preferred_element_type=F32)
        pooled_ref[0, :, lanes] = (z * pscale_ref[:, lanes]).astype(BF16)
    ue_ref[0:POOL_HALO, :] = u[tm - POOL_HALO:, :]

    @pl.when(i == n_tiles - 1)
    def _():
        ptail_ref[0] = u[tm - POOL_HALO:, :]


def _in_proj_prompt(x, mod, g1, w_qkvu, cos, sin, w_pool, pool_scale, tm):
    B, S, D = x.shape
    n_tiles = S // tm
    n_w = w_qkvu.shape[1]
    tails, kv_shapes, kv_specs = [], [], []
    for W in ATTN_WINDOWS:
        Wg = min(W, S)
        if Wg >= tm:
            first = n_tiles - Wg // tm
            rows = tm
        else:
            first = n_tiles - 1
            rows = Wg
        tails.append((first, rows))
        kv_shapes.append(jax.ShapeDtypeStruct((B, 2, GROUP_WIDTH, Wg), F32))
        kv_specs.append(pl.BlockSpec((1, 2, GROUP_WIDTH, rows),
                                     lambda b, i, first=first: (b, 0, 0, jnp.maximum(i - first, 0))))
    tok = lambda width: pl.BlockSpec((1, tm, width), lambda b, i: (b, i, 0))
    dil_specs = [pl.BlockSpec((1, tm // d, d * GROUP_WIDTH), lambda b, i: (b, i, 0)) for d in ATTN_DILATIONS]
    dil_shapes = [jax.ShapeDtypeStruct((B, S // d, d * GROUP_WIDTH), BF16) for d in ATTN_DILATIONS]
    outs = pl.pallas_call(
        functools.partial(_in_proj_prompt_kernel, tm=tm, n_tiles=n_tiles, tails=tuple(tails)),
        grid=(B, n_tiles),
        in_specs=[tok(D),
                  pl.BlockSpec((1, N_COND, 1, D), lambda b, i: (b, 0, 0, 0)),
                  _full((1, D)),
                  _full((D, n_w)),
                  pl.BlockSpec((tm, LANE), lambda b, i: (i, 0)),
                  pl.BlockSpec((tm, LANE), lambda b, i: (i, 0)),
                  _full(w_pool.shape),
                  _full((1, POOL_WIDTH))],
        out_specs=dil_specs * 3 + [tok(POOL_WIDTH)] + kv_specs
                  + [pl.BlockSpec((1, POOL_HALO, POOL_WIDTH), lambda b, i: (b, 0, 0))],
        out_shape=dil_shapes * 3
                  + [jax.ShapeDtypeStruct((B, S, POOL_WIDTH), BF16)] + kv_shapes
                  + [jax.ShapeDtypeStruct((B, POOL_HALO, POOL_WIDTH), F32)],
        scratch_shapes=[pltpu.VMEM((ATTN_WIDTH // LANE, tm, LANE), F32)] * 3
                       + [pltpu.VMEM((tm + POOL_HALO, POOL_WIDTH), F32)],
        compiler_params=_cparams(2),
        name="in_proj_prompt",
    )(x, mod, g1, w_qkvu, cos, sin, w_pool, pool_scale)
    return outs


def _in_proj_sample_kernel(x_ref, mod_ref, g1_ref, w_ref, cos_ref, sin_ref, q_ref, kvn_ref, u_ref):
    h = _norm_mod(x_ref[0], g1_ref[...], mod_ref[0, 1], mod_ref[0, 0]).astype(BF16)
    y = jnp.dot(h, w_ref[...], preferred_element_type=F32)
    cos = cos_ref[...]
    sin = sin_ref[...]
    q = _rope(y[:, :ATTN_WIDTH], cos, sin) * (HEAD_DIM ** -0.5)
    k = _rope(y[:, ATTN_WIDTH:2 * ATTN_WIDTH], cos, sin)
    v = y[:, 2 * ATTN_WIDTH:3 * ATTN_WIDTH]
    q_ref[...] = q.astype(BF16)
    for g in range(N_GROUPS):
        gl = slice(g * GROUP_WIDTH, (g + 1) * GROUP_WIDTH)
        kvn_ref[:, 2 * g * GROUP_WIDTH:(2 * g + 1) * GROUP_WIDTH] = k[:, gl]
        kvn_ref[:, (2 * g + 1) * GROUP_WIDTH:(2 * g + 2) * GROUP_WIDTH] = v[:, gl]
    u_ref[...] = y[:, 3 * ATTN_WIDTH:3 * ATTN_WIDTH + POOL_WIDTH]


def _in_proj_sample(x, mod, g1, w_qkvu, cos, sin):
    _, T, D = x.shape
    n_w = w_qkvu.shape[1]
    return pl.pallas_call(
        _in_proj_sample_kernel,
        grid=(1,),
        in_specs=[_full((1, T, D)), _full((1, N_COND, T, D)), _full((1, D)), _full((D, n_w)),
                  _full((T, LANE)), _full((T, LANE))],
        out_specs=[_full((T, ATTN_WIDTH)), _full((T, 2 * ATTN_WIDTH)), _full((T, POOL_WIDTH))],
        out_shape=[jax.ShapeDtypeStruct((T, ATTN_WIDTH), BF16),
                   jax.ShapeDtypeStruct((T, 2 * ATTN_WIDTH), F32),
                   jax.ShapeDtypeStruct((T, POOL_WIDTH), F32)],
        compiler_params=_cparams(1),
        name="in_proj_sample",
    )(x, mod, g1, w_qkvu, cos, sin)


def _attn_kernel(q_ref, kc_ref, kp_ref, vc_ref, vp_ref, o_ref, lse_ref, kbuf, vbuf, *, nq):
    n0 = pl.program_id(1)
    kbuf[0:SPAN, :] = kp_ref[0]
    kbuf[SPAN:, :] = kc_ref[0]
    vbuf[0:SPAN, :] = vp_ref[0]
    vbuf[SPAN:, :] = vc_ref[0]
    qi = lax.broadcasted_iota(jnp.int32, (SPAN, 2 * SPAN), 0)
    kj = lax.broadcasted_iota(jnp.int32, (SPAN, 2 * SPAN), 1)
    band = (kj >= qi) & (kj <= qi + SPAN)
    band_first = band & (kj >= jnp.where(n0 > 0, 0, SPAN))
    lane = lax.broadcasted_iota(jnp.int32, (SPAN, LANE), 1)
    low_head = lane < HEAD_DIM
    for j in range(nq):
        valid = band_first if j == 0 else band
        rows = slice(j * SPAN, (j + 1) * SPAN)
        krows = slice(j * SPAN, (j + 2) * SPAN)
        for c in range(GROUP_WIDTH // LANE):
            cl = slice(c * LANE, (c + 1) * LANE)
            q = q_ref[0, rows, cl]
            kk = kbuf[krows, cl]
            vv = vbuf[krows, cl]
            o_pair, lse_pair = [], []
            for hh in range(2):
                mask_h = low_head if hh == 0 else jnp.logical_not(low_head)
                qm = jnp.where(mask_h, q, jnp.zeros_like(q))
                s = lax.dot_general(qm, kk, (((1,), (1,)), ((), ())), preferred_element_type=F32)
                s = jnp.where(valid, s, NEG)
                m = jnp.max(s, axis=-1, keepdims=True)
                p = jnp.exp(s - m)
                den = jnp.sum(p, axis=-1, keepdims=True)
                o_pair.append(jnp.dot(p.astype(BF16), vv, preferred_element_type=F32) / den)
                lse_pair.append(m + jnp.log(den))
            o_ref[0, rows, cl] = jnp.where(low_head, o_pair[0], o_pair[1]).astype(BF16)
            lse_ref[0, rows, cl] = jnp.where(low_head, lse_pair[0], lse_pair[1])


def _attn_prompt(qv, kv, vv, g, nq):
    B, M, width = qv.shape
    d = width // GROUP_WIDTH
    nq = min(nq, M // SPAN)
    cur = pl.BlockSpec((1, nq * SPAN, GROUP_WIDTH), lambda b, n, r: (b, n, r))
    prev = pl.BlockSpec((1, SPAN, GROUP_WIDTH), lambda b, n, r: (b, jnp.maximum(n * nq - 1, 0), r))
    out = cur
    return pl.pallas_call(
        functools.partial(_attn_kernel, nq=nq),
        grid=(B, M // (nq * SPAN), d),
        in_specs=[cur, cur, prev, cur, prev],
        out_specs=[out, out],
        out_shape=[jax.ShapeDtypeStruct((B, M, d * GROUP_WIDTH), BF16),
                   jax.ShapeDtypeStruct((B, M, d * GROUP_WIDTH), F32)],
        scratch_shapes=[pltpu.VMEM(((nq + 1) * SPAN, GROUP_WIDTH), BF16)] * 2,
        compiler_params=_cparams(3),
        name=f"attn_prompt_g{g}",
    )(qv, kv, kv, vv, vv)


def _sample_kernel(qbd_ref, kvn_ref, c0_ref, c1_ref, c2_ref, sp_ref, u_ref, wpool_ref, pscale_ref,
                   attn_ref, pooled_ref, o0_ref, o1_ref, o2_ref, po_ref, ue_ref, diff_ref, nt_ref, *, t_new):
    rows = HEADS_PER_GROUP * t_new
    caches = (c0_ref, c1_ref, c2_ref)
    outs = (o0_ref, o1_ref, o2_ref)

    s_cache, s_new, valid_new = [], [], []
    m = jnp.full((rows, 1), NEG, F32)
    for g in range(N_GROUPS):
        d = ATTN_DILATIONS[g]
        c_ref = caches[g]
        L = c_ref.shape[3]
        qb = qbd_ref[0, g]
        s = jnp.dot(qb, c_ref[0, 0].astype(BF16), preferred_element_type=F32)
        t_row = lax.broadcasted_iota(jnp.int32, (rows, L), 0) >> HEAD_SHIFT
        delta = L + t_row - lax.broadcasted_iota(jnp.int32, (rows, L), 1)
        ok = ((delta & (d - 1)) == 0) & (delta <= SPAN * d)
        s = jnp.where(ok, s, NEG)
        s_cache.append(s)
        m = jnp.maximum(m, jnp.max(s, axis=-1, keepdims=True))
        qf = qb.astype(F32)
        t_col = lax.broadcasted_iota(jnp.int32, (rows, 1), 0) >> HEAD_SHIFT
        sn, okn = [], []
        for tn in range(t_new):
            kn = kvn_ref[0, tn:tn + 1, 2 * g * GROUP_WIDTH:(2 * g + 1) * GROUP_WIDTH]
            kn = kn.astype(BF16).astype(F32)
            dn = t_col - tn
            ok_n = (dn >= 0) & ((dn & (d - 1)) == 0)
            s1 = jnp.where(ok_n, jnp.sum(qf * kn, axis=-1, keepdims=True), NEG)
            sn.append(s1)
            okn.append(ok_n)
            m = jnp.maximum(m, s1)
        s_new.append(sn)
        valid_new.append(okn)

    den = jnp.zeros((rows, 1), F32)
    acc = jnp.zeros((rows, GROUP_WIDTH), F32)
    for g in range(N_GROUPS):
        c_ref = caches[g]
        p = jnp.exp(s_cache[g] - m)
        den = den + jnp.sum(p, axis=-1, keepdims=True)
        acc = acc + lax.dot_general(p.astype(BF16), c_ref[0, 1].astype(BF16), (((1,), (1,)), ((), ())),
                                    preferred_element_type=F32)
        for tn in range(t_new):
            pn = jnp.exp(s_new[g][tn] - m)
            den = den + pn
            vn = kvn_ref[0, tn:tn + 1, (2 * g + 1) * GROUP_WIDTH:(2 * g + 2) * GROUP_WIDTH]
            acc = acc + pn * vn
    row_head = lax.broadcasted_iota(jnp.int32, (rows, GROUP_WIDTH), 0) & (HEADS_PER_GROUP - 1)
    lane_head = lax.broadcasted_iota(jnp.int32, (rows, GROUP_WIDTH), 1) >> LANE_HEAD_SHIFT
    o_diag = jnp.where(row_head == lane_head, acc / den, 0.0).astype(BF16)
    sel = ((lax.broadcasted_iota(jnp.int32, (rows, rows), 1) >> HEAD_SHIFT)
           == lax.broadcasted_iota(jnp.int32, (rows, rows), 0)).astype(BF16)
    attn = jnp.dot(sel, o_diag, preferred_element_type=F32)
    attn_ref[0] = attn[0:t_new].astype(BF16)

    tail_lane = lax.broadcasted_iota(jnp.int32, (GROUP_WIDTH, LANE), 1)
    nt_ref[...] = jnp.zeros(nt_ref.shape, F32)
    for g in range(N_GROUPS):
        c_ref, o_ref = caches[g], outs[g]
        L = c_ref.shape[3]
        for kv in range(2):
            col0 = (2 * g + kv) * GROUP_WIDTH
            nt_ref[LANE - t_new:LANE, :] = kvn_ref[0, :, col0:col0 + GROUP_WIDTH]
            new_t = nt_ref[...].T
            rolled = pltpu.roll(c_ref[0, kv], L - t_new, 1)
            if L > LANE:
                o_ref[0, kv, :, 0:L - LANE] = rolled[:, 0:L - LANE]
            o_ref[0, kv, :, L - LANE:L] = jnp.where(tail_lane >= LANE - t_new, new_t, rolled[:, L - LANE:L])

    ue_ref[0:POOL_STATE_LEN, :] = sp_ref[0]
    ue_ref[POOL_STATE_LEN:POOL_STATE_LEN + t_new, :] = u_ref[0]
    diff_ref[...] = jnp.zeros(diff_ref.shape, F32)
    for tn in range(t_new):
        r = POOL_STATE_LEN + tn
        for gi, w in enumerate(POOL_WINDOWS):
            lanes = slice(gi * POOL_GROUP_WIDTH, (gi + 1) * POOL_GROUP_WIDTH)
            win = jnp.sum(ue_ref[r - w + 1:r + 1, lanes], axis=0, keepdims=True)
            diff_ref[tn:tn + 1, lanes] = win / float(w) - ue_ref[r:r + 1, lanes]
    for gi in range(len(POOL_WINDOWS)):
        lanes = slice(gi * POOL_GROUP_WIDTH, (gi + 1) * POOL_GROUP_WIDTH)
        z = jnp.dot(diff_ref[:, lanes].astype(BF16), wpool_ref[gi], preferred_element_type=F32)
        pooled_ref[0, :, lanes] = (z[0:t_new] * pscale_ref[:, lanes]).astype(BF16)
    po_ref[0] = ue_ref[t_new:t_new + POOL_STATE_LEN, :]


def _sample_step(qbd, kvn, caches, state_pool, u, w_pool, pool_scale):
    N, t_new, _ = u.shape
    rows = HEADS_PER_GROUP * t_new
    per_n = lambda shape: pl.BlockSpec((1,) + tuple(shape), lambda n: (n,) + (0,) * len(shape))
    cache_specs = [per_n(c.shape[1:]) for c in caches]
    return pl.pallas_call(
        functools.partial(_sample_kernel, t_new=t_new),
        grid=(N,),
        in_specs=[per_n(qbd.shape[1:]), per_n(kvn.shape[1:])] + cache_specs
                 + [per_n(state_pool.shape[1:]), per_n(u.shape[1:]), _full(w_pool.shape), _full((1, POOL_WIDTH))],
        out_specs=[per_n((t_new, GROUP_WIDTH)), per_n((t_new, POOL_WIDTH))] + cache_specs
                  + [per_n(state_pool.shape[1:])],
        out_shape=[jax.ShapeDtypeStruct((N, t_new, GROUP_WIDTH), BF16),
                   jax.ShapeDtypeStruct((N, t_new, POOL_WIDTH), BF16)]
                  + [jax.ShapeDtypeStruct(c.shape, F32) for c in caches]
                  + [jax.ShapeDtypeStruct(state_pool.shape, F32)],
        scratch_shapes=[pltpu.VMEM((POOL_STATE_LEN + t_new + 5, POOL_WIDTH), F32),
                        pltpu.VMEM((rows, POOL_WIDTH), F32),
                        pltpu.VMEM((LANE, GROUP_WIDTH), F32)],
        compiler_params=_cparams(1),
        name="sample_step",
    )(qbd, kvn, *caches, state_pool, u, w_pool, pool_scale)


def _route(logits):
    lane = lax.broadcasted_iota(jnp.int32, logits.shape, 1).astype(F32)
    is_grp = (lane >= GROUP_LANE0) & (lane < GROUP_LANE0 + MOE_GROUPS)
    gl = jnp.where(is_grp, logits, NEG)
    gmax = jnp.max(gl, axis=-1, keepdims=True)
    gidx = jnp.min(jnp.where(gl == gmax, lane, BIG_LANE), axis=-1, keepdims=True) - GROUP_LANE0
    gsum = jnp.sum(jnp.where(is_grp, jnp.exp(gl - gmax), 0.0), axis=-1, keepdims=True)
    grp_w = 1.0 / gsum
    lo = gidx * EXPERTS_PER_GROUP
    in_grp = (lane >= lo) & (lane < lo + EXPERTS_PER_GROUP)
    el = jnp.where(in_grp, logits, NEG)
    v1 = jnp.max(el, axis=-1, keepdims=True)
    i1 = jnp.min(jnp.where(el == v1, lane, BIG_LANE), axis=-1, keepdims=True)
    el2 = jnp.where(lane == i1, NEG, el)
    v2 = jnp.max(el2, axis=-1, keepdims=True)
    i2 = jnp.min(jnp.where(el2 == v2, lane, BIG_LANE), axis=-1, keepdims=True)
    e = jnp.exp(v2 - v1)
    w1 = grp_w / (1.0 + e)
    w2 = grp_w * e / (1.0 + e)
    return lane, i1, i2, w1, w2


def _merge_kernel(*refs, n_groups, routed):
    x_ref, mod_ref, g1_ref, g2_ref = refs[0:4]
    n_attn = 2 * n_groups if n_groups > 1 else 1
    attn_refs = refs[4:4 + n_attn]
    rest = refs[4 + n_attn:]
    pooled_ref, wg_ref, wa_ref, wp_ref, wo_ref, wrh_ref, wrl_ref, br_ref = rest[:8]
    n_out = 4 if routed else 3
    out_refs = rest[8:8 + n_out]
    x1_ref, h2_ref, route_ref = out_refs[:3]
    scratch_refs = rest[8 + n_out:]
    stage_refs = scratch_refs[:n_attn] if n_groups > 1 else ()

    x = x_ref[0]
    tm = x.shape[0]
    if n_groups > 1:
        vals = []
        for idx, ref in enumerate(attn_refs):
            d = ATTN_DILATIONS[idx % n_groups]
            if d == 1:
                vals.append(ref[0].astype(F32))
                continue
            stage_ref = stage_refs[idx]
            chunks = GROUP_WIDTH // LANE
            for r in range(d):
                for c in range(chunks):
                    stage_ref[c, pl.ds(r, tm // d, stride=d), :] = (
                        ref[0, :, r * GROUP_WIDTH + c * LANE:r * GROUP_WIDTH + (c + 1) * LANE].astype(F32))
            vals.append(jnp.concatenate([stage_ref[c] for c in range(chunks)], axis=1))
        os_, ls = vals[:n_groups], vals[n_groups:]
        lmax = functools.reduce(jnp.maximum, ls)
        es = [jnp.exp(l - lmax) for l in ls]
        attn = sum(e * o for e, o in zip(es, os_)) / sum(es)
    else:
        attn = attn_refs[0][0]
    a = jnp.dot(attn.astype(BF16), wa_ref[...], preferred_element_type=F32)
    p = jnp.dot(pooled_ref[0], wp_ref[...], preferred_element_type=F32)
    h = _norm_mod(x, g1_ref[...], mod_ref[0, 1], mod_ref[0, 0]).astype(BF16)
    gates = jnp.dot(h, wg_ref[...], preferred_element_type=F32)
    D = x.shape[1]
    merged = jax.nn.sigmoid(gates[:, :D]) * a + jax.nn.sigmoid(gates[:, D:]) * p
    y = jnp.dot(merged.astype(BF16), wo_ref[...], preferred_element_type=F32)
    x1 = x + mod_ref[0, 2] * y
    x1_ref[0] = x1
    h2 = _norm_mod(x1, g2_ref[...], mod_ref[0, 4], mod_ref[0, 3])
    h2_hi = h2.astype(BF16)
    h2_lo = (h2 - h2_hi.astype(F32)).astype(BF16)
    logits = (jnp.dot(h2_hi, wrh_ref[...], preferred_element_type=F32)
              + jnp.dot(h2_lo, wrh_ref[...], preferred_element_type=F32)
              + jnp.dot(h2_hi, wrl_ref[...], preferred_element_type=F32)) + br_ref[...]
    lane, i1, i2, w1, w2 = _route(logits)
    if not routed:
        h2_ref[0] = h2_hi
        route_ref[0] = jnp.where(lane == i1, w1, jnp.where(lane == i2, w2, 0.0))
        return

    h2_ref[0] = h2
    counts_ref, carry_ref = out_refs[3], scratch_refs[-1]

    @pl.when((pl.program_id(0) == 0) & (pl.program_id(1) == 0))
    def _():
        carry_ref[...] = jnp.zeros(carry_ref.shape, F32)

    hit = ((lane == i1) | (lane == i2)).astype(BF16)
    ltri = (lax.broadcasted_iota(jnp.int32, (tm, tm), 0) >= lax.broadcasted_iota(jnp.int32, (tm, tm), 1))
    prefix = jnp.dot(ltri.astype(BF16), hit, preferred_element_type=F32) + carry_ref[...]
    rank1 = jnp.sum(jnp.where(lane == i1, prefix, 0.0), axis=-1, keepdims=True) - 1.0
    rank2 = jnp.sum(jnp.where(lane == i2, prefix, 0.0), axis=-1, keepdims=True) - 1.0
    carry_ref[...] = prefix[tm - 1:tm, :]
    counts_ref[...] = jnp.broadcast_to(prefix[tm - 1:tm, :], counts_ref.shape)
    cols = (i1, i2, w1, w2, rank1, rank2)
    route = jnp.zeros(logits.shape, F32)
    for c, col in enumerate(cols):
        route = jnp.where(lane == float(c), col, route)
    route_ref[0] = route


def _merge(x, mod, g1, g2, attn_inputs, pooled, w_gates, w_attn_out, w_pool_out, w_o, wr_hi, wr_lo, b_r, tm,
           routed):
    B, S, D = x.shape
    R = mod.shape[2]
    n_groups = len(attn_inputs) // 2 if len(attn_inputs) > 1 else 1
    tok = lambda width: pl.BlockSpec((1, tm, width), lambda b, i: (b, i, 0))
    if R == 1:
        mod_spec = pl.BlockSpec((1, N_COND, 1, D), lambda b, i: (b, 0, 0, 0))
    else:
        mod_spec = pl.BlockSpec((1, N_COND, tm, D), lambda b, i: (b, 0, i, 0))
    weights = (w_gates, w_attn_out, w_pool_out, w_o, wr_hi, wr_lo, b_r)
    if n_groups > 1:
        attn_specs = [pl.BlockSpec((1, tm // (a.shape[2] // GROUP_WIDTH), a.shape[2]), lambda b, i: (b, i, 0))
                      for a in attn_inputs]
        scratch = [pltpu.VMEM((GROUP_WIDTH // LANE, tm, LANE), F32)] * len(attn_inputs)
    else:
        attn_specs = [tok(GROUP_WIDTH)]
        scratch = []
    out_specs = [tok(D), tok(D), tok(ROUTER_LANES)]
    out_shape = [jax.ShapeDtypeStruct((B, S, D), F32),
                 jax.ShapeDtypeStruct((B, S, D), F32 if routed else BF16),
                 jax.ShapeDtypeStruct((B, S, ROUTER_LANES), F32)]
    if routed:
        out_specs.append(_full((SUBLANE, ROUTER_LANES)))
        out_shape.append(jax.ShapeDtypeStruct((SUBLANE, ROUTER_LANES), F32))
        scratch = scratch + [pltpu.VMEM((1, ROUTER_LANES), F32)]
    return pl.pallas_call(
        functools.partial(_merge_kernel, n_groups=n_groups, routed=routed),
        grid=(B, S // tm),
        in_specs=[tok(D), mod_spec, _full((1, D)), _full((1, D))]
                 + attn_specs + [tok(POOL_WIDTH)]
                 + [_full(w.shape) for w in weights],
        out_specs=out_specs,
        out_shape=out_shape,
        scratch_shapes=scratch,
        compiler_params=_cparams(2),
        name="merge",
    )(x, mod, g1, g2, *attn_inputs, pooled, *weights)


def _moe_kernel(h2_ref, comb_ref, x1_ref, mod_ref, gf_ref, wgu_ref, wd_ref, y_ref, acc_ref, *, d_expert):
    e = pl.program_id(2)

    @pl.when(e == 0)
    def _():
        acc_ref[...] = jnp.zeros(acc_ref.shape, F32)

    gu = jnp.dot(h2_ref[0], wgu_ref[0], preferred_element_type=F32)
    comb = comb_ref[0]
    lane = lax.broadcasted_iota(jnp.int32, comb.shape, 1)
    c_e = jnp.sum(jnp.where(lane == e, comb, 0.0), axis=-1, keepdims=True)
    act = _silu(gu[:, :d_expert]) * gu[:, d_expert:] * c_e
    acc_ref[...] += jnp.dot(act.astype(BF16), wd_ref[0], preferred_element_type=F32)

    @pl.when(e == pl.num_programs(2) - 1)
    def _():
        x2 = x1_ref[0] + mod_ref[0, 5] * acc_ref[...]
        var = jnp.mean(x2 * x2, axis=-1, keepdims=True)
        y_ref[0] = x2 * lax.rsqrt(var + EPS) * gf_ref[...]


def _moe(h2, comb, x1, mod, gf, w_gate_up, w_down, tm):
    B, S, D = x1.shape
    R = mod.shape[2]
    n_exp, _, two_f = w_gate_up.shape
    tok = lambda width: pl.BlockSpec((1, tm, width), lambda b, i, e: (b, i, 0))
    if R == 1:
        mod_spec = pl.BlockSpec((1, N_COND, 1, D), lambda b, i, e: (b, 0, 0, 0))
    else:
        mod_spec = pl.BlockSpec((1, N_COND, tm, D), lambda b, i, e: (b, 0, i, 0))
    return pl.pallas_call(
        functools.partial(_moe_kernel, d_expert=two_f // 2),
        grid=(B, S // tm, n_exp),
        in_specs=[tok(D), tok(ROUTER_LANES), tok(D), mod_spec, _full((1, D)),
                  pl.BlockSpec((1, D, two_f), lambda b, i, e: (e, 0, 0)),
                  pl.BlockSpec((1, two_f // 2, D), lambda b, i, e: (e, 0, 0))],
        out_specs=tok(D),
        out_shape=jax.ShapeDtypeStruct((B, S, D), F32),
        scratch_shapes=[pltpu.VMEM((tm, D), F32)],
        compiler_params=_cparams(3),
        name="moe",
    )(h2, comb, x1, mod, gf, w_gate_up, w_down)


EXPERT_TILE = 256


def _routing_tables(route, counts, tile):
    T = route.shape[0]
    cnt = counts[0, :N_EXPERTS].astype(jnp.int32)
    padded = ((cnt + tile - 1) // tile) * tile
    ends = jnp.cumsum(padded)
    base = ends - padded
    e12 = route[:, 0:2].astype(jnp.int32)
    pos = jnp.take(base, e12, axis=0) + route[:, 4:6].astype(jnp.int32)
    n_tiles = 2 * T // tile + N_EXPERTS
    starts = jnp.arange(n_tiles, dtype=jnp.int32) * tile
    n_used = ends[-1] // tile
    tile_expert = jnp.sum(starts[:, None] >= ends[None, :], axis=1).astype(jnp.int32)
    last = jnp.take(tile_expert, n_used - 1)
    tile_expert = jnp.where(jnp.arange(n_tiles) < n_used, tile_expert, last)
    pads = jnp.stack([base + cnt, ends], axis=1).reshape(-1).astype(jnp.int32)
    return pos, tile_expert, n_used.reshape(1).astype(jnp.int32), pads, n_tiles


def _row_copy(src_ref, src_row, dst_ref, dst_row, sem):
    return pltpu.make_async_copy(src_ref.at[pl.ds(src_row, 1)], dst_ref.at[pl.ds(dst_row, 1)], sem)


def _dispatch_kernel(pads_ref, pos_ref, h_ref, xs_ref, zero_ref, sem, *, tm, tile):
    @pl.when(pl.program_id(0) == 0)
    def _():
        zero_ref[...] = jnp.zeros(zero_ref.shape, zero_ref.dtype)
        n_tiles = xs_ref.shape[0] // tile
        first_unused = pads_ref[2 * N_EXPERTS - 1] // tile

        def tile_copy(j):
            return pltpu.make_async_copy(zero_ref, xs_ref.at[pl.ds(pl.multiple_of(j * tile, tile), tile)], sem)

        def fill_tile(j, c):
            tile_copy(j).start()
            return c

        def drain_tile(j, c):
            tile_copy(j).wait()
            return c

        lax.fori_loop(first_unused, n_tiles, fill_tile, 0)
        lax.fori_loop(first_unused, n_tiles, drain_tile, 0)

        def per_expert(e, carry):
            lo, hi = pads_ref[2 * e], pads_ref[2 * e + 1]

            def fill(r, c):
                _row_copy(zero_ref, 0, xs_ref, r, sem).start()
                return c

            def drain(r, c):
                _row_copy(zero_ref, 0, xs_ref, r, sem).wait()
                return c

            lax.fori_loop(lo, hi, fill, 0)
            lax.fori_loop(lo, hi, drain, 0)
            return carry

        lax.fori_loop(0, N_EXPERTS, per_expert, 0)

    def issue(i, carry):
        _row_copy(h_ref, i, xs_ref, pos_ref[0, 0, 2 * i], sem).start()
        _row_copy(h_ref, i, xs_ref, pos_ref[0, 0, 2 * i + 1], sem).start()
        return carry

    lax.fori_loop(0, tm, issue, 0, unroll=8)
    for _ in range(2):
        pltpu.make_async_copy(h_ref, xs_ref.at[pl.ds(0, tm)], sem).wait()


def _dispatch(h2p, pos, pads, n_rows, tm, tile):
    T, width = h2p.shape
    grid_spec = pltpu.PrefetchScalarGridSpec(
        num_scalar_prefetch=1,
        grid=(T // tm,),
        in_specs=[pl.BlockSpec((1, 1, 2 * tm), lambda i, pads: (i, 0, 0), memory_space=pltpu.SMEM),
                  pl.BlockSpec((tm, width), lambda i, pads: (i, 0))],
        out_specs=pl.BlockSpec(memory_space=pl.ANY),
        scratch_shapes=[pltpu.VMEM((tile, width), h2p.dtype), pltpu.SemaphoreType.DMA(())],
    )
    return pl.pallas_call(
        functools.partial(_dispatch_kernel, tm=tm, tile=tile),
        grid_spec=grid_spec,
        out_shape=jax.ShapeDtypeStruct((n_rows, width), h2p.dtype),
        compiler_params=_cparams(1),
        name="moe_dispatch",
    )(pads, pos.reshape(T // tm, 1, 2 * tm), h2p)


def _experts_kernel(te_ref, nu_ref, xs_ref, wgu_ref, wd_ref, ys_ref, *, d_expert):
    @pl.when(pl.program_id(0) >= nu_ref[0])
    def _():
        ys_ref[...] = jnp.zeros(ys_ref.shape, F32)

    @pl.when(pl.program_id(0) < nu_ref[0])
    def _():
        gu = jnp.dot(xs_ref[...].astype(BF16), wgu_ref[0], preferred_element_type=F32)
        act = _silu(gu[:, :d_expert]) * gu[:, d_expert:]
        ys_ref[...] = jnp.dot(act.astype(BF16), wd_ref[0], preferred_element_type=F32)


def _experts(xs, tile_expert, n_used, w_gate_up, w_down, tile):
    n_rows, width = xs.shape
    _, D, two_f = w_gate_up.shape
    grid_spec = pltpu.PrefetchScalarGridSpec(
        num_scalar_prefetch=2,
        grid=(n_rows // tile,),
        in_specs=[pl.BlockSpec((tile, width), lambda j, te, nu: (jnp.minimum(j, nu[0] - 1), 0)),
                  pl.BlockSpec((1, D, two_f), lambda j, te, nu: (te[j], 0, 0)),
                  pl.BlockSpec((1, two_f // 2, D), lambda j, te, nu: (te[j], 0, 0))],
        out_specs=pl.BlockSpec((tile, D), lambda j, te, nu: (j, 0)),
    )
    return pl.pallas_call(
        functools.partial(_experts_kernel, d_expert=two_f // 2),
        grid_spec=grid_spec,
        out_shape=jax.ShapeDtypeStruct((n_rows, D), F32),
        compiler_params=_cparams(1),
        name="moe_experts",
    )(tile_expert, n_used, xs, w_gate_up, w_down)


def _combine_kernel(pos_ref, pos_next_ref, ys_ref, route_ref, x1_ref, mod_ref, gf_ref, y_ref,
                    ya_ref, yb_ref, sems, *, tm):
    step = pl.program_id(0) * pl.num_programs(1) + pl.program_id(1)
    n_steps = pl.num_programs(0) * pl.num_programs(1)
    slot = step % 2

    def start_gathers(p_ref, s):
        def issue(i, carry):
            _row_copy(ys_ref, p_ref[0, 0, 2 * i], ya_ref.at[s], i, sems.at[s]).start()
            _row_copy(ys_ref, p_ref[0, 0, 2 * i + 1], yb_ref.at[s], i, sems.at[s]).start()
            return carry

        lax.fori_loop(0, tm, issue, 0, unroll=8)

    @pl.when(step == 0)
    def _():
        start_gathers(pos_ref, 0)

    @pl.when(step + 1 < n_steps)
    def _():
        start_gathers(pos_next_ref, 1 - slot)

    for buf in (ya_ref, yb_ref):
        pltpu.make_async_copy(ys_ref.at[pl.ds(0, tm)], buf.at[slot], sems.at[slot]).wait()
    route = route_ref[0]
    lane = lax.broadcasted_iota(jnp.int32, route.shape, 1)
    w1 = jnp.sum(jnp.where(lane == 2, route, 0.0), axis=-1, keepdims=True)
    w2 = jnp.sum(jnp.where(lane == 3, route, 0.0), axis=-1, keepdims=True)
    x2 = x1_ref[0] + mod_ref[0, 5] * (w1 * ya_ref[slot] + w2 * yb_ref[slot])
    var = jnp.mean(x2 * x2, axis=-1, keepdims=True)
    y_ref[0] = x2 * lax.rsqrt(var + EPS) * gf_ref[...]


def _combine(ys, pos, route, x1, mod, gf, tm):
    B, S, D = x1.shape
    n_t = S // tm
    tok = lambda width: pl.BlockSpec((1, tm, width), lambda b, i: (b, i, 0))
    last = B * n_t - 1
    pos_spec = lambda ahead: pl.BlockSpec(
        (1, 1, 2 * tm), lambda b, i: (jnp.minimum(b * n_t + i + ahead, last), 0, 0), memory_space=pltpu.SMEM)
    pos_steps = pos.reshape(B * n_t, 1, 2 * tm)
    return pl.pallas_call(
        functools.partial(_combine_kernel, tm=tm),
        grid=(B, n_t),
        in_specs=[pos_spec(0), pos_spec(1),
                  pl.BlockSpec(memory_space=pl.ANY),
                  tok(ROUTER_LANES), tok(D),
                  pl.BlockSpec((1, N_COND, 1, D), lambda b, i: (b, 0, 0, 0)),
                  _full((1, D))],
        out_specs=tok(D),
        out_shape=jax.ShapeDtypeStruct((B, S, D), F32),
        scratch_shapes=[pltpu.VMEM((2, tm, D), F32), pltpu.VMEM((2, tm, D), F32),
                        pltpu.SemaphoreType.DMA((2,))],
        compiler_params=_cparams(2),
        name="moe_combine",
    )(pos_steps, pos_steps, ys, route, x1, mod, gf)


def _rope_tables(pos):
    half = HEAD_DIM // 2
    inv = ROPE_THETA ** (-jnp.arange(half, dtype=F32) * 2.0 / HEAD_DIM)
    ang = pos.astype(F32)[:, None] * inv[None, :]
    cos, sin = jnp.cos(ang), jnp.sin(ang)
    reps = LANE // HEAD_DIM
    return (jnp.tile(jnp.concatenate([cos, cos], axis=-1), (1, reps)),
            jnp.tile(jnp.concatenate([-sin, sin], axis=-1), (1, reps)))


def kernel(x_prompt, x_sample, cache_kv_w128, cache_kv_w512, cache_kv_w2048, state_pool, c_prompt, c_sample, norm1_g, w_ada, b_ada, w_in, w_attn_out, w_pool, pool_scale, w_pool_out, w_o, norm2_g, w_grp, b_grp, w_exp_router, b_exp_router, w_gate_up, w_down, final_norm_g):
    B, S, D = x_prompt.shape
    N, T, _ = x_sample.shape
    depth = norm1_g.shape[0]
    assert depth == 1, "single trunk layer"
    tm = min(512, S)
    assert S % tm == 0 and all(S % (SPAN * d) == 0 for d in ATTN_DILATIONS)

    n_qkvu = 3 * ATTN_WIDTH + POOL_WIDTH
    w_qkvu = w_in[0, :, :n_qkvu].astype(BF16)
    w_gates = w_in[0, :, n_qkvu:].astype(BF16)
    wa, wpo, wo = w_attn_out[0].astype(BF16), w_pool_out[0].astype(BF16), w_o[0].astype(BF16)
    wpool = w_pool[0].astype(BF16)
    pscale = pool_scale[0].reshape(1, POOL_WIDTH)
    w_r = jnp.concatenate([w_exp_router[0], w_grp[0]], axis=1)
    w_r = jnp.pad(w_r, ((0, 0), (0, ROUTER_LANES - w_r.shape[1])))
    wr_hi = w_r.astype(BF16)
    wr_lo = (w_r - wr_hi.astype(F32)).astype(BF16)
    b_r = jnp.pad(jnp.concatenate([b_exp_router[0], b_grp[0]]), (0, ROUTER_LANES - N_EXPERTS - MOE_GROUPS))
    b_r = b_r.reshape(1, ROUTER_LANES)
    wgu, wd = w_gate_up[0].astype(BF16), w_down[0].astype(BF16)
    g1, g2, gf = norm1_g[0].reshape(1, D), norm2_g[0].reshape(1, D), final_norm_g.reshape(1, D)

    mod = _ada(jnp.concatenate([c_prompt, c_sample], axis=0), w_ada[0], b_ada[0])
    mod_p = mod[:B].reshape(B, N_COND, 1, D)
    mod_s = jnp.repeat(mod[B:].reshape(N, N_COND, D), T, axis=0)
    mod_s = jnp.transpose(mod_s, (1, 0, 2)).reshape(1, N_COND, N * T, D)

    cos_p, sin_p = _rope_tables(jnp.arange(S, dtype=jnp.int32))
    outs = _in_proj_prompt(x_prompt, mod_p, g1, w_qkvu, cos_p, sin_p, wpool, pscale, tm)
    qs, ks, vs = outs[0:3], outs[3:6], outs[6:9]
    pooled, kv0, kv1, kv2, ptail = outs[9:14]
    attn_parts = [_attn_prompt(qs[g], ks[g], vs[g], g, nq=4) for g in range(N_GROUPS)]
    attn_inputs = [o for o, _ in attn_parts] + [l for _, l in attn_parts]
    x1, h2p, route, counts = _merge(x_prompt, mod_p, g1, g2, attn_inputs, pooled, w_gates, wa, wpo, wo,
                                    wr_hi, wr_lo, b_r, tm, routed=True)
    pos, tile_expert, n_used, pads, n_tiles = _routing_tables(route.reshape(B * S, ROUTER_LANES), counts,
                                                              EXPERT_TILE)
    xs = _dispatch(h2p.reshape(B * S, D), pos, pads, n_tiles * EXPERT_TILE, tm, EXPERT_TILE)
    ys = _experts(xs, tile_expert, n_used, wgu, wd, EXPERT_TILE)
    y_prompt = _combine(ys, pos, route, x1, mod_p, gf, tm=min(256, S))

    def kv_shape(a):
        a = a.reshape(a.shape[0], 2, HEADS_PER_GROUP, HEAD_DIM, a.shape[3])
        return jnp.transpose(a, (0, 4, 1, 2, 3))[None]

    pool_prompt = ptail[:, POOL_HALO - POOL_STATE_LEN:, :][None]

    TS = N * T
    pos_s = PAST_LEN + jnp.arange(T, dtype=jnp.int32)
    cos_s, sin_s = _rope_tables(pos_s)
    cos_s, sin_s = jnp.tile(cos_s, (N, 1)), jnp.tile(sin_s, (N, 1))
    xs = x_sample.reshape(1, TS, D)
    q_s, kvn, u_s = _in_proj_sample(xs, mod_s, g1, w_qkvu, cos_s, sin_s)
    eye = jnp.eye(HEADS_PER_GROUP, dtype=BF16)
    qbd = jnp.einsum('ntghe,hk->ngthke', q_s.reshape(N, T, N_GROUPS, HEADS_PER_GROUP, HEAD_DIM), eye)
    qbd = qbd.reshape(N, N_GROUPS, T * HEADS_PER_GROUP, GROUP_WIDTH)
    caches = [jnp.transpose(c[0], (0, 2, 3, 4, 1)).reshape(N, 2, GROUP_WIDTH, c.shape[2])
              for c in (cache_kv_w128, cache_kv_w512, cache_kv_w2048)]
    attn_s, pooled_s, ko0, ko1, ko2, pool_s = _sample_step(
        qbd, kvn.reshape(N, T, 2 * ATTN_WIDTH), caches, state_pool[0], u_s.reshape(N, T, POOL_WIDTH), wpool, pscale)
    x1s, h2s, comb_s = _merge(xs, mod_s, g1, g2, [attn_s.reshape(1, TS, GROUP_WIDTH)],
                              pooled_s.reshape(1, TS, POOL_WIDTH), w_gates, wa, wpo, wo, wr_hi, wr_lo, b_r, TS,
                              routed=False)
    y_sample = _moe(h2s, comb_s, x1s, mod_s, gf, wgu, wd, tm=TS).reshape(N, T, D)

    return (y_prompt, y_sample, kv_shape(kv0), kv_shape(kv1), kv_shape(kv2), pool_prompt,
            kv_shape(ko0), kv_shape(ko1), kv_shape(ko2), pool_s[None])
```

```python
import functools

import jax
import jax.numpy as jnp
from jax import lax
from jax.experimental import pallas as pl
from jax.experimental.pallas import tpu as pltpu

F32 = jnp.float32
BF16 = jnp.bfloat16

HEAD_DIM = 64
HEADS_PER_GROUP = 4
HEAD_SHIFT = 2
LANE_HEAD_SHIFT = 6
GROUP_WIDTH = HEADS_PER_GROUP * HEAD_DIM
ATTN_WINDOWS = (128, 512, 2048)
ATTN_DILATIONS = (1, 4, 16)
N_GROUPS = 3
SPAN = 128
ATTN_WIDTH = N_GROUPS * GROUP_WIDTH
ROPE_THETA = 10000.0
PAST_LEN = 8192
POOL_WINDOWS = (2, 4, 8, 16)
POOL_GROUP_WIDTH = 128
POOL_WIDTH = 512
POOL_STATE_LEN = 15
POOL_HALO = 16
MOE_GROUPS = 4
EXPERTS_PER_GROUP = 8
N_EXPERTS = 32
N_COND = 6
EPS = 1e-6

LANE = 128
SUBLANE = 8
VMEM_LIMIT_BYTES = 56 * 1024 * 1024

NEG = -1e30
BIG_LANE = 1e9

ROUTER_LANES = LANE
GROUP_LANE0 = N_EXPERTS


def _cparams(n_axes):
    return pltpu.CompilerParams(dimension_semantics=("arbitrary",) * n_axes,
                                vmem_limit_bytes=VMEM_LIMIT_BYTES)


def _full(shape):
    nd = len(shape)
    return pl.BlockSpec(tuple(shape), lambda *_: (0,) * nd)


def _norm_mod(x, g, scale, shift):
    var = jnp.mean(x * x, axis=-1, keepdims=True)
    return (x * lax.rsqrt(var + EPS) * g) * (1.0 + scale) + shift


def _rope(x, cos, sin):
    lane = lax.broadcasted_iota(jnp.int32, (x.shape[0], LANE), 1)
    first_half = (lane & (HEAD_DIM - 1)) < (HEAD_DIM // 2)
    outs = []
    for c in range(x.shape[1] // LANE):
        xc = x[:, c * LANE:(c + 1) * LANE]
        partner = jnp.where(first_half, pltpu.roll(xc, LANE - HEAD_DIM // 2, 1),
                            pltpu.roll(xc, HEAD_DIM // 2, 1))
        outs.append(xc * cos + partner * sin)
    return jnp.concatenate(outs, axis=1)


def _silu(x):
    return x * jax.nn.sigmoid(x)


def _ada_kernel(c_ref, w_ref, b_ref, o_ref):
    s = _silu(c_ref[...]).astype(BF16)
    o_ref[...] = jnp.dot(s, w_ref[...].astype(BF16), preferred_element_type=F32) + b_ref[...]


def _ada(c_all, w_ada, b_ada):
    rows, d = c_all.shape
    n_out = w_ada.shape[1]
    tn = 1024
    return pl.pallas_call(
        _ada_kernel,
        grid=(n_out // tn,),
        in_specs=[_full((rows, d)),
                  pl.BlockSpec((d, tn), lambda j: (0, j)),
                  pl.BlockSpec((1, tn), lambda j: (0, j))],
        out_specs=pl.BlockSpec((rows, tn), lambda j: (0, j)),
        out_shape=jax.ShapeDtypeStruct((rows, n_out), F32),
        compiler_params=_cparams(1),
        name="ada",
    )(c_all, w_ada, b_ada.reshape(1, n_out))


def _in_proj_prompt_kernel(x_ref, mod_ref, g1_ref, w_ref, cos_ref, sin_ref, wpool_ref, pscale_ref, *rest,
                           tm, n_tiles, tails):
    qkv_refs = (rest[0:3], rest[3:6], rest[6:9])
    pooled_ref, kv0_ref, kv1_ref, kv2_ref, ptail_ref = rest[9:14]
    stage_refs = rest[14:17]
    ue_ref = rest[17]
    i = pl.program_id(1)
    x = x_ref[0]
    h = _norm_mod(x, g1_ref[...], mod_ref[0, 1], mod_ref[0, 0]).astype(BF16)
    cos = cos_ref[...]
    sin = sin_ref[...]
    y = jnp.dot(h, w_ref[...], preferred_element_type=F32)
    q = _rope(y[:, :ATTN_WIDTH], cos, sin) * (HEAD_DIM ** -0.5)
    k = _rope(y[:, ATTN_WIDTH:2 * ATTN_WIDTH], cos, sin)
    v = y[:, 2 * ATTN_WIDTH:3 * ATTN_WIDTH]
    u = y[:, 3 * ATTN_WIDTH:3 * ATTN_WIDTH + POOL_WIDTH]

    chunks = GROUP_WIDTH // LANE
    for val, out_refs, stage_ref in zip((q, k, v), qkv_refs, stage_refs):
        for g, d in enumerate(ATTN_DILATIONS):
            gl = slice(g * GROUP_WIDTH, (g + 1) * GROUP_WIDTH)
            if d == 1:
                out_refs[g][0] = val[:, gl].astype(BF16)
                continue
            for c in range(chunks):
                stage_ref[g * chunks + c] = val[:, g * GROUP_WIDTH + c * LANE:g * GROUP_WIDTH + (c + 1) * LANE]
            for r in range(d):
                for c in range(chunks):
                    out_refs[g][0, :, r * GROUP_WIDTH + c * LANE:r * GROUP_WIDTH + (c + 1) * LANE] = (
                        stage_ref[g * chunks + c, pl.ds(r, tm // d, stride=d), :].astype(BF16))

    for g, kv_ref in enumerate((kv0_ref, kv1_ref, kv2_ref)):
        first_tile, rows = tails[g]
        lo = tm - rows

        @pl.when(i >= first_tile)
        def _(kv_ref=kv_ref, g=g, lo=lo):
            kv_ref[0, 0] = k[lo:, g * GROUP_WIDTH:(g + 1) * GROUP_WIDTH].T
            kv_ref[0, 1] = v[lo:, g * GROUP_WIDTH:(g + 1) * GROUP_WIDTH].T

    @pl.when(i == 0)
    def _():
        ue_ref[0:POOL_HALO, :] = jnp.zeros((POOL_HALO, POOL_WIDTH), F32)

    ue_ref[POOL_HALO:, :] = u
    pos1 = (i * tm + 1 + lax.broadcasted_iota(jnp.int32, (tm, 1), 0)).astype(F32)
    for gi, w in enumerate(POOL_WINDOWS):
        lanes = slice(gi * POOL_GROUP_WIDTH, (gi + 1) * POOL_GROUP_WIDTH)
        acc = ue_ref[pl.ds(POOL_HALO, tm), lanes]
        for j in range(1, w):
            acc = acc + ue_ref[pl.ds(POOL_HALO - j, tm), lanes]
        mean = acc / jnp.minimum(float(w), pos1)
        z = jnp.dot((mean - u[:, lanes]).astype(BF16), wpool_ref[gi], preferred_element_type=F32)
        pooled_ref[0, :, lanes] = (z * pscale_ref[:, lanes]).astype(BF16)
    ue_ref[0:POOL_HALO, :] = u[tm - POOL_HALO:, :]

    @pl.when(i == n_tiles - 1)
    def _():
        ptail_ref[0] = u[tm - POOL_HALO:, :]


def _in_proj_prompt(x, mod, g1, w_qkvu, cos, sin, w_pool, pool_scale, tm):
    B, S, D = x.shape
    n_tiles = S // tm
    n_w = w_qkvu.shape[1]
    tails, kv_shapes, kv_specs = [], [], []
    for W in ATTN_WINDOWS:
        Wg = min(W, S)
        if Wg >= tm:
            first = n_tiles - Wg // tm
            rows = tm
        else:
            first = n_tiles - 1
            rows = Wg
        tails.append((first, rows))
        kv_shapes.append(jax.ShapeDtypeStruct((B, 2, GROUP_WIDTH, Wg), F32))
        kv_specs.append(pl.BlockSpec((1, 2, GROUP_WIDTH, rows),
                                     lambda b, i, first=first: (b, 0, 0, jnp.maximum(i - first, 0))))
    tok = lambda width: pl.BlockSpec((1, tm, width), lambda b, i: (b, i, 0))
    dil_specs = [pl.BlockSpec((1, tm // d, d * GROUP_WIDTH), lambda b, i: (b, i, 0)) for d in ATTN_DILATIONS]
    dil_shapes = [jax.ShapeDtypeStruct((B, S // d, d * GROUP_WIDTH), BF16) for d in ATTN_DILATIONS]
    outs = pl.pallas_call(
        functools.partial(_in_proj_prompt_kernel, tm=tm, n_tiles=n_tiles, tails=tuple(tails)),
        grid=(B, n_tiles),
        in_specs=[tok(D),
                  pl.BlockSpec((1, N_COND, 1, D), lambda b, i: (b, 0, 0, 0)),
                  _full((1, D)),
                  _full((D, n_w)),
                  pl.BlockSpec((tm, LANE), lambda b, i: (i, 0)),
                  pl.BlockSpec((tm, LANE), lambda b, i: (i, 0)),
                  _full(w_pool.shape),
                  _full((1, POOL_WIDTH))],
        out_specs=dil_specs * 3 + [tok(POOL_WIDTH)] + kv_specs
                  + [pl.BlockSpec((1, POOL_HALO, POOL_WIDTH), lambda b, i: (b, 0, 0))],
        out_shape=dil_shapes * 3
                  + [jax.ShapeDtypeStruct((B, S, POOL_WIDTH), BF16)] + kv_shapes
                  + [jax.ShapeDtypeStruct((B, POOL_HALO, POOL_WIDTH), F32)],
        scratch_shapes=[pltpu.VMEM((ATTN_WIDTH // LANE, tm, LANE), F32)] * 3
                       + [pltpu.VMEM((tm + POOL_HALO, POOL_WIDTH), F32)],
        compiler_params=_cparams(2),
        name="in_proj_prompt",
    )(x, mod, g1, w_qkvu, cos, sin, w_pool, pool_scale)
    return outs


def _in_proj_sample_kernel(x_ref, mod_ref, g1_ref, w_ref, cos_ref, sin_ref, q_ref, kvn_ref, u_ref):
    h = _norm_mod(x_ref[0], g1_ref[...], mod_ref[0, 1], mod_ref[0, 0]).astype(BF16)
    y = jnp.dot(h, w_ref[...], preferred_element_type=F32)
    cos = cos_ref[...]
    sin = sin_ref[...]
    q = _rope(y[:, :ATTN_WIDTH], cos, sin) * (HEAD_DIM ** -0.5)
    k = _rope(y[:, ATTN_WIDTH:2 * ATTN_WIDTH], cos, sin)
    v = y[:, 2 * ATTN_WIDTH:3 * ATTN_WIDTH]
    q_ref[...] = q.astype(BF16)
    for g in range(N_GROUPS):
        gl = slice(g * GROUP_WIDTH, (g + 1) * GROUP_WIDTH)
        kvn_ref[:, 2 * g * GROUP_WIDTH:(2 * g + 1) * GROUP_WIDTH] = k[:, gl]
        kvn_ref[:, (2 * g + 1) * GROUP_WIDTH:(2 * g + 2) * GROUP_WIDTH] = v[:, gl]
    u_ref[...] = y[:, 3 * ATTN_WIDTH:3 * ATTN_WIDTH + POOL_WIDTH]


def _in_proj_sample(x, mod, g1, w_qkvu, cos, sin):
    _, T, D = x.shape
    n_w = w_qkvu.shape[1]
    return pl.pallas_call(
        _in_proj_sample_kernel,
        grid=(1,),
        in_specs=[_full((1, T, D)), _full((1, N_COND, T, D)), _full((1, D)), _full((D, n_w)),
                  _full((T, LANE)), _full((T, LANE))],
        out_specs=[_full((T, ATTN_WIDTH)), _full((T, 2 * ATTN_WIDTH)), _full((T, POOL_WIDTH))],
        out_shape=[jax.ShapeDtypeStruct((T, ATTN_WIDTH), BF16),
                   jax.ShapeDtypeStruct((T, 2 * ATTN_WIDTH), F32),
                   jax.ShapeDtypeStruct((T, POOL_WIDTH), F32)],
        compiler_params=_cparams(1),
        name="in_proj_sample",
    )(x, mod, g1, w_qkvu, cos, sin)


def _attn_kernel(q_ref, kc_ref, kp_ref, vc_ref, vp_ref, o_ref, lse_ref, kbuf, vbuf, *, nq):
    n0 = pl.program_id(1)
    kbuf[0:SPAN, :] = kp_ref[0]
    kbuf[SPAN:, :] = kc_ref[0]
    vbuf[0:SPAN, :] = vp_ref[0]
    vbuf[SPAN:, :] = vc_ref[0]
    qi = lax.broadcasted_iota(jnp.int32, (SPAN, 2 * SPAN), 0)
    kj = lax.broadcasted_iota(jnp.int32, (SPAN, 2 * SPAN), 1)
    band = (kj >= qi) & (kj <= qi + SPAN)
    band_first = band & (kj >= jnp.where(n0 > 0, 0, SPAN))
    lane = lax.broadcasted_iota(jnp.int32, (SPAN, LANE), 1)
    low_head = lane < HEAD_DIM
    for j in range(nq):
        valid = band_first if j == 0 else band
        rows = slice(j * SPAN, (j + 1) * SPAN)
        krows = slice(j * SPAN, (j + 2) * SPAN)
        for c in range(GROUP_WIDTH // LANE):
            cl = slice(c * LANE, (c + 1) * LANE)
            q = q_ref[0, rows, cl]
            kk = kbuf[krows, cl]
            vv = vbuf[krows, cl]
            o_pair, lse_pair = [], []
            for hh in range(2):
                mask_h = low_head if hh == 0 else jnp.logical_not(low_head)
                qm = jnp.where(mask_h, q, jnp.zeros_like(q))
                s = lax.dot_general(qm, kk, (((1,), (1,)), ((), ())), preferred_element_type=F32)
                s = jnp.where(valid, s, NEG)
                m = jnp.max(s, axis=-1, keepdims=True)
                p = jnp.exp(s - m)
                den = jnp.sum(p, axis=-1, keepdims=True)
                o_pair.append(jnp.dot(p.astype(BF16), vv, preferred_element_type=F32) / den)
                lse_pair.append(m + jnp.log(den))
            o_ref[0, rows, cl] = jnp.where(low_head, o_pair[0], o_pair[1]).astype(BF16)
            lse_ref[0, rows, cl] = jnp.where(low_head, lse_pair[0], lse_pair[1])


def _attn_prompt(qv, kv, vv, g, nq):
    B, M, width = qv.shape
    d = width // GROUP_WIDTH
    nq = min(nq, M // SPAN)
    cur = pl.BlockSpec((1, nq * SPAN, GROUP_WIDTH), lambda b, n, r: (b, n, r))
    prev = pl.BlockSpec((1, SPAN, GROUP_WIDTH), lambda b, n, r: (b, jnp.maximum(n * nq - 1, 0), r))
    out = cur
    return pl.pallas_call(
        functools.partial(_attn_kernel, nq=nq),
        grid=(B, M // (nq * SPAN), d),
        in_specs=[cur, cur, prev, cur, prev],
        out_specs=[out, out],
        out_shape=[jax.ShapeDtypeStruct((B, M, d * GROUP_WIDTH), BF16),
                   jax.ShapeDtypeStruct((B, M, d * GROUP_WIDTH), F32)],
        scratch_shapes=[pltpu.VMEM(((nq + 1) * SPAN, GROUP_WIDTH), BF16)] * 2,
        compiler_params=_cparams(3),
        name=f"attn_prompt_g{g}",
    )(qv, kv, kv, vv, vv)


def _sample_kernel(qbd_ref, kvn_ref, c0_ref, c1_ref, c2_ref, sp_ref, u_ref, wpool_ref, pscale_ref,
                   attn_ref, pooled_ref, o0_ref, o1_ref, o2_ref, po_ref, ue_ref, diff_ref, nt_ref, *, t_new):
    rows = HEADS_PER_GROUP * t_new
    caches = (c0_ref, c1_ref, c2_ref)
    outs = (o0_ref, o1_ref, o2_ref)

    s_cache, s_new, valid_new = [], [], []
    m = jnp.full((rows, 1), NEG, F32)
    for g in range(N_GROUPS):
        d = ATTN_DILATIONS[g]
        c_ref = caches[g]
        L = c_ref.shape[3]
        qb = qbd_ref[0, g]
        s = jnp.dot(qb, c_ref[0, 0].astype(BF16), preferred_element_type=F32)
        t_row = lax.broadcasted_iota(jnp.int32, (rows, L), 0) >> HEAD_SHIFT
        delta = L + t_row - lax.broadcasted_iota(jnp.int32, (rows, L), 1)
        ok = ((delta & (d - 1)) == 0) & (delta <= SPAN * d)
        s = jnp.where(ok, s, NEG)
        s_cache.append(s)
        m = jnp.maximum(m, jnp.max(s, axis=-1, keepdims=True))
        qf = qb.astype(F32)
        t_col = lax.broadcasted_iota(jnp.int32, (rows, 1), 0) >> HEAD_SHIFT
        sn, okn = [], []
        for tn in range(t_new):
            kn = kvn_ref[0, tn:tn + 1, 2 * g * GROUP_WIDTH:(2 * g + 1) * GROUP_WIDTH]
            kn = kn.astype(BF16).astype(F32)
            dn = t_col - tn
            ok_n = (dn >= 0) & ((dn & (d - 1)) == 0)
            s1 = jnp.where(ok_n, jnp.sum(qf * kn, axis=-1, keepdims=True), NEG)
            sn.append(s1)
            okn.append(ok_n)
            m = jnp.maximum(m, s1)
        s_new.append(sn)
        valid_new.append(okn)

    den = jnp.zeros((rows, 1), F32)
    acc = jnp.zeros((rows, GROUP_WIDTH), F32)
    for g in range(N_GROUPS):
        c_ref = caches[g]
        p = jnp.exp(s_cache[g] - m)
        den = den + jnp.sum(p, axis=-1, keepdims=True)
        acc = acc + lax.dot_general(p.astype(BF16), c_ref[0, 1].astype(BF16), (((1,), (1,)), ((), ())),
                                    preferred_element_type=F32)
        for tn in range(t_new):
            pn = jnp.exp(s_new[g][tn] - m)
            den = den + pn
            vn = kvn_ref[0, tn:tn + 1, (2 * g + 1) * GROUP_WIDTH:(2 * g + 2) * GROUP_WIDTH]
            acc = acc + pn * vn
    row_head = lax.broadcasted_iota(jnp.int32, (rows, GROUP_WIDTH), 0) & (HEADS_PER_GROUP - 1)
    lane_head = lax.broadcasted_iota(jnp.int32, (rows, GROUP_WIDTH), 1) >> LANE_HEAD_SHIFT
    o_diag = jnp.where(row_head == lane_head, acc / den, 0.0).astype(BF16)
    sel = ((lax.broadcasted_iota(jnp.int32, (rows, rows), 1) >> HEAD_SHIFT)
           == lax.broadcasted_iota(jnp.int32, (rows, rows), 0)).astype(BF16)
    attn = jnp.dot(sel, o_diag, preferred_element_type=F32)
    attn_ref[0] = attn[0:t_new].astype(BF16)

    tail_lane = lax.broadcasted_iota(jnp.int32, (GROUP_WIDTH, LANE), 1)
    nt_ref[...] = jnp.zeros(nt_ref.shape, F32)
    for g in range(N_GROUPS):
        c_ref, o_ref = caches[g], outs[g]
        L = c_ref.shape[3]
        for kv in range(2):
            col0 = (2 * g + kv) * GROUP_WIDTH
            nt_ref[LANE - t_new:LANE, :] = kvn_ref[0, :, col0:col0 + GROUP_WIDTH]
            new_t = nt_ref[...].T
            rolled = pltpu.roll(c_ref[0, kv], L - t_new, 1)
            if L > LANE:
                o_ref[0, kv, :, 0:L - LANE] = rolled[:, 0:L - LANE]
            o_ref[0, kv, :, L - LANE:L] = jnp.where(tail_lane >= LANE - t_new, new_t, rolled[:, L - LANE:L])

    ue_ref[0:POOL_STATE_LEN, :] = sp_ref[0]
    ue_ref[POOL_STATE_LEN:POOL_STATE_LEN + t_new, :] = u_ref[0]
    diff_ref[...] = jnp.zeros(diff_ref.shape, F32)
    for tn in range(t_new):
        r = POOL_STATE_LEN + tn
        for gi, w in enumerate(POOL_WINDOWS):
            lanes = slice(gi * POOL_GROUP_WIDTH, (gi + 1) * POOL_GROUP_WIDTH)
            win = jnp.sum(ue_ref[r - w + 1:r + 1, lanes], axis=0, keepdims=True)
            diff_ref[tn:tn + 1, lanes] = win / float(w) - ue_ref[r:r + 1, lanes]
    for gi in range(len(POOL_WINDOWS)):
        lanes = slice(gi * POOL_GROUP_WIDTH, (gi + 1) * POOL_GROUP_WIDTH)
        z = jnp.dot(diff_ref[:, lanes].astype(BF16), wpool_ref[gi], preferred_element_type=F32)
        pooled_ref[0, :, lanes] = (z[0:t_new] * pscale_ref[:, lanes]).astype(BF16)
    po_ref[0] = ue_ref[t_new:t_new + POOL_STATE_LEN, :]


def _sample_step(qbd, kvn, caches, state_pool, u, w_pool, pool_scale):
    N, t_new, _ = u.shape
    rows = HEADS_PER_GROUP * t_new
    per_n = lambda shape: pl.BlockSpec((1,) + tuple(shape), lambda n: (n,) + (0,) * len(shape))
    cache_specs = [per_n(c.shape[1:]) for c in caches]
    return pl.pallas_call(
        functools.partial(_sample_kernel, t_new=t_new),
        grid=(N,),
        in_specs=[per_n(qbd.shape[1:]), per_n(kvn.shape[1:])] + cache_specs
                 + [per_n(state_pool.shape[1:]), per_n(u.shape[1:]), _full(w_pool.shape), _full((1, POOL_WIDTH))],
        out_specs=[per_n((t_new, GROUP_WIDTH)), per_n((t_new, POOL_WIDTH))] + cache_specs
                  + [per_n(state_pool.shape[1:])],
        out_shape=[jax.ShapeDtypeStruct((N, t_new, GROUP_WIDTH), BF16),
                   jax.ShapeDtypeStruct((N, t_new, POOL_WIDTH), BF16)]
                  + [jax.ShapeDtypeStruct(c.shape, F32) for c in caches]
                  + [jax.ShapeDtypeStruct(state_pool.shape, F32)],
        scratch_shapes=[pltpu.VMEM((POOL_STATE_LEN + t_new + 5, POOL_WIDTH), F32),
                        pltpu.VMEM((rows, POOL_WIDTH), F32),
                        pltpu.VMEM((LANE, GROUP_WIDTH), F32)],
        compiler_params=_cparams(1),
        name="sample_step",
    )(qbd, kvn, *caches, state_pool, u, w_pool, pool_scale)


def _route(logits):
    lane = lax.broadcasted_iota(jnp.int32, logits.shape, 1).astype(F32)
    is_grp = (lane >= GROUP_LANE0) & (lane < GROUP_LANE0 + MOE_GROUPS)
    gl = jnp.where(is_grp, logits, NEG)
    gmax = jnp.max(gl, axis=-1, keepdims=True)
    gidx = jnp.min(jnp.where(gl == gmax, lane, BIG_LANE), axis=-1, keepdims=True) - GROUP_LANE0
    gsum = jnp.sum(jnp.where(is_grp, jnp.exp(gl - gmax), 0.0), axis=-1, keepdims=True)
    grp_w = 1.0 / gsum
    lo = gidx * EXPERTS_PER_GROUP
    in_grp = (lane >= lo) & (lane < lo + EXPERTS_PER_GROUP)
    el = jnp.where(in_grp, logits, NEG)
    v1 = jnp.max(el, axis=-1, keepdims=True)
    i1 = jnp.min(jnp.where(el == v1, lane, BIG_LANE), axis=-1, keepdims=True)
    el2 = jnp.where(lane == i1, NEG, el)
    v2 = jnp.max(el2, axis=-1, keepdims=True)
    i2 = jnp.min(jnp.where(el2 == v2, lane, BIG_LANE), axis=-1, keepdims=True)
    e = jnp.exp(v2 - v1)
    w1 = grp_w / (1.0 + e)
    w2 = grp_w * e / (1.0 + e)
    return lane, i1, i2, w1, w2


def _merge_kernel(*refs, n_groups, routed):
    x_ref, mod_ref, g1_ref, g2_ref = refs[0:4]
    n_attn = 2 * n_groups if n_groups > 1 else 1
    attn_refs = refs[4:4 + n_attn]
    rest = refs[4 + n_attn:]
    pooled_ref, wg_ref, wa_ref, wp_ref, wo_ref, wrh_ref, wrl_ref, br_ref = rest[:8]
    n_out = 4 if routed else 3
    out_refs = rest[8:8 + n_out]
    x1_ref, h2_ref, route_ref = out_refs[:3]
    scratch_refs = rest[8 + n_out:]
    stage_refs = scratch_refs[:n_attn] if n_groups > 1 else ()

    x = x_ref[0]
    tm = x.shape[0]
    if n_groups > 1:
        vals = []
        for idx, ref in enumerate(attn_refs):
            d = ATTN_DILATIONS[idx % n_groups]
            if d == 1:
                vals.append(ref[0].astype(F32))
                continue
            stage_ref = stage_refs[idx]
            chunks = GROUP_WIDTH // LANE
            for r in range(d):
                for c in range(chunks):
                    stage_ref[c, pl.ds(r, tm // d, stride=d), :] = (
                        ref[0, :, r * GROUP_WIDTH + c * LANE:r * GROUP_WIDTH + (c + 1) * LANE].astype(F32))
            vals.append(jnp.concatenate([stage_ref[c] for c in range(chunks)], axis=1))
        os_, ls = vals[:n_groups], vals[n_groups:]
        lmax = functools.reduce(jnp.maximum, ls)
        es = [jnp.exp(l - lmax) for l in ls]
        attn = sum(e * o for e, o in zip(es, os_)) / sum(es)
    else:
        attn = attn_refs[0][0]
    a = jnp.dot(attn.astype(BF16), wa_ref[...], preferred_element_type=F32)
    p = jnp.dot(pooled_ref[0], wp_ref[...], preferred_element_type=F32)
    h = _norm_mod(x, g1_ref[...], mod_ref[0, 1], mod_ref[0, 0]).astype(BF16)
    gates = jnp.dot(h, wg_ref[...], preferred_element_type=F32)
    D = x.shape[1]
    merged = jax.nn.sigmoid(gates[:, :D]) * a + jax.nn.sigmoid(gates[:, D:]) * p
    y = jnp.dot(merged.astype(BF16), wo_ref[...], preferred_element_type=F32)
    x1 = x + mod_ref[0, 2] * y
    x1_ref[0] = x1
    h2 = _norm_mod(x1, g2_ref[...], mod_ref[0, 4], mod_ref[0, 3])
    h2_hi = h2.astype(BF16)
    h2_lo = (h2 - h2_hi.astype(F32)).astype(BF16)
    logits = (jnp.dot(h2_hi, wrh_ref[...], preferred_element_type=F32)
              + jnp.dot(h2_lo, wrh_ref[...], preferred_element_type=F32)
              + jnp.dot(h2_hi, wrl_ref[...], preferred_element_type=F32)) + br_ref[...]
    lane, i1, i2, w1, w2 = _route(logits)
    if not routed:
        h2_ref[0] = h2_hi
        route_ref[0] = jnp.where(lane == i1, w1, jnp.where(lane == i2, w2, 0.0))
        return

    h2_ref[0] = h2
    counts_ref, carry_ref = out_refs[3], scratch_refs[-1]

    @pl.when((pl.program_id(0) == 0) & (pl.program_id(1) == 0))
    def _():
        carry_ref[...] = jnp.zeros(carry_ref.shape, F32)

    hit = ((lane == i1) | (lane == i2)).astype(BF16)
    ltri = (lax.broadcasted_iota(jnp.int32, (tm, tm), 0) >= lax.broadcasted_iota(jnp.int32, (tm, tm), 1))
    prefix = jnp.dot(ltri.astype(BF16), hit, preferred_element_type=F32) + carry_ref[...]
    rank1 = jnp.sum(jnp.where(lane == i1, prefix, 0.0), axis=-1, keepdims=True) - 1.0
    rank2 = jnp.sum(jnp.where(lane == i2, prefix, 0.0), axis=-1, keepdims=True) - 1.0
    carry_ref[...] = prefix[tm - 1:tm, :]
    counts_ref[...] = jnp.broadcast_to(prefix[tm - 1:tm, :], counts_ref.shape)
    cols = (i1, i2, w1, w2, rank1, rank2)
    route = jnp.zeros(logits.shape, F32)
    for c, col in enumerate(cols):
        route = jnp.where(lane == float(c), col, route)
    route_ref[0] = route


def _merge(x, mod, g1, g2, attn_inputs, pooled, w_gates, w_attn_out, w_pool_out, w_o, wr_hi, wr_lo, b_r, tm,
           routed):
    B, S, D = x.shape
    R = mod.shape[2]
    n_groups = len(attn_inputs) // 2 if len(attn_inputs) > 1 else 1
    tok = lambda width: pl.BlockSpec((1, tm, width), lambda b, i: (b, i, 0))
    if R == 1:
        mod_spec = pl.BlockSpec((1, N_COND, 1, D), lambda b, i: (b, 0, 0, 0))
    else:
        mod_spec = pl.BlockSpec((1, N_COND, tm, D), lambda b, i: (b, 0, i, 0))
    weights = (w_gates, w_attn_out, w_pool_out, w_o, wr_hi, wr_lo, b_r)
    if n_groups > 1:
        attn_specs = [pl.BlockSpec((1, tm // (a.shape[2] // GROUP_WIDTH), a.shape[2]), lambda b, i: (b, i, 0))
                      for a in attn_inputs]
        scratch = [pltpu.VMEM((GROUP_WIDTH // LANE, tm, LANE), F32)] * len(attn_inputs)
    else:
        attn_specs = [tok(GROUP_WIDTH)]
        scratch = []
    out_specs = [tok(D), tok(D), tok(ROUTER_LANES)]
    out_shape = [jax.ShapeDtypeStruct((B, S, D), F32),
                 jax.ShapeDtypeStruct((B, S, D), F32 if routed else BF16),
                 jax.ShapeDtypeStruct((B, S, ROUTER_LANES), F32)]
    if routed:
        out_specs.append(_full((SUBLANE, ROUTER_LANES)))
        out_shape.append(jax.ShapeDtypeStruct((SUBLANE, ROUTER_LANES), F32))
        scratch = scratch + [pltpu.VMEM((1, ROUTER_LANES), F32)]
    return pl.pallas_call(
        functools.partial(_merge_kernel, n_groups=n_groups, routed=routed),
        grid=(B, S // tm),
        in_specs=[tok(D), mod_spec, _full((1, D)), _full((1, D))]
                 + attn_specs + [tok(POOL_WIDTH)]
                 + [_full(w.shape) for w in weights],
        out_specs=out_specs,
        out_shape=out_shape,
        scratch_shapes=scratch,
        compiler_params=_cparams(2),
        name="merge",
    )(x, mod, g1, g2, *attn_inputs, pooled, *weights)


def _moe_kernel(h2_ref, comb_ref, x1_ref, mod_ref, gf_ref, wgu_ref, wd_ref, y_ref, acc_ref, *, d_expert):
    e = pl.program_id(2)

    @pl.when(e == 0)
    def _():
        acc_ref[...] = jnp.zeros(acc_ref.shape, F32)

    gu = jnp.dot(h2_ref[0], wgu_ref[0].astype(BF16), preferred_element_type=F32)
    comb = comb_ref[0]
    lane = lax.broadcasted_iota(jnp.int32, comb.shape, 1)
    c_e = jnp.sum(jnp.where(lane == e, comb, 0.0), axis=-1, keepdims=True)
    act = _silu(gu[:, :d_expert]) * gu[:, d_expert:] * c_e
    acc_ref[...] += jnp.dot(act.astype(BF16), wd_ref[0].astype(BF16), preferred_element_type=F32)

    @pl.when(e == pl.num_programs(2) - 1)
    def _():
        x2 = x1_ref[0] + mod_ref[0, 5] * acc_ref[...]
        var = jnp.mean(x2 * x2, axis=-1, keepdims=True)
        y_ref[0] = x2 * lax.rsqrt(var + EPS) * gf_ref[...]


def _moe(h2, comb, x1, mod, gf, w_gate_up, w_down, tm):
    B, S, D = x1.shape
    R = mod.shape[2]
    n_exp, _, two_f = w_gate_up.shape
    tok = lambda width: pl.BlockSpec((1, tm, width), lambda b, i, e: (b, i, 0))
    if R == 1:
        mod_spec = pl.BlockSpec((1, N_COND, 1, D), lambda b, i, e: (b, 0, 0, 0))
    else:
        mod_spec = pl.BlockSpec((1, N_COND, tm, D), lambda b, i, e: (b, 0, i, 0))
    return pl.pallas_call(
        functools.partial(_moe_kernel, d_expert=two_f // 2),
        grid=(B, S // tm, n_exp),
        in_specs=[tok(D), tok(ROUTER_LANES), tok(D), mod_spec, _full((1, D)),
                  pl.BlockSpec((1, D, two_f), lambda b, i, e: (e, 0, 0)),
                  pl.BlockSpec((1, two_f // 2, D), lambda b, i, e: (e, 0, 0))],
        out_specs=tok(D),
        out_shape=jax.ShapeDtypeStruct((B, S, D), F32),
        scratch_shapes=[pltpu.VMEM((tm, D), F32)],
        compiler_params=_cparams(3),
        name="moe",
    )(h2, comb, x1, mod, gf, w_gate_up, w_down)


EXPERT_TILE = 512


def _routing_tables(route, counts, tile):
    T = route.shape[0]
    cnt = counts[0, :N_EXPERTS].astype(jnp.int32)
    padded = ((cnt + tile - 1) // tile) * tile
    ends = jnp.cumsum(padded)
    base = ends - padded
    e12 = route[:, 0:2].astype(jnp.int32)
    hit = e12[:, :, None] == jnp.arange(N_EXPERTS, dtype=jnp.int32)
    pos = jnp.sum(jnp.where(hit, base, 0), axis=-1) + route[:, 4:6].astype(jnp.int32)
    n_tiles = 2 * T // tile + N_EXPERTS
    starts = jnp.arange(n_tiles, dtype=jnp.int32) * tile
    n_used = ends[-1] // tile
    tile_expert = jnp.sum(starts[:, None] >= ends[None, :], axis=1).astype(jnp.int32)
    last = jnp.take(tile_expert, n_used - 1)
    tile_expert = jnp.where(jnp.arange(n_tiles) < n_used, tile_expert, last)
    pads = jnp.stack([base + cnt, ends], axis=1).reshape(-1).astype(jnp.int32)
    return pos, tile_expert, n_used.reshape(1).astype(jnp.int32), pads, n_tiles


def _row_copy(src_ref, src_row, dst_ref, dst_row, sem):
    return pltpu.make_async_copy(src_ref.at[pl.ds(src_row, 1)], dst_ref.at[pl.ds(dst_row, 1)], sem)


def _dispatch_kernel(pads_ref, pos_ref, h_ref, xs_ref, zero_ref, sem, *, tm, tile):
    @pl.when(pl.program_id(0) == 0)
    def _():
        zero_ref[...] = jnp.zeros(zero_ref.shape, zero_ref.dtype)
        n_tiles = xs_ref.shape[0] // tile
        first_unused = pads_ref[2 * N_EXPERTS - 1] // tile

        def tile_copy(j):
            return pltpu.make_async_copy(zero_ref, xs_ref.at[pl.ds(pl.multiple_of(j * tile, tile), tile)], sem)

        def fill_tile(j, c):
            tile_copy(j).start()
            return c

        def drain_tile(j, c):
            tile_copy(j).wait()
            return c

        lax.fori_loop(first_unused, n_tiles, fill_tile, 0)
        lax.fori_loop(first_unused, n_tiles, drain_tile, 0)

        def per_expert(e, carry):
            lo, hi = pads_ref[2 * e], pads_ref[2 * e + 1]

            def fill(r, c):
                _row_copy(zero_ref, 0, xs_ref, r, sem).start()
                return c

            def drain(r, c):
                _row_copy(zero_ref, 0, xs_ref, r, sem).wait()
                return c

            lax.fori_loop(lo, hi, fill, 0)
            lax.fori_loop(lo, hi, drain, 0)
            return carry

        lax.fori_loop(0, N_EXPERTS, per_expert, 0)

    def issue(i, carry):
        _row_copy(h_ref, i, xs_ref, pos_ref[0, 0, 2 * i], sem).start()
        _row_copy(h_ref, i, xs_ref, pos_ref[0, 0, 2 * i + 1], sem).start()
        return carry

    lax.fori_loop(0, tm, issue, 0, unroll=8)
    for _ in range(2):
        pltpu.make_async_copy(h_ref, xs_ref.at[pl.ds(0, tm)], sem).wait()


def _dispatch(h2p, pos, pads, n_rows, tm, tile):
    T, width = h2p.shape
    grid_spec = pltpu.PrefetchScalarGridSpec(
        num_scalar_prefetch=1,
        grid=(T // tm,),
        in_specs=[pl.BlockSpec((1, 1, 2 * tm), lambda i, pads: (i, 0, 0), memory_space=pltpu.SMEM),
                  pl.BlockSpec((tm, width), lambda i, pads: (i, 0))],
        out_specs=pl.BlockSpec(memory_space=pl.ANY),
        scratch_shapes=[pltpu.VMEM((tile, width), h2p.dtype), pltpu.SemaphoreType.DMA(())],
    )
    return pl.pallas_call(
        functools.partial(_dispatch_kernel, tm=tm, tile=tile),
        grid_spec=grid_spec,
        out_shape=jax.ShapeDtypeStruct((n_rows, width), h2p.dtype),
        compiler_params=_cparams(1),
        name="moe_dispatch",
    )(pads, pos.reshape(T // tm, 1, 2 * tm), h2p)


def _experts_kernel(te_ref, nu_ref, xs_ref, wgu_ref, wd_ref, ys_ref, wgu_bf_ref, wd_bf_ref, *, d_expert):
    j = pl.program_id(0)

    @pl.when(j >= nu_ref[0])
    def _():
        ys_ref[...] = jnp.zeros(ys_ref.shape, F32)

    @pl.when((j == 0) | (te_ref[j] != te_ref[jnp.maximum(j - 1, 0)]))
    def _():
        wgu_bf_ref[...] = wgu_ref[0].astype(BF16)
        wd_bf_ref[...] = wd_ref[0].astype(BF16)

    @pl.when(j < nu_ref[0])
    def _():
        gu = jnp.dot(xs_ref[...].astype(BF16), wgu_bf_ref[...], preferred_element_type=F32)
        act = _silu(gu[:, :d_expert]) * gu[:, d_expert:]
        ys_ref[...] = jnp.dot(act.astype(BF16), wd_bf_ref[...], preferred_element_type=F32)


def _experts(xs, tile_expert, n_used, w_gate_up, w_down, tile):
    n_rows, width = xs.shape
    _, D, two_f = w_gate_up.shape
    grid_spec = pltpu.PrefetchScalarGridSpec(
        num_scalar_prefetch=2,
        grid=(n_rows // tile,),
        in_specs=[pl.BlockSpec((tile, width), lambda j, te, nu: (jnp.minimum(j, nu[0] - 1), 0)),
                  pl.BlockSpec((1, D, two_f), lambda j, te, nu: (te[j], 0, 0)),
                  pl.BlockSpec((1, two_f // 2, D), lambda j, te, nu: (te[j], 0, 0))],
        out_specs=pl.BlockSpec((tile, D), lambda j, te, nu: (j, 0)),
        scratch_shapes=[pltpu.VMEM((D, two_f), BF16), pltpu.VMEM((two_f // 2, D), BF16)],
    )
    return pl.pallas_call(
        functools.partial(_experts_kernel, d_expert=two_f // 2),
        grid_spec=grid_spec,
        out_shape=jax.ShapeDtypeStruct((n_rows, D), F32),
        compiler_params=_cparams(1),
        name="moe_experts",
    )(tile_expert, n_used, xs, w_gate_up, w_down)


def _combine_kernel(pos_ref, pos_next_ref, ys_ref, route_ref, x1_ref, mod_ref, gf_ref, y_ref,
                    ya_ref, yb_ref, sems, *, tm):
    step = pl.program_id(0) * pl.num_programs(1) + pl.program_id(1)
    n_steps = pl.num_programs(0) * pl.num_programs(1)
    slot = step % 2

    def start_gathers(p_ref, s):
        def issue(i, carry):
            _row_copy(ys_ref, p_ref[0, 0, 2 * i], ya_ref.at[s], i, sems.at[s]).start()
            _row_copy(ys_ref, p_ref[0, 0, 2 * i + 1], yb_ref.at[s], i, sems.at[s]).start()
            return carry

        lax.fori_loop(0, tm, issue, 0, unroll=8)

    @pl.when(step == 0)
    def _():
        start_gathers(pos_ref, 0)

    @pl.when(step + 1 < n_steps)
    def _():
        start_gathers(pos_next_ref, 1 - slot)

    for buf in (ya_ref, yb_ref):
        pltpu.make_async_copy(ys_ref.at[pl.ds(0, tm)], buf.at[slot], sems.at[slot]).wait()
    route = route_ref[0]
    lane = lax.broadcasted_iota(jnp.int32, route.shape, 1)
    w1 = jnp.sum(jnp.where(lane == 2, route, 0.0), axis=-1, keepdims=True)
    w2 = jnp.sum(jnp.where(lane == 3, route, 0.0), axis=-1, keepdims=True)
    x2 = x1_ref[0] + mod_ref[0, 5] * (w1 * ya_ref[slot] + w2 * yb_ref[slot])
    var = jnp.mean(x2 * x2, axis=-1, keepdims=True)
    y_ref[0] = x2 * lax.rsqrt(var + EPS) * gf_ref[...]


def _combine(ys, pos, route, x1, mod, gf, tm):
    B, S, D = x1.shape
    n_t = S // tm
    tok = lambda width: pl.BlockSpec((1, tm, width), lambda b, i: (b, i, 0))
    last = B * n_t - 1
    pos_spec = lambda ahead: pl.BlockSpec(
        (1, 1, 2 * tm), lambda b, i: (jnp.minimum(b * n_t + i + ahead, last), 0, 0), memory_space=pltpu.SMEM)
    pos_steps = pos.reshape(B * n_t, 1, 2 * tm)
    return pl.pallas_call(
        functools.partial(_combine_kernel, tm=tm),
        grid=(B, n_t),
        in_specs=[pos_spec(0), pos_spec(1),
                  pl.BlockSpec(memory_space=pl.ANY),
                  tok(ROUTER_LANES), tok(D),
                  pl.BlockSpec((1, N_COND, 1, D), lambda b, i: (b, 0, 0, 0)),
                  _full((1, D))],
        out_specs=tok(D),
        out_shape=jax.ShapeDtypeStruct((B, S, D), F32),
        scratch_shapes=[pltpu.VMEM((2, tm, D), F32), pltpu.VMEM((2, tm, D), F32),
                        pltpu.SemaphoreType.DMA((2,))],
        compiler_params=_cparams(2),
        name="moe_combine",
    )(pos_steps, pos_steps, ys, route, x1, mod, gf)


def _rope_tables(pos):
    half = HEAD_DIM // 2
    inv = ROPE_THETA ** (-jnp.arange(half, dtype=F32) * 2.0 / HEAD_DIM)
    ang = pos.astype(F32)[:, None] * inv[None, :]
    cos, sin = jnp.cos(ang), jnp.sin(ang)
    reps = LANE // HEAD_DIM
    return (jnp.tile(jnp.concatenate([cos, cos], axis=-1), (1, reps)),
            jnp.tile(jnp.concatenate([-sin, sin], axis=-1), (1, reps)))


def kernel(x_prompt, x_sample, cache_kv_w128, cache_kv_w512, cache_kv_w2048, state_pool, c_prompt, c_sample, norm1_g, w_ada, b_ada, w_in, w_attn_out, w_pool, pool_scale, w_pool_out, w_o, norm2_g, w_grp, b_grp, w_exp_router, b_exp_router, w_gate_up, w_down, final_norm_g):
    B, S, D = x_prompt.shape
    N, T, _ = x_sample.shape
    depth = norm1_g.shape[0]
    assert depth == 1, "single trunk layer"
    tm = min(512, S)
    assert S % tm == 0 and all(S % (SPAN * d) == 0 for d in ATTN_DILATIONS)

    n_qkvu = 3 * ATTN_WIDTH + POOL_WIDTH
    w_qkvu = w_in[0, :, :n_qkvu].astype(BF16)
    w_gates = w_in[0, :, n_qkvu:].astype(BF16)
    wa, wpo, wo = w_attn_out[0].astype(BF16), w_pool_out[0].astype(BF16), w_o[0].astype(BF16)
    wpool = w_pool[0].astype(BF16)
    pscale = pool_scale[0].reshape(1, POOL_WIDTH)
    w_r = jnp.concatenate([w_exp_router[0], w_grp[0]], axis=1)
    w_r = jnp.pad(w_r, ((0, 0), (0, ROUTER_LANES - w_r.shape[1])))
    wr_hi = w_r.astype(BF16)
    wr_lo = (w_r - wr_hi.astype(F32)).astype(BF16)
    b_r = jnp.pad(jnp.concatenate([b_exp_router[0], b_grp[0]]), (0, ROUTER_LANES - N_EXPERTS - MOE_GROUPS))
    b_r = b_r.reshape(1, ROUTER_LANES)
    wgu, wd = w_gate_up[0], w_down[0]
    g1, g2, gf = norm1_g[0].reshape(1, D), norm2_g[0].reshape(1, D), final_norm_g.reshape(1, D)

    mod = _ada(jnp.concatenate([c_prompt, c_sample], axis=0), w_ada[0], b_ada[0])
    mod_p = mod[:B].reshape(B, N_COND, 1, D)
    mod_s = jnp.repeat(mod[B:].reshape(N, N_COND, D), T, axis=0)
    mod_s = jnp.transpose(mod_s, (1, 0, 2)).reshape(1, N_COND, N * T, D)

    cos_p, sin_p = _rope_tables(jnp.arange(S, dtype=jnp.int32))
    outs = _in_proj_prompt(x_prompt, mod_p, g1, w_qkvu, cos_p, sin_p, wpool, pscale, tm)
    qs, ks, vs = outs[0:3], outs[3:6], outs[6:9]
    pooled, kv0, kv1, kv2, ptail = outs[9:14]
    attn_parts = [_attn_prompt(qs[g], ks[g], vs[g], g, nq=8) for g in range(N_GROUPS)]
    attn_inputs = [o for o, _ in attn_parts] + [l for _, l in attn_parts]
    x1, h2p, route, counts = _merge(x_prompt, mod_p, g1, g2, attn_inputs, pooled, w_gates, wa, wpo, wo,
                                    wr_hi, wr_lo, b_r, tm, routed=True)
    pos, tile_expert, n_used, pads, n_tiles = _routing_tables(route.reshape(B * S, ROUTER_LANES), counts,
                                                              EXPERT_TILE)
    xs = _dispatch(h2p.reshape(B * S, D), pos, pads, n_tiles * EXPERT_TILE, tm, EXPERT_TILE)
    ys = _experts(xs, tile_expert, n_used, wgu, wd, EXPERT_TILE)
    y_prompt = _combine(ys, pos, route, x1, mod_p, gf, tm=min(512, S))

    def kv_shape(a):
        a = a.reshape(a.shape[0], 2, HEADS_PER_GROUP, HEAD_DIM, a.shape[3])
        return jnp.transpose(a, (0, 4, 1, 2, 3))[None]

    pool_prompt = ptail[:, POOL_HALO - POOL_STATE_LEN:, :][None]

    TS = N * T
    pos_s = PAST_LEN + jnp.arange(T, dtype=jnp.int32)
    cos_s, sin_s = _rope_tables(pos_s)
    cos_s, sin_s = jnp.tile(cos_s, (N, 1)), jnp.tile(sin_s, (N, 1))
    xs = x_sample.reshape(1, TS, D)
    q_s, kvn, u_s = _in_proj_sample(xs, mod_s, g1, w_qkvu, cos_s, sin_s)
    eye = jnp.eye(HEADS_PER_GROUP, dtype=BF16)
    qbd = jnp.einsum('ntghe,hk->ngthke', q_s.reshape(N, T, N_GROUPS, HEADS_PER_GROUP, HEAD_DIM), eye)
    qbd = qbd.reshape(N, N_GROUPS, T * HEADS_PER_GROUP, GROUP_WIDTH)
    caches = [jnp.transpose(c[0], (0, 2, 3, 4, 1)).reshape(N, 2, GROUP_WIDTH, c.shape[2])
              for c in (cache_kv_w128, cache_kv_w512, cache_kv_w2048)]
    attn_s, pooled_s, ko0, ko1, ko2, pool_s = _sample_step(
        qbd, kvn.reshape(N, T, 2 * ATTN_WIDTH), caches, state_pool[0], u_s.reshape(N, T, POOL_WIDTH), wpool, pscale)
    x1s, h2s, comb_s = _merge(xs, mod_s, g1, g2, [attn_s.reshape(1, TS, GROUP_WIDTH)],
                              pooled_s.reshape(1, TS, POOL_WIDTH), w_gates, wa, wpo, wo, wr_hi, wr_lo, b_r, TS,
                              routed=False)
    y_sample = _moe(h2s, comb_s, x1s, mod_s, gf, wgu, wd, tm=TS).reshape(N, T, D)

    return (y_prompt, y_sample, kv_shape(kv0), kv_shape(kv1), kv_shape(kv2), pool_prompt,
            kv_shape(ko0), kv_shape(ko1), kv_shape(ko2), pool_s[None])
```

```python
import functools

import jax
import jax.numpy as jnp
from jax import lax
from jax.experimental import pallas as pl
from jax.experimental.pallas import tpu as pltpu

F32 = jnp.float32
BF16 = jnp.bfloat16

HEAD_DIM = 64
HEADS_PER_GROUP = 4
HEAD_SHIFT = 2
LANE_HEAD_SHIFT = 6
GROUP_WIDTH = HEADS_PER_GROUP * HEAD_DIM
ATTN_WINDOWS = (128, 512, 2048)
ATTN_DILATIONS = (1, 4, 16)
N_GROUPS = 3
SPAN = 128
ATTN_WIDTH = N_GROUPS * GROUP_WIDTH
ROPE_THETA = 10000.0
PAST_LEN = 8192
POOL_WINDOWS = (2, 4, 8, 16)
POOL_GROUP_WIDTH = 128
POOL_WIDTH = 512
POOL_STATE_LEN = 15
POOL_HALO = 16
MOE_GROUPS = 4
EXPERTS_PER_GROUP = 8
N_EXPERTS = 32
N_COND = 6
EPS = 1e-6

LANE = 128
SUBLANE = 8
VMEM_LIMIT_BYTES = 56 * 1024 * 1024

NEG = -1e30
BIG_LANE = 1e9

ROUTER_LANES = LANE
GROUP_LANE0 = N_EXPERTS


def _cparams(n_axes):
    return pltpu.CompilerParams(dimension_semantics=("arbitrary",) * n_axes,
                                vmem_limit_bytes=VMEM_LIMIT_BYTES)


def _full(shape):
    nd = len(shape)
    return pl.BlockSpec(tuple(shape), lambda *_: (0,) * nd)


def _norm_mod(x, g, scale, shift):
    var = jnp.mean(x * x, axis=-1, keepdims=True)
    return (x * lax.rsqrt(var + EPS) * g) * (1.0 + scale) + shift


def _rope(x, cos, sin):
    lane = lax.broadcasted_iota(jnp.int32, (x.shape[0], LANE), 1)
    first_half = (lane & (HEAD_DIM - 1)) < (HEAD_DIM // 2)
    outs = []
    for c in range(x.shape[1] // LANE):
        xc = x[:, c * LANE:(c + 1) * LANE]
        partner = jnp.where(first_half, pltpu.roll(xc, LANE - HEAD_DIM // 2, 1),
                            pltpu.roll(xc, HEAD_DIM // 2, 1))
        outs.append(xc * cos + partner * sin)
    return jnp.concatenate(outs, axis=1)


def _silu(x):
    return x * jax.nn.sigmoid(x)


def _ada_kernel(c_ref, w_ref, b_ref, o_ref):
    s = _silu(c_ref[...]).astype(BF16)
    o_ref[...] = jnp.dot(s, w_ref[...].astype(BF16), preferred_element_type=F32) + b_ref[...]


def _ada(c_all, w_ada, b_ada):
    rows, d = c_all.shape
    n_out = w_ada.shape[1]
    tn = 1024
    return pl.pallas_call(
        _ada_kernel,
        grid=(n_out // tn,),
        in_specs=[_full((rows, d)),
                  pl.BlockSpec((d, tn), lambda j: (0, j)),
                  pl.BlockSpec((1, tn), lambda j: (0, j))],
        out_specs=pl.BlockSpec((rows, tn), lambda j: (0, j)),
        out_shape=jax.ShapeDtypeStruct((rows, n_out), F32),
        compiler_params=_cparams(1),
        name="ada",
    )(c_all, w_ada, b_ada.reshape(1, n_out))


def _in_proj_prompt_kernel(x_ref, mod_ref, g1_ref, w_ref, cos_ref, sin_ref, wpool_ref, pscale_ref, *rest,
                           tm, n_tiles, tails):
    qkv_refs = (rest[0:3], rest[3:6], rest[6:9])
    pooled_ref, kv0_ref, kv1_ref, kv2_ref, ptail_ref = rest[9:14]
    stage_refs = rest[14:17]
    ue_ref = rest[17]
    i = pl.program_id(1)
    x = x_ref[0]
    h = _norm_mod(x, g1_ref[...], mod_ref[0, 1], mod_ref[0, 0]).astype(BF16)
    cos = cos_ref[...]
    sin = sin_ref[...]
    y = jnp.dot(h, w_ref[...], preferred_element_type=F32)
    q = _rope(y[:, :ATTN_WIDTH], cos, sin) * (HEAD_DIM ** -0.5)
    k = _rope(y[:, ATTN_WIDTH:2 * ATTN_WIDTH], cos, sin)
    v = y[:, 2 * ATTN_WIDTH:3 * ATTN_WIDTH]
    u = y[:, 3 * ATTN_WIDTH:3 * ATTN_WIDTH + POOL_WIDTH]

    chunks = GROUP_WIDTH // LANE
    for val, out_refs, stage_ref in zip((q, k, v), qkv_refs, stage_refs):
        for g, d in enumerate(ATTN_DILATIONS):
            gl = slice(g * GROUP_WIDTH, (g + 1) * GROUP_WIDTH)
            if d == 1:
                out_refs[g][0] = val[:, gl].astype(BF16)
                continue
            for c in range(chunks):
                stage_ref[g * chunks + c] = val[:, g * GROUP_WIDTH + c * LANE:g * GROUP_WIDTH + (c + 1) * LANE]
            for r in range(d):
                for c in range(chunks):
                    out_refs[g][0, :, r * GROUP_WIDTH + c * LANE:r * GROUP_WIDTH + (c + 1) * LANE] = (
                        stage_ref[g * chunks + c, pl.ds(r, tm // d, stride=d), :].astype(BF16))

    for g, kv_ref in enumerate((kv0_ref, kv1_ref, kv2_ref)):
        first_tile, rows = tails[g]
        lo = tm - rows

        @pl.when(i >= first_tile)
        def _(kv_ref=kv_ref, g=g, lo=lo):
            kv_ref[0, 0] = k[lo:, g * GROUP_WIDTH:(g + 1) * GROUP_WIDTH].T
            kv_ref[0, 1] = v[lo:, g * GROUP_WIDTH:(g + 1) * GROUP_WIDTH].T

    @pl.when(i == 0)
    def _():
        ue_ref[0:POOL_HALO, :] = jnp.zeros((POOL_HALO, POOL_WIDTH), F32)

    ue_ref[POOL_HALO:, :] = u
    pos1 = (i * tm + 1 + lax.broadcasted_iota(jnp.int32, (tm, 1), 0)).astype(F32)
    for gi, w in enumerate(POOL_WINDOWS):
        lanes = slice(gi * POOL_GROUP_WIDTH, (gi + 1) * POOL_GROUP_WIDTH)
        acc = ue_ref[pl.ds(POOL_HALO, tm), lanes]
        for j in range(1, w):
            acc = acc + ue_ref[pl.ds(POOL_HALO - j, tm), lanes]
        mean = acc / jnp.minimum(float(w), pos1)
        z = jnp.dot((mean - u[:, lanes]).astype(BF16), wpool_ref[gi], preferred_element_type=F32)
        pooled_ref[0, :, lanes] = (z * pscale_ref[:, lanes]).astype(BF16)
    ue_ref[0:POOL_HALO, :] = u[tm - POOL_HALO:, :]

    @pl.when(i == n_tiles - 1)
    def _():
        ptail_ref[0] = u[tm - POOL_HALO:, :]


def _in_proj_prompt(x, mod, g1, w_qkvu, cos, sin, w_pool, pool_scale, tm):
    B, S, D = x.shape
    n_tiles = S // tm
    n_w = w_qkvu.shape[1]
    tails, kv_shapes, kv_specs = [], [], []
    for W in ATTN_WINDOWS:
        Wg = min(W, S)
        if Wg >= tm:
            first = n_tiles - Wg // tm
            rows = tm
        else:
            first = n_tiles - 1
            rows = Wg
        tails.append((first, rows))
        kv_shapes.append(jax.ShapeDtypeStruct((B, 2, GROUP_WIDTH, Wg), F32))
        kv_specs.append(pl.BlockSpec((1, 2, GROUP_WIDTH, rows),
                                     lambda b, i, first=first: (b, 0, 0, jnp.maximum(i - first, 0))))
    tok = lambda width: pl.BlockSpec((1, tm, width), lambda b, i: (b, i, 0))
    dil_specs = [pl.BlockSpec((1, tm // d, d * GROUP_WIDTH), lambda b, i: (b, i, 0)) for d in ATTN_DILATIONS]
    dil_shapes = [jax.ShapeDtypeStruct((B, S // d, d * GROUP_WIDTH), BF16) for d in ATTN_DILATIONS]
    outs = pl.pallas_call(
        functools.partial(_in_proj_prompt_kernel, tm=tm, n_tiles=n_tiles, tails=tuple(tails)),
        grid=(B, n_tiles),
        in_specs=[tok(D),
                  pl.BlockSpec((1, N_COND, 1, D), lambda b, i: (b, 0, 0, 0)),
                  _full((1, D)),
                  _full((D, n_w)),
                  pl.BlockSpec((tm, LANE), lambda b, i: (i, 0)),
                  pl.BlockSpec((tm, LANE), lambda b, i: (i, 0)),
                  _full(w_pool.shape),
                  _full((1, POOL_WIDTH))],
        out_specs=dil_specs * 3 + [tok(POOL_WIDTH)] + kv_specs
                  + [pl.BlockSpec((1, POOL_HALO, POOL_WIDTH), lambda b, i: (b, 0, 0))],
        out_shape=dil_shapes * 3
                  + [jax.ShapeDtypeStruct((B, S, POOL_WIDTH), BF16)] + kv_shapes
                  + [jax.ShapeDtypeStruct((B, POOL_HALO, POOL_WIDTH), F32)],
        scratch_shapes=[pltpu.VMEM((ATTN_WIDTH // LANE, tm, LANE), F32)] * 3
                       + [pltpu.VMEM((tm + POOL_HALO, POOL_WIDTH), F32)],
        compiler_params=_cparams(2),
        name="in_proj_prompt",
    )(x, mod, g1, w_qkvu, cos, sin, w_pool, pool_scale)
    return outs


def _in_proj_sample_kernel(x_ref, mod_ref, g1_ref, w_ref, cos_ref, sin_ref, q_ref, kvn_ref, u_ref):
    h = _norm_mod(x_ref[0], g1_ref[...], mod_ref[0, 1], mod_ref[0, 0]).astype(BF16)
    y = jnp.dot(h, w_ref[...], preferred_element_type=F32)
    cos = cos_ref[...]
    sin = sin_ref[...]
    q = _rope(y[:, :ATTN_WIDTH], cos, sin) * (HEAD_DIM ** -0.5)
    k = _rope(y[:, ATTN_WIDTH:2 * ATTN_WIDTH], cos, sin)
    v = y[:, 2 * ATTN_WIDTH:3 * ATTN_WIDTH]
    q_ref[...] = q.astype(BF16)
    for g in range(N_GROUPS):
        gl = slice(g * GROUP_WIDTH, (g + 1) * GROUP_WIDTH)
        kvn_ref[:, 2 * g * GROUP_WIDTH:(2 * g + 1) * GROUP_WIDTH] = k[:, gl]
        kvn_ref[:, (2 * g + 1) * GROUP_WIDTH:(2 * g + 2) * GROUP_WIDTH] = v[:, gl]
    u_ref[...] = y[:, 3 * ATTN_WIDTH:3 * ATTN_WIDTH + POOL_WIDTH]


def _in_proj_sample(x, mod, g1, w_qkvu, cos, sin):
    _, T, D = x.shape
    n_w = w_qkvu.shape[1]
    return pl.pallas_call(
        _in_proj_sample_kernel,
        grid=(1,),
        in_specs=[_full((1, T, D)), _full((1, N_COND, T, D)), _full((1, D)), _full((D, n_w)),
                  _full((T, LANE)), _full((T, LANE))],
        out_specs=[_full((T, ATTN_WIDTH)), _full((T, 2 * ATTN_WIDTH)), _full((T, POOL_WIDTH))],
        out_shape=[jax.ShapeDtypeStruct((T, ATTN_WIDTH), BF16),
                   jax.ShapeDtypeStruct((T, 2 * ATTN_WIDTH), F32),
                   jax.ShapeDtypeStruct((T, POOL_WIDTH), F32)],
        compiler_params=_cparams(1),
        name="in_proj_sample",
    )(x, mod, g1, w_qkvu, cos, sin)


def _attn_kernel(q_ref, kc_ref, kp_ref, vc_ref, vp_ref, o_ref, lse_ref, kbuf, vbuf, *, nq):
    n0 = pl.program_id(1)
    kbuf[0:SPAN, :] = kp_ref[0]
    kbuf[SPAN:, :] = kc_ref[0]
    vbuf[0:SPAN, :] = vp_ref[0]
    vbuf[SPAN:, :] = vc_ref[0]
    qi = lax.broadcasted_iota(jnp.int32, (SPAN, 2 * SPAN), 0)
    kj = lax.broadcasted_iota(jnp.int32, (SPAN, 2 * SPAN), 1)
    band = (kj >= qi) & (kj <= qi + SPAN)
    band_first = band & (kj >= jnp.where(n0 > 0, 0, SPAN))
    lane = lax.broadcasted_iota(jnp.int32, (SPAN, LANE), 1)
    low_head = lane < HEAD_DIM
    for j in range(nq):
        valid = band_first if j == 0 else band
        rows = slice(j * SPAN, (j + 1) * SPAN)
        krows = slice(j * SPAN, (j + 2) * SPAN)
        for c in range(GROUP_WIDTH // LANE):
            cl = slice(c * LANE, (c + 1) * LANE)
            q = q_ref[0, rows, cl]
            kk = kbuf[krows, cl]
            vv = vbuf[krows, cl]
            o_pair, lse_pair = [], []
            for hh in range(2):
                mask_h = low_head if hh == 0 else jnp.logical_not(low_head)
                qm = jnp.where(mask_h, q, jnp.zeros_like(q))
                s = lax.dot_general(qm, kk, (((1,), (1,)), ((), ())), preferred_element_type=F32)
                s = jnp.where(valid, s, NEG)
                m = jnp.max(s, axis=-1, keepdims=True)
                p = jnp.exp(s - m)
                den = jnp.sum(p, axis=-1, keepdims=True)
                o_pair.append(jnp.dot(p.astype(BF16), vv, preferred_element_type=F32) / den)
                lse_pair.append(m + jnp.log(den))
            o_ref[0, rows, cl] = jnp.where(low_head, o_pair[0], o_pair[1]).astype(BF16)
            lse_ref[0, rows, cl] = jnp.where(low_head, lse_pair[0], lse_pair[1])


def _attn_prompt(qv, kv, vv, g, nq):
    B, M, width = qv.shape
    d = width // GROUP_WIDTH
    nq = min(nq, M // SPAN)
    cur = pl.BlockSpec((1, nq * SPAN, GROUP_WIDTH), lambda b, n, r: (b, n, r))
    prev = pl.BlockSpec((1, SPAN, GROUP_WIDTH), lambda b, n, r: (b, jnp.maximum(n * nq - 1, 0), r))
    out = cur
    return pl.pallas_call(
        functools.partial(_attn_kernel, nq=nq),
        grid=(B, M // (nq * SPAN), d),
        in_specs=[cur, cur, prev, cur, prev],
        out_specs=[out, out],
        out_shape=[jax.ShapeDtypeStruct((B, M, d * GROUP_WIDTH), BF16),
                   jax.ShapeDtypeStruct((B, M, d * GROUP_WIDTH), F32)],
        scratch_shapes=[pltpu.VMEM(((nq + 1) * SPAN, GROUP_WIDTH), BF16)] * 2,
        compiler_params=_cparams(3),
        name=f"attn_prompt_g{g}",
    )(qv, kv, kv, vv, vv)


def _sample_kernel(qbd_ref, kvn_ref, c0_ref, c1_ref, c2_ref, sp_ref, u_ref, wpool_ref, pscale_ref,
                   attn_ref, pooled_ref, o0_ref, o1_ref, o2_ref, po_ref, ue_ref, diff_ref, nt_ref, *, t_new):
    rows = HEADS_PER_GROUP * t_new
    caches = (c0_ref, c1_ref, c2_ref)
    outs = (o0_ref, o1_ref, o2_ref)

    s_cache, s_new, valid_new = [], [], []
    m = jnp.full((rows, 1), NEG, F32)
    for g in range(N_GROUPS):
        d = ATTN_DILATIONS[g]
        c_ref = caches[g]
        L = c_ref.shape[3]
        qb = qbd_ref[0, g]
        s = jnp.dot(qb, c_ref[0, 0].astype(BF16), preferred_element_type=F32)
        t_row = lax.broadcasted_iota(jnp.int32, (rows, L), 0) >> HEAD_SHIFT
        delta = L + t_row - lax.broadcasted_iota(jnp.int32, (rows, L), 1)
        ok = ((delta & (d - 1)) == 0) & (delta <= SPAN * d)
        s = jnp.where(ok, s, NEG)
        s_cache.append(s)
        m = jnp.maximum(m, jnp.max(s, axis=-1, keepdims=True))
        qf = qb.astype(F32)
        t_col = lax.broadcasted_iota(jnp.int32, (rows, 1), 0) >> HEAD_SHIFT
        sn, okn = [], []
        for tn in range(t_new):
            kn = kvn_ref[0, tn:tn + 1, 2 * g * GROUP_WIDTH:(2 * g + 1) * GROUP_WIDTH]
            kn = kn.astype(BF16).astype(F32)
            dn = t_col - tn
            ok_n = (dn >= 0) & ((dn & (d - 1)) == 0)
            s1 = jnp.where(ok_n, jnp.sum(qf * kn, axis=-1, keepdims=True), NEG)
            sn.append(s1)
            okn.append(ok_n)
            m = jnp.maximum(m, s1)
        s_new.append(sn)
        valid_new.append(okn)

    den = jnp.zeros((rows, 1), F32)
    acc = jnp.zeros((rows, GROUP_WIDTH), F32)
    for g in range(N_GROUPS):
        c_ref = caches[g]
        p = jnp.exp(s_cache[g] - m)
        den = den + jnp.sum(p, axis=-1, keepdims=True)
        acc = acc + lax.dot_general(p.astype(BF16), c_ref[0, 1].astype(BF16), (((1,), (1,)), ((), ())),
                                    preferred_element_type=F32)
        for tn in range(t_new):
            pn = jnp.exp(s_new[g][tn] - m)
            den = den + pn
            vn = kvn_ref[0, tn:tn + 1, (2 * g + 1) * GROUP_WIDTH:(2 * g + 2) * GROUP_WIDTH]
            acc = acc + pn * vn
    row_head = lax.broadcasted_iota(jnp.int32, (rows, GROUP_WIDTH), 0) & (HEADS_PER_GROUP - 1)
    lane_head = lax.broadcasted_iota(jnp.int32, (rows, GROUP_WIDTH), 1) >> LANE_HEAD_SHIFT
    o_diag = jnp.where(row_head == lane_head, acc / den, 0.0).astype(BF16)
    sel = ((lax.broadcasted_iota(jnp.int32, (rows, rows), 1) >> HEAD_SHIFT)
           == lax.broadcasted_iota(jnp.int32, (rows, rows), 0)).astype(BF16)
    attn = jnp.dot(sel, o_diag, preferred_element_type=F32)
    attn_ref[0] = attn[0:t_new].astype(BF16)

    tail_lane = lax.broadcasted_iota(jnp.int32, (GROUP_WIDTH, LANE), 1)
    nt_ref[...] = jnp.zeros(nt_ref.shape, F32)
    for g in range(N_GROUPS):
        c_ref, o_ref = caches[g], outs[g]
        L = c_ref.shape[3]
        for kv in range(2):
            col0 = (2 * g + kv) * GROUP_WIDTH
            nt_ref[LANE - t_new:LANE, :] = kvn_ref[0, :, col0:col0 + GROUP_WIDTH]
            new_t = nt_ref[...].T
            rolled = pltpu.roll(c_ref[0, kv], L - t_new, 1)
            if L > LANE:
                o_ref[0, kv, :, 0:L - LANE] = rolled[:, 0:L - LANE]
            o_ref[0, kv, :, L - LANE:L] = jnp.where(tail_lane >= LANE - t_new, new_t, rolled[:, L - LANE:L])

    ue_ref[0:POOL_STATE_LEN, :] = sp_ref[0]
    ue_ref[POOL_STATE_LEN:POOL_STATE_LEN + t_new, :] = u_ref[0]
    diff_ref[...] = jnp.zeros(diff_ref.shape, F32)
    for tn in range(t_new):
        r = POOL_STATE_LEN + tn
        for gi, w in enumerate(POOL_WINDOWS):
            lanes = slice(gi * POOL_GROUP_WIDTH, (gi + 1) * POOL_GROUP_WIDTH)
            win = jnp.sum(ue_ref[r - w + 1:r + 1, lanes], axis=0, keepdims=True)
            diff_ref[tn:tn + 1, lanes] = win / float(w) - ue_ref[r:r + 1, lanes]
    for gi in range(len(POOL_WINDOWS)):
        lanes = slice(gi * POOL_GROUP_WIDTH, (gi + 1) * POOL_GROUP_WIDTH)
        z = jnp.dot(diff_ref[:, lanes].astype(BF16), wpool_ref[gi], preferred_element_type=F32)
        pooled_ref[0, :, lanes] = (z[0:t_new] * pscale_ref[:, lanes]).astype(BF16)
    po_ref[0] = ue_ref[t_new:t_new + POOL_STATE_LEN, :]


def _sample_step(qbd, kvn, caches, state_pool, u, w_pool, pool_scale):
    N, t_new, _ = u.shape
    rows = HEADS_PER_GROUP * t_new
    per_n = lambda shape: pl.BlockSpec((1,) + tuple(shape), lambda n: (n,) + (0,) * len(shape))
    cache_specs = [per_n(c.shape[1:]) for c in caches]
    return pl.pallas_call(
        functools.partial(_sample_kernel, t_new=t_new),
        grid=(N,),
        in_specs=[per_n(qbd.shape[1:]), per_n(kvn.shape[1:])] + cache_specs
                 + [per_n(state_pool.shape[1:]), per_n(u.shape[1:]), _full(w_pool.shape), _full((1, POOL_WIDTH))],
        out_specs=[per_n((t_new, GROUP_WIDTH)), per_n((t_new, POOL_WIDTH))] + cache_specs
                  + [per_n(state_pool.shape[1:])],
        out_shape=[jax.ShapeDtypeStruct((N, t_new, GROUP_WIDTH), BF16),
                   jax.ShapeDtypeStruct((N, t_new, POOL_WIDTH), BF16)]
                  + [jax.ShapeDtypeStruct(c.shape, F32) for c in caches]
                  + [jax.ShapeDtypeStruct(state_pool.shape, F32)],
        scratch_shapes=[pltpu.VMEM((POOL_STATE_LEN + t_new + 5, POOL_WIDTH), F32),
                        pltpu.VMEM((rows, POOL_WIDTH), F32),
                        pltpu.VMEM((LANE, GROUP_WIDTH), F32)],
        compiler_params=_cparams(1),
        name="sample_step",
    )(qbd, kvn, *caches, state_pool, u, w_pool, pool_scale)


def _route(logits):
    lane = lax.broadcasted_iota(jnp.int32, logits.shape, 1).astype(F32)
    is_grp = (lane >= GROUP_LANE0) & (lane < GROUP_LANE0 + MOE_GROUPS)
    gl = jnp.where(is_grp, logits, NEG)
    gmax = jnp.max(gl, axis=-1, keepdims=True)
    gidx = jnp.min(jnp.where(gl == gmax, lane, BIG_LANE), axis=-1, keepdims=True) - GROUP_LANE0
    gsum = jnp.sum(jnp.where(is_grp, jnp.exp(gl - gmax), 0.0), axis=-1, keepdims=True)
    grp_w = 1.0 / gsum
    lo = gidx * EXPERTS_PER_GROUP
    in_grp = (lane >= lo) & (lane < lo + EXPERTS_PER_GROUP)
    el = jnp.where(in_grp, logits, NEG)
    v1 = jnp.max(el, axis=-1, keepdims=True)
    i1 = jnp.min(jnp.where(el == v1, lane, BIG_LANE), axis=-1, keepdims=True)
    el2 = jnp.where(lane == i1, NEG, el)
    v2 = jnp.max(el2, axis=-1, keepdims=True)
    i2 = jnp.min(jnp.where(el2 == v2, lane, BIG_LANE), axis=-1, keepdims=True)
    e = jnp.exp(v2 - v1)
    w1 = grp_w / (1.0 + e)
    w2 = grp_w * e / (1.0 + e)
    return lane, i1, i2, w1, w2


def _merge_kernel(*refs, n_groups):
    x_ref, mod_ref, g1_ref, g2_ref = refs[0:4]
    n_attn = 2 * n_groups if n_groups > 1 else 1
    attn_refs = refs[4:4 + n_attn]
    rest = refs[4 + n_attn:]
    pooled_ref, wg_ref, wa_ref, wp_ref, wo_ref, wrh_ref, wrc_ref, br_ref = rest[:8]
    x1_ref, h2_ref, route_ref, counts_ref = rest[8:12]
    scratch_refs = rest[12:]
    stage_refs = scratch_refs[:n_attn] if n_groups > 1 else ()
    carry_ref = scratch_refs[-1]

    x = x_ref[0]
    tm = x.shape[0]
    if n_groups > 1:
        vals = []
        for idx, ref in enumerate(attn_refs):
            d = ATTN_DILATIONS[idx % n_groups]
            if d == 1:
                vals.append(ref[0].astype(F32))
                continue
            stage_ref = stage_refs[idx]
            chunks = GROUP_WIDTH // LANE
            for r in range(d):
                for c in range(chunks):
                    stage_ref[c, pl.ds(r, tm // d, stride=d), :] = (
                        ref[0, :, r * GROUP_WIDTH + c * LANE:r * GROUP_WIDTH + (c + 1) * LANE].astype(F32))
            vals.append(jnp.concatenate([stage_ref[c] for c in range(chunks)], axis=1))
        os_, ls = vals[:n_groups], vals[n_groups:]
        lmax = functools.reduce(jnp.maximum, ls)
        es = [jnp.exp(l - lmax) for l in ls]
        attn = sum(e * o for e, o in zip(es, os_)) / sum(es)
    else:
        attn = attn_refs[0][0]
    a = jnp.dot(attn.astype(BF16), wa_ref[...], preferred_element_type=F32)
    p = jnp.dot(pooled_ref[0], wp_ref[...], preferred_element_type=F32)
    h = _norm_mod(x, g1_ref[...], mod_ref[0, 1], mod_ref[0, 0]).astype(BF16)
    gates = jnp.dot(h, wg_ref[...], preferred_element_type=F32)
    D = x.shape[1]
    merged = jax.nn.sigmoid(gates[:, :D]) * a + jax.nn.sigmoid(gates[:, D:]) * p
    y = jnp.dot(merged.astype(BF16), wo_ref[...], preferred_element_type=F32)
    x1 = x + mod_ref[0, 2] * y
    x1_ref[0] = x1
    h2 = _norm_mod(x1, g2_ref[...], mod_ref[0, 4], mod_ref[0, 3])
    h2_hi = h2.astype(BF16)
    h2_lo = (h2 - h2_hi.astype(F32)).astype(BF16)
    hi_terms = jnp.dot(h2_hi, wrc_ref[...], preferred_element_type=F32)
    logits = (hi_terms[:, :ROUTER_LANES] + jnp.dot(h2_lo, wrh_ref[...], preferred_element_type=F32)
              + hi_terms[:, ROUTER_LANES:]) + br_ref[...]
    lane, i1, i2, w1, w2 = _route(logits)
    h2_ref[0] = h2

    @pl.when((pl.program_id(0) == 0) & (pl.program_id(1) == 0))
    def _():
        carry_ref[...] = jnp.zeros(carry_ref.shape, F32)

    hit = ((lane == i1) | (lane == i2)).astype(BF16)
    ltri = (lax.broadcasted_iota(jnp.int32, (tm, tm), 0) >= lax.broadcasted_iota(jnp.int32, (tm, tm), 1))
    prefix = jnp.dot(ltri.astype(BF16), hit, preferred_element_type=F32) + carry_ref[...]
    rank1 = jnp.sum(jnp.where(lane == i1, prefix, 0.0), axis=-1, keepdims=True) - 1.0
    rank2 = jnp.sum(jnp.where(lane == i2, prefix, 0.0), axis=-1, keepdims=True) - 1.0
    carry_ref[...] = prefix[tm - 1:tm, :]
    counts_ref[...] = jnp.broadcast_to(prefix[tm - 1:tm, :], counts_ref.shape)
    cols = (i1, i2, w1, w2, rank1, rank2)
    route = jnp.zeros(logits.shape, F32)
    for c, col in enumerate(cols):
        route = jnp.where(lane == float(c), col, route)
    route_ref[0] = route


def _merge(x, mod, g1, g2, attn_inputs, pooled, w_gates, w_attn_out, w_pool_out, w_o, wr_hi, wr_hilo, b_r, tm):
    B, S, D = x.shape
    R = mod.shape[2]
    n_groups = len(attn_inputs) // 2 if len(attn_inputs) > 1 else 1
    tok = lambda width: pl.BlockSpec((1, tm, width), lambda b, i: (b, i, 0))
    if R == 1:
        mod_spec = pl.BlockSpec((1, N_COND, 1, D), lambda b, i: (b, 0, 0, 0))
    else:
        mod_spec = pl.BlockSpec((1, N_COND, tm, D), lambda b, i: (b, 0, i, 0))
    weights = (w_gates, w_attn_out, w_pool_out, w_o, wr_hi, wr_hilo, b_r)
    if n_groups > 1:
        attn_specs = [pl.BlockSpec((1, tm // (a.shape[2] // GROUP_WIDTH), a.shape[2]), lambda b, i: (b, i, 0))
                      for a in attn_inputs]
        scratch = [pltpu.VMEM((GROUP_WIDTH // LANE, tm, LANE), F32)] * len(attn_inputs)
    else:
        attn_specs = [tok(GROUP_WIDTH)]
        scratch = []
    out_specs = [tok(D), tok(D), tok(ROUTER_LANES), _full((SUBLANE, ROUTER_LANES))]
    out_shape = [jax.ShapeDtypeStruct((B, S, D), F32),
                 jax.ShapeDtypeStruct((B, S, D), F32),
                 jax.ShapeDtypeStruct((B, S, ROUTER_LANES), F32),
                 jax.ShapeDtypeStruct((SUBLANE, ROUTER_LANES), F32)]
    scratch = scratch + [pltpu.VMEM((1, ROUTER_LANES), F32)]
    return pl.pallas_call(
        functools.partial(_merge_kernel, n_groups=n_groups),
        grid=(B, S // tm),
        in_specs=[tok(D), mod_spec, _full((1, D)), _full((1, D))]
                 + attn_specs + [tok(POOL_WIDTH)]
                 + [_full(w.shape) for w in weights],
        out_specs=out_specs,
        out_shape=out_shape,
        scratch_shapes=scratch,
        compiler_params=_cparams(2),
        name="merge",
    )(x, mod, g1, g2, *attn_inputs, pooled, *weights)


EXPERT_TILE = 512
PAD_CHUNK = 32


def _routing_tables(routes, counts, tile):
    cnts = [c[0, :N_EXPERTS].astype(jnp.int32) for c in counts]
    cnt = sum(cnts)
    padded = ((cnt + tile - 1) // tile) * tile
    ends = jnp.cumsum(padded)
    base = ends - padded
    experts = jnp.arange(N_EXPERTS, dtype=jnp.int32)
    poss = []
    for route in routes:
        hit = route[:, 0:2].astype(jnp.int32)[:, :, None] == experts
        poss.append(jnp.sum(jnp.where(hit, base, 0), axis=-1) + route[:, 4:6].astype(jnp.int32))
        base = base + cnts[len(poss) - 1]
    n_pairs = 2 * sum(r.shape[0] for r in routes)
    n_tiles = -(-n_pairs // tile) + N_EXPERTS
    starts = jnp.arange(n_tiles, dtype=jnp.int32) * tile
    n_used = ends[-1] // tile
    tile_expert = jnp.sum(starts[:, None] >= ends[None, :], axis=1).astype(jnp.int32)
    last = jnp.take(tile_expert, n_used - 1)
    tile_expert = jnp.where(jnp.arange(n_tiles) < n_used, tile_expert, last)
    pads = jnp.stack([ends - padded + cnts[0], ends], axis=1).reshape(-1).astype(jnp.int32)
    return poss, tile_expert, n_used.reshape(1).astype(jnp.int32), pads, n_tiles


def _row_copy(src_ref, src_row, dst_ref, dst_row, sem):
    return pltpu.make_async_copy(src_ref.at[pl.ds(src_row, 1)], dst_ref.at[pl.ds(dst_row, 1)], sem)


def _dispatch_kernel(pads_ref, pos_ref, h_ref, *rest, tm, tile, first):
    xs_ref, zero_ref, sem = rest[-3:]

    def zero_fill():
        zero_ref[...] = jnp.zeros(zero_ref.shape, zero_ref.dtype)
        n_tiles = xs_ref.shape[0] // tile
        first_unused = pads_ref[2 * N_EXPERTS - 1] // tile

        def tile_copy(j):
            return pltpu.make_async_copy(zero_ref, xs_ref.at[pl.ds(pl.multiple_of(j * tile, tile), tile)], sem)

        def fill_tile(j, c):
            tile_copy(j).start()
            return c

        def drain_tile(j, c):
            tile_copy(j).wait()
            return c

        lax.fori_loop(first_unused, n_tiles, fill_tile, 0)
        lax.fori_loop(first_unused, n_tiles, drain_tile, 0)

        def chunk_copy(c):
            rows = pl.ds(pl.multiple_of(c * PAD_CHUNK, PAD_CHUNK), PAD_CHUNK)
            return pltpu.make_async_copy(zero_ref.at[pl.ds(0, PAD_CHUNK)], xs_ref.at[rows], sem)

        def for_each_pad_piece(on_row, on_chunk):
            def per_expert(e, carry):
                lo, hi = pads_ref[2 * e], pads_ref[2 * e + 1]
                first_chunk = (lo + PAD_CHUNK - 1) // PAD_CHUNK
                lax.fori_loop(lo, jnp.minimum(first_chunk * PAD_CHUNK, hi), on_row, 0)
                lax.fori_loop(first_chunk, hi // PAD_CHUNK, on_chunk, 0)
                return carry

            lax.fori_loop(0, N_EXPERTS, per_expert, 0)

        def start_row(r, c):
            _row_copy(zero_ref, 0, xs_ref, r, sem).start()
            return c

        def wait_row(r, c):
            _row_copy(zero_ref, 0, xs_ref, r, sem).wait()
            return c

        def start_chunk(c, carry):
            chunk_copy(c).start()
            return carry

        def wait_chunk(c, carry):
            chunk_copy(c).wait()
            return carry

        for_each_pad_piece(start_row, start_chunk)
        for_each_pad_piece(wait_row, wait_chunk)

    if first:
        pl.when(pl.program_id(0) == 0)(zero_fill)

    def issue(k, carry):
        base = pl.multiple_of(k * SUBLANE, SUBLANE)
        for j in range(SUBLANE):
            i = base + j
            _row_copy(h_ref, i, xs_ref, pos_ref[0, 0, 2 * i], sem).start()
            _row_copy(h_ref, i, xs_ref, pos_ref[0, 0, 2 * i + 1], sem).start()
        return carry

    lax.fori_loop(0, tm // SUBLANE, issue, 0)
    for _ in range(2):
        pltpu.make_async_copy(h_ref, xs_ref.at[pl.ds(0, tm)], sem).wait()


def _dispatch(h2, pos, pads, n_rows, tm, tile, xs_prev=None):
    T, width = h2.shape
    in_specs = [pl.BlockSpec((1, 1, 2 * tm), lambda i, pads: (i, 0, 0), memory_space=pltpu.SMEM),
                pl.BlockSpec((tm, width), lambda i, pads: (i, 0))]
    args = [pads, pos.reshape(T // tm, 1, 2 * tm), h2]
    aliases = {}
    if xs_prev is not None:
        in_specs.append(pl.BlockSpec(memory_space=pl.ANY))
        args.append(xs_prev)
        aliases = {3: 0}
    grid_spec = pltpu.PrefetchScalarGridSpec(
        num_scalar_prefetch=1,
        grid=(T // tm,),
        in_specs=in_specs,
        out_specs=pl.BlockSpec(memory_space=pl.ANY),
        scratch_shapes=[pltpu.VMEM((tile, width), h2.dtype), pltpu.SemaphoreType.DMA(())],
    )
    return pl.pallas_call(
        functools.partial(_dispatch_kernel, tm=tm, tile=tile, first=xs_prev is None),
        grid_spec=grid_spec,
        out_shape=jax.ShapeDtypeStruct((n_rows, width), h2.dtype),
        input_output_aliases=aliases,
        compiler_params=_cparams(1),
        name="moe_dispatch",
    )(*args)


def _experts_kernel(te_ref, nu_ref, xs_ref, wgu_ref, wd_ref, ys_ref, wgu_bf_ref, wd_bf_ref, *, d_expert):
    j = pl.program_id(0)

    @pl.when(j >= nu_ref[0])
    def _():
        ys_ref[...] = jnp.zeros(ys_ref.shape, F32)

    @pl.when((j == 0) | (te_ref[j] != te_ref[jnp.maximum(j - 1, 0)]))
    def _():
        wgu_bf_ref[...] = wgu_ref[0].astype(BF16)
        wd_bf_ref[...] = wd_ref[0].astype(BF16)

    @pl.when(j < nu_ref[0])
    def _():
        gu = jnp.dot(xs_ref[...].astype(BF16), wgu_bf_ref[...], preferred_element_type=F32)
        act = _silu(gu[:, :d_expert]) * gu[:, d_expert:]
        ys_ref[...] = jnp.dot(act.astype(BF16), wd_bf_ref[...], preferred_element_type=F32)


def _experts(xs, tile_expert, n_used, w_gate_up, w_down, tile):
    n_rows, width = xs.shape
    _, D, two_f = w_gate_up.shape
    grid_spec = pltpu.PrefetchScalarGridSpec(
        num_scalar_prefetch=2,
        grid=(n_rows // tile,),
        in_specs=[pl.BlockSpec((tile, width), lambda j, te, nu: (jnp.minimum(j, nu[0] - 1), 0)),
                  pl.BlockSpec((1, D, two_f), lambda j, te, nu: (te[j], 0, 0)),
                  pl.BlockSpec((1, two_f // 2, D), lambda j, te, nu: (te[j], 0, 0))],
        out_specs=pl.BlockSpec((tile, D), lambda j, te, nu: (j, 0)),
        scratch_shapes=[pltpu.VMEM((D, two_f), BF16), pltpu.VMEM((two_f // 2, D), BF16)],
    )
    return pl.pallas_call(
        functools.partial(_experts_kernel, d_expert=two_f // 2),
        grid_spec=grid_spec,
        out_shape=jax.ShapeDtypeStruct((n_rows, D), F32),
        compiler_params=_cparams(1),
        name="moe_experts",
    )(tile_expert, n_used, xs, w_gate_up, w_down)


def _combine_kernel(pos_ref, pos_next_ref, ys_ref, route_ref, x1_ref, mod_ref, gf_ref, y_ref,
                    ya_ref, yb_ref, sems, *, tm):
    step = pl.program_id(0) * pl.num_programs(1) + pl.program_id(1)
    n_steps = pl.num_programs(0) * pl.num_programs(1)
    slot = step % 2

    def start_gathers(p_ref, s):
        def issue(k, carry):
            base = pl.multiple_of(k * SUBLANE, SUBLANE)
            for j in range(SUBLANE):
                i = base + j
                _row_copy(ys_ref, p_ref[0, 0, 2 * i], ya_ref.at[s], i, sems.at[s]).start()
                _row_copy(ys_ref, p_ref[0, 0, 2 * i + 1], yb_ref.at[s], i, sems.at[s]).start()
            return carry

        lax.fori_loop(0, tm // SUBLANE, issue, 0)

    @pl.when(step == 0)
    def _():
        start_gathers(pos_ref, 0)

    @pl.when(step + 1 < n_steps)
    def _():
        start_gathers(pos_next_ref, 1 - slot)

    for buf in (ya_ref, yb_ref):
        pltpu.make_async_copy(ys_ref.at[pl.ds(0, tm)], buf.at[slot], sems.at[slot]).wait()
    route = route_ref[0]
    lane = lax.broadcasted_iota(jnp.int32, route.shape, 1)
    w1 = jnp.sum(jnp.where(lane == 2, route, 0.0), axis=-1, keepdims=True)
    w2 = jnp.sum(jnp.where(lane == 3, route, 0.0), axis=-1, keepdims=True)
    x2 = x1_ref[0] + mod_ref[0, 5] * (w1 * ya_ref[slot] + w2 * yb_ref[slot])
    var = jnp.mean(x2 * x2, axis=-1, keepdims=True)
    y_ref[0] = x2 * lax.rsqrt(var + EPS) * gf_ref[...]


def _combine(ys, pos, route, x1, mod, gf, tm):
    B, S, D = x1.shape
    n_t = S // tm
    tok = lambda width: pl.BlockSpec((1, tm, width), lambda b, i: (b, i, 0))
    last = B * n_t - 1
    pos_spec = lambda ahead: pl.BlockSpec(
        (1, 1, 2 * tm), lambda b, i: (jnp.minimum(b * n_t + i + ahead, last), 0, 0), memory_space=pltpu.SMEM)
    pos_steps = pos.reshape(B * n_t, 1, 2 * tm)
    if mod.shape[2] == 1:
        mod_spec = pl.BlockSpec((1, N_COND, 1, D), lambda b, i: (b, 0, 0, 0))
    else:
        mod_spec = pl.BlockSpec((1, N_COND, tm, D), lambda b, i: (b, 0, i, 0))
    return pl.pallas_call(
        functools.partial(_combine_kernel, tm=tm),
        grid=(B, n_t),
        in_specs=[pos_spec(0), pos_spec(1),
                  pl.BlockSpec(memory_space=pl.ANY),
                  tok(ROUTER_LANES), tok(D),
                  mod_spec,
                  _full((1, D))],
        out_specs=tok(D),
        out_shape=jax.ShapeDtypeStruct((B, S, D), F32),
        scratch_shapes=[pltpu.VMEM((2, tm, D), F32), pltpu.VMEM((2, tm, D), F32),
                        pltpu.SemaphoreType.DMA((2,))],
        compiler_params=_cparams(2),
        name="moe_combine",
    )(pos_steps, pos_steps, ys, route, x1, mod, gf)


def _rope_tables(pos):
    half = HEAD_DIM // 2
    inv = ROPE_THETA ** (-jnp.arange(half, dtype=F32) * 2.0 / HEAD_DIM)
    ang = pos.astype(F32)[:, None] * inv[None, :]
    cos, sin = jnp.cos(ang), jnp.sin(ang)
    reps = LANE // HEAD_DIM
    return (jnp.tile(jnp.concatenate([cos, cos], axis=-1), (1, reps)),
            jnp.tile(jnp.concatenate([-sin, sin], axis=-1), (1, reps)))


def kernel(x_prompt, x_sample, cache_kv_w128, cache_kv_w512, cache_kv_w2048, state_pool, c_prompt, c_sample, norm1_g, w_ada, b_ada, w_in, w_attn_out, w_pool, pool_scale, w_pool_out, w_o, norm2_g, w_grp, b_grp, w_exp_router, b_exp_router, w_gate_up, w_down, final_norm_g):
    B, S, D = x_prompt.shape
    N, T, _ = x_sample.shape
    depth = norm1_g.shape[0]
    assert depth == 1, "single trunk layer"
    tm = min(512, S)
    assert S % tm == 0 and all(S % (SPAN * d) == 0 for d in ATTN_DILATIONS)

    n_qkvu = 3 * ATTN_WIDTH + POOL_WIDTH
    w_qkvu = w_in[0, :, :n_qkvu].astype(BF16)
    w_gates = w_in[0, :, n_qkvu:].astype(BF16)
    wa, wpo, wo = w_attn_out[0].astype(BF16), w_pool_out[0].astype(BF16), w_o[0].astype(BF16)
    wpool = w_pool[0].astype(BF16)
    pscale = pool_scale[0].reshape(1, POOL_WIDTH)
    w_r = jnp.concatenate([w_exp_router[0], w_grp[0]], axis=1)
    w_r = jnp.pad(w_r, ((0, 0), (0, ROUTER_LANES - w_r.shape[1])))
    wr_hi = w_r.astype(BF16)
    wr_hilo = jnp.concatenate([wr_hi, (w_r - wr_hi.astype(F32)).astype(BF16)], axis=1)
    b_r = jnp.pad(jnp.concatenate([b_exp_router[0], b_grp[0]]), (0, ROUTER_LANES - N_EXPERTS - MOE_GROUPS))
    b_r = b_r.reshape(1, ROUTER_LANES)
    wgu, wd = w_gate_up[0], w_down[0]
    g1, g2, gf = norm1_g[0].reshape(1, D), norm2_g[0].reshape(1, D), final_norm_g.reshape(1, D)

    mod = _ada(jnp.concatenate([c_prompt, c_sample], axis=0), w_ada[0], b_ada[0])
    mod_p = mod[:B].reshape(B, N_COND, 1, D)
    mod_s = jnp.repeat(mod[B:].reshape(N, N_COND, D), T, axis=0)
    mod_s = jnp.transpose(mod_s, (1, 0, 2)).reshape(1, N_COND, N * T, D)

    cos_p, sin_p = _rope_tables(jnp.arange(S, dtype=jnp.int32))
    outs = _in_proj_prompt(x_prompt, mod_p, g1, w_qkvu, cos_p, sin_p, wpool, pscale, tm)
    qs, ks, vs = outs[0:3], outs[3:6], outs[6:9]
    pooled, kv0, kv1, kv2, ptail = outs[9:14]
    attn_parts = [_attn_prompt(qs[g], ks[g], vs[g], g, nq=8) for g in range(N_GROUPS)]
    attn_inputs = [o for o, _ in attn_parts] + [l for _, l in attn_parts]
    x1, h2, route, counts = _merge(x_prompt, mod_p, g1, g2, attn_inputs, pooled, w_gates, wa, wpo, wo,
                                   wr_hi, wr_hilo, b_r, tm)

    def kv_shape(a):
        a = a.reshape(a.shape[0], 2, HEADS_PER_GROUP, HEAD_DIM, a.shape[3])
        return jnp.transpose(a, (0, 4, 1, 2, 3))[None]

    pool_prompt = ptail[:, POOL_HALO - POOL_STATE_LEN:, :][None]

    TS = N * T
    pos_s = PAST_LEN + jnp.arange(T, dtype=jnp.int32)
    cos_s, sin_s = _rope_tables(pos_s)
    cos_s, sin_s = jnp.tile(cos_s, (N, 1)), jnp.tile(sin_s, (N, 1))
    xs = x_sample.reshape(1, TS, D)
    q_s, kvn, u_s = _in_proj_sample(xs, mod_s, g1, w_qkvu, cos_s, sin_s)
    eye = jnp.eye(HEADS_PER_GROUP, dtype=BF16)
    qbd = jnp.einsum('ntghe,hk->ngthke', q_s.reshape(N, T, N_GROUPS, HEADS_PER_GROUP, HEAD_DIM), eye)
    qbd = qbd.reshape(N, N_GROUPS, T * HEADS_PER_GROUP, GROUP_WIDTH)
    caches = [jnp.transpose(c[0], (0, 2, 3, 4, 1)).reshape(N, 2, GROUP_WIDTH, c.shape[2])
              for c in (cache_kv_w128, cache_kv_w512, cache_kv_w2048)]
    attn_s, pooled_s, ko0, ko1, ko2, pool_s = _sample_step(
        qbd, kvn.reshape(N, T, 2 * ATTN_WIDTH), caches, state_pool[0], u_s.reshape(N, T, POOL_WIDTH), wpool, pscale)
    x1s, h2s, route_s, counts_s = _merge(xs, mod_s, g1, g2, [attn_s.reshape(1, TS, GROUP_WIDTH)],
                                         pooled_s.reshape(1, TS, POOL_WIDTH), w_gates, wa, wpo, wo,
                                         wr_hi, wr_hilo, b_r, TS)

    (pos, pos_s), tile_expert, n_used, pads, n_tiles = _routing_tables(
        [route.reshape(B * S, ROUTER_LANES), route_s.reshape(TS, ROUTER_LANES)], [counts, counts_s], EXPERT_TILE)
    sorted_rows = _dispatch(h2.reshape(B * S, D), pos, pads, n_tiles * EXPERT_TILE, tm, EXPERT_TILE)
    sorted_rows = _dispatch(h2s.reshape(TS, D), pos_s, pads, n_tiles * EXPERT_TILE, TS, EXPERT_TILE,
                            xs_prev=sorted_rows)
    ys = _experts(sorted_rows, tile_expert, n_used, wgu, wd, EXPERT_TILE)
    y_prompt = _combine(ys, pos, route, x1, mod_p, gf, tm=min(512, S))
    y_sample = _combine(ys, pos_s, route_s, x1s, mod_s, gf, tm=TS).reshape(N, T, D)

    return (y_prompt, y_sample, kv_shape(kv0), kv_shape(kv1), kv_shape(kv2), pool_prompt,
            kv_shape(ko0), kv_shape(ko1), kv_shape(ko2), pool_s[None])
```

```python
import functools

import jax
import jax.numpy as jnp
from jax import lax
from jax.experimental import pallas as pl
from jax.experimental.pallas import tpu as pltpu

F32 = jnp.float32
BF16 = jnp.bfloat16

HEAD_DIM = 64
HEADS_PER_GROUP = 4
HEAD_SHIFT = 2
LANE_HEAD_SHIFT = 6
GROUP_WIDTH = HEADS_PER_GROUP * HEAD_DIM
ATTN_WINDOWS = (128, 512, 2048)
ATTN_DILATIONS = (1, 4, 16)
N_GROUPS = 3
SPAN = 128
ATTN_WIDTH = N_GROUPS * GROUP_WIDTH
ROPE_THETA = 10000.0
PAST_LEN = 8192
POOL_WINDOWS = (2, 4, 8, 16)
POOL_GROUP_WIDTH = 128
POOL_WIDTH = 512
POOL_STATE_LEN = 15
POOL_HALO = 16
MOE_GROUPS = 4
EXPERTS_PER_GROUP = 8
N_EXPERTS = 32
N_COND = 6
EPS = 1e-6

LANE = 128
SUBLANE = 8
VMEM_LIMIT_BYTES = 56 * 1024 * 1024

NEG = -1e30
BIG_LANE = 1e9

ROUTER_LANES = LANE
GROUP_LANE0 = N_EXPERTS


def _cparams(n_axes):
    return pltpu.CompilerParams(dimension_semantics=("arbitrary",) * n_axes,
                                vmem_limit_bytes=VMEM_LIMIT_BYTES)


def _full(shape):
    nd = len(shape)
    return pl.BlockSpec(tuple(shape), lambda *_: (0,) * nd)


def _norm_mod(x, g, scale, shift):
    var = jnp.mean(x * x, axis=-1, keepdims=True)
    return (x * lax.rsqrt(var + EPS) * g) * (1.0 + scale) + shift


def _rope(x, cos, sin):
    lane = lax.broadcasted_iota(jnp.int32, (x.shape[0], LANE), 1)
    first_half = (lane & (HEAD_DIM - 1)) < (HEAD_DIM // 2)
    outs = []
    for c in range(x.shape[1] // LANE):
        xc = x[:, c * LANE:(c + 1) * LANE]
        partner = jnp.where(first_half, pltpu.roll(xc, LANE - HEAD_DIM // 2, 1),
                            pltpu.roll(xc, HEAD_DIM // 2, 1))
        outs.append(xc * cos + partner * sin)
    return jnp.concatenate(outs, axis=1)


def _silu(x):
    return x * jax.nn.sigmoid(x)


def _ada_kernel(c_ref, w_ref, b_ref, o_ref):
    s = _silu(c_ref[...]).astype(BF16)
    o_ref[...] = jnp.dot(s, w_ref[...].astype(BF16), preferred_element_type=F32) + b_ref[...]


def _ada(c_all, w_ada, b_ada):
    rows, d = c_all.shape
    n_out = w_ada.shape[1]
    tn = 1024
    return pl.pallas_call(
        _ada_kernel,
        grid=(n_out // tn,),
        in_specs=[_full((rows, d)),
                  pl.BlockSpec((d, tn), lambda j: (0, j)),
                  pl.BlockSpec((1, tn), lambda j: (0, j))],
        out_specs=pl.BlockSpec((rows, tn), lambda j: (0, j)),
        out_shape=jax.ShapeDtypeStruct((rows, n_out), F32),
        compiler_params=_cparams(1),
        name="ada",
    )(c_all, w_ada, b_ada.reshape(1, n_out))


def _in_proj_prompt_kernel(x_ref, mod_ref, g1_ref, w_ref, cos_ref, sin_ref, wpool_ref, pscale_ref, *rest,
                           tm, n_tiles, tails):
    qkv_refs = (rest[0:3], rest[3:6], rest[6:9])
    pooled_ref, kv0_ref, kv1_ref, kv2_ref, ptail_ref = rest[9:14]
    stage_refs = rest[14:17]
    ue_ref = rest[17]
    i = pl.program_id(1)
    x = x_ref[0]
    h = _norm_mod(x, g1_ref[...], mod_ref[0, 1], mod_ref[0, 0]).astype(BF16)
    cos = cos_ref[...]
    sin = sin_ref[...]
    y = jnp.dot(h, w_ref[...], preferred_element_type=F32)
    q = _rope(y[:, :ATTN_WIDTH], cos, sin) * (HEAD_DIM ** -0.5)
    k = _rope(y[:, ATTN_WIDTH:2 * ATTN_WIDTH], cos, sin)
    v = y[:, 2 * ATTN_WIDTH:3 * ATTN_WIDTH]
    u = y[:, 3 * ATTN_WIDTH:3 * ATTN_WIDTH + POOL_WIDTH]

    chunks = GROUP_WIDTH // LANE
    for val, out_refs, stage_ref in zip((q, k, v), qkv_refs, stage_refs):
        for g, d in enumerate(ATTN_DILATIONS):
            gl = slice(g * GROUP_WIDTH, (g + 1) * GROUP_WIDTH)
            if d == 1:
                out_refs[g][0] = val[:, gl].astype(BF16)
                continue
            for c in range(chunks):
                stage_ref[g * chunks + c] = val[:, g * GROUP_WIDTH + c * LANE:g * GROUP_WIDTH + (c + 1) * LANE]
            for r in range(d):
                for c in range(chunks):
                    out_refs[g][0, :, r * GROUP_WIDTH + c * LANE:r * GROUP_WIDTH + (c + 1) * LANE] = (
                        stage_ref[g * chunks + c, pl.ds(r, tm // d, stride=d), :].astype(BF16))

    for g, kv_ref in enumerate((kv0_ref, kv1_ref, kv2_ref)):
        first_tile, rows = tails[g]
        lo = tm - rows

        @pl.when(i >= first_tile)
        def _(kv_ref=kv_ref, g=g, lo=lo):
            kv_ref[0, 0] = k[lo:, g * GROUP_WIDTH:(g + 1) * GROUP_WIDTH].T
            kv_ref[0, 1] = v[lo:, g * GROUP_WIDTH:(g + 1) * GROUP_WIDTH].T

    @pl.when(i == 0)
    def _():
        ue_ref[0:POOL_HALO, :] = jnp.zeros((POOL_HALO, POOL_WIDTH), F32)

    ue_ref[POOL_HALO:, :] = u
    pos1 = (i * tm + 1 + lax.broadcasted_iota(jnp.int32, (tm, 1), 0)).astype(F32)
    for gi, w in enumerate(POOL_WINDOWS):
        lanes = slice(gi * POOL_GROUP_WIDTH, (gi + 1) * POOL_GROUP_WIDTH)
        acc = ue_ref[pl.ds(POOL_HALO, tm), lanes]
        for j in range(1, w):
            acc = acc + ue_ref[pl.ds(POOL_HALO - j, tm), lanes]
        mean = acc / jnp.minimum(float(w), pos1)
        z = jnp.dot((mean - u[:, lanes]).astype(BF16), wpool_ref[gi], preferred_element_type=F32)
        pooled_ref[0, :, lanes] = (z * pscale_ref[:, lanes]).astype(BF16)
    ue_ref[0:POOL_HALO, :] = u[tm - POOL_HALO:, :]

    @pl.when(i == n_tiles - 1)
    def _():
        ptail_ref[0] = u[tm - POOL_HALO:, :]


def _in_proj_prompt(x, mod, g1, w_qkvu, cos, sin, w_pool, pool_scale, tm):
    B, S, D = x.shape
    n_tiles = S // tm
    n_w = w_qkvu.shape[1]
    tails, kv_shapes, kv_specs = [], [], []
    for W in ATTN_WINDOWS:
        Wg = min(W, S)
        if Wg >= tm:
            first = n_tiles - Wg // tm
            rows = tm
        else:
            first = n_tiles - 1
            rows = Wg
        tails.append((first, rows))
        kv_shapes.append(jax.ShapeDtypeStruct((B, 2, GROUP_WIDTH, Wg), F32))
        kv_specs.append(pl.BlockSpec((1, 2, GROUP_WIDTH, rows),
                                     lambda b, i, first=first: (b, 0, 0, jnp.maximum(i - first, 0))))
    tok = lambda width: pl.BlockSpec((1, tm, width), lambda b, i: (b, i, 0))
    dil_specs = [pl.BlockSpec((1, tm // d, d * GROUP_WIDTH), lambda b, i: (b, i, 0)) for d in ATTN_DILATIONS]
    dil_shapes = [jax.ShapeDtypeStruct((B, S // d, d * GROUP_WIDTH), BF16) for d in ATTN_DILATIONS]
    outs = pl.pallas_call(
        functools.partial(_in_proj_prompt_kernel, tm=tm, n_tiles=n_tiles, tails=tuple(tails)),
        grid=(B, n_tiles),
        in_specs=[tok(D),
                  pl.BlockSpec((1, N_COND, 1, D), lambda b, i: (b, 0, 0, 0)),
                  _full((1, D)),
                  _full((D, n_w)),
                  pl.BlockSpec((tm, LANE), lambda b, i: (i, 0)),
                  pl.BlockSpec((tm, LANE), lambda b, i: (i, 0)),
                  _full(w_pool.shape),
                  _full((1, POOL_WIDTH))],
        out_specs=dil_specs * 3 + [tok(POOL_WIDTH)] + kv_specs
                  + [pl.BlockSpec((1, POOL_HALO, POOL_WIDTH), lambda b, i: (b, 0, 0))],
        out_shape=dil_shapes * 3
                  + [jax.ShapeDtypeStruct((B, S, POOL_WIDTH), BF16)] + kv_shapes
                  + [jax.ShapeDtypeStruct((B, POOL_HALO, POOL_WIDTH), F32)],
        scratch_shapes=[pltpu.VMEM((ATTN_WIDTH // LANE, tm, LANE), F32)] * 3
                       + [pltpu.VMEM((tm + POOL_HALO, POOL_WIDTH), F32)],
        compiler_params=_cparams(2),
        name="in_proj_prompt",
    )(x, mod, g1, w_qkvu, cos, sin, w_pool, pool_scale)
    return outs


def _in_proj_sample_kernel(x_ref, mod_ref, g1_ref, w_ref, cos_ref, sin_ref, q_ref, kvn_ref, u_ref):
    h = _norm_mod(x_ref[0], g1_ref[...], mod_ref[0, 1], mod_ref[0, 0]).astype(BF16)
    y = jnp.dot(h, w_ref[...], preferred_element_type=F32)
    cos = cos_ref[...]
    sin = sin_ref[...]
    q = _rope(y[:, :ATTN_WIDTH], cos, sin) * (HEAD_DIM ** -0.5)
    k = _rope(y[:, ATTN_WIDTH:2 * ATTN_WIDTH], cos, sin)
    v = y[:, 2 * ATTN_WIDTH:3 * ATTN_WIDTH]
    q_ref[...] = q.astype(BF16)
    for g in range(N_GROUPS):
        gl = slice(g * GROUP_WIDTH, (g + 1) * GROUP_WIDTH)
        kvn_ref[:, 2 * g * GROUP_WIDTH:(2 * g + 1) * GROUP_WIDTH] = k[:, gl]
        kvn_ref[:, (2 * g + 1) * GROUP_WIDTH:(2 * g + 2) * GROUP_WIDTH] = v[:, gl]
    u_ref[...] = y[:, 3 * ATTN_WIDTH:3 * ATTN_WIDTH + POOL_WIDTH]


def _in_proj_sample(x, mod, g1, w_qkvu, cos, sin):
    _, T, D = x.shape
    n_w = w_qkvu.shape[1]
    return pl.pallas_call(
        _in_proj_sample_kernel,
        grid=(1,),
        in_specs=[_full((1, T, D)), _full((1, N_COND, T, D)), _full((1, D)), _full((D, n_w)),
                  _full((T, LANE)), _full((T, LANE))],
        out_specs=[_full((T, ATTN_WIDTH)), _full((T, 2 * ATTN_WIDTH)), _full((T, POOL_WIDTH))],
        out_shape=[jax.ShapeDtypeStruct((T, ATTN_WIDTH), BF16),
                   jax.ShapeDtypeStruct((T, 2 * ATTN_WIDTH), F32),
                   jax.ShapeDtypeStruct((T, POOL_WIDTH), F32)],
        compiler_params=_cparams(1),
        name="in_proj_sample",
    )(x, mod, g1, w_qkvu, cos, sin)


def _attn_body(q_ref, kc_ref, kp_ref, vc_ref, vp_ref, o_ref, lse_ref, kbuf, vbuf, n0, nq):
    kbuf[0:SPAN, :] = kp_ref[0]
    kbuf[SPAN:, :] = kc_ref[0]
    vbuf[0:SPAN, :] = vp_ref[0]
    vbuf[SPAN:, :] = vc_ref[0]
    qi = lax.broadcasted_iota(jnp.int32, (SPAN, 2 * SPAN), 0)
    kj = lax.broadcasted_iota(jnp.int32, (SPAN, 2 * SPAN), 1)
    band = (kj >= qi) & (kj <= qi + SPAN)
    band_first = band & (kj >= jnp.where(n0 > 0, 0, SPAN))
    lane = lax.broadcasted_iota(jnp.int32, (SPAN, LANE), 1)
    low_head = lane < HEAD_DIM
    for j in range(nq):
        valid = band_first if j == 0 else band
        rows = slice(j * SPAN, (j + 1) * SPAN)
        krows = slice(j * SPAN, (j + 2) * SPAN)
        for c in range(GROUP_WIDTH // LANE):
            cl = slice(c * LANE, (c + 1) * LANE)
            q = q_ref[0, rows, cl]
            kk = kbuf[krows, cl]
            vv = vbuf[krows, cl]
            o_pair, lse_pair = [], []
            for hh in range(2):
                mask_h = low_head if hh == 0 else jnp.logical_not(low_head)
                qm = jnp.where(mask_h, q, jnp.zeros_like(q))
                s = lax.dot_general(qm, kk, (((1,), (1,)), ((), ())), preferred_element_type=F32)
                s = jnp.where(valid, s, NEG)
                m = jnp.max(s, axis=-1, keepdims=True)
                p = jnp.exp(s - m)
                den = jnp.sum(p, axis=-1, keepdims=True)
                o_pair.append(jnp.dot(p.astype(BF16), vv, preferred_element_type=F32) / den)
                lse_pair.append(m + jnp.log(den))
            o_ref[0, rows, cl] = jnp.where(low_head, o_pair[0], o_pair[1]).astype(BF16)
            lse_ref[0, rows, cl] = jnp.where(low_head, lse_pair[0], lse_pair[1])


def _sample_body(qbd_ref, kvn_ref, c0_ref, c1_ref, c2_ref, sp_ref, u_ref, wpool_ref, pscale_ref,
                 attn_ref, pooled_ref, o0_ref, o1_ref, o2_ref, po_ref, ue_ref, diff_ref, nt_ref, t_new):
    rows = HEADS_PER_GROUP * t_new
    caches = (c0_ref, c1_ref, c2_ref)
    outs = (o0_ref, o1_ref, o2_ref)

    s_cache, s_new, valid_new = [], [], []
    m = jnp.full((rows, 1), NEG, F32)
    for g in range(N_GROUPS):
        d = ATTN_DILATIONS[g]
        c_ref = caches[g]
        L = c_ref.shape[3]
        qb = qbd_ref[0, g]
        s = jnp.dot(qb, c_ref[0, 0].astype(BF16), preferred_element_type=F32)
        t_row = lax.broadcasted_iota(jnp.int32, (rows, L), 0) >> HEAD_SHIFT
        delta = L + t_row - lax.broadcasted_iota(jnp.int32, (rows, L), 1)
        ok = ((delta & (d - 1)) == 0) & (delta <= SPAN * d)
        s = jnp.where(ok, s, NEG)
        s_cache.append(s)
        m = jnp.maximum(m, jnp.max(s, axis=-1, keepdims=True))
        qf = qb.astype(F32)
        t_col = lax.broadcasted_iota(jnp.int32, (rows, 1), 0) >> HEAD_SHIFT
        sn, okn = [], []
        for tn in range(t_new):
            kn = kvn_ref[0, tn:tn + 1, 2 * g * GROUP_WIDTH:(2 * g + 1) * GROUP_WIDTH]
            kn = kn.astype(BF16).astype(F32)
            dn = t_col - tn
            ok_n = (dn >= 0) & ((dn & (d - 1)) == 0)
            s1 = jnp.where(ok_n, jnp.sum(qf * kn, axis=-1, keepdims=True), NEG)
            sn.append(s1)
            okn.append(ok_n)
            m = jnp.maximum(m, s1)
        s_new.append(sn)
        valid_new.append(okn)

    den = jnp.zeros((rows, 1), F32)
    acc = jnp.zeros((rows, GROUP_WIDTH), F32)
    for g in range(N_GROUPS):
        c_ref = caches[g]
        p = jnp.exp(s_cache[g] - m)
        den = den + jnp.sum(p, axis=-1, keepdims=True)
        acc = acc + lax.dot_general(p.astype(BF16), c_ref[0, 1].astype(BF16), (((1,), (1,)), ((), ())),
                                    preferred_element_type=F32)
        for tn in range(t_new):
            pn = jnp.exp(s_new[g][tn] - m)
            den = den + pn
            vn = kvn_ref[0, tn:tn + 1, (2 * g + 1) * GROUP_WIDTH:(2 * g + 2) * GROUP_WIDTH]
            acc = acc + pn * vn
    row_head = lax.broadcasted_iota(jnp.int32, (rows, GROUP_WIDTH), 0) & (HEADS_PER_GROUP - 1)
    lane_head = lax.broadcasted_iota(jnp.int32, (rows, GROUP_WIDTH), 1) >> LANE_HEAD_SHIFT
    o_diag = jnp.where(row_head == lane_head, acc / den, 0.0).astype(BF16)
    sel = ((lax.broadcasted_iota(jnp.int32, (rows, rows), 1) >> HEAD_SHIFT)
           == lax.broadcasted_iota(jnp.int32, (rows, rows), 0)).astype(BF16)
    attn = jnp.dot(sel, o_diag, preferred_element_type=F32)
    attn_ref[0] = attn[0:t_new].astype(BF16)

    tail_lane = lax.broadcasted_iota(jnp.int32, (GROUP_WIDTH, LANE), 1)
    nt_ref[...] = jnp.zeros(nt_ref.shape, F32)
    for g in range(N_GROUPS):
        c_ref, o_ref = caches[g], outs[g]
        L = c_ref.shape[3]
        for kv in range(2):
            col0 = (2 * g + kv) * GROUP_WIDTH
            nt_ref[LANE - t_new:LANE, :] = kvn_ref[0, :, col0:col0 + GROUP_WIDTH]
            new_t = nt_ref[...].T
            rolled = pltpu.roll(c_ref[0, kv], L - t_new, 1)
            if L > LANE:
                o_ref[0, kv, :, 0:L - LANE] = rolled[:, 0:L - LANE]
            o_ref[0, kv, :, L - LANE:L] = jnp.where(tail_lane >= LANE - t_new, new_t, rolled[:, L - LANE:L])

    ue_ref[0:POOL_STATE_LEN, :] = sp_ref[0]
    ue_ref[POOL_STATE_LEN:POOL_STATE_LEN + t_new, :] = u_ref[0]
    diff_ref[...] = jnp.zeros(diff_ref.shape, F32)
    for tn in range(t_new):
        r = POOL_STATE_LEN + tn
        for gi, w in enumerate(POOL_WINDOWS):
            lanes = slice(gi * POOL_GROUP_WIDTH, (gi + 1) * POOL_GROUP_WIDTH)
            win = jnp.sum(ue_ref[r - w + 1:r + 1, lanes], axis=0, keepdims=True)
            diff_ref[tn:tn + 1, lanes] = win / float(w) - ue_ref[r:r + 1, lanes]
    for gi in range(len(POOL_WINDOWS)):
        lanes = slice(gi * POOL_GROUP_WIDTH, (gi + 1) * POOL_GROUP_WIDTH)
        z = jnp.dot(diff_ref[:, lanes].astype(BF16), wpool_ref[gi], preferred_element_type=F32)
        pooled_ref[0, :, lanes] = (z[0:t_new] * pscale_ref[:, lanes]).astype(BF16)
    po_ref[0] = ue_ref[t_new:t_new + POOL_STATE_LEN, :]


N_ATTN_IN = 5
N_SAMPLE_IN = 9
N_SAMPLE_OUT = 6


def _attn_sample_kernel(*refs, plans, n_seq, t_new):
    n_att = N_GROUPS * N_ATTN_IN
    attn_in = refs[:n_att]
    sample_in = refs[n_att:n_att + N_SAMPLE_IN]
    outs = refs[n_att + N_SAMPLE_IN:]
    attn_out = outs[:2 * N_GROUPS]
    sample_out = outs[2 * N_GROUPS:2 * N_GROUPS + N_SAMPLE_OUT]
    scratch = outs[2 * N_GROUPS + N_SAMPLE_OUT:]
    kv_bufs, sample_scratch = scratch[:2 * N_GROUPS], scratch[2 * N_GROUPS:]
    s = pl.program_id(0)

    for g, (first, count, n_sup, d, nq) in enumerate(plans):
        @pl.when((s >= first) & (s < first + count))
        def _(g=g, first=first, n_sup=n_sup, d=d, nq=nq):
            n0 = ((s - first) // d) % n_sup
            _attn_body(*attn_in[g * N_ATTN_IN:(g + 1) * N_ATTN_IN], attn_out[2 * g], attn_out[2 * g + 1],
                       kv_bufs[2 * g], kv_bufs[2 * g + 1], n0, nq)

    @pl.when(s < n_seq)
    def _():
        _sample_body(*sample_in, *sample_out, *sample_scratch, t_new)


def _attn_and_sample(qs, ks, vs, qbd, kvn, caches, state_pool, u, w_pool, pool_scale, nq_max):
    N, t_new, _ = u.shape
    rows = HEADS_PER_GROUP * t_new
    plans, in_specs, out_specs, out_shape, scratch, args = [], [], [], [], [], []
    first = 0
    for g in range(N_GROUPS):
        B, M, width = qs[g].shape
        d = width // GROUP_WIDTH
        nq = min(nq_max, M // SPAN)
        n_sup = M // (nq * SPAN)
        count = B * n_sup * d
        plans.append((first, count, n_sup, d, nq))

        def block_index(s, first=first, count=count, n_sup=n_sup, d=d):
            local = jnp.clip(s - first, 0, count - 1)
            return local // (n_sup * d), (local // d) % n_sup, local % d

        def cur_map(s, block_index=block_index):
            b, n, r = block_index(s)
            return b, n, r

        def prev_map(s, block_index=block_index, nq=nq):
            b, n, r = block_index(s)
            return b, jnp.maximum(n * nq - 1, 0), r

        cur = pl.BlockSpec((1, nq * SPAN, GROUP_WIDTH), cur_map)
        prev = pl.BlockSpec((1, SPAN, GROUP_WIDTH), prev_map)
        in_specs += [cur, cur, prev, cur, prev]
        args += [qs[g], ks[g], ks[g], vs[g], vs[g]]
        out_specs += [cur, cur]
        out_shape += [jax.ShapeDtypeStruct((B, M, width), BF16), jax.ShapeDtypeStruct((B, M, width), F32)]
        scratch += [pltpu.VMEM(((nq + 1) * SPAN, GROUP_WIDTH), BF16)] * 2
        first += count
    n_steps = max(first, N)

    per_n = lambda shape: pl.BlockSpec((1,) + tuple(shape),
                                       lambda s: (jnp.minimum(s, N - 1),) + (0,) * len(shape))
    cache_specs = [per_n(c.shape[1:]) for c in caches]
    in_specs += ([per_n(qbd.shape[1:]), per_n(kvn.shape[1:])] + cache_specs
                 + [per_n(state_pool.shape[1:]), per_n(u.shape[1:]), _full(w_pool.shape), _full((1, POOL_WIDTH))])
    args += [qbd, kvn, *caches, state_pool, u, w_pool, pool_scale]
    out_specs += ([per_n((t_new, GROUP_WIDTH)), per_n((t_new, POOL_WIDTH))] + cache_specs
                  + [per_n(state_pool.shape[1:])])
    out_shape += ([jax.ShapeDtypeStruct((N, t_new, GROUP_WIDTH), BF16),
                   jax.ShapeDtypeStruct((N, t_new, POOL_WIDTH), BF16)]
                  + [jax.ShapeDtypeStruct(c.shape, F32) for c in caches]
                  + [jax.ShapeDtypeStruct(state_pool.shape, F32)])
    scratch += [pltpu.VMEM((POOL_STATE_LEN + t_new + 5, POOL_WIDTH), F32),
                pltpu.VMEM((rows, POOL_WIDTH), F32),
                pltpu.VMEM((LANE, GROUP_WIDTH), F32)]
    outs = pl.pallas_call(
        functools.partial(_attn_sample_kernel, plans=tuple(plans), n_seq=N, t_new=t_new),
        grid=(n_steps,),
        in_specs=in_specs,
        out_specs=out_specs,
        out_shape=out_shape,
        scratch_shapes=scratch,
        compiler_params=_cparams(1),
        name="attn_and_sample",
    )(*args)
    return [(outs[2 * g], outs[2 * g + 1]) for g in range(N_GROUPS)], outs[2 * N_GROUPS:]


def _route(logits):
    lane = lax.broadcasted_iota(jnp.int32, logits.shape, 1).astype(F32)
    is_grp = (lane >= GROUP_LANE0) & (lane < GROUP_LANE0 + MOE_GROUPS)
    gl = jnp.where(is_grp, logits, NEG)
    gmax = jnp.max(gl, axis=-1, keepdims=True)
    gidx = jnp.min(jnp.where(gl == gmax, lane, BIG_LANE), axis=-1, keepdims=True) - GROUP_LANE0
    gsum = jnp.sum(jnp.where(is_grp, jnp.exp(gl - gmax), 0.0), axis=-1, keepdims=True)
    grp_w = 1.0 / gsum
    lo = gidx * EXPERTS_PER_GROUP
    in_grp = (lane >= lo) & (lane < lo + EXPERTS_PER_GROUP)
    el = jnp.where(in_grp, logits, NEG)
    v1 = jnp.max(el, axis=-1, keepdims=True)
    i1 = jnp.min(jnp.where(el == v1, lane, BIG_LANE), axis=-1, keepdims=True)
    el2 = jnp.where(lane == i1, NEG, el)
    v2 = jnp.max(el2, axis=-1, keepdims=True)
    i2 = jnp.min(jnp.where(el2 == v2, lane, BIG_LANE), axis=-1, keepdims=True)
    e = jnp.exp(v2 - v1)
    w1 = grp_w / (1.0 + e)
    w2 = grp_w * e / (1.0 + e)
    return lane, i1, i2, w1, w2


def _merge_kernel(*refs, n_groups):
    x_ref, mod_ref, g1_ref, g2_ref = refs[0:4]
    n_attn = 2 * n_groups if n_groups > 1 else 1
    attn_refs = refs[4:4 + n_attn]
    rest = refs[4 + n_attn:]
    pooled_ref, wg_ref, wa_ref, wp_ref, wo_ref, wrh_ref, wrc_ref, br_ref = rest[:8]
    x1_ref, h2_ref, route_ref, counts_ref = rest[8:12]
    scratch_refs = rest[12:]
    stage_refs = scratch_refs[:n_attn] if n_groups > 1 else ()
    carry_ref = scratch_refs[-1]

    x = x_ref[0]
    tm = x.shape[0]
    if n_groups > 1:
        vals = []
        for idx, ref in enumerate(attn_refs):
            d = ATTN_DILATIONS[idx % n_groups]
            if d == 1:
                vals.append(ref[0].astype(F32))
                continue
            stage_ref = stage_refs[idx]
            chunks = GROUP_WIDTH // LANE
            for r in range(d):
                for c in range(chunks):
                    stage_ref[c, pl.ds(r, tm // d, stride=d), :] = (
                        ref[0, :, r * GROUP_WIDTH + c * LANE:r * GROUP_WIDTH + (c + 1) * LANE].astype(F32))
            vals.append(jnp.concatenate([stage_ref[c] for c in range(chunks)], axis=1))
        os_, ls = vals[:n_groups], vals[n_groups:]
        lmax = functools.reduce(jnp.maximum, ls)
        es = [jnp.exp(l - lmax) for l in ls]
        attn = sum(e * o for e, o in zip(es, os_)) / sum(es)
    else:
        attn = attn_refs[0][0]
    a = jnp.dot(attn.astype(BF16), wa_ref[...], preferred_element_type=F32)
    p = jnp.dot(pooled_ref[0], wp_ref[...], preferred_element_type=F32)
    h = _norm_mod(x, g1_ref[...], mod_ref[0, 1], mod_ref[0, 0]).astype(BF16)
    gates = jnp.dot(h, wg_ref[...], preferred_element_type=F32)
    D = x.shape[1]
    merged = jax.nn.sigmoid(gates[:, :D]) * a + jax.nn.sigmoid(gates[:, D:]) * p
    y = jnp.dot(merged.astype(BF16), wo_ref[...], preferred_element_type=F32)
    x1 = x + mod_ref[0, 2] * y
    x1_ref[0] = x1
    h2 = _norm_mod(x1, g2_ref[...], mod_ref[0, 4], mod_ref[0, 3])
    h2_hi = h2.astype(BF16)
    h2_lo = (h2 - h2_hi.astype(F32)).astype(BF16)
    hi_terms = jnp.dot(h2_hi, wrc_ref[...], preferred_element_type=F32)
    logits = (hi_terms[:, :ROUTER_LANES] + jnp.dot(h2_lo, wrh_ref[...], preferred_element_type=F32)
              + hi_terms[:, ROUTER_LANES:]) + br_ref[...]
    lane, i1, i2, w1, w2 = _route(logits)
    h2_ref[0] = h2

    @pl.when((pl.program_id(0) == 0) & (pl.program_id(1) == 0))
    def _():
        carry_ref[...] = jnp.zeros(carry_ref.shape, F32)

    hit = ((lane == i1) | (lane == i2)).astype(BF16)
    ltri = (lax.broadcasted_iota(jnp.int32, (tm, tm), 0) >= lax.broadcasted_iota(jnp.int32, (tm, tm), 1))
    prefix = jnp.dot(ltri.astype(BF16), hit, preferred_element_type=F32) + carry_ref[...]
    rank1 = jnp.sum(jnp.where(lane == i1, prefix, 0.0), axis=-1, keepdims=True) - 1.0
    rank2 = jnp.sum(jnp.where(lane == i2, prefix, 0.0), axis=-1, keepdims=True) - 1.0
    carry_ref[...] = prefix[tm - 1:tm, :]
    counts_ref[...] = jnp.broadcast_to(prefix[tm - 1:tm, :], counts_ref.shape)
    cols = (i1, i2, w1, w2, rank1, rank2)
    route = jnp.zeros(logits.shape, F32)
    for c, col in enumerate(cols):
        route = jnp.where(lane == float(c), col, route)
    route_ref[0] = route


def _merge(x, mod, g1, g2, attn_inputs, pooled, w_gates, w_attn_out, w_pool_out, w_o, wr_hi, wr_hilo, b_r, tm):
    B, S, D = x.shape
    R = mod.shape[2]
    n_groups = len(attn_inputs) // 2 if len(attn_inputs) > 1 else 1
    tok = lambda width: pl.BlockSpec((1, tm, width), lambda b, i: (b, i, 0))
    if R == 1:
        mod_spec = pl.BlockSpec((1, N_COND, 1, D), lambda b, i: (b, 0, 0, 0))
    else:
        mod_spec = pl.BlockSpec((1, N_COND, tm, D), lambda b, i: (b, 0, i, 0))
    weights = (w_gates, w_attn_out, w_pool_out, w_o, wr_hi, wr_hilo, b_r)
    if n_groups > 1:
        attn_specs = [pl.BlockSpec((1, tm // (a.shape[2] // GROUP_WIDTH), a.shape[2]), lambda b, i: (b, i, 0))
                      for a in attn_inputs]
        scratch = [pltpu.VMEM((GROUP_WIDTH // LANE, tm, LANE), F32)] * len(attn_inputs)
    else:
        attn_specs = [tok(GROUP_WIDTH)]
        scratch = []
    out_specs = [tok(D), tok(D), tok(ROUTER_LANES), _full((SUBLANE, ROUTER_LANES))]
    out_shape = [jax.ShapeDtypeStruct((B, S, D), F32),
                 jax.ShapeDtypeStruct((B, S, D), F32),
                 jax.ShapeDtypeStruct((B, S, ROUTER_LANES), F32),
                 jax.ShapeDtypeStruct((SUBLANE, ROUTER_LANES), F32)]
    scratch = scratch + [pltpu.VMEM((1, ROUTER_LANES), F32)]
    return pl.pallas_call(
        functools.partial(_merge_kernel, n_groups=n_groups),
        grid=(B, S // tm),
        in_specs=[tok(D), mod_spec, _full((1, D)), _full((1, D))]
                 + attn_specs + [tok(POOL_WIDTH)]
                 + [_full(w.shape) for w in weights],
        out_specs=out_specs,
        out_shape=out_shape,
        scratch_shapes=scratch,
        compiler_params=_cparams(2),
        name="merge",
    )(x, mod, g1, g2, *attn_inputs, pooled, *weights)


EXPERT_TILE = 512
PAD_CHUNK = 32


def _routing_tables(routes, counts, tile):
    cnts = [c[0, :N_EXPERTS].astype(jnp.int32) for c in counts]
    cnt = sum(cnts)
    padded = ((cnt + tile - 1) // tile) * tile
    ends = jnp.cumsum(padded)
    base = ends - padded
    experts = jnp.arange(N_EXPERTS, dtype=jnp.int32)
    poss = []
    for route in routes:
        hit = route[:, 0:2].astype(jnp.int32)[:, :, None] == experts
        poss.append(jnp.sum(jnp.where(hit, base, 0), axis=-1) + route[:, 4:6].astype(jnp.int32))
        base = base + cnts[len(poss) - 1]
    n_pairs = 2 * sum(r.shape[0] for r in routes)
    n_tiles = -(-n_pairs // tile) + N_EXPERTS
    starts = jnp.arange(n_tiles, dtype=jnp.int32) * tile
    n_used = ends[-1] // tile
    tile_expert = jnp.sum(starts[:, None] >= ends[None, :], axis=1).astype(jnp.int32)
    last = jnp.take(tile_expert, n_used - 1)
    tile_expert = jnp.where(jnp.arange(n_tiles) < n_used, tile_expert, last)
    pads = jnp.stack([ends - padded + cnts[0], ends], axis=1).reshape(-1).astype(jnp.int32)
    return poss, tile_expert, n_used.reshape(1).astype(jnp.int32), pads, n_tiles


def _row_copy(src_ref, src_row, dst_ref, dst_row, sem):
    return pltpu.make_async_copy(src_ref.at[pl.ds(src_row, 1)], dst_ref.at[pl.ds(dst_row, 1)], sem)


def _dispatch_kernel(pads_ref, pos_ref, h_ref, *rest, tm, tile, first):
    xs_ref, zero_ref, sem = rest[-3:]

    def zero_fill():
        zero_ref[...] = jnp.zeros(zero_ref.shape, zero_ref.dtype)
        n_tiles = xs_ref.shape[0] // tile
        first_unused = pads_ref[2 * N_EXPERTS - 1] // tile

        def tile_copy(j):
            return pltpu.make_async_copy(zero_ref, xs_ref.at[pl.ds(pl.multiple_of(j * tile, tile), tile)], sem)

        def fill_tile(j, c):
            tile_copy(j).start()
            return c

        def drain_tile(j, c):
            tile_copy(j).wait()
            return c

        lax.fori_loop(first_unused, n_tiles, fill_tile, 0)
        lax.fori_loop(first_unused, n_tiles, drain_tile, 0)

        def chunk_copy(c):
            rows = pl.ds(pl.multiple_of(c * PAD_CHUNK, PAD_CHUNK), PAD_CHUNK)
            return pltpu.make_async_copy(zero_ref.at[pl.ds(0, PAD_CHUNK)], xs_ref.at[rows], sem)

        def for_each_pad_piece(on_row, on_chunk):
            def per_expert(e, carry):
                lo, hi = pads_ref[2 * e], pads_ref[2 * e + 1]
                first_chunk = (lo + PAD_CHUNK - 1) // PAD_CHUNK
                lax.fori_loop(lo, jnp.minimum(first_chunk * PAD_CHUNK, hi), on_row, 0)
                lax.fori_loop(first_chunk, hi // PAD_CHUNK, on_chunk, 0)
                return carry

            lax.fori_loop(0, N_EXPERTS, per_expert, 0)

        def start_row(r, c):
            _row_copy(zero_ref, 0, xs_ref, r, sem).start()
            return c

        def wait_row(r, c):
            _row_copy(zero_ref, 0, xs_ref, r, sem).wait()
            return c

        def start_chunk(c, carry):
            chunk_copy(c).start()
            return carry

        def wait_chunk(c, carry):
            chunk_copy(c).wait()
            return carry

        for_each_pad_piece(start_row, start_chunk)
        for_each_pad_piece(wait_row, wait_chunk)

    if first:
        pl.when(pl.program_id(0) == 0)(zero_fill)

    def issue(k, carry):
        base = pl.multiple_of(k * SUBLANE, SUBLANE)
        for j in range(SUBLANE):
            i = base + j
            _row_copy(h_ref, i, xs_ref, pos_ref[0, 0, 2 * i], sem).start()
            _row_copy(h_ref, i, xs_ref, pos_ref[0, 0, 2 * i + 1], sem).start()
        return carry

    lax.fori_loop(0, tm // SUBLANE, issue, 0)
    for _ in range(2):
        pltpu.make_async_copy(h_ref, xs_ref.at[pl.ds(0, tm)], sem).wait()


def _dispatch(h2, pos, pads, n_rows, tm, tile, xs_prev=None):
    T, width = h2.shape
    in_specs = [pl.BlockSpec((1, 1, 2 * tm), lambda i, pads: (i, 0, 0), memory_space=pltpu.SMEM),
                pl.BlockSpec((tm, width), lambda i, pads: (i, 0))]
    args = [pads, pos.reshape(T // tm, 1, 2 * tm), h2]
    aliases = {}
    if xs_prev is not None:
        in_specs.append(pl.BlockSpec(memory_space=pl.ANY))
        args.append(xs_prev)
        aliases = {3: 0}
    grid_spec = pltpu.PrefetchScalarGridSpec(
        num_scalar_prefetch=1,
        grid=(T // tm,),
        in_specs=in_specs,
        out_specs=pl.BlockSpec(memory_space=pl.ANY),
        scratch_shapes=[pltpu.VMEM((tile, width), h2.dtype), pltpu.SemaphoreType.DMA(())],
    )
    return pl.pallas_call(
        functools.partial(_dispatch_kernel, tm=tm, tile=tile, first=xs_prev is None),
        grid_spec=grid_spec,
        out_shape=jax.ShapeDtypeStruct((n_rows, width), h2.dtype),
        input_output_aliases=aliases,
        compiler_params=_cparams(1),
        name="moe_dispatch",
    )(*args)


def _experts_kernel(te_ref, nu_ref, xs_ref, wgu_ref, wd_ref, ys_ref, wgu_bf_ref, wd_bf_ref, *, d_expert):
    j = pl.program_id(0)

    @pl.when(j >= nu_ref[0])
    def _():
        ys_ref[...] = jnp.zeros(ys_ref.shape, F32)

    @pl.when((j == 0) | (te_ref[j] != te_ref[jnp.maximum(j - 1, 0)]))
    def _():
        wgu_bf_ref[...] = wgu_ref[0].astype(BF16)
        wd_bf_ref[...] = wd_ref[0].astype(BF16)

    @pl.when(j < nu_ref[0])
    def _():
        gu = jnp.dot(xs_ref[...].astype(BF16), wgu_bf_ref[...], preferred_element_type=F32)
        act = _silu(gu[:, :d_expert]) * gu[:, d_expert:]
        ys_ref[...] = jnp.dot(act.astype(BF16), wd_bf_ref[...], preferred_element_type=F32)


def _experts(xs, tile_expert, n_used, w_gate_up, w_down, tile):
    n_rows, width = xs.shape
    _, D, two_f = w_gate_up.shape
    grid_spec = pltpu.PrefetchScalarGridSpec(
        num_scalar_prefetch=2,
        grid=(n_rows // tile,),
        in_specs=[pl.BlockSpec((tile, width), lambda j, te, nu: (jnp.minimum(j, nu[0] - 1), 0)),
                  pl.BlockSpec((1, D, two_f), lambda j, te, nu: (te[j], 0, 0)),
                  pl.BlockSpec((1, two_f // 2, D), lambda j, te, nu: (te[j], 0, 0))],
        out_specs=pl.BlockSpec((tile, D), lambda j, te, nu: (j, 0)),
        scratch_shapes=[pltpu.VMEM((D, two_f), BF16), pltpu.VMEM((two_f // 2, D), BF16)],
    )
    return pl.pallas_call(
        functools.partial(_experts_kernel, d_expert=two_f // 2),
        grid_spec=grid_spec,
        out_shape=jax.ShapeDtypeStruct((n_rows, D), F32),
        compiler_params=_cparams(1),
        name="moe_experts",
    )(tile_expert, n_used, xs, w_gate_up, w_down)


def _combine_kernel(pos_ref, pos_next_ref, ys_ref, route_ref, x1_ref, mod_ref, gf_ref, y_ref,
                    ya_ref, yb_ref, sems, *, tm):
    step = pl.program_id(0) * pl.num_programs(1) + pl.program_id(1)
    n_steps = pl.num_programs(0) * pl.num_programs(1)
    slot = step % 2

    def start_gathers(p_ref, s):
        def issue(k, carry):
            base = pl.multiple_of(k * SUBLANE, SUBLANE)
            for j in range(SUBLANE):
                i = base + j
                _row_copy(ys_ref, p_ref[0, 0, 2 * i], ya_ref.at[s], i, sems.at[s]).start()
                _row_copy(ys_ref, p_ref[0, 0, 2 * i + 1], yb_ref.at[s], i, sems.at[s]).start()
            return carry

        lax.fori_loop(0, tm // SUBLANE, issue, 0)

    @pl.when(step == 0)
    def _():
        start_gathers(pos_ref, 0)

    @pl.when(step + 1 < n_steps)
    def _():
        start_gathers(pos_next_ref, 1 - slot)

    for buf in (ya_ref, yb_ref):
        pltpu.make_async_copy(ys_ref.at[pl.ds(0, tm)], buf.at[slot], sems.at[slot]).wait()
    route = route_ref[0]
    lane = lax.broadcasted_iota(jnp.int32, route.shape, 1)
    w1 = jnp.sum(jnp.where(lane == 2, route, 0.0), axis=-1, keepdims=True)
    w2 = jnp.sum(jnp.where(lane == 3, route, 0.0), axis=-1, keepdims=True)
    x2 = x1_ref[0] + mod_ref[0, 5] * (w1 * ya_ref[slot] + w2 * yb_ref[slot])
    var = jnp.mean(x2 * x2, axis=-1, keepdims=True)
    y_ref[0] = x2 * lax.rsqrt(var + EPS) * gf_ref[...]


def _combine(ys, pos, route, x1, mod, gf, tm):
    B, S, D = x1.shape
    n_t = S // tm
    tok = lambda width: pl.BlockSpec((1, tm, width), lambda b, i: (b, i, 0))
    last = B * n_t - 1
    pos_spec = lambda ahead: pl.BlockSpec(
        (1, 1, 2 * tm), lambda b, i: (jnp.minimum(b * n_t + i + ahead, last), 0, 0), memory_space=pltpu.SMEM)
    pos_steps = pos.reshape(B * n_t, 1, 2 * tm)
    if mod.shape[2] == 1:
        mod_spec = pl.BlockSpec((1, N_COND, 1, D), lambda b, i: (b, 0, 0, 0))
    else:
        mod_spec = pl.BlockSpec((1, N_COND, tm, D), lambda b, i: (b, 0, i, 0))
    return pl.pallas_call(
        functools.partial(_combine_kernel, tm=tm),
        grid=(B, n_t),
        in_specs=[pos_spec(0), pos_spec(1),
                  pl.BlockSpec(memory_space=pl.ANY),
                  tok(ROUTER_LANES), tok(D),
                  mod_spec,
                  _full((1, D))],
        out_specs=tok(D),
        out_shape=jax.ShapeDtypeStruct((B, S, D), F32),
        scratch_shapes=[pltpu.VMEM((2, tm, D), F32), pltpu.VMEM((2, tm, D), F32),
                        pltpu.SemaphoreType.DMA((2,))],
        compiler_params=_cparams(2),
        name="moe_combine",
    )(pos_steps, pos_steps, ys, route, x1, mod, gf)


def _rope_tables(pos):
    half = HEAD_DIM // 2
    inv = ROPE_THETA ** (-jnp.arange(half, dtype=F32) * 2.0 / HEAD_DIM)
    ang = pos.astype(F32)[:, None] * inv[None, :]
    cos, sin = jnp.cos(ang), jnp.sin(ang)
    reps = LANE // HEAD_DIM
    return (jnp.tile(jnp.concatenate([cos, cos], axis=-1), (1, reps)),
            jnp.tile(jnp.concatenate([-sin, sin], axis=-1), (1, reps)))


def kernel(x_prompt, x_sample, cache_kv_w128, cache_kv_w512, cache_kv_w2048, state_pool, c_prompt, c_sample, norm1_g, w_ada, b_ada, w_in, w_attn_out, w_pool, pool_scale, w_pool_out, w_o, norm2_g, w_grp, b_grp, w_exp_router, b_exp_router, w_gate_up, w_down, final_norm_g):
    B, S, D = x_prompt.shape
    N, T, _ = x_sample.shape
    depth = norm1_g.shape[0]
    assert depth == 1, "single trunk layer"
    tm = min(512, S)
    assert S % tm == 0 and all(S % (SPAN * d) == 0 for d in ATTN_DILATIONS)

    n_qkvu = 3 * ATTN_WIDTH + POOL_WIDTH
    w_qkvu = w_in[0, :, :n_qkvu].astype(BF16)
    w_gates = w_in[0, :, n_qkvu:].astype(BF16)
    wa, wpo, wo = w_attn_out[0].astype(BF16), w_pool_out[0].astype(BF16), w_o[0].astype(BF16)
    wpool = w_pool[0].astype(BF16)
    pscale = pool_scale[0].reshape(1, POOL_WIDTH)
    w_r = jnp.concatenate([w_exp_router[0], w_grp[0]], axis=1)
    w_r = jnp.pad(w_r, ((0, 0), (0, ROUTER_LANES - w_r.shape[1])))
    wr_hi = w_r.astype(BF16)
    wr_hilo = jnp.concatenate([wr_hi, (w_r - wr_hi.astype(F32)).astype(BF16)], axis=1)
    b_r = jnp.pad(jnp.concatenate([b_exp_router[0], b_grp[0]]), (0, ROUTER_LANES - N_EXPERTS - MOE_GROUPS))
    b_r = b_r.reshape(1, ROUTER_LANES)
    wgu, wd = w_gate_up[0], w_down[0]
    g1, g2, gf = norm1_g[0].reshape(1, D), norm2_g[0].reshape(1, D), final_norm_g.reshape(1, D)

    mod = _ada(jnp.concatenate([c_prompt, c_sample], axis=0), w_ada[0], b_ada[0])
    mod_p = mod[:B].reshape(B, N_COND, 1, D)
    mod_s = jnp.repeat(mod[B:].reshape(N, N_COND, D), T, axis=0)
    mod_s = jnp.transpose(mod_s, (1, 0, 2)).reshape(1, N_COND, N * T, D)

    cos_p, sin_p = _rope_tables(jnp.arange(S, dtype=jnp.int32))
    outs = _in_proj_prompt(x_prompt, mod_p, g1, w_qkvu, cos_p, sin_p, wpool, pscale, tm)
    qs, ks, vs = outs[0:3], outs[3:6], outs[6:9]
    pooled, kv0, kv1, kv2, ptail = outs[9:14]
    pool_prompt = ptail[:, POOL_HALO - POOL_STATE_LEN:, :][None]

    TS = N * T
    pos_s = PAST_LEN + jnp.arange(T, dtype=jnp.int32)
    cos_s, sin_s = _rope_tables(pos_s)
    cos_s, sin_s = jnp.tile(cos_s, (N, 1)), jnp.tile(sin_s, (N, 1))
    xs = x_sample.reshape(1, TS, D)
    q_s, kvn, u_s = _in_proj_sample(xs, mod_s, g1, w_qkvu, cos_s, sin_s)
    eye = jnp.eye(HEADS_PER_GROUP, dtype=BF16)
    qbd = jnp.einsum('ntghe,hk->ngthke', q_s.reshape(N, T, N_GROUPS, HEADS_PER_GROUP, HEAD_DIM), eye)
    qbd = qbd.reshape(N, N_GROUPS, T * HEADS_PER_GROUP, GROUP_WIDTH)
    caches = [jnp.transpose(c[0], (0, 2, 3, 4, 1)).reshape(N, 2, GROUP_WIDTH, c.shape[2])
              for c in (cache_kv_w128, cache_kv_w512, cache_kv_w2048)]

    attn_parts, (attn_s, pooled_s, ko0, ko1, ko2, pool_s) = _attn_and_sample(
        qs, ks, vs, qbd, kvn.reshape(N, T, 2 * ATTN_WIDTH), caches, state_pool[0],
        u_s.reshape(N, T, POOL_WIDTH), wpool, pscale, nq_max=8)
    attn_inputs = [o for o, _ in attn_parts] + [l for _, l in attn_parts]
    x1, h2, route, counts = _merge(x_prompt, mod_p, g1, g2, attn_inputs, pooled, w_gates, wa, wpo, wo,
                                   wr_hi, wr_hilo, b_r, tm)
    x1s, h2s, route_s, counts_s = _merge(xs, mod_s, g1, g2, [attn_s.reshape(1, TS, GROUP_WIDTH)],
                                         pooled_s.reshape(1, TS, POOL_WIDTH), w_gates, wa, wpo, wo,
                                         wr_hi, wr_hilo, b_r, TS)

    (pos, pos_s), tile_expert, n_used, pads, n_tiles = _routing_tables(
        [route.reshape(B * S, ROUTER_LANES), route_s.reshape(TS, ROUTER_LANES)], [counts, counts_s], EXPERT_TILE)
    sorted_rows = _dispatch(h2.reshape(B * S, D), pos, pads, n_tiles * EXPERT_TILE, tm, EXPERT_TILE)
    sorted_rows = _dispatch(h2s.reshape(TS, D), pos_s, pads, n_tiles * EXPERT_TILE, TS, EXPERT_TILE,
                            xs_prev=sorted_rows)
    ys = _experts(sorted_rows, tile_expert, n_used, wgu, wd, EXPERT_TILE)
    y_prompt = _combine(ys, pos, route, x1, mod_p, gf, tm=min(512, S))
    y_sample = _combine(ys, pos_s, route_s, x1s, mod_s, gf, tm=TS).reshape(N, T, D)

    def kv_shape(a):
        a = a.reshape(a.shape[0], 2, HEADS_PER_GROUP, HEAD_DIM, a.shape[3])
        return jnp.transpose(a, (0, 4, 1, 2, 3))[None]

    return (y_prompt, y_sample, kv_shape(kv0), kv_shape(kv1), kv_shape(kv2), pool_prompt,
            kv_shape(ko0), kv_shape(ko1), kv_shape(ko2), pool_s[None])
```

```python
import functools

import jax
import jax.numpy as jnp
from jax import lax
from jax.experimental import pallas as pl
from jax.experimental.pallas import tpu as pltpu

F32 = jnp.float32
BF16 = jnp.bfloat16

HEAD_DIM = 64
HEADS_PER_GROUP = 4
HEAD_SHIFT = 2
LANE_HEAD_SHIFT = 6
GROUP_WIDTH = HEADS_PER_GROUP * HEAD_DIM
ATTN_WINDOWS = (128, 512, 2048)
ATTN_DILATIONS = (1, 4, 16)
N_GROUPS = 3
SPAN = 128
ATTN_WIDTH = N_GROUPS * GROUP_WIDTH
ROPE_THETA = 10000.0
PAST_LEN = 8192
POOL_WINDOWS = (2, 4, 8, 16)
POOL_GROUP_WIDTH = 128
POOL_WIDTH = 512
POOL_STATE_LEN = 15
POOL_HALO = 16
MOE_GROUPS = 4
EXPERTS_PER_GROUP = 8
N_EXPERTS = 32
N_COND = 6
EPS = 1e-6

LANE = 128
SUBLANE = 8
VMEM_LIMIT_BYTES = 56 * 1024 * 1024

TOKEN_TILE = 512
ATTN_BLOCKS_PER_STEP = 8

NEG = -1e30
BIG_LANE = 1e9

ROUTER_LANES = LANE
GROUP_LANE0 = N_EXPERTS


def _cparams(n_axes):
    return pltpu.CompilerParams(dimension_semantics=("arbitrary",) * n_axes,
                                vmem_limit_bytes=VMEM_LIMIT_BYTES)


def _full(shape):
    nd = len(shape)
    return pl.BlockSpec(tuple(shape), lambda *_: (0,) * nd)


def _norm_mod(x, g, scale, shift):
    var = jnp.mean(x * x, axis=-1, keepdims=True)
    return (x * lax.rsqrt(var + EPS) * g) * (1.0 + scale) + shift


def _rope(x, cos, sin):
    lane = lax.broadcasted_iota(jnp.int32, (x.shape[0], LANE), 1)
    first_half = (lane & (HEAD_DIM - 1)) < (HEAD_DIM // 2)
    outs = []
    for c in range(x.shape[1] // LANE):
        xc = x[:, c * LANE:(c + 1) * LANE]
        partner = jnp.where(first_half, pltpu.roll(xc, LANE - HEAD_DIM // 2, 1),
                            pltpu.roll(xc, HEAD_DIM // 2, 1))
        outs.append(xc * cos + partner * sin)
    return jnp.concatenate(outs, axis=1)


def _silu(x):
    return x * jax.nn.sigmoid(x)


def _ada_kernel(c_ref, w_ref, b_ref, o_ref):
    s = _silu(c_ref[...]).astype(BF16)
    o_ref[...] = jnp.dot(s, w_ref[...].astype(BF16), preferred_element_type=F32) + b_ref[...]


def _ada(c_all, w_ada, b_ada):
    rows, d = c_all.shape
    n_out = w_ada.shape[1]
    tn = 1024
    return pl.pallas_call(
        _ada_kernel,
        grid=(n_out // tn,),
        in_specs=[_full((rows, d)),
                  pl.BlockSpec((d, tn), lambda j: (0, j)),
                  pl.BlockSpec((1, tn), lambda j: (0, j))],
        out_specs=pl.BlockSpec((rows, tn), lambda j: (0, j)),
        out_shape=jax.ShapeDtypeStruct((rows, n_out), F32),
        compiler_params=_cparams(1),
        name="ada",
    )(c_all, w_ada, b_ada.reshape(1, n_out))


def _in_proj_prompt_kernel(x_ref, mod_ref, g1_ref, w_ref, cos_ref, sin_ref, wpool_ref, pscale_ref, *rest,
                           tm, n_tiles, tails):
    qkv_refs = (rest[0:3], rest[3:6], rest[6:9])
    pooled_ref, kv0_ref, kv1_ref, kv2_ref, ptail_ref = rest[9:14]
    stage_refs = rest[14:17]
    ue_ref = rest[17]
    i = pl.program_id(1)
    x = x_ref[0]
    h = _norm_mod(x, g1_ref[...], mod_ref[0, 1], mod_ref[0, 0]).astype(BF16)
    cos = cos_ref[...]
    sin = sin_ref[...]
    y = jnp.dot(h, w_ref[...], preferred_element_type=F32)
    q = _rope(y[:, :ATTN_WIDTH], cos, sin) * (HEAD_DIM ** -0.5)
    k = _rope(y[:, ATTN_WIDTH:2 * ATTN_WIDTH], cos, sin)
    v = y[:, 2 * ATTN_WIDTH:3 * ATTN_WIDTH]
    u = y[:, 3 * ATTN_WIDTH:3 * ATTN_WIDTH + POOL_WIDTH]

    chunks = GROUP_WIDTH // LANE
    for val, out_refs, stage_ref in zip((q, k, v), qkv_refs, stage_refs):
        for g, d in enumerate(ATTN_DILATIONS):
            gl = slice(g * GROUP_WIDTH, (g + 1) * GROUP_WIDTH)
            if d == 1:
                out_refs[g][0] = val[:, gl].astype(BF16)
                continue
            for c in range(chunks):
                stage_ref[g * chunks + c] = val[:, g * GROUP_WIDTH + c * LANE:g * GROUP_WIDTH + (c + 1) * LANE]
            for r in range(d):
                for c in range(chunks):
                    out_refs[g][0, :, r * GROUP_WIDTH + c * LANE:r * GROUP_WIDTH + (c + 1) * LANE] = (
                        stage_ref[g * chunks + c, pl.ds(r, tm // d, stride=d), :].astype(BF16))

    for g, kv_ref in enumerate((kv0_ref, kv1_ref, kv2_ref)):
        first_tile, rows = tails[g]
        lo = tm - rows

        @pl.when(i >= first_tile)
        def _(kv_ref=kv_ref, g=g, lo=lo):
            kv_ref[0, 0] = k[lo:, g * GROUP_WIDTH:(g + 1) * GROUP_WIDTH].T
            kv_ref[0, 1] = v[lo:, g * GROUP_WIDTH:(g + 1) * GROUP_WIDTH].T

    @pl.when(i == 0)
    def _():
        ue_ref[0:POOL_HALO, :] = jnp.zeros((POOL_HALO, POOL_WIDTH), F32)

    ue_ref[POOL_HALO:, :] = u
    pos1 = (i * tm + 1 + lax.broadcasted_iota(jnp.int32, (tm, 1), 0)).astype(F32)
    for gi, w in enumerate(POOL_WINDOWS):
        lanes = slice(gi * POOL_GROUP_WIDTH, (gi + 1) * POOL_GROUP_WIDTH)
        acc = ue_ref[pl.ds(POOL_HALO, tm), lanes]
        for j in range(1, w):
            acc = acc + ue_ref[pl.ds(POOL_HALO - j, tm), lanes]
        mean = acc / jnp.minimum(float(w), pos1)
        z = jnp.dot((mean - u[:, lanes]).astype(BF16), wpool_ref[gi], preferred_element_type=F32)
        pooled_ref[0, :, lanes] = (z * pscale_ref[:, lanes]).astype(BF16)
    ue_ref[0:POOL_HALO, :] = u[tm - POOL_HALO:, :]

    @pl.when(i == n_tiles - 1)
    def _():
        ptail_ref[0] = u[tm - POOL_HALO:, :]


def _in_proj_prompt(x, mod, g1, w_qkvu, cos, sin, w_pool, pool_scale, tm):
    B, S, D = x.shape
    n_tiles = S // tm
    n_w = w_qkvu.shape[1]
    tails, kv_shapes, kv_specs = [], [], []
    for W in ATTN_WINDOWS:
        Wg = min(W, S)
        if Wg >= tm:
            first = n_tiles - Wg // tm
            rows = tm
        else:
            first = n_tiles - 1
            rows = Wg
        tails.append((first, rows))
        kv_shapes.append(jax.ShapeDtypeStruct((B, 2, GROUP_WIDTH, Wg), F32))
        kv_specs.append(pl.BlockSpec((1, 2, GROUP_WIDTH, rows),
                                     lambda b, i, first=first: (b, 0, 0, jnp.maximum(i - first, 0))))
    tok = lambda width: pl.BlockSpec((1, tm, width), lambda b, i: (b, i, 0))
    dil_specs = [pl.BlockSpec((1, tm // d, d * GROUP_WIDTH), lambda b, i: (b, i, 0)) for d in ATTN_DILATIONS]
    dil_shapes = [jax.ShapeDtypeStruct((B, S // d, d * GROUP_WIDTH), BF16) for d in ATTN_DILATIONS]
    outs = pl.pallas_call(
        functools.partial(_in_proj_prompt_kernel, tm=tm, n_tiles=n_tiles, tails=tuple(tails)),
        grid=(B, n_tiles),
        in_specs=[tok(D),
                  pl.BlockSpec((1, N_COND, 1, D), lambda b, i: (b, 0, 0, 0)),
                  _full((1, D)),
                  _full((D, n_w)),
                  pl.BlockSpec((tm, LANE), lambda b, i: (i, 0)),
                  pl.BlockSpec((tm, LANE), lambda b, i: (i, 0)),
                  _full(w_pool.shape),
                  _full((1, POOL_WIDTH))],
        out_specs=dil_specs * 3 + [tok(POOL_WIDTH)] + kv_specs
                  + [pl.BlockSpec((1, POOL_HALO, POOL_WIDTH), lambda b, i: (b, 0, 0))],
        out_shape=dil_shapes * 3
                  + [jax.ShapeDtypeStruct((B, S, POOL_WIDTH), BF16)] + kv_shapes
                  + [jax.ShapeDtypeStruct((B, POOL_HALO, POOL_WIDTH), F32)],
        scratch_shapes=[pltpu.VMEM((ATTN_WIDTH // LANE, tm, LANE), F32)] * 3
                       + [pltpu.VMEM((tm + POOL_HALO, POOL_WIDTH), F32)],
        compiler_params=_cparams(2),
        name="in_proj_prompt",
    )(x, mod, g1, w_qkvu, cos, sin, w_pool, pool_scale)
    return outs


def _in_proj_sample_kernel(x_ref, mod_ref, g1_ref, w_ref, cos_ref, sin_ref, q_ref, kvn_ref, u_ref):
    h = _norm_mod(x_ref[0], g1_ref[...], mod_ref[0, 1], mod_ref[0, 0]).astype(BF16)
    y = jnp.dot(h, w_ref[...], preferred_element_type=F32)
    cos = cos_ref[...]
    sin = sin_ref[...]
    q = _rope(y[:, :ATTN_WIDTH], cos, sin) * (HEAD_DIM ** -0.5)
    k = _rope(y[:, ATTN_WIDTH:2 * ATTN_WIDTH], cos, sin)
    v = y[:, 2 * ATTN_WIDTH:3 * ATTN_WIDTH]
    q_ref[...] = q.astype(BF16)
    for g in range(N_GROUPS):
        gl = slice(g * GROUP_WIDTH, (g + 1) * GROUP_WIDTH)
        kvn_ref[:, 2 * g * GROUP_WIDTH:(2 * g + 1) * GROUP_WIDTH] = k[:, gl]
        kvn_ref[:, (2 * g + 1) * GROUP_WIDTH:(2 * g + 2) * GROUP_WIDTH] = v[:, gl]
    u_ref[...] = y[:, 3 * ATTN_WIDTH:3 * ATTN_WIDTH + POOL_WIDTH]


def _in_proj_sample(x, mod, g1, w_qkvu, cos, sin):
    _, T, D = x.shape
    n_w = w_qkvu.shape[1]
    return pl.pallas_call(
        _in_proj_sample_kernel,
        grid=(1,),
        in_specs=[_full((1, T, D)), _full((1, N_COND, T, D)), _full((1, D)), _full((D, n_w)),
                  _full((T, LANE)), _full((T, LANE))],
        out_specs=[_full((T, ATTN_WIDTH)), _full((T, 2 * ATTN_WIDTH)), _full((T, POOL_WIDTH))],
        out_shape=[jax.ShapeDtypeStruct((T, ATTN_WIDTH), BF16),
                   jax.ShapeDtypeStruct((T, 2 * ATTN_WIDTH), F32),
                   jax.ShapeDtypeStruct((T, POOL_WIDTH), F32)],
        compiler_params=_cparams(1),
        name="in_proj_sample",
    )(x, mod, g1, w_qkvu, cos, sin)


def _attn_body(q_ref, kc_ref, kp_ref, vc_ref, vp_ref, o_ref, lse_ref, kbuf, vbuf, n0, nq):
    kbuf[0:SPAN, :] = kp_ref[0]
    kbuf[SPAN:, :] = kc_ref[0]
    vbuf[0:SPAN, :] = vp_ref[0]
    vbuf[SPAN:, :] = vc_ref[0]
    qi = lax.broadcasted_iota(jnp.int32, (SPAN, 2 * SPAN), 0)
    kj = lax.broadcasted_iota(jnp.int32, (SPAN, 2 * SPAN), 1)
    band = (kj >= qi) & (kj <= qi + SPAN)
    band_first = band & (kj >= jnp.where(n0 > 0, 0, SPAN))
    lane = lax.broadcasted_iota(jnp.int32, (SPAN, LANE), 1)
    low_head = lane < HEAD_DIM
    for j in range(nq):
        valid = band_first if j == 0 else band
        rows = slice(j * SPAN, (j + 1) * SPAN)
        krows = slice(j * SPAN, (j + 2) * SPAN)
        for c in range(GROUP_WIDTH // LANE):
            cl = slice(c * LANE, (c + 1) * LANE)
            q = q_ref[0, rows, cl]
            kk = kbuf[krows, cl]
            vv = vbuf[krows, cl]
            o_pair, lse_pair = [], []
            for hh in range(2):
                mask_h = low_head if hh == 0 else jnp.logical_not(low_head)
                qm = jnp.where(mask_h, q, jnp.zeros_like(q))
                s = lax.dot_general(qm, kk, (((1,), (1,)), ((), ())), preferred_element_type=F32)
                s = jnp.where(valid, s, NEG)
                m = jnp.max(s, axis=-1, keepdims=True)
                p = jnp.exp(s - m)
                den = jnp.sum(p, axis=-1, keepdims=True)
                o_pair.append(jnp.dot(p.astype(BF16), vv, preferred_element_type=F32) / den)
                lse_pair.append(m + jnp.log(den))
            o_ref[0, rows, cl] = jnp.where(low_head, o_pair[0], o_pair[1]).astype(BF16)
            lse_ref[0, rows, cl] = jnp.where(low_head, lse_pair[0], lse_pair[1])


def _sample_body(qbd_ref, kvn_ref, c0_ref, c1_ref, c2_ref, sp_ref, u_ref, wpool_ref, pscale_ref,
                 attn_ref, pooled_ref, o0_ref, o1_ref, o2_ref, po_ref, ue_ref, diff_ref, nt_ref, t_new):
    rows = HEADS_PER_GROUP * t_new
    caches = (c0_ref, c1_ref, c2_ref)
    outs = (o0_ref, o1_ref, o2_ref)

    s_cache, s_new, valid_new = [], [], []
    m = jnp.full((rows, 1), NEG, F32)
    for g in range(N_GROUPS):
        d = ATTN_DILATIONS[g]
        c_ref = caches[g]
        L = c_ref.shape[3]
        qb = qbd_ref[0, g]
        s = jnp.dot(qb, c_ref[0, 0].astype(BF16), preferred_element_type=F32)
        t_row = lax.broadcasted_iota(jnp.int32, (rows, L), 0) >> HEAD_SHIFT
        delta = L + t_row - lax.broadcasted_iota(jnp.int32, (rows, L), 1)
        ok = ((delta & (d - 1)) == 0) & (delta <= SPAN * d)
        s = jnp.where(ok, s, NEG)
        s_cache.append(s)
        m = jnp.maximum(m, jnp.max(s, axis=-1, keepdims=True))
        qf = qb.astype(F32)
        t_col = lax.broadcasted_iota(jnp.int32, (rows, 1), 0) >> HEAD_SHIFT
        sn, okn = [], []
        for tn in range(t_new):
            kn = kvn_ref[0, tn:tn + 1, 2 * g * GROUP_WIDTH:(2 * g + 1) * GROUP_WIDTH]
            kn = kn.astype(BF16).astype(F32)
            dn = t_col - tn
            ok_n = (dn >= 0) & ((dn & (d - 1)) == 0)
            s1 = jnp.where(ok_n, jnp.sum(qf * kn, axis=-1, keepdims=True), NEG)
            sn.append(s1)
            okn.append(ok_n)
            m = jnp.maximum(m, s1)
        s_new.append(sn)
        valid_new.append(okn)

    den = jnp.zeros((rows, 1), F32)
    acc = jnp.zeros((rows, GROUP_WIDTH), F32)
    for g in range(N_GROUPS):
        c_ref = caches[g]
        p = jnp.exp(s_cache[g] - m)
        den = den + jnp.sum(p, axis=-1, keepdims=True)
        acc = acc + lax.dot_general(p.astype(BF16), c_ref[0, 1].astype(BF16), (((1,), (1,)), ((), ())),
                                    preferred_element_type=F32)
        for tn in range(t_new):
            pn = jnp.exp(s_new[g][tn] - m)
            den = den + pn
            vn = kvn_ref[0, tn:tn + 1, (2 * g + 1) * GROUP_WIDTH:(2 * g + 2) * GROUP_WIDTH]
            acc = acc + pn * vn
    row_head = lax.broadcasted_iota(jnp.int32, (rows, GROUP_WIDTH), 0) & (HEADS_PER_GROUP - 1)
    lane_head = lax.broadcasted_iota(jnp.int32, (rows, GROUP_WIDTH), 1) >> LANE_HEAD_SHIFT
    o_diag = jnp.where(row_head == lane_head, acc / den, 0.0).astype(BF16)
    sel = ((lax.broadcasted_iota(jnp.int32, (rows, rows), 1) >> HEAD_SHIFT)
           == lax.broadcasted_iota(jnp.int32, (rows, rows), 0)).astype(BF16)
    attn = jnp.dot(sel, o_diag, preferred_element_type=F32)
    attn_ref[0] = attn[0:t_new].astype(BF16)

    tail_lane = lax.broadcasted_iota(jnp.int32, (GROUP_WIDTH, LANE), 1)
    nt_ref[...] = jnp.zeros(nt_ref.shape, F32)
    for g in range(N_GROUPS):
        c_ref, o_ref = caches[g], outs[g]
        L = c_ref.shape[3]
        for kv in range(2):
            col0 = (2 * g + kv) * GROUP_WIDTH
            nt_ref[LANE - t_new:LANE, :] = kvn_ref[0, :, col0:col0 + GROUP_WIDTH]
            new_t = nt_ref[...].T
            rolled = pltpu.roll(c_ref[0, kv], L - t_new, 1)
            if L > LANE:
                o_ref[0, kv, :, 0:L - LANE] = rolled[:, 0:L - LANE]
            o_ref[0, kv, :, L - LANE:L] = jnp.where(tail_lane >= LANE - t_new, new_t, rolled[:, L - LANE:L])

    ue_ref[0:POOL_STATE_LEN, :] = sp_ref[0]
    ue_ref[POOL_STATE_LEN:POOL_STATE_LEN + t_new, :] = u_ref[0]
    diff_ref[...] = jnp.zeros(diff_ref.shape, F32)
    for tn in range(t_new):
        r = POOL_STATE_LEN + tn
        for gi, w in enumerate(POOL_WINDOWS):
            lanes = slice(gi * POOL_GROUP_WIDTH, (gi + 1) * POOL_GROUP_WIDTH)
            win = jnp.sum(ue_ref[r - w + 1:r + 1, lanes], axis=0, keepdims=True)
            diff_ref[tn:tn + 1, lanes] = win / float(w) - ue_ref[r:r + 1, lanes]
    for gi in range(len(POOL_WINDOWS)):
        lanes = slice(gi * POOL_GROUP_WIDTH, (gi + 1) * POOL_GROUP_WIDTH)
        z = jnp.dot(diff_ref[:, lanes].astype(BF16), wpool_ref[gi], preferred_element_type=F32)
        pooled_ref[0, :, lanes] = (z[0:t_new] * pscale_ref[:, lanes]).astype(BF16)
    po_ref[0] = ue_ref[t_new:t_new + POOL_STATE_LEN, :]


N_ATTN_IN = 5
N_SAMPLE_IN = 9
N_SAMPLE_OUT = 6


def _attn_sample_kernel(*refs, plans, n_seq, t_new):
    n_att = N_GROUPS * N_ATTN_IN
    attn_in = refs[:n_att]
    sample_in = refs[n_att:n_att + N_SAMPLE_IN]
    outs = refs[n_att + N_SAMPLE_IN:]
    attn_out = outs[:2 * N_GROUPS]
    sample_out = outs[2 * N_GROUPS:2 * N_GROUPS + N_SAMPLE_OUT]
    scratch = outs[2 * N_GROUPS + N_SAMPLE_OUT:]
    kv_bufs, sample_scratch = scratch[:2 * N_GROUPS], scratch[2 * N_GROUPS:]
    s = pl.program_id(0)

    for g, (first, count, n_sup, d, nq) in enumerate(plans):
        @pl.when((s >= first) & (s < first + count))
        def _(g=g, first=first, n_sup=n_sup, d=d, nq=nq):
            n0 = ((s - first) // d) % n_sup
            _attn_body(*attn_in[g * N_ATTN_IN:(g + 1) * N_ATTN_IN], attn_out[2 * g], attn_out[2 * g + 1],
                       kv_bufs[2 * g], kv_bufs[2 * g + 1], n0, nq)

    @pl.when(s < n_seq)
    def _():
        _sample_body(*sample_in, *sample_out, *sample_scratch, t_new)


def _attn_and_sample(qs, ks, vs, qbd, kvn, caches, state_pool, u, w_pool, pool_scale, nq_max):
    N, t_new, _ = u.shape
    rows = HEADS_PER_GROUP * t_new
    plans, in_specs, out_specs, out_shape, scratch, args = [], [], [], [], [], []
    first = 0
    for g in range(N_GROUPS):
        B, M, width = qs[g].shape
        d = width // GROUP_WIDTH
        nq = min(nq_max, M // SPAN)
        n_sup = M // (nq * SPAN)
        count = B * n_sup * d
        plans.append((first, count, n_sup, d, nq))

        def block_index(s, first=first, count=count, n_sup=n_sup, d=d):
            local = jnp.clip(s - first, 0, count - 1)
            return local // (n_sup * d), (local // d) % n_sup, local % d

        def cur_map(s, block_index=block_index):
            b, n, r = block_index(s)
            return b, n, r

        def prev_map(s, block_index=block_index, nq=nq):
            b, n, r = block_index(s)
            return b, jnp.maximum(n * nq - 1, 0), r

        cur = pl.BlockSpec((1, nq * SPAN, GROUP_WIDTH), cur_map)
        prev = pl.BlockSpec((1, SPAN, GROUP_WIDTH), prev_map)
        in_specs += [cur, cur, prev, cur, prev]
        args += [qs[g], ks[g], ks[g], vs[g], vs[g]]
        out_specs += [cur, cur]
        out_shape += [jax.ShapeDtypeStruct((B, M, width), BF16), jax.ShapeDtypeStruct((B, M, width), F32)]
        scratch += [pltpu.VMEM(((nq + 1) * SPAN, GROUP_WIDTH), BF16)] * 2
        first += count
    n_steps = max(first, N)

    per_n = lambda shape: pl.BlockSpec((1,) + tuple(shape),
                                       lambda s: (jnp.minimum(s, N - 1),) + (0,) * len(shape))
    cache_specs = [per_n(c.shape[1:]) for c in caches]
    in_specs += ([per_n(qbd.shape[1:]), per_n(kvn.shape[1:])] + cache_specs
                 + [per_n(state_pool.shape[1:]), per_n(u.shape[1:]), _full(w_pool.shape), _full((1, POOL_WIDTH))])
    args += [qbd, kvn, *caches, state_pool, u, w_pool, pool_scale]
    out_specs += ([per_n((t_new, GROUP_WIDTH)), per_n((t_new, POOL_WIDTH))] + cache_specs
                  + [per_n(state_pool.shape[1:])])
    out_shape += ([jax.ShapeDtypeStruct((N, t_new, GROUP_WIDTH), BF16),
                   jax.ShapeDtypeStruct((N, t_new, POOL_WIDTH), BF16)]
                  + [jax.ShapeDtypeStruct(c.shape, F32) for c in caches]
                  + [jax.ShapeDtypeStruct(state_pool.shape, F32)])
    scratch += [pltpu.VMEM((POOL_STATE_LEN + t_new + 5, POOL_WIDTH), F32),
                pltpu.VMEM((rows, POOL_WIDTH), F32),
                pltpu.VMEM((LANE, GROUP_WIDTH), F32)]
    outs = pl.pallas_call(
        functools.partial(_attn_sample_kernel, plans=tuple(plans), n_seq=N, t_new=t_new),
        grid=(n_steps,),
        in_specs=in_specs,
        out_specs=out_specs,
        out_shape=out_shape,
        scratch_shapes=scratch,
        compiler_params=_cparams(1),
        name="attn_and_sample",
    )(*args)
    return [(outs[2 * g], outs[2 * g + 1]) for g in range(N_GROUPS)], outs[2 * N_GROUPS:]


def _route(logits):
    lane = lax.broadcasted_iota(jnp.int32, logits.shape, 1).astype(F32)
    is_grp = (lane >= GROUP_LANE0) & (lane < GROUP_LANE0 + MOE_GROUPS)
    gl = jnp.where(is_grp, logits, NEG)
    gmax = jnp.max(gl, axis=-1, keepdims=True)
    gidx = jnp.min(jnp.where(gl == gmax, lane, BIG_LANE), axis=-1, keepdims=True) - GROUP_LANE0
    gsum = jnp.sum(jnp.where(is_grp, jnp.exp(gl - gmax), 0.0), axis=-1, keepdims=True)
    grp_w = 1.0 / gsum
    lo = gidx * EXPERTS_PER_GROUP
    in_grp = (lane >= lo) & (lane < lo + EXPERTS_PER_GROUP)
    el = jnp.where(in_grp, logits, NEG)
    v1 = jnp.max(el, axis=-1, keepdims=True)
    i1 = jnp.min(jnp.where(el == v1, lane, BIG_LANE), axis=-1, keepdims=True)
    el2 = jnp.where(lane == i1, NEG, el)
    v2 = jnp.max(el2, axis=-1, keepdims=True)
    i2 = jnp.min(jnp.where(el2 == v2, lane, BIG_LANE), axis=-1, keepdims=True)
    e = jnp.exp(v2 - v1)
    w1 = grp_w / (1.0 + e)
    w2 = grp_w * e / (1.0 + e)
    return lane, i1, i2, w1, w2


def _merge_kernel(*refs, n_groups):
    x_ref, mod_ref, g1_ref, g2_ref = refs[0:4]
    n_attn = 2 * n_groups if n_groups > 1 else 1
    attn_refs = refs[4:4 + n_attn]
    rest = refs[4 + n_attn:]
    pooled_ref, wg_ref, wa_ref, wp_ref, wo_ref, wrh_ref, wrc_ref, br_ref = rest[:8]
    x1_ref, h2_ref, route_ref, counts_ref, route_t_ref = rest[8:13]
    scratch_refs = rest[13:]
    stage_refs = scratch_refs[:n_attn] if n_groups > 1 else ()
    carry_ref = scratch_refs[-1]

    x = x_ref[0]
    tm = x.shape[0]
    if n_groups > 1:
        vals = []
        for idx, ref in enumerate(attn_refs):
            d = ATTN_DILATIONS[idx % n_groups]
            if d == 1:
                vals.append(ref[0].astype(F32))
                continue
            stage_ref = stage_refs[idx]
            chunks = GROUP_WIDTH // LANE
            for r in range(d):
                for c in range(chunks):
                    stage_ref[c, pl.ds(r, tm // d, stride=d), :] = (
                        ref[0, :, r * GROUP_WIDTH + c * LANE:r * GROUP_WIDTH + (c + 1) * LANE].astype(F32))
            vals.append(jnp.concatenate([stage_ref[c] for c in range(chunks)], axis=1))
        os_, ls = vals[:n_groups], vals[n_groups:]
        lmax = functools.reduce(jnp.maximum, ls)
        es = [jnp.exp(l - lmax) for l in ls]
        attn = sum(e * o for e, o in zip(es, os_)) / sum(es)
    else:
        attn = attn_refs[0][0]
    a = jnp.dot(attn.astype(BF16), wa_ref[...], preferred_element_type=F32)
    p = jnp.dot(pooled_ref[0], wp_ref[...], preferred_element_type=F32)
    h = _norm_mod(x, g1_ref[...], mod_ref[0, 1], mod_ref[0, 0]).astype(BF16)
    gates = jnp.dot(h, wg_ref[...], preferred_element_type=F32)
    D = x.shape[1]
    merged = jax.nn.sigmoid(gates[:, :D]) * a + jax.nn.sigmoid(gates[:, D:]) * p
    y = jnp.dot(merged.astype(BF16), wo_ref[...], preferred_element_type=F32)
    x1 = x + mod_ref[0, 2] * y
    x1_ref[0] = x1
    h2 = _norm_mod(x1, g2_ref[...], mod_ref[0, 4], mod_ref[0, 3])
    h2_hi = h2.astype(BF16)
    h2_lo = (h2 - h2_hi.astype(F32)).astype(BF16)
    hi_terms = jnp.dot(h2_hi, wrc_ref[...], preferred_element_type=F32)
    logits = (hi_terms[:, :ROUTER_LANES] + jnp.dot(h2_lo, wrh_ref[...], preferred_element_type=F32)
              + hi_terms[:, ROUTER_LANES:]) + br_ref[...]
    lane, i1, i2, w1, w2 = _route(logits)
    h2_ref[0] = h2

    @pl.when((pl.program_id(0) == 0) & (pl.program_id(1) == 0))
    def _():
        carry_ref[...] = jnp.zeros(carry_ref.shape, F32)

    hit = ((lane == i1) | (lane == i2)).astype(BF16)
    ltri = (lax.broadcasted_iota(jnp.int32, (tm, tm), 0) >= lax.broadcasted_iota(jnp.int32, (tm, tm), 1))
    prefix = jnp.dot(ltri.astype(BF16), hit, preferred_element_type=F32) + carry_ref[...]
    rank1 = jnp.sum(jnp.where(lane == i1, prefix, 0.0), axis=-1, keepdims=True) - 1.0
    rank2 = jnp.sum(jnp.where(lane == i2, prefix, 0.0), axis=-1, keepdims=True) - 1.0
    carry_ref[...] = prefix[tm - 1:tm, :]
    counts_ref[...] = jnp.broadcast_to(prefix[tm - 1:tm, :], counts_ref.shape)
    cols = (i1, i2, w1, w2, rank1, rank2)
    route = jnp.zeros(logits.shape, F32)
    for c, col in enumerate(cols):
        route = jnp.where(lane == float(c), col, route)
    route_ref[0] = route
    route_t_ref[...] = route.T[0:SUBLANE, :]


def _merge(x, mod, g1, g2, attn_inputs, pooled, w_gates, w_attn_out, w_pool_out, w_o, wr_hi, wr_hilo, b_r, tm):
    B, S, D = x.shape
    R = mod.shape[2]
    n_groups = len(attn_inputs) // 2 if len(attn_inputs) > 1 else 1
    tok = lambda width: pl.BlockSpec((1, tm, width), lambda b, i: (b, i, 0))
    if R == 1:
        mod_spec = pl.BlockSpec((1, N_COND, 1, D), lambda b, i: (b, 0, 0, 0))
    else:
        mod_spec = pl.BlockSpec((1, N_COND, tm, D), lambda b, i: (b, 0, i, 0))
    weights = (w_gates, w_attn_out, w_pool_out, w_o, wr_hi, wr_hilo, b_r)
    if n_groups > 1:
        attn_specs = [pl.BlockSpec((1, tm // (a.shape[2] // GROUP_WIDTH), a.shape[2]), lambda b, i: (b, i, 0))
                      for a in attn_inputs]
        scratch = [pltpu.VMEM((GROUP_WIDTH // LANE, tm, LANE), F32)] * len(attn_inputs)
    else:
        attn_specs = [tok(GROUP_WIDTH)]
        scratch = []
    n_t = S // tm
    out_specs = [tok(D), tok(D), tok(ROUTER_LANES), _full((SUBLANE, ROUTER_LANES)),
                 pl.BlockSpec((SUBLANE, tm), lambda b, i: (0, b * n_t + i))]
    out_shape = [jax.ShapeDtypeStruct((B, S, D), F32),
                 jax.ShapeDtypeStruct((B, S, D), F32),
                 jax.ShapeDtypeStruct((B, S, ROUTER_LANES), F32),
                 jax.ShapeDtypeStruct((SUBLANE, ROUTER_LANES), F32),
                 jax.ShapeDtypeStruct((SUBLANE, B * S), F32)]
    scratch = scratch + [pltpu.VMEM((1, ROUTER_LANES), F32)]
    return pl.pallas_call(
        functools.partial(_merge_kernel, n_groups=n_groups),
        grid=(B, S // tm),
        in_specs=[tok(D), mod_spec, _full((1, D)), _full((1, D))]
                 + attn_specs + [tok(POOL_WIDTH)]
                 + [_full(w.shape) for w in weights],
        out_specs=out_specs,
        out_shape=out_shape,
        scratch_shapes=scratch,
        compiler_params=_cparams(2),
        name="merge",
    )(x, mod, g1, g2, *attn_inputs, pooled, *weights)


EXPERT_TILE = 512
PAD_CHUNK = 32


def _routing_tables(routes, counts, tile):
    cnts = [c[0, :N_EXPERTS].astype(jnp.int32) for c in counts]
    cnt = sum(cnts)
    padded = ((cnt + tile - 1) // tile) * tile
    ends = jnp.cumsum(padded)
    base = ends - padded
    poss = []
    for route in routes:
        e12 = route[0:2].astype(jnp.int32)
        start = jnp.zeros_like(e12)
        for e in range(N_EXPERTS):
            start = jnp.where(e12 == e, base[e], start)
        poss.append(start + route[4:6].astype(jnp.int32))
        base = base + cnts[len(poss) - 1]
    n_pairs = 2 * sum(r.shape[1] for r in routes)
    n_tiles = -(-n_pairs // tile) + N_EXPERTS
    starts = jnp.arange(n_tiles, dtype=jnp.int32) * tile
    n_used = ends[-1] // tile
    tile_expert = jnp.sum(starts[:, None] >= ends[None, :], axis=1).astype(jnp.int32)
    last = jnp.take(tile_expert, n_used - 1)
    tile_expert = jnp.where(jnp.arange(n_tiles) < n_used, tile_expert, last)
    pads = jnp.stack([ends - padded + cnts[0], ends], axis=1).reshape(-1).astype(jnp.int32)
    return poss, tile_expert, n_used.reshape(1).astype(jnp.int32), pads, n_tiles


def _pos_steps(pos, tm):
    steps = pos.shape[1] // tm
    return jnp.transpose(pos.reshape(2, steps, tm), (1, 0, 2)).reshape(steps, 1, 2 * tm)


def _row_copy(src_ref, src_row, dst_ref, dst_row, sem):
    return pltpu.make_async_copy(src_ref.at[pl.ds(src_row, 1)], dst_ref.at[pl.ds(dst_row, 1)], sem)


def _dispatch_kernel(pads_ref, pos_ref, h_ref, *rest, tm, tile, first):
    xs_ref, zero_ref, sem = rest[-3:]

    def zero_fill():
        zero_ref[...] = jnp.zeros(zero_ref.shape, zero_ref.dtype)
        n_tiles = xs_ref.shape[0] // tile
        first_unused = pads_ref[2 * N_EXPERTS - 1] // tile

        def tile_copy(j):
            return pltpu.make_async_copy(zero_ref, xs_ref.at[pl.ds(pl.multiple_of(j * tile, tile), tile)], sem)

        def fill_tile(j, c):
            tile_copy(j).start()
            return c

        def drain_tile(j, c):
            tile_copy(j).wait()
            return c

        lax.fori_loop(first_unused, n_tiles, fill_tile, 0)
        lax.fori_loop(first_unused, n_tiles, drain_tile, 0)

        def chunk_copy(c):
            rows = pl.ds(pl.multiple_of(c * PAD_CHUNK, PAD_CHUNK), PAD_CHUNK)
            return pltpu.make_async_copy(zero_ref.at[pl.ds(0, PAD_CHUNK)], xs_ref.at[rows], sem)

        def for_each_pad_piece(on_row, on_chunk):
            def per_expert(e, carry):
                lo, hi = pads_ref[2 * e], pads_ref[2 * e + 1]
                first_chunk = (lo + PAD_CHUNK - 1) // PAD_CHUNK
                lax.fori_loop(lo, jnp.minimum(first_chunk * PAD_CHUNK, hi), on_row, 0)
                lax.fori_loop(first_chunk, hi // PAD_CHUNK, on_chunk, 0)
                return carry

            lax.fori_loop(0, N_EXPERTS, per_expert, 0)

        def start_row(r, c):
            _row_copy(zero_ref, 0, xs_ref, r, sem).start()
            return c

        def wait_row(r, c):
            _row_copy(zero_ref, 0, xs_ref, r, sem).wait()
            return c

        def start_chunk(c, carry):
            chunk_copy(c).start()
            return carry

        def wait_chunk(c, carry):
            chunk_copy(c).wait()
            return carry

        for_each_pad_piece(start_row, start_chunk)
        for_each_pad_piece(wait_row, wait_chunk)

    if first:
        pl.when(pl.program_id(0) == 0)(zero_fill)

    def issue(k, carry):
        base = pl.multiple_of(k * SUBLANE, SUBLANE)
        for j in range(SUBLANE):
            i = base + j
            _row_copy(h_ref, i, xs_ref, pos_ref[0, 0, i], sem).start()
            _row_copy(h_ref, i, xs_ref, pos_ref[0, 0, tm + i], sem).start()
        return carry

    lax.fori_loop(0, tm // SUBLANE, issue, 0)
    for _ in range(2):
        pltpu.make_async_copy(h_ref, xs_ref.at[pl.ds(0, tm)], sem).wait()


def _dispatch(h2, pos, pads, n_rows, tm, tile, xs_prev=None):
    T, width = h2.shape
    in_specs = [pl.BlockSpec((1, 1, 2 * tm), lambda i, pads: (i, 0, 0), memory_space=pltpu.SMEM),
                pl.BlockSpec((tm, width), lambda i, pads: (i, 0))]
    args = [pads, _pos_steps(pos, tm), h2]
    aliases = {}
    if xs_prev is not None:
        in_specs.append(pl.BlockSpec(memory_space=pl.ANY))
        args.append(xs_prev)
        aliases = {3: 0}
    grid_spec = pltpu.PrefetchScalarGridSpec(
        num_scalar_prefetch=1,
        grid=(T // tm,),
        in_specs=in_specs,
        out_specs=pl.BlockSpec(memory_space=pl.ANY),
        scratch_shapes=[pltpu.VMEM((tile, width), h2.dtype), pltpu.SemaphoreType.DMA(())],
    )
    return pl.pallas_call(
        functools.partial(_dispatch_kernel, tm=tm, tile=tile, first=xs_prev is None),
        grid_spec=grid_spec,
        out_shape=jax.ShapeDtypeStruct((n_rows, width), h2.dtype),
        input_output_aliases=aliases,
        compiler_params=_cparams(1),
        name="moe_dispatch",
    )(*args)


def _experts_kernel(te_ref, nu_ref, xs_ref, wgu_ref, wd_ref, ys_ref, wgu_bf_ref, wd_bf_ref, *, d_expert):
    j = pl.program_id(0)

    @pl.when(j >= nu_ref[0])
    def _():
        ys_ref[...] = jnp.zeros(ys_ref.shape, F32)

    @pl.when((j == 0) | (te_ref[j] != te_ref[jnp.maximum(j - 1, 0)]))
    def _():
        wgu_bf_ref[...] = wgu_ref[0].astype(BF16)
        wd_bf_ref[...] = wd_ref[0].astype(BF16)

    @pl.when(j < nu_ref[0])
    def _():
        gu = jnp.dot(xs_ref[...].astype(BF16), wgu_bf_ref[...], preferred_element_type=F32)
        act = _silu(gu[:, :d_expert]) * gu[:, d_expert:]
        ys_ref[...] = jnp.dot(act.astype(BF16), wd_bf_ref[...], preferred_element_type=F32)


def _experts(xs, tile_expert, n_used, w_gate_up, w_down, tile):
    n_rows, width = xs.shape
    _, D, two_f = w_gate_up.shape
    grid_spec = pltpu.PrefetchScalarGridSpec(
        num_scalar_prefetch=2,
        grid=(n_rows // tile,),
        in_specs=[pl.BlockSpec((tile, width), lambda j, te, nu: (jnp.minimum(j, nu[0] - 1), 0)),
                  pl.BlockSpec((1, D, two_f), lambda j, te, nu: (te[j], 0, 0)),
                  pl.BlockSpec((1, two_f // 2, D), lambda j, te, nu: (te[j], 0, 0))],
        out_specs=pl.BlockSpec((tile, D), lambda j, te, nu: (j, 0)),
        scratch_shapes=[pltpu.VMEM((D, two_f), BF16), pltpu.VMEM((two_f // 2, D), BF16)],
    )
    return pl.pallas_call(
        functools.partial(_experts_kernel, d_expert=two_f // 2),
        grid_spec=grid_spec,
        out_shape=jax.ShapeDtypeStruct((n_rows, D), F32),
        compiler_params=_cparams(1),
        name="moe_experts",
    )(tile_expert, n_used, xs, w_gate_up, w_down)


def _combine_kernel(pos_ref, pos_next_ref, ys_ref, route_ref, x1_ref, mod_ref, gf_ref, y_ref,
                    ya_ref, yb_ref, sems, *, tm):
    step = pl.program_id(0) * pl.num_programs(1) + pl.program_id(1)
    n_steps = pl.num_programs(0) * pl.num_programs(1)
    slot = step % 2

    def start_gathers(p_ref, s):
        def issue(k, carry):
            base = pl.multiple_of(k * SUBLANE, SUBLANE)
            for j in range(SUBLANE):
                i = base + j
                _row_copy(ys_ref, p_ref[0, 0, i], ya_ref.at[s], i, sems.at[s]).start()
                _row_copy(ys_ref, p_ref[0, 0, tm + i], yb_ref.at[s], i, sems.at[s]).start()
            return carry

        lax.fori_loop(0, tm // SUBLANE, issue, 0)

    @pl.when(step == 0)
    def _():
        start_gathers(pos_ref, 0)

    @pl.when(step + 1 < n_steps)
    def _():
        start_gathers(pos_next_ref, 1 - slot)

    for buf in (ya_ref, yb_ref):
        pltpu.make_async_copy(ys_ref.at[pl.ds(0, tm)], buf.at[slot], sems.at[slot]).wait()
    route = route_ref[0]
    lane = lax.broadcasted_iota(jnp.int32, route.shape, 1)
    w1 = jnp.sum(jnp.where(lane == 2, route, 0.0), axis=-1, keepdims=True)
    w2 = jnp.sum(jnp.where(lane == 3, route, 0.0), axis=-1, keepdims=True)
    x2 = x1_ref[0] + mod_ref[0, 5] * (w1 * ya_ref[slot] + w2 * yb_ref[slot])
    var = jnp.mean(x2 * x2, axis=-1, keepdims=True)
    y_ref[0] = x2 * lax.rsqrt(var + EPS) * gf_ref[...]


def _combine(ys, pos, route, x1, mod, gf, tm):
    B, S, D = x1.shape
    n_t = S // tm
    tok = lambda width: pl.BlockSpec((1, tm, width), lambda b, i: (b, i, 0))
    last = B * n_t - 1
    pos_spec = lambda ahead: pl.BlockSpec(
        (1, 1, 2 * tm), lambda b, i: (jnp.minimum(b * n_t + i + ahead, last), 0, 0), memory_space=pltpu.SMEM)
    pos_steps = _pos_steps(pos, tm)
    if mod.shape[2] == 1:
        mod_spec = pl.BlockSpec((1, N_COND, 1, D), lambda b, i: (b, 0, 0, 0))
    else:
        mod_spec = pl.BlockSpec((1, N_COND, tm, D), lambda b, i: (b, 0, i, 0))
    return pl.pallas_call(
        functools.partial(_combine_kernel, tm=tm),
        grid=(B, n_t),
        in_specs=[pos_spec(0), pos_spec(1),
                  pl.BlockSpec(memory_space=pl.ANY),
                  tok(ROUTER_LANES), tok(D),
                  mod_spec,
                  _full((1, D))],
        out_specs=tok(D),
        out_shape=jax.ShapeDtypeStruct((B, S, D), F32),
        scratch_shapes=[pltpu.VMEM((2, tm, D), F32), pltpu.VMEM((2, tm, D), F32),
                        pltpu.SemaphoreType.DMA((2,))],
        compiler_params=_cparams(2),
        name="moe_combine",
    )(pos_steps, pos_steps, ys, route, x1, mod, gf)


def _rope_tables(pos):
    half = HEAD_DIM // 2
    inv = ROPE_THETA ** (-jnp.arange(half, dtype=F32) * 2.0 / HEAD_DIM)
    ang = pos.astype(F32)[:, None] * inv[None, :]
    cos, sin = jnp.cos(ang), jnp.sin(ang)
    reps = LANE // HEAD_DIM
    return (jnp.tile(jnp.concatenate([cos, cos], axis=-1), (1, reps)),
            jnp.tile(jnp.concatenate([-sin, sin], axis=-1), (1, reps)))


def kernel(x_prompt, x_sample, cache_kv_w128, cache_kv_w512, cache_kv_w2048, state_pool, c_prompt, c_sample, norm1_g, w_ada, b_ada, w_in, w_attn_out, w_pool, pool_scale, w_pool_out, w_o, norm2_g, w_grp, b_grp, w_exp_router, b_exp_router, w_gate_up, w_down, final_norm_g):
    B, S, D = x_prompt.shape
    N, T, _ = x_sample.shape
    depth = norm1_g.shape[0]
    assert depth == 1, "single trunk layer"
    tm = min(TOKEN_TILE, S)
    assert S % tm == 0 and all(S % (SPAN * d) == 0 for d in ATTN_DILATIONS)

    n_qkvu = 3 * ATTN_WIDTH + POOL_WIDTH
    w_qkvu = w_in[0, :, :n_qkvu].astype(BF16)
    w_gates = w_in[0, :, n_qkvu:].astype(BF16)
    wa, wpo, wo = w_attn_out[0].astype(BF16), w_pool_out[0].astype(BF16), w_o[0].astype(BF16)
    wpool = w_pool[0].astype(BF16)
    pscale = pool_scale[0].reshape(1, POOL_WIDTH)
    w_r = jnp.concatenate([w_exp_router[0], w_grp[0]], axis=1)
    w_r = jnp.pad(w_r, ((0, 0), (0, ROUTER_LANES - w_r.shape[1])))
    wr_hi = w_r.astype(BF16)
    wr_hilo = jnp.concatenate([wr_hi, (w_r - wr_hi.astype(F32)).astype(BF16)], axis=1)
    b_r = jnp.pad(jnp.concatenate([b_exp_router[0], b_grp[0]]), (0, ROUTER_LANES - N_EXPERTS - MOE_GROUPS))
    b_r = b_r.reshape(1, ROUTER_LANES)
    wgu, wd = w_gate_up[0], w_down[0]
    g1, g2, gf = norm1_g[0].reshape(1, D), norm2_g[0].reshape(1, D), final_norm_g.reshape(1, D)

    mod = _ada(jnp.concatenate([c_prompt, c_sample], axis=0), w_ada[0], b_ada[0])
    mod_p = mod[:B].reshape(B, N_COND, 1, D)
    mod_s = jnp.repeat(mod[B:].reshape(N, N_COND, D), T, axis=0)
    mod_s = jnp.transpose(mod_s, (1, 0, 2)).reshape(1, N_COND, N * T, D)

    cos_p, sin_p = _rope_tables(jnp.arange(S, dtype=jnp.int32))
    outs = _in_proj_prompt(x_prompt, mod_p, g1, w_qkvu, cos_p, sin_p, wpool, pscale, tm)
    qs, ks, vs = outs[0:3], outs[3:6], outs[6:9]
    pooled, kv0, kv1, kv2, ptail = outs[9:14]
    pool_prompt = ptail[:, POOL_HALO - POOL_STATE_LEN:, :][None]

    TS = N * T
    pos_s = PAST_LEN + jnp.arange(T, dtype=jnp.int32)
    cos_s, sin_s = _rope_tables(pos_s)
    cos_s, sin_s = jnp.tile(cos_s, (N, 1)), jnp.tile(sin_s, (N, 1))
    xs = x_sample.reshape(1, TS, D)
    q_s, kvn, u_s = _in_proj_sample(xs, mod_s, g1, w_qkvu, cos_s, sin_s)
    eye = jnp.eye(HEADS_PER_GROUP, dtype=BF16)
    qbd = jnp.einsum('ntghe,hk->ngthke', q_s.reshape(N, T, N_GROUPS, HEADS_PER_GROUP, HEAD_DIM), eye)
    qbd = qbd.reshape(N, N_GROUPS, T * HEADS_PER_GROUP, GROUP_WIDTH)
    caches = [jnp.transpose(c[0], (0, 2, 3, 4, 1)).reshape(N, 2, GROUP_WIDTH, c.shape[2])
              for c in (cache_kv_w128, cache_kv_w512, cache_kv_w2048)]

    attn_parts, (attn_s, pooled_s, ko0, ko1, ko2, pool_s) = _attn_and_sample(
        qs, ks, vs, qbd, kvn.reshape(N, T, 2 * ATTN_WIDTH), caches, state_pool[0],
        u_s.reshape(N, T, POOL_WIDTH), wpool, pscale, nq_max=ATTN_BLOCKS_PER_STEP)
    attn_inputs = [o for o, _ in attn_parts] + [l for _, l in attn_parts]
    x1, h2, route, counts, route_t = _merge(x_prompt, mod_p, g1, g2, attn_inputs, pooled, w_gates, wa, wpo, wo,
                                            wr_hi, wr_hilo, b_r, tm)
    x1s, h2s, route_s, counts_s, route_ts = _merge(xs, mod_s, g1, g2, [attn_s.reshape(1, TS, GROUP_WIDTH)],
                                                   pooled_s.reshape(1, TS, POOL_WIDTH), w_gates, wa, wpo, wo,
                                                   wr_hi, wr_hilo, b_r, TS)

    (pos, pos_s), tile_expert, n_used, pads, n_tiles = _routing_tables(
        [route_t, route_ts], [counts, counts_s], EXPERT_TILE)
    sorted_rows = _dispatch(h2.reshape(B * S, D), pos, pads, n_tiles * EXPERT_TILE, tm, EXPERT_TILE)
    sorted_rows = _dispatch(h2s.reshape(TS, D), pos_s, pads, n_tiles * EXPERT_TILE, TS, EXPERT_TILE,
                            xs_prev=sorted_rows)
    ys = _experts(sorted_rows, tile_expert, n_used, wgu, wd, EXPERT_TILE)
    y_prompt = _combine(ys, pos, route, x1, mod_p, gf, tm=tm)
    y_sample = _combine(ys, pos_s, route_s, x1s, mod_s, gf, tm=TS).reshape(N, T, D)

    def kv_shape(a):
        a = a.reshape(a.shape[0], 2, HEADS_PER_GROUP, HEAD_DIM, a.shape[3])
        return jnp.transpose(a, (0, 4, 1, 2, 3))[None]

    return (y_prompt, y_sample, kv_shape(kv0), kv_shape(kv1), kv_shape(kv2), pool_prompt,
            kv_shape(ko0), kv_shape(ko1), kv_shape(ko2), pool_s[None])
```

```python
import functools

import jax
import jax.numpy as jnp
from jax import lax
from jax.experimental import pallas as pl
from jax.experimental.pallas import tpu as pltpu

F32 = jnp.float32
BF16 = jnp.bfloat16

HEAD_DIM = 64
HEADS_PER_GROUP = 4
HEAD_SHIFT = 2
LANE_HEAD_SHIFT = 6
GROUP_WIDTH = HEADS_PER_GROUP * HEAD_DIM
ATTN_WINDOWS = (128, 512, 2048)
ATTN_DILATIONS = (1, 4, 16)
N_GROUPS = 3
SPAN = 128
ATTN_WIDTH = N_GROUPS * GROUP_WIDTH
ROPE_THETA = 10000.0
PAST_LEN = 8192
POOL_WINDOWS = (2, 4, 8, 16)
POOL_GROUP_WIDTH = 128
POOL_WIDTH = 512
POOL_STATE_LEN = 15
POOL_HALO = 16
MOE_GROUPS = 4
EXPERTS_PER_GROUP = 8
N_EXPERTS = 32
N_COND = 6
EPS = 1e-6

LANE = 128
SUBLANE = 8
VMEM_LIMIT_BYTES = 56 * 1024 * 1024

TOKEN_TILE = 512
ATTN_BLOCKS_PER_STEP = 8

NEG = -1e30
BIG_LANE = 1e9

ROUTER_LANES = LANE
GROUP_LANE0 = N_EXPERTS


def _cparams(n_axes):
    return pltpu.CompilerParams(dimension_semantics=("arbitrary",) * n_axes,
                                vmem_limit_bytes=VMEM_LIMIT_BYTES)


def _full(shape):
    nd = len(shape)
    return pl.BlockSpec(tuple(shape), lambda *_: (0,) * nd)


def _norm_mod(x, g, scale, shift):
    var = jnp.mean(x * x, axis=-1, keepdims=True)
    return (x * lax.rsqrt(var + EPS) * g) * (1.0 + scale) + shift


def _rope(x, cos, sin):
    lane = lax.broadcasted_iota(jnp.int32, (x.shape[0], LANE), 1)
    first_half = (lane & (HEAD_DIM - 1)) < (HEAD_DIM // 2)
    outs = []
    for c in range(x.shape[1] // LANE):
        xc = x[:, c * LANE:(c + 1) * LANE]
        partner = jnp.where(first_half, pltpu.roll(xc, LANE - HEAD_DIM // 2, 1),
                            pltpu.roll(xc, HEAD_DIM // 2, 1))
        outs.append(xc * cos + partner * sin)
    return jnp.concatenate(outs, axis=1)


def _silu(x):
    return x * jax.nn.sigmoid(x)


def _ada_kernel(c_ref, w_ref, b_ref, o_ref):
    s = _silu(c_ref[...]).astype(BF16)
    o_ref[...] = jnp.dot(s, w_ref[...].astype(BF16), preferred_element_type=F32) + b_ref[...]


def _ada(c_all, w_ada, b_ada):
    rows, d = c_all.shape
    n_out = w_ada.shape[1]
    tn = 1024
    return pl.pallas_call(
        _ada_kernel,
        grid=(n_out // tn,),
        in_specs=[_full((rows, d)),
                  pl.BlockSpec((d, tn), lambda j: (0, j)),
                  pl.BlockSpec((1, tn), lambda j: (0, j))],
        out_specs=pl.BlockSpec((rows, tn), lambda j: (0, j)),
        out_shape=jax.ShapeDtypeStruct((rows, n_out), F32),
        compiler_params=_cparams(1),
        name="ada",
    )(c_all, w_ada, b_ada.reshape(1, n_out))


def _in_proj_prompt_kernel(x_ref, mod_ref, g1_ref, w_ref, cos_ref, sin_ref, wpool_ref, pscale_ref, *rest,
                           tm, n_tiles, tails):
    qkv_refs = (rest[0:3], rest[3:6], rest[6:9])
    pooled_ref, kv0_ref, kv1_ref, kv2_ref, ptail_ref = rest[9:14]
    stage_refs = rest[14:17]
    ue_ref = rest[17]
    i = pl.program_id(1)
    x = x_ref[0]
    h = _norm_mod(x, g1_ref[...], mod_ref[0, 1], mod_ref[0, 0]).astype(BF16)
    cos = cos_ref[...]
    sin = sin_ref[...]
    y = jnp.dot(h, w_ref[...], preferred_element_type=F32)
    q = _rope(y[:, :ATTN_WIDTH], cos, sin) * (HEAD_DIM ** -0.5)
    k = _rope(y[:, ATTN_WIDTH:2 * ATTN_WIDTH], cos, sin)
    v = y[:, 2 * ATTN_WIDTH:3 * ATTN_WIDTH]
    u = y[:, 3 * ATTN_WIDTH:3 * ATTN_WIDTH + POOL_WIDTH]

    chunks = GROUP_WIDTH // LANE
    for val, out_refs, stage_ref in zip((q, k, v), qkv_refs, stage_refs):
        for g, d in enumerate(ATTN_DILATIONS):
            gl = slice(g * GROUP_WIDTH, (g + 1) * GROUP_WIDTH)
            if d == 1:
                out_refs[g][0] = val[:, gl].astype(BF16)
                continue
            for c in range(chunks):
                stage_ref[g * chunks + c] = val[:, g * GROUP_WIDTH + c * LANE:g * GROUP_WIDTH + (c + 1) * LANE]
            for r in range(d):
                for c in range(chunks):
                    out_refs[g][0, :, r * GROUP_WIDTH + c * LANE:r * GROUP_WIDTH + (c + 1) * LANE] = (
                        stage_ref[g * chunks + c, pl.ds(r, tm // d, stride=d), :].astype(BF16))

    for g, kv_ref in enumerate((kv0_ref, kv1_ref, kv2_ref)):
        first_tile, rows = tails[g]
        lo = tm - rows

        @pl.when(i >= first_tile)
        def _(kv_ref=kv_ref, g=g, lo=lo):
            kv_ref[0, 0] = k[lo:, g * GROUP_WIDTH:(g + 1) * GROUP_WIDTH].T
            kv_ref[0, 1] = v[lo:, g * GROUP_WIDTH:(g + 1) * GROUP_WIDTH].T

    @pl.when(i == 0)
    def _():
        ue_ref[0:POOL_HALO, :] = jnp.zeros((POOL_HALO, POOL_WIDTH), F32)

    ue_ref[POOL_HALO:, :] = u
    pos1 = (i * tm + 1 + lax.broadcasted_iota(jnp.int32, (tm, 1), 0)).astype(F32)
    for gi, w in enumerate(POOL_WINDOWS):
        lanes = slice(gi * POOL_GROUP_WIDTH, (gi + 1) * POOL_GROUP_WIDTH)
        acc = ue_ref[pl.ds(POOL_HALO, tm), lanes]
        for j in range(1, w):
            acc = acc + ue_ref[pl.ds(POOL_HALO - j, tm), lanes]
        mean = acc / jnp.minimum(float(w), pos1)
        z = jnp.dot((mean - u[:, lanes]).astype(BF16), wpool_ref[gi], preferred_element_type=F32)
        pooled_ref[0, :, lanes] = (z * pscale_ref[:, lanes]).astype(BF16)
    ue_ref[0:POOL_HALO, :] = u[tm - POOL_HALO:, :]

    @pl.when(i == n_tiles - 1)
    def _():
        ptail_ref[0] = u[tm - POOL_HALO:, :]


def _in_proj_prompt(x, mod, g1, w_qkvu, cos, sin, w_pool, pool_scale, tm):
    B, S, D = x.shape
    n_tiles = S // tm
    n_w = w_qkvu.shape[1]
    tails, kv_shapes, kv_specs = [], [], []
    for W in ATTN_WINDOWS:
        Wg = min(W, S)
        if Wg >= tm:
            first = n_tiles - Wg // tm
            rows = tm
        else:
            first = n_tiles - 1
            rows = Wg
        tails.append((first, rows))
        kv_shapes.append(jax.ShapeDtypeStruct((B, 2, GROUP_WIDTH, Wg), F32))
        kv_specs.append(pl.BlockSpec((1, 2, GROUP_WIDTH, rows),
                                     lambda b, i, first=first: (b, 0, 0, jnp.maximum(i - first, 0))))
    tok = lambda width: pl.BlockSpec((1, tm, width), lambda b, i: (b, i, 0))
    dil_specs = [pl.BlockSpec((1, tm // d, d * GROUP_WIDTH), lambda b, i: (b, i, 0)) for d in ATTN_DILATIONS]
    dil_shapes = [jax.ShapeDtypeStruct((B, S // d, d * GROUP_WIDTH), BF16) for d in ATTN_DILATIONS]
    outs = pl.pallas_call(
        functools.partial(_in_proj_prompt_kernel, tm=tm, n_tiles=n_tiles, tails=tuple(tails)),
        grid=(B, n_tiles),
        in_specs=[tok(D),
                  pl.BlockSpec((1, N_COND, 1, D), lambda b, i: (b, 0, 0, 0)),
                  _full((1, D)),
                  _full((D, n_w)),
                  pl.BlockSpec((tm, LANE), lambda b, i: (i, 0)),
                  pl.BlockSpec((tm, LANE), lambda b, i: (i, 0)),
                  _full(w_pool.shape),
                  _full((1, POOL_WIDTH))],
        out_specs=dil_specs * 3 + [tok(POOL_WIDTH)] + kv_specs
                  + [pl.BlockSpec((1, POOL_HALO, POOL_WIDTH), lambda b, i: (b, 0, 0))],
        out_shape=dil_shapes * 3
                  + [jax.ShapeDtypeStruct((B, S, POOL_WIDTH), BF16)] + kv_shapes
                  + [jax.ShapeDtypeStruct((B, POOL_HALO, POOL_WIDTH), F32)],
        scratch_shapes=[pltpu.VMEM((ATTN_WIDTH // LANE, tm, LANE), F32)] * 3
                       + [pltpu.VMEM((tm + POOL_HALO, POOL_WIDTH), F32)],
        compiler_params=_cparams(2),
        name="in_proj_prompt",
    )(x, mod, g1, w_qkvu, cos, sin, w_pool, pool_scale)
    return outs


def _in_proj_sample_kernel(x_ref, mod_ref, g1_ref, w_ref, cos_ref, sin_ref, q_ref, kvn_ref, u_ref):
    h = _norm_mod(x_ref[0], g1_ref[...], mod_ref[0, 1], mod_ref[0, 0]).astype(BF16)
    y = jnp.dot(h, w_ref[...], preferred_element_type=F32)
    cos = cos_ref[...]
    sin = sin_ref[...]
    q = _rope(y[:, :ATTN_WIDTH], cos, sin) * (HEAD_DIM ** -0.5)
    k = _rope(y[:, ATTN_WIDTH:2 * ATTN_WIDTH], cos, sin)
    v = y[:, 2 * ATTN_WIDTH:3 * ATTN_WIDTH]
    q_ref[...] = q.astype(BF16)
    for g in range(N_GROUPS):
        gl = slice(g * GROUP_WIDTH, (g + 1) * GROUP_WIDTH)
        kvn_ref[:, 2 * g * GROUP_WIDTH:(2 * g + 1) * GROUP_WIDTH] = k[:, gl]
        kvn_ref[:, (2 * g + 1) * GROUP_WIDTH:(2 * g + 2) * GROUP_WIDTH] = v[:, gl]
    u_ref[...] = y[:, 3 * ATTN_WIDTH:3 * ATTN_WIDTH + POOL_WIDTH]


def _in_proj_sample(x, mod, g1, w_qkvu, cos, sin):
    _, T, D = x.shape
    n_w = w_qkvu.shape[1]
    return pl.pallas_call(
        _in_proj_sample_kernel,
        grid=(1,),
        in_specs=[_full((1, T, D)), _full((1, N_COND, T, D)), _full((1, D)), _full((D, n_w)),
                  _full((T, LANE)), _full((T, LANE))],
        out_specs=[_full((T, ATTN_WIDTH)), _full((T, 2 * ATTN_WIDTH)), _full((T, POOL_WIDTH))],
        out_shape=[jax.ShapeDtypeStruct((T, ATTN_WIDTH), BF16),
                   jax.ShapeDtypeStruct((T, 2 * ATTN_WIDTH), F32),
                   jax.ShapeDtypeStruct((T, POOL_WIDTH), F32)],
        compiler_params=_cparams(1),
        name="in_proj_sample",
    )(x, mod, g1, w_qkvu, cos, sin)


def _attn_body(q_ref, kc_ref, kp_ref, vc_ref, vp_ref, o_ref, lse_ref, kbuf, vbuf, n0, nq):
    kbuf[0:SPAN, :] = kp_ref[0]
    kbuf[SPAN:, :] = kc_ref[0]
    vbuf[0:SPAN, :] = vp_ref[0]
    vbuf[SPAN:, :] = vc_ref[0]
    qi = lax.broadcasted_iota(jnp.int32, (SPAN, 2 * SPAN), 0)
    kj = lax.broadcasted_iota(jnp.int32, (SPAN, 2 * SPAN), 1)
    band = (kj >= qi) & (kj <= qi + SPAN)
    band_first = band & (kj >= jnp.where(n0 > 0, 0, SPAN))
    lane = lax.broadcasted_iota(jnp.int32, (SPAN, LANE), 1)
    low_head = lane < HEAD_DIM
    for j in range(nq):
        valid = band_first if j == 0 else band
        rows = slice(j * SPAN, (j + 1) * SPAN)
        krows = slice(j * SPAN, (j + 2) * SPAN)
        for c in range(GROUP_WIDTH // LANE):
            cl = slice(c * LANE, (c + 1) * LANE)
            q = q_ref[0, rows, cl]
            kk = kbuf[krows, cl]
            vv = vbuf[krows, cl]
            o_pair, lse_pair = [], []
            for hh in range(2):
                mask_h = low_head if hh == 0 else jnp.logical_not(low_head)
                qm = jnp.where(mask_h, q, jnp.zeros_like(q))
                s = lax.dot_general(qm, kk, (((1,), (1,)), ((), ())), preferred_element_type=F32)
                s = jnp.where(valid, s, NEG)
                m = jnp.max(s, axis=-1, keepdims=True)
                p = jnp.exp(s - m)
                den = jnp.sum(p, axis=-1, keepdims=True)
                o_pair.append(jnp.dot(p.astype(BF16), vv, preferred_element_type=F32) / den)
                lse_pair.append(m + jnp.log(den))
            o_ref[0, rows, cl] = jnp.where(low_head, o_pair[0], o_pair[1]).astype(BF16)
            lse_ref[0, rows, cl] = jnp.where(low_head, lse_pair[0], lse_pair[1])


def _sample_body(qbd_ref, kvn_ref, c0_ref, c1_ref, c2_ref, sp_ref, u_ref, wpool_ref, pscale_ref,
                 attn_ref, pooled_ref, o0_ref, o1_ref, o2_ref, po_ref, ue_ref, diff_ref, nt_ref, t_new):
    rows = HEADS_PER_GROUP * t_new
    caches = (c0_ref, c1_ref, c2_ref)
    outs = (o0_ref, o1_ref, o2_ref)

    s_cache, s_new, valid_new = [], [], []
    m = jnp.full((rows, 1), NEG, F32)
    for g in range(N_GROUPS):
        d = ATTN_DILATIONS[g]
        c_ref = caches[g]
        L = c_ref.shape[3]
        qb = qbd_ref[0, g]
        s = jnp.dot(qb, c_ref[0, 0].astype(BF16), preferred_element_type=F32)
        t_row = lax.broadcasted_iota(jnp.int32, (rows, L), 0) >> HEAD_SHIFT
        delta = L + t_row - lax.broadcasted_iota(jnp.int32, (rows, L), 1)
        ok = ((delta & (d - 1)) == 0) & (delta <= SPAN * d)
        s = jnp.where(ok, s, NEG)
        s_cache.append(s)
        m = jnp.maximum(m, jnp.max(s, axis=-1, keepdims=True))
        qf = qb.astype(F32)
        t_col = lax.broadcasted_iota(jnp.int32, (rows, 1), 0) >> HEAD_SHIFT
        sn, okn = [], []
        for tn in range(t_new):
            kn = kvn_ref[0, tn:tn + 1, 2 * g * GROUP_WIDTH:(2 * g + 1) * GROUP_WIDTH]
            kn = kn.astype(BF16).astype(F32)
            dn = t_col - tn
            ok_n = (dn >= 0) & ((dn & (d - 1)) == 0)
            s1 = jnp.where(ok_n, jnp.sum(qf * kn, axis=-1, keepdims=True), NEG)
            sn.append(s1)
            okn.append(ok_n)
            m = jnp.maximum(m, s1)
        s_new.append(sn)
        valid_new.append(okn)

    den = jnp.zeros((rows, 1), F32)
    acc = jnp.zeros((rows, GROUP_WIDTH), F32)
    for g in range(N_GROUPS):
        c_ref = caches[g]
        p = jnp.exp(s_cache[g] - m)
        den = den + jnp.sum(p, axis=-1, keepdims=True)
        acc = acc + lax.dot_general(p.astype(BF16), c_ref[0, 1].astype(BF16), (((1,), (1,)), ((), ())),
                                    preferred_element_type=F32)
        for tn in range(t_new):
            pn = jnp.exp(s_new[g][tn] - m)
            den = den + pn
            vn = kvn_ref[0, tn:tn + 1, (2 * g + 1) * GROUP_WIDTH:(2 * g + 2) * GROUP_WIDTH]
            acc = acc + pn * vn
    row_head = lax.broadcasted_iota(jnp.int32, (rows, GROUP_WIDTH), 0) & (HEADS_PER_GROUP - 1)
    lane_head = lax.broadcasted_iota(jnp.int32, (rows, GROUP_WIDTH), 1) >> LANE_HEAD_SHIFT
    o_diag = jnp.where(row_head == lane_head, acc / den, 0.0).astype(BF16)
    sel = ((lax.broadcasted_iota(jnp.int32, (rows, rows), 1) >> HEAD_SHIFT)
           == lax.broadcasted_iota(jnp.int32, (rows, rows), 0)).astype(BF16)
    attn = jnp.dot(sel, o_diag, preferred_element_type=F32)
    attn_ref[0] = attn[0:t_new].astype(BF16)

    tail_lane = lax.broadcasted_iota(jnp.int32, (GROUP_WIDTH, LANE), 1)
    nt_ref[...] = jnp.zeros(nt_ref.shape, F32)
    for g in range(N_GROUPS):
        c_ref, o_ref = caches[g], outs[g]
        L = c_ref.shape[3]
        for kv in range(2):
            col0 = (2 * g + kv) * GROUP_WIDTH
            nt_ref[LANE - t_new:LANE, :] = kvn_ref[0, :, col0:col0 + GROUP_WIDTH]
            new_t = nt_ref[...].T
            rolled = pltpu.roll(c_ref[0, kv], L - t_new, 1)
            if L > LANE:
                o_ref[0, kv, :, 0:L - LANE] = rolled[:, 0:L - LANE]
            o_ref[0, kv, :, L - LANE:L] = jnp.where(tail_lane >= LANE - t_new, new_t, rolled[:, L - LANE:L])

    ue_ref[0:POOL_STATE_LEN, :] = sp_ref[0]
    ue_ref[POOL_STATE_LEN:POOL_STATE_LEN + t_new, :] = u_ref[0]
    diff_ref[...] = jnp.zeros(diff_ref.shape, F32)
    for tn in range(t_new):
        r = POOL_STATE_LEN + tn
        for gi, w in enumerate(POOL_WINDOWS):
            lanes = slice(gi * POOL_GROUP_WIDTH, (gi + 1) * POOL_GROUP_WIDTH)
            win = jnp.sum(ue_ref[r - w + 1:r + 1, lanes], axis=0, keepdims=True)
            diff_ref[tn:tn + 1, lanes] = win / float(w) - ue_ref[r:r + 1, lanes]
    for gi in range(len(POOL_WINDOWS)):
        lanes = slice(gi * POOL_GROUP_WIDTH, (gi + 1) * POOL_GROUP_WIDTH)
        z = jnp.dot(diff_ref[:, lanes].astype(BF16), wpool_ref[gi], preferred_element_type=F32)
        pooled_ref[0, :, lanes] = (z[0:t_new] * pscale_ref[:, lanes]).astype(BF16)
    po_ref[0] = ue_ref[t_new:t_new + POOL_STATE_LEN, :]


N_ATTN_IN = 5
N_SAMPLE_IN = 9
N_SAMPLE_OUT = 6


def _attn_sample_kernel(*refs, plans, n_seq, t_new):
    n_att = N_GROUPS * N_ATTN_IN
    attn_in = refs[:n_att]
    sample_in = refs[n_att:n_att + N_SAMPLE_IN]
    outs = refs[n_att + N_SAMPLE_IN:]
    attn_out = outs[:2 * N_GROUPS]
    sample_out = outs[2 * N_GROUPS:2 * N_GROUPS + N_SAMPLE_OUT]
    scratch = outs[2 * N_GROUPS + N_SAMPLE_OUT:]
    kv_bufs, sample_scratch = scratch[:2 * N_GROUPS], scratch[2 * N_GROUPS:]
    s = pl.program_id(0)

    for g, (first, count, n_sup, d, nq) in enumerate(plans):
        @pl.when((s >= first) & (s < first + count))
        def _(g=g, first=first, n_sup=n_sup, d=d, nq=nq):
            n0 = ((s - first) // d) % n_sup
            _attn_body(*attn_in[g * N_ATTN_IN:(g + 1) * N_ATTN_IN], attn_out[2 * g], attn_out[2 * g + 1],
                       kv_bufs[2 * g], kv_bufs[2 * g + 1], n0, nq)

    @pl.when(s < n_seq)
    def _():
        _sample_body(*sample_in, *sample_out, *sample_scratch, t_new)


def _attn_and_sample(qs, ks, vs, qbd, kvn, caches, state_pool, u, w_pool, pool_scale, nq_max):
    N, t_new, _ = u.shape
    rows = HEADS_PER_GROUP * t_new
    plans, in_specs, out_specs, out_shape, scratch, args = [], [], [], [], [], []
    first = 0
    for g in range(N_GROUPS):
        B, M, width = qs[g].shape
        d = width // GROUP_WIDTH
        nq = min(nq_max, M // SPAN)
        n_sup = M // (nq * SPAN)
        count = B * n_sup * d
        plans.append((first, count, n_sup, d, nq))

        def block_index(s, first=first, count=count, n_sup=n_sup, d=d):
            local = jnp.clip(s - first, 0, count - 1)
            return local // (n_sup * d), (local // d) % n_sup, local % d

        def cur_map(s, block_index=block_index):
            b, n, r = block_index(s)
            return b, n, r

        def prev_map(s, block_index=block_index, nq=nq):
            b, n, r = block_index(s)
            return b, jnp.maximum(n * nq - 1, 0), r

        cur = pl.BlockSpec((1, nq * SPAN, GROUP_WIDTH), cur_map)
        prev = pl.BlockSpec((1, SPAN, GROUP_WIDTH), prev_map)
        in_specs += [cur, cur, prev, cur, prev]
        args += [qs[g], ks[g], ks[g], vs[g], vs[g]]
        out_specs += [cur, cur]
        out_shape += [jax.ShapeDtypeStruct((B, M, width), BF16), jax.ShapeDtypeStruct((B, M, width), F32)]
        scratch += [pltpu.VMEM(((nq + 1) * SPAN, GROUP_WIDTH), BF16)] * 2
        first += count
    n_steps = max(first, N)

    per_n = lambda shape: pl.BlockSpec((1,) + tuple(shape),
                                       lambda s: (jnp.minimum(s, N - 1),) + (0,) * len(shape))
    cache_specs = [per_n(c.shape[1:]) for c in caches]
    in_specs += ([per_n(qbd.shape[1:]), per_n(kvn.shape[1:])] + cache_specs
                 + [per_n(state_pool.shape[1:]), per_n(u.shape[1:]), _full(w_pool.shape), _full((1, POOL_WIDTH))])
    args += [qbd, kvn, *caches, state_pool, u, w_pool, pool_scale]
    out_specs += ([per_n((t_new, GROUP_WIDTH)), per_n((t_new, POOL_WIDTH))] + cache_specs
                  + [per_n(state_pool.shape[1:])])
    out_shape += ([jax.ShapeDtypeStruct((N, t_new, GROUP_WIDTH), BF16),
                   jax.ShapeDtypeStruct((N, t_new, POOL_WIDTH), BF16)]
                  + [jax.ShapeDtypeStruct(c.shape, F32) for c in caches]
                  + [jax.ShapeDtypeStruct(state_pool.shape, F32)])
    scratch += [pltpu.VMEM((POOL_STATE_LEN + t_new + 5, POOL_WIDTH), F32),
                pltpu.VMEM((rows, POOL_WIDTH), F32),
                pltpu.VMEM((LANE, GROUP_WIDTH), F32)]
    outs = pl.pallas_call(
        functools.partial(_attn_sample_kernel, plans=tuple(plans), n_seq=N, t_new=t_new),
        grid=(n_steps,),
        in_specs=in_specs,
        out_specs=out_specs,
        out_shape=out_shape,
        scratch_shapes=scratch,
        compiler_params=_cparams(1),
        name="attn_and_sample",
    )(*args)
    return [(outs[2 * g], outs[2 * g + 1]) for g in range(N_GROUPS)], outs[2 * N_GROUPS:]


def _route(logits):
    lane = lax.broadcasted_iota(jnp.int32, logits.shape, 1).astype(F32)
    is_grp = (lane >= GROUP_LANE0) & (lane < GROUP_LANE0 + MOE_GROUPS)
    gl = jnp.where(is_grp, logits, NEG)
    gmax = jnp.max(gl, axis=-1, keepdims=True)
    gidx = jnp.min(jnp.where(gl == gmax, lane, BIG_LANE), axis=-1, keepdims=True) - GROUP_LANE0
    gsum = jnp.sum(jnp.where(is_grp, jnp.exp(gl - gmax), 0.0), axis=-1, keepdims=True)
    grp_w = 1.0 / gsum
    lo = gidx * EXPERTS_PER_GROUP
    in_grp = (lane >= lo) & (lane < lo + EXPERTS_PER_GROUP)
    el = jnp.where(in_grp, logits, NEG)
    v1 = jnp.max(el, axis=-1, keepdims=True)
    i1 = jnp.min(jnp.where(el == v1, lane, BIG_LANE), axis=-1, keepdims=True)
    el2 = jnp.where(lane == i1, NEG, el)
    v2 = jnp.max(el2, axis=-1, keepdims=True)
    i2 = jnp.min(jnp.where(el2 == v2, lane, BIG_LANE), axis=-1, keepdims=True)
    e = jnp.exp(v2 - v1)
    w1 = grp_w / (1.0 + e)
    w2 = grp_w * e / (1.0 + e)
    return lane, i1, i2, w1, w2


def _merge_kernel(*refs, n_groups):
    x_ref, mod_ref, g1_ref, g2_ref = refs[0:4]
    n_attn = 2 * n_groups if n_groups > 1 else 1
    attn_refs = refs[4:4 + n_attn]
    rest = refs[4 + n_attn:]
    pooled_ref, wg_ref, wa_ref, wp_ref, wo_ref, wrh_ref, wrc_ref, br_ref = rest[:8]
    x1_ref, h2_ref, route_ref, counts_ref, route_t_ref = rest[8:13]
    scratch_refs = rest[13:]
    stage_refs = scratch_refs[:n_attn] if n_groups > 1 else ()
    carry_ref = scratch_refs[-1]

    x = x_ref[0]
    tm = x.shape[0]
    if n_groups > 1:
        vals = []
        for idx, ref in enumerate(attn_refs):
            d = ATTN_DILATIONS[idx % n_groups]
            if d == 1:
                vals.append(ref[0].astype(F32))
                continue
            stage_ref = stage_refs[idx]
            chunks = GROUP_WIDTH // LANE
            for r in range(d):
                for c in range(chunks):
                    stage_ref[c, pl.ds(r, tm // d, stride=d), :] = (
                        ref[0, :, r * GROUP_WIDTH + c * LANE:r * GROUP_WIDTH + (c + 1) * LANE].astype(F32))
            vals.append(jnp.concatenate([stage_ref[c] for c in range(chunks)], axis=1))
        os_, ls = vals[:n_groups], vals[n_groups:]
        lmax = functools.reduce(jnp.maximum, ls)
        es = [jnp.exp(l - lmax) for l in ls]
        attn = sum(e * o for e, o in zip(es, os_)) / sum(es)
    else:
        attn = attn_refs[0][0]
    a = jnp.dot(attn.astype(BF16), wa_ref[...], preferred_element_type=F32)
    p = jnp.dot(pooled_ref[0], wp_ref[...], preferred_element_type=F32)
    h = _norm_mod(x, g1_ref[...], mod_ref[0, 1], mod_ref[0, 0]).astype(BF16)
    gates = jnp.dot(h, wg_ref[...], preferred_element_type=F32)
    D = x.shape[1]
    merged = jax.nn.sigmoid(gates[:, :D]) * a + jax.nn.sigmoid(gates[:, D:]) * p
    y = jnp.dot(merged.astype(BF16), wo_ref[...], preferred_element_type=F32)
    x1 = x + mod_ref[0, 2] * y
    x1_ref[0] = x1
    h2 = _norm_mod(x1, g2_ref[...], mod_ref[0, 4], mod_ref[0, 3])
    h2_hi = h2.astype(BF16)
    h2_lo = (h2 - h2_hi.astype(F32)).astype(BF16)
    hi_terms = jnp.dot(h2_hi, wrc_ref[...], preferred_element_type=F32)
    logits = (hi_terms[:, :ROUTER_LANES] + jnp.dot(h2_lo, wrh_ref[...], preferred_element_type=F32)
              + hi_terms[:, ROUTER_LANES:]) + br_ref[...]
    lane, i1, i2, w1, w2 = _route(logits)
    h2_ref[0] = h2

    @pl.when((pl.program_id(0) == 0) & (pl.program_id(1) == 0))
    def _():
        carry_ref[...] = jnp.zeros(carry_ref.shape, F32)

    hit = ((lane == i1) | (lane == i2)).astype(BF16)
    ltri = (lax.broadcasted_iota(jnp.int32, (tm, tm), 0) >= lax.broadcasted_iota(jnp.int32, (tm, tm), 1))
    prefix = jnp.dot(ltri.astype(BF16), hit, preferred_element_type=F32) + carry_ref[...]
    rank1 = jnp.sum(jnp.where(lane == i1, prefix, 0.0), axis=-1, keepdims=True) - 1.0
    rank2 = jnp.sum(jnp.where(lane == i2, prefix, 0.0), axis=-1, keepdims=True) - 1.0
    carry_ref[...] = prefix[tm - 1:tm, :]
    counts_ref[...] = jnp.broadcast_to(prefix[tm - 1:tm, :], counts_ref.shape)
    cols = (i1, i2, w1, w2, rank1, rank2)
    route = jnp.zeros(logits.shape, F32)
    for c, col in enumerate(cols):
        route = jnp.where(lane == float(c), col, route)
    route_ref[0] = route
    route_t_ref[...] = route.T[0:SUBLANE, :]


def _merge(x, mod, g1, g2, attn_inputs, pooled, w_gates, w_attn_out, w_pool_out, w_o, wr_hi, wr_hilo, b_r, tm):
    B, S, D = x.shape
    R = mod.shape[2]
    n_groups = len(attn_inputs) // 2 if len(attn_inputs) > 1 else 1
    tok = lambda width: pl.BlockSpec((1, tm, width), lambda b, i: (b, i, 0))
    if R == 1:
        mod_spec = pl.BlockSpec((1, N_COND, 1, D), lambda b, i: (b, 0, 0, 0))
    else:
        mod_spec = pl.BlockSpec((1, N_COND, tm, D), lambda b, i: (b, 0, i, 0))
    weights = (w_gates, w_attn_out, w_pool_out, w_o, wr_hi, wr_hilo, b_r)
    if n_groups > 1:
        attn_specs = [pl.BlockSpec((1, tm // (a.shape[2] // GROUP_WIDTH), a.shape[2]), lambda b, i: (b, i, 0))
                      for a in attn_inputs]
        scratch = [pltpu.VMEM((GROUP_WIDTH // LANE, tm, LANE), F32)] * len(attn_inputs)
    else:
        attn_specs = [tok(GROUP_WIDTH)]
        scratch = []
    n_t = S // tm
    out_specs = [tok(D), tok(D), tok(ROUTER_LANES), _full((SUBLANE, ROUTER_LANES)),
                 pl.BlockSpec((SUBLANE, tm), lambda b, i: (0, b * n_t + i))]
    out_shape = [jax.ShapeDtypeStruct((B, S, D), F32),
                 jax.ShapeDtypeStruct((B, S, D), F32),
                 jax.ShapeDtypeStruct((B, S, ROUTER_LANES), F32),
                 jax.ShapeDtypeStruct((SUBLANE, ROUTER_LANES), F32),
                 jax.ShapeDtypeStruct((SUBLANE, B * S), F32)]
    scratch = scratch + [pltpu.VMEM((1, ROUTER_LANES), F32)]
    return pl.pallas_call(
        functools.partial(_merge_kernel, n_groups=n_groups),
        grid=(B, S // tm),
        in_specs=[tok(D), mod_spec, _full((1, D)), _full((1, D))]
                 + attn_specs + [tok(POOL_WIDTH)]
                 + [_full(w.shape) for w in weights],
        out_specs=out_specs,
        out_shape=out_shape,
        scratch_shapes=scratch,
        compiler_params=_cparams(2),
        name="merge",
    )(x, mod, g1, g2, *attn_inputs, pooled, *weights)


EXPERT_TILE = 512
PAD_CHUNK = 32


def _routing_tables(routes, counts, tile):
    cnts = [c[0, :N_EXPERTS].astype(jnp.int32) for c in counts]
    cnt = sum(cnts)
    padded = ((cnt + tile - 1) // tile) * tile
    ends = jnp.cumsum(padded)
    base = ends - padded
    poss = []
    for route in routes:
        T = route.shape[1]
        dense = (2, T // LANE, LANE) if T % LANE == 0 else (2, T)
        e12 = route[0:2].astype(jnp.int32).reshape(dense)
        start = jnp.zeros_like(e12)
        for e in range(N_EXPERTS):
            start = jnp.where(e12 == e, base[e], start)
        poss.append((start + route[4:6].astype(jnp.int32).reshape(dense)).reshape(2, T))
        base = base + cnts[len(poss) - 1]
    n_pairs = 2 * sum(r.shape[1] for r in routes)
    n_tiles = -(-n_pairs // tile) + N_EXPERTS
    starts = jnp.arange(n_tiles, dtype=jnp.int32) * tile
    n_used = ends[-1] // tile
    tile_expert = jnp.sum(starts[:, None] >= ends[None, :], axis=1).astype(jnp.int32)
    last = jnp.take(tile_expert, n_used - 1)
    tile_expert = jnp.where(jnp.arange(n_tiles) < n_used, tile_expert, last)
    pads = jnp.stack([ends - padded + cnts[0], ends], axis=1).reshape(-1).astype(jnp.int32)
    return poss, tile_expert, n_used.reshape(1).astype(jnp.int32), pads, n_tiles


def _pos_steps(pos, tm):
    steps = pos.shape[1] // tm
    return jnp.transpose(pos.reshape(2, steps, tm), (1, 0, 2)).reshape(steps, 1, 2 * tm)


def _row_copy(src_ref, src_row, dst_ref, dst_row, sem):
    return pltpu.make_async_copy(src_ref.at[pl.ds(src_row, 1)], dst_ref.at[pl.ds(dst_row, 1)], sem)


def _dispatch_kernel(pads_ref, pos_ref, h_ref, *rest, tm, tile, first):
    xs_ref, zero_ref, sem = rest[-3:]

    def zero_fill():
        zero_ref[...] = jnp.zeros(zero_ref.shape, zero_ref.dtype)
        n_tiles = xs_ref.shape[0] // tile
        first_unused = pads_ref[2 * N_EXPERTS - 1] // tile

        def tile_copy(j):
            return pltpu.make_async_copy(zero_ref, xs_ref.at[pl.ds(pl.multiple_of(j * tile, tile), tile)], sem)

        def fill_tile(j, c):
            tile_copy(j).start()
            return c

        def drain_tile(j, c):
            tile_copy(j).wait()
            return c

        lax.fori_loop(first_unused, n_tiles, fill_tile, 0)
        lax.fori_loop(first_unused, n_tiles, drain_tile, 0)

        def chunk_copy(c):
            rows = pl.ds(pl.multiple_of(c * PAD_CHUNK, PAD_CHUNK), PAD_CHUNK)
            return pltpu.make_async_copy(zero_ref.at[pl.ds(0, PAD_CHUNK)], xs_ref.at[rows], sem)

        def for_each_pad_piece(on_row, on_chunk):
            def per_expert(e, carry):
                lo, hi = pads_ref[2 * e], pads_ref[2 * e + 1]
                first_chunk = (lo + PAD_CHUNK - 1) // PAD_CHUNK
                lax.fori_loop(lo, jnp.minimum(first_chunk * PAD_CHUNK, hi), on_row, 0)
                lax.fori_loop(first_chunk, hi // PAD_CHUNK, on_chunk, 0)
                return carry

            lax.fori_loop(0, N_EXPERTS, per_expert, 0)

        def start_row(r, c):
            _row_copy(zero_ref, 0, xs_ref, r, sem).start()
            return c

        def wait_row(r, c):
            _row_copy(zero_ref, 0, xs_ref, r, sem).wait()
            return c

        def start_chunk(c, carry):
            chunk_copy(c).start()
            return carry

        def wait_chunk(c, carry):
            chunk_copy(c).wait()
            return carry

        for_each_pad_piece(start_row, start_chunk)
        for_each_pad_piece(wait_row, wait_chunk)

    if first:
        pl.when(pl.program_id(0) == 0)(zero_fill)

    def issue(k, carry):
        base = pl.multiple_of(k * SUBLANE, SUBLANE)
        for j in range(SUBLANE):
            i = base + j
            _row_copy(h_ref, i, xs_ref, pos_ref[0, 0, i], sem).start()
            _row_copy(h_ref, i, xs_ref, pos_ref[0, 0, tm + i], sem).start()
        return carry

    lax.fori_loop(0, tm // SUBLANE, issue, 0)
    for _ in range(2):
        pltpu.make_async_copy(h_ref, xs_ref.at[pl.ds(0, tm)], sem).wait()


def _dispatch(h2, pos, pads, n_rows, tm, tile, xs_prev=None):
    T, width = h2.shape
    in_specs = [pl.BlockSpec((1, 1, 2 * tm), lambda i, pads: (i, 0, 0), memory_space=pltpu.SMEM),
                pl.BlockSpec((tm, width), lambda i, pads: (i, 0))]
    args = [pads, _pos_steps(pos, tm), h2]
    aliases = {}
    if xs_prev is not None:
        in_specs.append(pl.BlockSpec(memory_space=pl.ANY))
        args.append(xs_prev)
        aliases = {3: 0}
    grid_spec = pltpu.PrefetchScalarGridSpec(
        num_scalar_prefetch=1,
        grid=(T // tm,),
        in_specs=in_specs,
        out_specs=pl.BlockSpec(memory_space=pl.ANY),
        scratch_shapes=[pltpu.VMEM((tile, width), h2.dtype), pltpu.SemaphoreType.DMA(())],
    )
    return pl.pallas_call(
        functools.partial(_dispatch_kernel, tm=tm, tile=tile, first=xs_prev is None),
        grid_spec=grid_spec,
        out_shape=jax.ShapeDtypeStruct((n_rows, width), h2.dtype),
        input_output_aliases=aliases,
        compiler_params=_cparams(1),
        name="moe_dispatch",
    )(*args)


def _experts_kernel(te_ref, nu_ref, xs_ref, wgu_ref, wd_ref, ys_ref, wgu_bf_ref, wd_bf_ref, *, d_expert):
    j = pl.program_id(0)

    @pl.when(j >= nu_ref[0])
    def _():
        ys_ref[...] = jnp.zeros(ys_ref.shape, F32)

    @pl.when((j == 0) | (te_ref[j] != te_ref[jnp.maximum(j - 1, 0)]))
    def _():
        wgu_bf_ref[...] = wgu_ref[0].astype(BF16)
        wd_bf_ref[...] = wd_ref[0].astype(BF16)

    @pl.when(j < nu_ref[0])
    def _():
        gu = jnp.dot(xs_ref[...].astype(BF16), wgu_bf_ref[...], preferred_element_type=F32)
        act = _silu(gu[:, :d_expert]) * gu[:, d_expert:]
        ys_ref[...] = jnp.dot(act.astype(BF16), wd_bf_ref[...], preferred_element_type=F32)


def _experts(xs, tile_expert, n_used, w_gate_up, w_down, tile):
    n_rows, width = xs.shape
    _, D, two_f = w_gate_up.shape
    grid_spec = pltpu.PrefetchScalarGridSpec(
        num_scalar_prefetch=2,
        grid=(n_rows // tile,),
        in_specs=[pl.BlockSpec((tile, width), lambda j, te, nu: (jnp.minimum(j, nu[0] - 1), 0)),
                  pl.BlockSpec((1, D, two_f), lambda j, te, nu: (te[j], 0, 0)),
                  pl.BlockSpec((1, two_f // 2, D), lambda j, te, nu: (te[j], 0, 0))],
        out_specs=pl.BlockSpec((tile, D), lambda j, te, nu: (j, 0)),
        scratch_shapes=[pltpu.VMEM((D, two_f), BF16), pltpu.VMEM((two_f // 2, D), BF16)],
    )
    return pl.pallas_call(
        functools.partial(_experts_kernel, d_expert=two_f // 2),
        grid_spec=grid_spec,
        out_shape=jax.ShapeDtypeStruct((n_rows, D), F32),
        compiler_params=_cparams(1),
        name="moe_experts",
    )(tile_expert, n_used, xs, w_gate_up, w_down)


def _combine_kernel(pos_ref, pos_next_ref, ys_ref, route_ref, x1_ref, mod_ref, gf_ref, y_ref,
                    ya_ref, yb_ref, sems, *, tm):
    step = pl.program_id(0) * pl.num_programs(1) + pl.program_id(1)
    n_steps = pl.num_programs(0) * pl.num_programs(1)
    slot = step % 2

    def start_gathers(p_ref, s):
        def issue(k, carry):
            base = pl.multiple_of(k * SUBLANE, SUBLANE)
            for j in range(SUBLANE):
                i = base + j
                _row_copy(ys_ref, p_ref[0, 0, i], ya_ref.at[s], i, sems.at[s]).start()
                _row_copy(ys_ref, p_ref[0, 0, tm + i], yb_ref.at[s], i, sems.at[s]).start()
            return carry

        lax.fori_loop(0, tm // SUBLANE, issue, 0)

    @pl.when(step == 0)
    def _():
        start_gathers(pos_ref, 0)

    @pl.when(step + 1 < n_steps)
    def _():
        start_gathers(pos_next_ref, 1 - slot)

    for buf in (ya_ref, yb_ref):
        pltpu.make_async_copy(ys_ref.at[pl.ds(0, tm)], buf.at[slot], sems.at[slot]).wait()
    route = route_ref[0]
    lane = lax.broadcasted_iota(jnp.int32, route.shape, 1)
    w1 = jnp.sum(jnp.where(lane == 2, route, 0.0), axis=-1, keepdims=True)
    w2 = jnp.sum(jnp.where(lane == 3, route, 0.0), axis=-1, keepdims=True)
    x2 = x1_ref[0] + mod_ref[0, 5] * (w1 * ya_ref[slot] + w2 * yb_ref[slot])
    var = jnp.mean(x2 * x2, axis=-1, keepdims=True)
    y_ref[0] = x2 * lax.rsqrt(var + EPS) * gf_ref[...]


def _combine(ys, pos, route, x1, mod, gf, tm):
    B, S, D = x1.shape
    n_t = S // tm
    tok = lambda width: pl.BlockSpec((1, tm, width), lambda b, i: (b, i, 0))
    last = B * n_t - 1
    pos_spec = lambda ahead: pl.BlockSpec(
        (1, 1, 2 * tm), lambda b, i: (jnp.minimum(b * n_t + i + ahead, last), 0, 0), memory_space=pltpu.SMEM)
    pos_steps = _pos_steps(pos, tm)
    if mod.shape[2] == 1:
        mod_spec = pl.BlockSpec((1, N_COND, 1, D), lambda b, i: (b, 0, 0, 0))
    else:
        mod_spec = pl.BlockSpec((1, N_COND, tm, D), lambda b, i: (b, 0, i, 0))
    return pl.pallas_call(
        functools.partial(_combine_kernel, tm=tm),
        grid=(B, n_t),
        in_specs=[pos_spec(0), pos_spec(1),
                  pl.BlockSpec(memory_space=pl.ANY),
                  tok(ROUTER_LANES), tok(D),
                  mod_spec,
                  _full((1, D))],
        out_specs=tok(D),
        out_shape=jax.ShapeDtypeStruct((B, S, D), F32),
        scratch_shapes=[pltpu.VMEM((2, tm, D), F32), pltpu.VMEM((2, tm, D), F32),
                        pltpu.SemaphoreType.DMA((2,))],
        compiler_params=_cparams(2),
        name="moe_combine",
    )(pos_steps, pos_steps, ys, route, x1, mod, gf)


def _rope_tables(pos):
    half = HEAD_DIM // 2
    inv = ROPE_THETA ** (-jnp.arange(half, dtype=F32) * 2.0 / HEAD_DIM)
    ang = pos.astype(F32)[:, None] * inv[None, :]
    cos, sin = jnp.cos(ang), jnp.sin(ang)
    reps = LANE // HEAD_DIM
    return (jnp.tile(jnp.concatenate([cos, cos], axis=-1), (1, reps)),
            jnp.tile(jnp.concatenate([-sin, sin], axis=-1), (1, reps)))


def kernel(x_prompt, x_sample, cache_kv_w128, cache_kv_w512, cache_kv_w2048, state_pool, c_prompt, c_sample, norm1_g, w_ada, b_ada, w_in, w_attn_out, w_pool, pool_scale, w_pool_out, w_o, norm2_g, w_grp, b_grp, w_exp_router, b_exp_router, w_gate_up, w_down, final_norm_g):
    B, S, D = x_prompt.shape
    N, T, _ = x_sample.shape
    depth = norm1_g.shape[0]
    assert depth == 1, "single trunk layer"
    tm = min(TOKEN_TILE, S)
    assert S % tm == 0 and all(S % (SPAN * d) == 0 for d in ATTN_DILATIONS)

    n_qkvu = 3 * ATTN_WIDTH + POOL_WIDTH
    w_qkvu = w_in[0, :, :n_qkvu].astype(BF16)
    w_gates = w_in[0, :, n_qkvu:].astype(BF16)
    wa, wpo, wo = w_attn_out[0].astype(BF16), w_pool_out[0].astype(BF16), w_o[0].astype(BF16)
    wpool = w_pool[0].astype(BF16)
    pscale = pool_scale[0].reshape(1, POOL_WIDTH)
    w_r = jnp.concatenate([w_exp_router[0], w_grp[0]], axis=1)
    w_r = jnp.pad(w_r, ((0, 0), (0, ROUTER_LANES - w_r.shape[1])))
    wr_hi = w_r.astype(BF16)
    wr_hilo = jnp.concatenate([wr_hi, (w_r - wr_hi.astype(F32)).astype(BF16)], axis=1)
    b_r = jnp.pad(jnp.concatenate([b_exp_router[0], b_grp[0]]), (0, ROUTER_LANES - N_EXPERTS - MOE_GROUPS))
    b_r = b_r.reshape(1, ROUTER_LANES)
    wgu, wd = w_gate_up[0], w_down[0]
    g1, g2, gf = norm1_g[0].reshape(1, D), norm2_g[0].reshape(1, D), final_norm_g.reshape(1, D)

    mod = _ada(jnp.concatenate([c_prompt, c_sample], axis=0), w_ada[0], b_ada[0])
    mod_p = mod[:B].reshape(B, N_COND, 1, D)
    mod_s = jnp.repeat(mod[B:].reshape(N, N_COND, D), T, axis=0)
    mod_s = jnp.transpose(mod_s, (1, 0, 2)).reshape(1, N_COND, N * T, D)

    cos_p, sin_p = _rope_tables(jnp.arange(S, dtype=jnp.int32))
    outs = _in_proj_prompt(x_prompt, mod_p, g1, w_qkvu, cos_p, sin_p, wpool, pscale, tm)
    qs, ks, vs = outs[0:3], outs[3:6], outs[6:9]
    pooled, kv0, kv1, kv2, ptail = outs[9:14]
    pool_prompt = ptail[:, POOL_HALO - POOL_STATE_LEN:, :][None]

    TS = N * T
    pos_s = PAST_LEN + jnp.arange(T, dtype=jnp.int32)
    cos_s, sin_s = _rope_tables(pos_s)
    cos_s, sin_s = jnp.tile(cos_s, (N, 1)), jnp.tile(sin_s, (N, 1))
    xs = x_sample.reshape(1, TS, D)
    q_s, kvn, u_s = _in_proj_sample(xs, mod_s, g1, w_qkvu, cos_s, sin_s)
    eye = jnp.eye(HEADS_PER_GROUP, dtype=BF16)
    qbd = jnp.einsum('ntghe,hk->ngthke', q_s.reshape(N, T, N_GROUPS, HEADS_PER_GROUP, HEAD_DIM), eye)
    qbd = qbd.reshape(N, N_GROUPS, T * HEADS_PER_GROUP, GROUP_WIDTH)
    caches = [jnp.transpose(c[0], (0, 2, 3, 4, 1)).reshape(N, 2, GROUP_WIDTH, c.shape[2])
              for c in (cache_kv_w128, cache_kv_w512, cache_kv_w2048)]

    attn_parts, (attn_s, pooled_s, ko0, ko1, ko2, pool_s) = _attn_and_sample(
        qs, ks, vs, qbd, kvn.reshape(N, T, 2 * ATTN_WIDTH), caches, state_pool[0],
        u_s.reshape(N, T, POOL_WIDTH), wpool, pscale, nq_max=ATTN_BLOCKS_PER_STEP)
    attn_inputs = [o for o, _ in attn_parts] + [l for _, l in attn_parts]
    x1, h2, route, counts, route_t = _merge(x_prompt, mod_p, g1, g2, attn_inputs, pooled, w_gates, wa, wpo, wo,
                                            wr_hi, wr_hilo, b_r, tm)
    x1s, h2s, route_s, counts_s, route_ts = _merge(xs, mod_s, g1, g2, [attn_s.reshape(1, TS, GROUP_WIDTH)],
                                                   pooled_s.reshape(1, TS, POOL_WIDTH), w_gates, wa, wpo, wo,
                                                   wr_hi, wr_hilo, b_r, TS)

    (pos, pos_s), tile_expert, n_used, pads, n_tiles = _routing_tables(
        [route_t, route_ts], [counts, counts_s], EXPERT_TILE)
    sorted_rows = _dispatch(h2.reshape(B * S, D), pos, pads, n_tiles * EXPERT_TILE, tm, EXPERT_TILE)
    sorted_rows = _dispatch(h2s.reshape(TS, D), pos_s, pads, n_tiles * EXPERT_TILE, TS, EXPERT_TILE,
                            xs_prev=sorted_rows)
    ys = _experts(sorted_rows, tile_expert, n_used, wgu, wd, EXPERT_TILE)
    y_prompt = _combine(ys, pos, route, x1, mod_p, gf, tm=tm)
    y_sample = _combine(ys, pos_s, route_s, x1s, mod_s, gf, tm=TS).reshape(N, T, D)

    def kv_shape(a):
        a = a.reshape(a.shape[0], 2, HEADS_PER_GROUP, HEAD_DIM, a.shape[3])
        return jnp.transpose(a, (0, 4, 1, 2, 3))[None]

    return (y_prompt, y_sample, kv_shape(kv0), kv_shape(kv1), kv_shape(kv2), pool_prompt,
            kv_shape(ko0), kv_shape(ko1), kv_shape(ko2), pool_s[None])
```

```python
import functools

import jax
import jax.numpy as jnp
from jax import lax
from jax.experimental import pallas as pl
from jax.experimental.pallas import tpu as pltpu

F32 = jnp.float32
BF16 = jnp.bfloat16

HEAD_DIM = 64
HEADS_PER_GROUP = 4
HEAD_SHIFT = 2
LANE_HEAD_SHIFT = 6
GROUP_WIDTH = HEADS_PER_GROUP * HEAD_DIM
ATTN_WINDOWS = (128, 512, 2048)
ATTN_DILATIONS = (1, 4, 16)
N_GROUPS = 3
SPAN = 128
ATTN_WIDTH = N_GROUPS * GROUP_WIDTH
ROPE_THETA = 10000.0
PAST_LEN = 8192
POOL_WINDOWS = (2, 4, 8, 16)
POOL_GROUP_WIDTH = 128
POOL_WIDTH = 512
POOL_STATE_LEN = 15
POOL_HALO = 16
MOE_GROUPS = 4
EXPERTS_PER_GROUP = 8
N_EXPERTS = 32
N_COND = 6
EPS = 1e-6

LANE = 128
SUBLANE = 8
VMEM_LIMIT_BYTES = 56 * 1024 * 1024

TOKEN_TILE = 512
ATTN_BLOCKS_PER_STEP = 8

NEG = -1e30
BIG_LANE = 1e9

ROUTER_LANES = LANE
GROUP_LANE0 = N_EXPERTS


def _cparams(n_axes):
    return pltpu.CompilerParams(dimension_semantics=("arbitrary",) * n_axes,
                                vmem_limit_bytes=VMEM_LIMIT_BYTES)


def _full(shape):
    nd = len(shape)
    return pl.BlockSpec(tuple(shape), lambda *_: (0,) * nd)


def _norm_mod(x, g, scale, shift):
    var = jnp.mean(x * x, axis=-1, keepdims=True)
    return (x * lax.rsqrt(var + EPS) * g) * (1.0 + scale) + shift


def _rope(x, cos, sin):
    lane = lax.broadcasted_iota(jnp.int32, (x.shape[0], LANE), 1)
    first_half = (lane & (HEAD_DIM - 1)) < (HEAD_DIM // 2)
    outs = []
    for c in range(x.shape[1] // LANE):
        xc = x[:, c * LANE:(c + 1) * LANE]
        partner = jnp.where(first_half, pltpu.roll(xc, LANE - HEAD_DIM // 2, 1),
                            pltpu.roll(xc, HEAD_DIM // 2, 1))
        outs.append(xc * cos + partner * sin)
    return jnp.concatenate(outs, axis=1)


def _silu(x):
    return x * jax.nn.sigmoid(x)


def _ada_kernel(c_ref, w_ref, b_ref, o_ref):
    s = _silu(c_ref[...]).astype(BF16)
    o_ref[0] = jnp.dot(s, w_ref[...].astype(BF16), preferred_element_type=F32) + b_ref[...]


def _ada(c_all, w_ada, b_ada):
    rows, d = c_all.shape
    n_out = w_ada.shape[1]
    tn = n_out // N_COND
    return pl.pallas_call(
        _ada_kernel,
        grid=(N_COND,),
        in_specs=[_full((rows, d)),
                  pl.BlockSpec((d, tn), lambda j: (0, j)),
                  pl.BlockSpec((1, tn), lambda j: (0, j))],
        out_specs=pl.BlockSpec((1, rows, tn), lambda j: (j, 0, 0)),
        out_shape=jax.ShapeDtypeStruct((N_COND, rows, tn), F32),
        compiler_params=_cparams(1),
        name="ada",
    )(c_all, w_ada, b_ada.reshape(1, n_out))


def _in_proj_prompt_kernel(x_ref, mod_ref, g1_ref, w_ref, cos_ref, sin_ref, wpool_ref, pscale_ref, *rest,
                           tm, n_tiles, tails):
    qkv_refs = (rest[0:3], rest[3:6], rest[6:9])
    pooled_ref, kv0_ref, kv1_ref, kv2_ref, ptail_ref = rest[9:14]
    stage_refs = rest[14:17]
    ue_ref = rest[17]
    i = pl.program_id(1)
    x = x_ref[0]
    h = _norm_mod(x, g1_ref[...], mod_ref[0, 1], mod_ref[0, 0]).astype(BF16)
    cos = cos_ref[...]
    sin = sin_ref[...]
    y = jnp.dot(h, w_ref[...], preferred_element_type=F32)
    q = _rope(y[:, :ATTN_WIDTH], cos, sin) * (HEAD_DIM ** -0.5)
    k = _rope(y[:, ATTN_WIDTH:2 * ATTN_WIDTH], cos, sin)
    v = y[:, 2 * ATTN_WIDTH:3 * ATTN_WIDTH]
    u = y[:, 3 * ATTN_WIDTH:3 * ATTN_WIDTH + POOL_WIDTH]

    chunks = GROUP_WIDTH // LANE
    for val, out_refs, stage_ref in zip((q, k, v), qkv_refs, stage_refs):
        for g, d in enumerate(ATTN_DILATIONS):
            gl = slice(g * GROUP_WIDTH, (g + 1) * GROUP_WIDTH)
            if d == 1:
                out_refs[g][0] = val[:, gl].astype(BF16)
                continue
            for c in range(chunks):
                stage_ref[g * chunks + c] = val[:, g * GROUP_WIDTH + c * LANE:g * GROUP_WIDTH + (c + 1) * LANE]
            for r in range(d):
                for c in range(chunks):
                    out_refs[g][0, :, r * GROUP_WIDTH + c * LANE:r * GROUP_WIDTH + (c + 1) * LANE] = (
                        stage_ref[g * chunks + c, pl.ds(r, tm // d, stride=d), :].astype(BF16))

    for g, kv_ref in enumerate((kv0_ref, kv1_ref, kv2_ref)):
        first_tile, rows = tails[g]
        lo = tm - rows

        @pl.when(i >= first_tile)
        def _(kv_ref=kv_ref, g=g, lo=lo):
            kv_ref[0, 0] = k[lo:, g * GROUP_WIDTH:(g + 1) * GROUP_WIDTH].T
            kv_ref[0, 1] = v[lo:, g * GROUP_WIDTH:(g + 1) * GROUP_WIDTH].T

    @pl.when(i == 0)
    def _():
        ue_ref[0:POOL_HALO, :] = jnp.zeros((POOL_HALO, POOL_WIDTH), F32)

    ue_ref[POOL_HALO:, :] = u
    pos1 = (i * tm + 1 + lax.broadcasted_iota(jnp.int32, (tm, 1), 0)).astype(F32)
    for gi, w in enumerate(POOL_WINDOWS):
        lanes = slice(gi * POOL_GROUP_WIDTH, (gi + 1) * POOL_GROUP_WIDTH)
        acc = ue_ref[pl.ds(POOL_HALO, tm), lanes]
        for j in range(1, w):
            acc = acc + ue_ref[pl.ds(POOL_HALO - j, tm), lanes]
        mean = acc / jnp.minimum(float(w), pos1)
        z = jnp.dot((mean - u[:, lanes]).astype(BF16), wpool_ref[gi], preferred_element_type=F32)
        pooled_ref[0, :, lanes] = (z * pscale_ref[:, lanes]).astype(BF16)
    ue_ref[0:POOL_HALO, :] = u[tm - POOL_HALO:, :]

    @pl.when(i == n_tiles - 1)
    def _():
        ptail_ref[0] = u[tm - POOL_HALO:, :]


def _in_proj_prompt(x, mod, g1, w_qkvu, cos, sin, w_pool, pool_scale, tm):
    B, S, D = x.shape
    n_tiles = S // tm
    n_w = w_qkvu.shape[1]
    tails, kv_shapes, kv_specs = [], [], []
    for W in ATTN_WINDOWS:
        Wg = min(W, S)
        if Wg >= tm:
            first = n_tiles - Wg // tm
            rows = tm
        else:
            first = n_tiles - 1
            rows = Wg
        tails.append((first, rows))
        kv_shapes.append(jax.ShapeDtypeStruct((B, 2, GROUP_WIDTH, Wg), F32))
        kv_specs.append(pl.BlockSpec((1, 2, GROUP_WIDTH, rows),
                                     lambda b, i, first=first: (b, 0, 0, jnp.maximum(i - first, 0))))
    tok = lambda width: pl.BlockSpec((1, tm, width), lambda b, i: (b, i, 0))
    dil_specs = [pl.BlockSpec((1, tm // d, d * GROUP_WIDTH), lambda b, i: (b, i, 0)) for d in ATTN_DILATIONS]
    dil_shapes = [jax.ShapeDtypeStruct((B, S // d, d * GROUP_WIDTH), BF16) for d in ATTN_DILATIONS]
    outs = pl.pallas_call(
        functools.partial(_in_proj_prompt_kernel, tm=tm, n_tiles=n_tiles, tails=tuple(tails)),
        grid=(B, n_tiles),
        in_specs=[tok(D),
                  pl.BlockSpec((1, N_COND, 1, D), lambda b, i: (b, 0, 0, 0)),
                  _full((1, D)),
                  _full((D, n_w)),
                  pl.BlockSpec((tm, LANE), lambda b, i: (i, 0)),
                  pl.BlockSpec((tm, LANE), lambda b, i: (i, 0)),
                  _full(w_pool.shape),
                  _full((1, POOL_WIDTH))],
        out_specs=dil_specs * 3 + [tok(POOL_WIDTH)] + kv_specs
                  + [pl.BlockSpec((1, POOL_HALO, POOL_WIDTH), lambda b, i: (b, 0, 0))],
        out_shape=dil_shapes * 3
                  + [jax.ShapeDtypeStruct((B, S, POOL_WIDTH), BF16)] + kv_shapes
                  + [jax.ShapeDtypeStruct((B, POOL_HALO, POOL_WIDTH), F32)],
        scratch_shapes=[pltpu.VMEM((ATTN_WIDTH // LANE, tm, LANE), F32)] * 3
                       + [pltpu.VMEM((tm + POOL_HALO, POOL_WIDTH), F32)],
        compiler_params=_cparams(2),
        name="in_proj_prompt",
    )(x, mod, g1, w_qkvu, cos, sin, w_pool, pool_scale)
    return outs


def _in_proj_sample_kernel(x_ref, mod_ref, g1_ref, w_ref, cos_ref, sin_ref, q_ref, kvn_ref, u_ref):
    h = _norm_mod(x_ref[0], g1_ref[...], mod_ref[0, 1], mod_ref[0, 0]).astype(BF16)
    y = jnp.dot(h, w_ref[...], preferred_element_type=F32)
    cos = cos_ref[...]
    sin = sin_ref[...]
    q = _rope(y[:, :ATTN_WIDTH], cos, sin) * (HEAD_DIM ** -0.5)
    k = _rope(y[:, ATTN_WIDTH:2 * ATTN_WIDTH], cos, sin)
    v = y[:, 2 * ATTN_WIDTH:3 * ATTN_WIDTH]
    q_ref[...] = q.astype(BF16)
    for g in range(N_GROUPS):
        gl = slice(g * GROUP_WIDTH, (g + 1) * GROUP_WIDTH)
        kvn_ref[:, 2 * g * GROUP_WIDTH:(2 * g + 1) * GROUP_WIDTH] = k[:, gl]
        kvn_ref[:, (2 * g + 1) * GROUP_WIDTH:(2 * g + 2) * GROUP_WIDTH] = v[:, gl]
    u_ref[...] = y[:, 3 * ATTN_WIDTH:3 * ATTN_WIDTH + POOL_WIDTH]


def _in_proj_sample(x, mod, g1, w_qkvu, cos, sin):
    _, T, D = x.shape
    n_w = w_qkvu.shape[1]
    return pl.pallas_call(
        _in_proj_sample_kernel,
        grid=(1,),
        in_specs=[_full((1, T, D)), _full((1, N_COND, T, D)), _full((1, D)), _full((D, n_w)),
                  _full((T, LANE)), _full((T, LANE))],
        out_specs=[_full((T, ATTN_WIDTH)), _full((T, 2 * ATTN_WIDTH)), _full((T, POOL_WIDTH))],
        out_shape=[jax.ShapeDtypeStruct((T, ATTN_WIDTH), BF16),
                   jax.ShapeDtypeStruct((T, 2 * ATTN_WIDTH), F32),
                   jax.ShapeDtypeStruct((T, POOL_WIDTH), F32)],
        compiler_params=_cparams(1),
        name="in_proj_sample",
    )(x, mod, g1, w_qkvu, cos, sin)


def _attn_body(q_ref, kc_ref, kp_ref, vc_ref, vp_ref, o_ref, lse_ref, kbuf, vbuf, n0, nq):
    kbuf[0:SPAN, :] = kp_ref[0]
    kbuf[SPAN:, :] = kc_ref[0]
    vbuf[0:SPAN, :] = vp_ref[0]
    vbuf[SPAN:, :] = vc_ref[0]
    qi = lax.broadcasted_iota(jnp.int32, (SPAN, 2 * SPAN), 0)
    kj = lax.broadcasted_iota(jnp.int32, (SPAN, 2 * SPAN), 1)
    band = (kj >= qi) & (kj <= qi + SPAN)
    band_first = band & (kj >= jnp.where(n0 > 0, 0, SPAN))
    lane = lax.broadcasted_iota(jnp.int32, (SPAN, LANE), 1)
    low_head = lane < HEAD_DIM
    for j in range(nq):
        valid = band_first if j == 0 else band
        rows = slice(j * SPAN, (j + 1) * SPAN)
        krows = slice(j * SPAN, (j + 2) * SPAN)
        for c in range(GROUP_WIDTH // LANE):
            cl = slice(c * LANE, (c + 1) * LANE)
            q = q_ref[0, rows, cl]
            kk = kbuf[krows, cl]
            vv = vbuf[krows, cl]
            o_pair, lse_pair = [], []
            for hh in range(2):
                mask_h = low_head if hh == 0 else jnp.logical_not(low_head)
                qm = jnp.where(mask_h, q, jnp.zeros_like(q))
                s = lax.dot_general(qm, kk, (((1,), (1,)), ((), ())), preferred_element_type=F32)
                s = jnp.where(valid, s, NEG)
                m = jnp.max(s, axis=-1, keepdims=True)
                p = jnp.exp(s - m)
                den = jnp.sum(p, axis=-1, keepdims=True)
                o_pair.append(jnp.dot(p.astype(BF16), vv, preferred_element_type=F32) / den)
                lse_pair.append(m + jnp.log(den))
            o_ref[0, rows, cl] = jnp.where(low_head, o_pair[0], o_pair[1]).astype(BF16)
            lse_ref[0, rows, cl] = jnp.where(low_head, lse_pair[0], lse_pair[1])


def _sample_body(qbd_ref, kvn_ref, c0_ref, c1_ref, c2_ref, sp_ref, u_ref, wpool_ref, pscale_ref,
                 attn_ref, pooled_ref, o0_ref, o1_ref, o2_ref, po_ref, ue_ref, diff_ref, nt_ref, t_new):
    rows = HEADS_PER_GROUP * t_new
    caches = (c0_ref, c1_ref, c2_ref)
    outs = (o0_ref, o1_ref, o2_ref)

    s_cache, s_new, valid_new = [], [], []
    m = jnp.full((rows, 1), NEG, F32)
    for g in range(N_GROUPS):
        d = ATTN_DILATIONS[g]
        c_ref = caches[g]
        L = c_ref.shape[3]
        qb = qbd_ref[0, g]
        s = jnp.dot(qb, c_ref[0, 0].astype(BF16), preferred_element_type=F32)
        t_row = lax.broadcasted_iota(jnp.int32, (rows, L), 0) >> HEAD_SHIFT
        delta = L + t_row - lax.broadcasted_iota(jnp.int32, (rows, L), 1)
        ok = ((delta & (d - 1)) == 0) & (delta <= SPAN * d)
        s = jnp.where(ok, s, NEG)
        s_cache.append(s)
        m = jnp.maximum(m, jnp.max(s, axis=-1, keepdims=True))
        qf = qb.astype(F32)
        t_col = lax.broadcasted_iota(jnp.int32, (rows, 1), 0) >> HEAD_SHIFT
        sn, okn = [], []
        for tn in range(t_new):
            kn = kvn_ref[0, tn:tn + 1, 2 * g * GROUP_WIDTH:(2 * g + 1) * GROUP_WIDTH]
            kn = kn.astype(BF16).astype(F32)
            dn = t_col - tn
            ok_n = (dn >= 0) & ((dn & (d - 1)) == 0)
            s1 = jnp.where(ok_n, jnp.sum(qf * kn, axis=-1, keepdims=True), NEG)
            sn.append(s1)
            okn.append(ok_n)
            m = jnp.maximum(m, s1)
        s_new.append(sn)
        valid_new.append(okn)

    den = jnp.zeros((rows, 1), F32)
    acc = jnp.zeros((rows, GROUP_WIDTH), F32)
    for g in range(N_GROUPS):
        c_ref = caches[g]
        p = jnp.exp(s_cache[g] - m)
        den = den + jnp.sum(p, axis=-1, keepdims=True)
        acc = acc + lax.dot_general(p.astype(BF16), c_ref[0, 1].astype(BF16), (((1,), (1,)), ((), ())),
                                    preferred_element_type=F32)
        for tn in range(t_new):
            pn = jnp.exp(s_new[g][tn] - m)
            den = den + pn
            vn = kvn_ref[0, tn:tn + 1, (2 * g + 1) * GROUP_WIDTH:(2 * g + 2) * GROUP_WIDTH]
            acc = acc + pn * vn
    row_head = lax.broadcasted_iota(jnp.int32, (rows, GROUP_WIDTH), 0) & (HEADS_PER_GROUP - 1)
    lane_head = lax.broadcasted_iota(jnp.int32, (rows, GROUP_WIDTH), 1) >> LANE_HEAD_SHIFT
    o_diag = jnp.where(row_head == lane_head, acc / den, 0.0).astype(BF16)
    sel = ((lax.broadcasted_iota(jnp.int32, (rows, rows), 1) >> HEAD_SHIFT)
           == lax.broadcasted_iota(jnp.int32, (rows, rows), 0)).astype(BF16)
    attn = jnp.dot(sel, o_diag, preferred_element_type=F32)
    attn_ref[0] = attn[0:t_new].astype(BF16)

    tail_lane = lax.broadcasted_iota(jnp.int32, (GROUP_WIDTH, LANE), 1)
    nt_ref[...] = jnp.zeros(nt_ref.shape, F32)
    for g in range(N_GROUPS):
        c_ref, o_ref = caches[g], outs[g]
        L = c_ref.shape[3]
        for kv in range(2):
            col0 = (2 * g + kv) * GROUP_WIDTH
            nt_ref[LANE - t_new:LANE, :] = kvn_ref[0, :, col0:col0 + GROUP_WIDTH]
            new_t = nt_ref[...].T
            rolled = pltpu.roll(c_ref[0, kv], L - t_new, 1)
            if L > LANE:
                o_ref[0, kv, :, 0:L - LANE] = rolled[:, 0:L - LANE]
            o_ref[0, kv, :, L - LANE:L] = jnp.where(tail_lane >= LANE - t_new, new_t, rolled[:, L - LANE:L])

    ue_ref[0:POOL_STATE_LEN, :] = sp_ref[0]
    ue_ref[POOL_STATE_LEN:POOL_STATE_LEN + t_new, :] = u_ref[0]
    diff_ref[...] = jnp.zeros(diff_ref.shape, F32)
    for tn in range(t_new):
        r = POOL_STATE_LEN + tn
        for gi, w in enumerate(POOL_WINDOWS):
            lanes = slice(gi * POOL_GROUP_WIDTH, (gi + 1) * POOL_GROUP_WIDTH)
            win = jnp.sum(ue_ref[r - w + 1:r + 1, lanes], axis=0, keepdims=True)
            diff_ref[tn:tn + 1, lanes] = win / float(w) - ue_ref[r:r + 1, lanes]
    for gi in range(len(POOL_WINDOWS)):
        lanes = slice(gi * POOL_GROUP_WIDTH, (gi + 1) * POOL_GROUP_WIDTH)
        z = jnp.dot(diff_ref[:, lanes].astype(BF16), wpool_ref[gi], preferred_element_type=F32)
        pooled_ref[0, :, lanes] = (z[0:t_new] * pscale_ref[:, lanes]).astype(BF16)
    po_ref[0] = ue_ref[t_new:t_new + POOL_STATE_LEN, :]


N_ATTN_IN = 5
N_SAMPLE_IN = 9
N_SAMPLE_OUT = 6


def _attn_sample_kernel(*refs, plans, n_seq, t_new):
    n_att = N_GROUPS * N_ATTN_IN
    attn_in = refs[:n_att]
    sample_in = refs[n_att:n_att + N_SAMPLE_IN]
    outs = refs[n_att + N_SAMPLE_IN:]
    attn_out = outs[:2 * N_GROUPS]
    sample_out = outs[2 * N_GROUPS:2 * N_GROUPS + N_SAMPLE_OUT]
    scratch = outs[2 * N_GROUPS + N_SAMPLE_OUT:]
    kv_bufs, sample_scratch = scratch[:2 * N_GROUPS], scratch[2 * N_GROUPS:]
    s = pl.program_id(0)

    for g, (first, count, n_sup, d, nq) in enumerate(plans):
        @pl.when((s >= first) & (s < first + count))
        def _(g=g, first=first, n_sup=n_sup, d=d, nq=nq):
            n0 = ((s - first) // d) % n_sup
            _attn_body(*attn_in[g * N_ATTN_IN:(g + 1) * N_ATTN_IN], attn_out[2 * g], attn_out[2 * g + 1],
                       kv_bufs[2 * g], kv_bufs[2 * g + 1], n0, nq)

    @pl.when(s < n_seq)
    def _():
        _sample_body(*sample_in, *sample_out, *sample_scratch, t_new)


def _attn_and_sample(qs, ks, vs, qbd, kvn, caches, state_pool, u, w_pool, pool_scale, nq_max):
    N, t_new, _ = u.shape
    rows = HEADS_PER_GROUP * t_new
    plans, in_specs, out_specs, out_shape, scratch, args = [], [], [], [], [], []
    first = 0
    for g in range(N_GROUPS):
        B, M, width = qs[g].shape
        d = width // GROUP_WIDTH
        nq = min(nq_max, M // SPAN)
        n_sup = M // (nq * SPAN)
        count = B * n_sup * d
        plans.append((first, count, n_sup, d, nq))

        def block_index(s, first=first, count=count, n_sup=n_sup, d=d):
            local = jnp.clip(s - first, 0, count - 1)
            return local // (n_sup * d), (local // d) % n_sup, local % d

        def cur_map(s, block_index=block_index):
            b, n, r = block_index(s)
            return b, n, r

        def prev_map(s, block_index=block_index, nq=nq):
            b, n, r = block_index(s)
            return b, jnp.maximum(n * nq - 1, 0), r

        cur = pl.BlockSpec((1, nq * SPAN, GROUP_WIDTH), cur_map)
        prev = pl.BlockSpec((1, SPAN, GROUP_WIDTH), prev_map)
        in_specs += [cur, cur, prev, cur, prev]
        args += [qs[g], ks[g], ks[g], vs[g], vs[g]]
        out_specs += [cur, cur]
        out_shape += [jax.ShapeDtypeStruct((B, M, width), BF16), jax.ShapeDtypeStruct((B, M, width), F32)]
        scratch += [pltpu.VMEM(((nq + 1) * SPAN, GROUP_WIDTH), BF16)] * 2
        first += count
    n_steps = max(first, N)

    per_n = lambda shape: pl.BlockSpec((1,) + tuple(shape),
                                       lambda s: (jnp.minimum(s, N - 1),) + (0,) * len(shape))
    cache_specs = [per_n(c.shape[1:]) for c in caches]
    in_specs += ([per_n(qbd.shape[1:]), per_n(kvn.shape[1:])] + cache_specs
                 + [per_n(state_pool.shape[1:]), per_n(u.shape[1:]), _full(w_pool.shape), _full((1, POOL_WIDTH))])
    args += [qbd, kvn, *caches, state_pool, u, w_pool, pool_scale]
    out_specs += ([per_n((t_new, GROUP_WIDTH)), per_n((t_new, POOL_WIDTH))] + cache_specs
                  + [per_n(state_pool.shape[1:])])
    out_shape += ([jax.ShapeDtypeStruct((N, t_new, GROUP_WIDTH), BF16),
                   jax.ShapeDtypeStruct((N, t_new, POOL_WIDTH), BF16)]
                  + [jax.ShapeDtypeStruct(c.shape, F32) for c in caches]
                  + [jax.ShapeDtypeStruct(state_pool.shape, F32)])
    scratch += [pltpu.VMEM((POOL_STATE_LEN + t_new + 5, POOL_WIDTH), F32),
                pltpu.VMEM((rows, POOL_WIDTH), F32),
                pltpu.VMEM((LANE, GROUP_WIDTH), F32)]
    outs = pl.pallas_call(
        functools.partial(_attn_sample_kernel, plans=tuple(plans), n_seq=N, t_new=t_new),
        grid=(n_steps,),
        in_specs=in_specs,
        out_specs=out_specs,
        out_shape=out_shape,
        scratch_shapes=scratch,
        compiler_params=_cparams(1),
        name="attn_and_sample",
    )(*args)
    return [(outs[2 * g], outs[2 * g + 1]) for g in range(N_GROUPS)], outs[2 * N_GROUPS:]


def _route(logits):
    lane = lax.broadcasted_iota(jnp.int32, logits.shape, 1).astype(F32)
    is_grp = (lane >= GROUP_LANE0) & (lane < GROUP_LANE0 + MOE_GROUPS)
    gl = jnp.where(is_grp, logits, NEG)
    gmax = jnp.max(gl, axis=-1, keepdims=True)
    gidx = jnp.min(jnp.where(gl == gmax, lane, BIG_LANE), axis=-1, keepdims=True) - GROUP_LANE0
    gsum = jnp.sum(jnp.where(is_grp, jnp.exp(gl - gmax), 0.0), axis=-1, keepdims=True)
    grp_w = 1.0 / gsum
    lo = gidx * EXPERTS_PER_GROUP
    in_grp = (lane >= lo) & (lane < lo + EXPERTS_PER_GROUP)
    el = jnp.where(in_grp, logits, NEG)
    v1 = jnp.max(el, axis=-1, keepdims=True)
    i1 = jnp.min(jnp.where(el == v1, lane, BIG_LANE), axis=-1, keepdims=True)
    el2 = jnp.where(lane == i1, NEG, el)
    v2 = jnp.max(el2, axis=-1, keepdims=True)
    i2 = jnp.min(jnp.where(el2 == v2, lane, BIG_LANE), axis=-1, keepdims=True)
    e = jnp.exp(v2 - v1)
    w1 = grp_w / (1.0 + e)
    w2 = grp_w * e / (1.0 + e)
    return lane, i1, i2, w1, w2


def _merge_kernel(*refs, n_groups):
    x_ref, mod_ref, g1_ref, g2_ref = refs[0:4]
    n_attn = 2 * n_groups if n_groups > 1 else 1
    attn_refs = refs[4:4 + n_attn]
    rest = refs[4 + n_attn:]
    pooled_ref, wg_ref, wa_ref, wp_ref, wo_ref, wrh_ref, wrc_ref, br_ref = rest[:8]
    x1_ref, h2_ref, route_ref, counts_ref, route_t_ref = rest[8:13]
    scratch_refs = rest[13:]
    stage_refs = scratch_refs[:n_attn] if n_groups > 1 else ()
    carry_ref = scratch_refs[-1]

    x = x_ref[0]
    tm = x.shape[0]
    if n_groups > 1:
        vals = []
        for idx, ref in enumerate(attn_refs):
            d = ATTN_DILATIONS[idx % n_groups]
            if d == 1:
                vals.append(ref[0].astype(F32))
                continue
            stage_ref = stage_refs[idx]
            chunks = GROUP_WIDTH // LANE
            for r in range(d):
                for c in range(chunks):
                    stage_ref[c, pl.ds(r, tm // d, stride=d), :] = (
                        ref[0, :, r * GROUP_WIDTH + c * LANE:r * GROUP_WIDTH + (c + 1) * LANE].astype(F32))
            vals.append(jnp.concatenate([stage_ref[c] for c in range(chunks)], axis=1))
        os_, ls = vals[:n_groups], vals[n_groups:]
        lmax = functools.reduce(jnp.maximum, ls)
        es = [jnp.exp(l - lmax) for l in ls]
        attn = sum(e * o for e, o in zip(es, os_)) / sum(es)
    else:
        attn = attn_refs[0][0]
    a = jnp.dot(attn.astype(BF16), wa_ref[...], preferred_element_type=F32)
    p = jnp.dot(pooled_ref[0], wp_ref[...], preferred_element_type=F32)
    h = _norm_mod(x, g1_ref[...], mod_ref[0, 1], mod_ref[0, 0]).astype(BF16)
    gates = jnp.dot(h, wg_ref[...], preferred_element_type=F32)
    D = x.shape[1]
    merged = jax.nn.sigmoid(gates[:, :D]) * a + jax.nn.sigmoid(gates[:, D:]) * p
    y = jnp.dot(merged.astype(BF16), wo_ref[...], preferred_element_type=F32)
    x1 = x + mod_ref[0, 2] * y
    x1_ref[0] = x1
    h2 = _norm_mod(x1, g2_ref[...], mod_ref[0, 4], mod_ref[0, 3])
    h2_hi = h2.astype(BF16)
    h2_lo = (h2 - h2_hi.astype(F32)).astype(BF16)
    hi_terms = jnp.dot(h2_hi, wrc_ref[...], preferred_element_type=F32)
    logits = (hi_terms[:, :ROUTER_LANES] + jnp.dot(h2_lo, wrh_ref[...], preferred_element_type=F32)
              + hi_terms[:, ROUTER_LANES:]) + br_ref[...]
    lane, i1, i2, w1, w2 = _route(logits)
    h2_ref[0] = h2

    @pl.when((pl.program_id(0) == 0) & (pl.program_id(1) == 0))
    def _():
        carry_ref[...] = jnp.zeros(carry_ref.shape, F32)

    hit = ((lane == i1) | (lane == i2)).astype(BF16)
    ltri = (lax.broadcasted_iota(jnp.int32, (tm, tm), 0) >= lax.broadcasted_iota(jnp.int32, (tm, tm), 1))
    prefix = jnp.dot(ltri.astype(BF16), hit, preferred_element_type=F32) + carry_ref[...]
    rank1 = jnp.sum(jnp.where(lane == i1, prefix, 0.0), axis=-1, keepdims=True) - 1.0
    rank2 = jnp.sum(jnp.where(lane == i2, prefix, 0.0), axis=-1, keepdims=True) - 1.0
    carry_ref[...] = prefix[tm - 1:tm, :]
    counts_ref[...] = jnp.broadcast_to(prefix[tm - 1:tm, :], counts_ref.shape)
    cols = (i1, i2, w1, w2, rank1, rank2)
    route = jnp.zeros(logits.shape, F32)
    for c, col in enumerate(cols):
        route = jnp.where(lane == float(c), col, route)
    route_ref[0] = route
    route_t_ref[...] = route.T[0:SUBLANE, :]


def _merge(x, mod, g1, g2, attn_inputs, pooled, w_gates, w_attn_out, w_pool_out, w_o, wr_hi, wr_hilo, b_r, tm):
    B, S, D = x.shape
    R = mod.shape[2]
    n_groups = len(attn_inputs) // 2 if len(attn_inputs) > 1 else 1
    tok = lambda width: pl.BlockSpec((1, tm, width), lambda b, i: (b, i, 0))
    if R == 1:
        mod_spec = pl.BlockSpec((1, N_COND, 1, D), lambda b, i: (b, 0, 0, 0))
    else:
        mod_spec = pl.BlockSpec((1, N_COND, tm, D), lambda b, i: (b, 0, i, 0))
    weights = (w_gates, w_attn_out, w_pool_out, w_o, wr_hi, wr_hilo, b_r)
    if n_groups > 1:
        attn_specs = [pl.BlockSpec((1, tm // (a.shape[2] // GROUP_WIDTH), a.shape[2]), lambda b, i: (b, i, 0))
                      for a in attn_inputs]
        scratch = [pltpu.VMEM((GROUP_WIDTH // LANE, tm, LANE), F32)] * len(attn_inputs)
    else:
        attn_specs = [tok(GROUP_WIDTH)]
        scratch = []
    n_t = S // tm
    out_specs = [tok(D), tok(D), tok(ROUTER_LANES), _full((SUBLANE, ROUTER_LANES)),
                 pl.BlockSpec((SUBLANE, tm), lambda b, i: (0, b * n_t + i))]
    out_shape = [jax.ShapeDtypeStruct((B, S, D), F32),
                 jax.ShapeDtypeStruct((B, S, D), F32),
                 jax.ShapeDtypeStruct((B, S, ROUTER_LANES), F32),
                 jax.ShapeDtypeStruct((SUBLANE, ROUTER_LANES), F32),
                 jax.ShapeDtypeStruct((SUBLANE, B * S), F32)]
    scratch = scratch + [pltpu.VMEM((1, ROUTER_LANES), F32)]
    return pl.pallas_call(
        functools.partial(_merge_kernel, n_groups=n_groups),
        grid=(B, S // tm),
        in_specs=[tok(D), mod_spec, _full((1, D)), _full((1, D))]
                 + attn_specs + [tok(POOL_WIDTH)]
                 + [_full(w.shape) for w in weights],
        out_specs=out_specs,
        out_shape=out_shape,
        scratch_shapes=scratch,
        compiler_params=_cparams(2),
        name="merge",
    )(x, mod, g1, g2, *attn_inputs, pooled, *weights)


EXPERT_TILE = 512
PAD_CHUNK = 32


def _positions_kernel(base_ref, route_ref, pos_ref):
    cols = route_ref[...].astype(jnp.int32)
    start = jnp.zeros_like(cols)
    for e in range(N_EXPERTS):
        start = jnp.where(cols == e, base_ref[e], start)
    pos_ref[...] = start + pltpu.roll(cols, SUBLANE - 4, 0)


def _positions(route_t, base):
    rows, T = route_t.shape
    grid_spec = pltpu.PrefetchScalarGridSpec(
        num_scalar_prefetch=1, grid=(1,),
        in_specs=[pl.BlockSpec((rows, T), lambda i, base: (0, 0))],
        out_specs=pl.BlockSpec((rows, T), lambda i, base: (0, 0)))
    pos = pl.pallas_call(
        _positions_kernel,
        grid_spec=grid_spec,
        out_shape=jax.ShapeDtypeStruct((rows, T), jnp.int32),
        compiler_params=_cparams(1),
        name="moe_positions",
    )(base.astype(jnp.int32), route_t)
    return pos[0:2]


def _routing_tables(routes, counts, tile):
    cnts = [c[0, :N_EXPERTS].astype(jnp.int32) for c in counts]
    cnt = sum(cnts)
    padded = ((cnt + tile - 1) // tile) * tile
    ends = jnp.cumsum(padded)
    base = ends - padded
    poss = []
    for route in routes:
        poss.append(_positions(route, base))
        base = base + cnts[len(poss) - 1]
    n_pairs = 2 * sum(r.shape[1] for r in routes)
    n_tiles = -(-n_pairs // tile) + N_EXPERTS
    starts = jnp.arange(n_tiles, dtype=jnp.int32) * tile
    n_used = ends[-1] // tile
    tile_expert = jnp.sum(starts[:, None] >= ends[None, :], axis=1).astype(jnp.int32)
    last = jnp.take(tile_expert, n_used - 1)
    tile_expert = jnp.where(jnp.arange(n_tiles) < n_used, tile_expert, last)
    pads = jnp.stack([ends - padded + cnts[0], ends], axis=1).reshape(-1).astype(jnp.int32)
    return poss, tile_expert, n_used.reshape(1).astype(jnp.int32), pads, n_tiles


def _pos_steps(pos, tm):
    steps = pos.shape[1] // tm
    return jnp.transpose(pos.reshape(2, steps, tm), (1, 0, 2)).reshape(steps, 1, 2 * tm)


def _row_copy(src_ref, src_row, dst_ref, dst_row, sem):
    return pltpu.make_async_copy(src_ref.at[pl.ds(src_row, 1)], dst_ref.at[pl.ds(dst_row, 1)], sem)


def _dispatch_kernel(pads_ref, pos_ref, h_ref, *rest, tm, tile, first):
    xs_ref, zero_ref, sem = rest[-3:]

    def zero_fill():
        zero_ref[...] = jnp.zeros(zero_ref.shape, zero_ref.dtype)
        n_tiles = xs_ref.shape[0] // tile
        first_unused = pads_ref[2 * N_EXPERTS - 1] // tile

        def tile_copy(j):
            return pltpu.make_async_copy(zero_ref, xs_ref.at[pl.ds(pl.multiple_of(j * tile, tile), tile)], sem)

        def fill_tile(j, c):
            tile_copy(j).start()
            return c

        def drain_tile(j, c):
            tile_copy(j).wait()
            return c

        lax.fori_loop(first_unused, n_tiles, fill_tile, 0)
        lax.fori_loop(first_unused, n_tiles, drain_tile, 0)

        def chunk_copy(c):
            rows = pl.ds(pl.multiple_of(c * PAD_CHUNK, PAD_CHUNK), PAD_CHUNK)
            return pltpu.make_async_copy(zero_ref.at[pl.ds(0, PAD_CHUNK)], xs_ref.at[rows], sem)

        def for_each_pad_piece(on_row, on_chunk):
            def per_expert(e, carry):
                lo, hi = pads_ref[2 * e], pads_ref[2 * e + 1]
                first_chunk = (lo + PAD_CHUNK - 1) // PAD_CHUNK
                lax.fori_loop(lo, jnp.minimum(first_chunk * PAD_CHUNK, hi), on_row, 0)
                lax.fori_loop(first_chunk, hi // PAD_CHUNK, on_chunk, 0)
                return carry

            lax.fori_loop(0, N_EXPERTS, per_expert, 0)

        def start_row(r, c):
            _row_copy(zero_ref, 0, xs_ref, r, sem).start()
            return c

        def wait_row(r, c):
            _row_copy(zero_ref, 0, xs_ref, r, sem).wait()
            return c

        def start_chunk(c, carry):
            chunk_copy(c).start()
            return carry

        def wait_chunk(c, carry):
            chunk_copy(c).wait()
            return carry

        for_each_pad_piece(start_row, start_chunk)
        for_each_pad_piece(wait_row, wait_chunk)

    if first:
        pl.when(pl.program_id(0) == 0)(zero_fill)

    def issue(k, carry):
        base = pl.multiple_of(k * SUBLANE, SUBLANE)
        for j in range(SUBLANE):
            i = base + j
            _row_copy(h_ref, i, xs_ref, pos_ref[0, 0, i], sem).start()
            _row_copy(h_ref, i, xs_ref, pos_ref[0, 0, tm + i], sem).start()
        return carry

    lax.fori_loop(0, tm // SUBLANE, issue, 0)
    for _ in range(2):
        pltpu.make_async_copy(h_ref, xs_ref.at[pl.ds(0, tm)], sem).wait()


def _dispatch(h2, pos, pads, n_rows, tm, tile, xs_prev=None):
    T, width = h2.shape
    in_specs = [pl.BlockSpec((1, 1, 2 * tm), lambda i, pads: (i, 0, 0), memory_space=pltpu.SMEM),
                pl.BlockSpec((tm, width), lambda i, pads: (i, 0))]
    args = [pads, _pos_steps(pos, tm), h2]
    aliases = {}
    if xs_prev is not None:
        in_specs.append(pl.BlockSpec(memory_space=pl.ANY))
        args.append(xs_prev)
        aliases = {3: 0}
    grid_spec = pltpu.PrefetchScalarGridSpec(
        num_scalar_prefetch=1,
        grid=(T // tm,),
        in_specs=in_specs,
        out_specs=pl.BlockSpec(memory_space=pl.ANY),
        scratch_shapes=[pltpu.VMEM((tile, width), h2.dtype), pltpu.SemaphoreType.DMA(())],
    )
    return pl.pallas_call(
        functools.partial(_dispatch_kernel, tm=tm, tile=tile, first=xs_prev is None),
        grid_spec=grid_spec,
        out_shape=jax.ShapeDtypeStruct((n_rows, width), h2.dtype),
        input_output_aliases=aliases,
        compiler_params=_cparams(1),
        name="moe_dispatch",
    )(*args)


def _experts_kernel(te_ref, nu_ref, xs_ref, wgu_ref, wd_ref, ys_ref, wgu_bf_ref, wd_bf_ref, *, d_expert):
    j = pl.program_id(0)

    @pl.when(j >= nu_ref[0])
    def _():
        ys_ref[...] = jnp.zeros(ys_ref.shape, F32)

    @pl.when((j == 0) | (te_ref[j] != te_ref[jnp.maximum(j - 1, 0)]))
    def _():
        wgu_bf_ref[...] = wgu_ref[0].astype(BF16)
        wd_bf_ref[...] = wd_ref[0].astype(BF16)

    @pl.when(j < nu_ref[0])
    def _():
        gu = jnp.dot(xs_ref[...].astype(BF16), wgu_bf_ref[...], preferred_element_type=F32)
        act = _silu(gu[:, :d_expert]) * gu[:, d_expert:]
        ys_ref[...] = jnp.dot(act.astype(BF16), wd_bf_ref[...], preferred_element_type=F32)


def _experts(xs, tile_expert, n_used, w_gate_up, w_down, tile):
    n_rows, width = xs.shape
    _, D, two_f = w_gate_up.shape
    grid_spec = pltpu.PrefetchScalarGridSpec(
        num_scalar_prefetch=2,
        grid=(n_rows // tile,),
        in_specs=[pl.BlockSpec((tile, width), lambda j, te, nu: (jnp.minimum(j, nu[0] - 1), 0)),
                  pl.BlockSpec((1, D, two_f), lambda j, te, nu: (te[j], 0, 0)),
                  pl.BlockSpec((1, two_f // 2, D), lambda j, te, nu: (te[j], 0, 0))],
        out_specs=pl.BlockSpec((tile, D), lambda j, te, nu: (j, 0)),
        scratch_shapes=[pltpu.VMEM((D, two_f), BF16), pltpu.VMEM((two_f // 2, D), BF16)],
    )
    return pl.pallas_call(
        functools.partial(_experts_kernel, d_expert=two_f // 2),
        grid_spec=grid_spec,
        out_shape=jax.ShapeDtypeStruct((n_rows, D), F32),
        compiler_params=_cparams(1),
        name="moe_experts",
    )(tile_expert, n_used, xs, w_gate_up, w_down)


def _combine_kernel(pos_ref, pos_next_ref, ys_ref, route_ref, x1_ref, mod_ref, gf_ref, y_ref,
                    ya_ref, yb_ref, sems, *, tm):
    step = pl.program_id(0) * pl.num_programs(1) + pl.program_id(1)
    n_steps = pl.num_programs(0) * pl.num_programs(1)
    slot = step % 2

    def start_gathers(p_ref, s):
        def issue(k, carry):
            base = pl.multiple_of(k * SUBLANE, SUBLANE)
            for j in range(SUBLANE):
                i = base + j
                _row_copy(ys_ref, p_ref[0, 0, i], ya_ref.at[s], i, sems.at[s]).start()
                _row_copy(ys_ref, p_ref[0, 0, tm + i], yb_ref.at[s], i, sems.at[s]).start()
            return carry

        lax.fori_loop(0, tm // SUBLANE, issue, 0)

    @pl.when(step == 0)
    def _():
        start_gathers(pos_ref, 0)

    @pl.when(step + 1 < n_steps)
    def _():
        start_gathers(pos_next_ref, 1 - slot)

    for buf in (ya_ref, yb_ref):
        pltpu.make_async_copy(ys_ref.at[pl.ds(0, tm)], buf.at[slot], sems.at[slot]).wait()
    route = route_ref[0]
    lane = lax.broadcasted_iota(jnp.int32, route.shape, 1)
    w1 = jnp.sum(jnp.where(lane == 2, route, 0.0), axis=-1, keepdims=True)
    w2 = jnp.sum(jnp.where(lane == 3, route, 0.0), axis=-1, keepdims=True)
    x2 = x1_ref[0] + mod_ref[0, 5] * (w1 * ya_ref[slot] + w2 * yb_ref[slot])
    var = jnp.mean(x2 * x2, axis=-1, keepdims=True)
    y_ref[0] = x2 * lax.rsqrt(var + EPS) * gf_ref[...]


def _combine(ys, pos, route, x1, mod, gf, tm):
    B, S, D = x1.shape
    n_t = S // tm
    tok = lambda width: pl.BlockSpec((1, tm, width), lambda b, i: (b, i, 0))
    last = B * n_t - 1
    pos_spec = lambda ahead: pl.BlockSpec(
        (1, 1, 2 * tm), lambda b, i: (jnp.minimum(b * n_t + i + ahead, last), 0, 0), memory_space=pltpu.SMEM)
    pos_steps = _pos_steps(pos, tm)
    if mod.shape[2] == 1:
        mod_spec = pl.BlockSpec((1, N_COND, 1, D), lambda b, i: (b, 0, 0, 0))
    else:
        mod_spec = pl.BlockSpec((1, N_COND, tm, D), lambda b, i: (b, 0, i, 0))
    return pl.pallas_call(
        functools.partial(_combine_kernel, tm=tm),
        grid=(B, n_t),
        in_specs=[pos_spec(0), pos_spec(1),
                  pl.BlockSpec(memory_space=pl.ANY),
                  tok(ROUTER_LANES), tok(D),
                  mod_spec,
                  _full((1, D))],
        out_specs=tok(D),
        out_shape=jax.ShapeDtypeStruct((B, S, D), F32),
        scratch_shapes=[pltpu.VMEM((2, tm, D), F32), pltpu.VMEM((2, tm, D), F32),
                        pltpu.SemaphoreType.DMA((2,))],
        compiler_params=_cparams(2),
        name="moe_combine",
    )(pos_steps, pos_steps, ys, route, x1, mod, gf)


def _rope_tables(pos):
    half = HEAD_DIM // 2
    inv = ROPE_THETA ** (-jnp.arange(half, dtype=F32) * 2.0 / HEAD_DIM)
    ang = pos.astype(F32)[:, None] * inv[None, :]
    cos, sin = jnp.cos(ang), jnp.sin(ang)
    reps = LANE // HEAD_DIM
    return (jnp.tile(jnp.concatenate([cos, cos], axis=-1), (1, reps)),
            jnp.tile(jnp.concatenate([-sin, sin], axis=-1), (1, reps)))


def kernel(x_prompt, x_sample, cache_kv_w128, cache_kv_w512, cache_kv_w2048, state_pool, c_prompt, c_sample, norm1_g, w_ada, b_ada, w_in, w_attn_out, w_pool, pool_scale, w_pool_out, w_o, norm2_g, w_grp, b_grp, w_exp_router, b_exp_router, w_gate_up, w_down, final_norm_g):
    B, S, D = x_prompt.shape
    N, T, _ = x_sample.shape
    depth = norm1_g.shape[0]
    assert depth == 1, "single trunk layer"
    tm = min(TOKEN_TILE, S)
    assert S % tm == 0 and all(S % (SPAN * d) == 0 for d in ATTN_DILATIONS)

    n_qkvu = 3 * ATTN_WIDTH + POOL_WIDTH
    w_qkvu = w_in[0, :, :n_qkvu].astype(BF16)
    w_gates = w_in[0, :, n_qkvu:].astype(BF16)
    wa, wpo, wo = w_attn_out[0].astype(BF16), w_pool_out[0].astype(BF16), w_o[0].astype(BF16)
    wpool = w_pool[0].astype(BF16)
    pscale = pool_scale[0].reshape(1, POOL_WIDTH)
    w_r = jnp.concatenate([w_exp_router[0], w_grp[0]], axis=1)
    w_r = jnp.pad(w_r, ((0, 0), (0, ROUTER_LANES - w_r.shape[1])))
    wr_hi = w_r.astype(BF16)
    wr_hilo = jnp.concatenate([wr_hi, (w_r - wr_hi.astype(F32)).astype(BF16)], axis=1)
    b_r = jnp.pad(jnp.concatenate([b_exp_router[0], b_grp[0]]), (0, ROUTER_LANES - N_EXPERTS - MOE_GROUPS))
    b_r = b_r.reshape(1, ROUTER_LANES)
    wgu, wd = w_gate_up[0], w_down[0]
    g1, g2, gf = norm1_g[0].reshape(1, D), norm2_g[0].reshape(1, D), final_norm_g.reshape(1, D)

    c_rows = jnp.concatenate([jnp.repeat(c_sample, T, axis=0), c_prompt], axis=0)
    mod = _ada(c_rows, w_ada[0], b_ada[0])
    mod_p = jnp.transpose(mod[:, N * T:], (1, 0, 2)).reshape(B, N_COND, 1, D)
    mod_s = mod[None]

    cos_p, sin_p = _rope_tables(jnp.arange(S, dtype=jnp.int32))
    outs = _in_proj_prompt(x_prompt, mod_p, g1, w_qkvu, cos_p, sin_p, wpool, pscale, tm)
    qs, ks, vs = outs[0:3], outs[3:6], outs[6:9]
    pooled, kv0, kv1, kv2, ptail = outs[9:14]
    pool_prompt = ptail[:, POOL_HALO - POOL_STATE_LEN:, :][None]

    TS = N * T
    pos_s = PAST_LEN + jnp.arange(T, dtype=jnp.int32)
    cos_s, sin_s = _rope_tables(pos_s)
    cos_s, sin_s = jnp.tile(cos_s, (N, 1)), jnp.tile(sin_s, (N, 1))
    xs = x_sample.reshape(1, TS, D)
    q_s, kvn, u_s = _in_proj_sample(xs, mod_s, g1, w_qkvu, cos_s, sin_s)
    eye = jnp.eye(HEADS_PER_GROUP, dtype=BF16)
    qbd = jnp.einsum('ntghe,hk->ngthke', q_s.reshape(N, T, N_GROUPS, HEADS_PER_GROUP, HEAD_DIM), eye)
    qbd = qbd.reshape(N, N_GROUPS, T * HEADS_PER_GROUP, GROUP_WIDTH)
    caches = [jnp.transpose(c[0], (0, 2, 3, 4, 1)).reshape(N, 2, GROUP_WIDTH, c.shape[2])
              for c in (cache_kv_w128, cache_kv_w512, cache_kv_w2048)]

    attn_parts, (attn_s, pooled_s, ko0, ko1, ko2, pool_s) = _attn_and_sample(
        qs, ks, vs, qbd, kvn.reshape(N, T, 2 * ATTN_WIDTH), caches, state_pool[0],
        u_s.reshape(N, T, POOL_WIDTH), wpool, pscale, nq_max=ATTN_BLOCKS_PER_STEP)
    attn_inputs = [o for o, _ in attn_parts] + [l for _, l in attn_parts]
    x1, h2, route, counts, route_t = _merge(x_prompt, mod_p, g1, g2, attn_inputs, pooled, w_gates, wa, wpo, wo,
                                            wr_hi, wr_hilo, b_r, tm)
    x1s, h2s, route_s, counts_s, route_ts = _merge(xs, mod_s, g1, g2, [attn_s.reshape(1, TS, GROUP_WIDTH)],
                                                   pooled_s.reshape(1, TS, POOL_WIDTH), w_gates, wa, wpo, wo,
                                                   wr_hi, wr_hilo, b_r, TS)

    (pos, pos_s), tile_expert, n_used, pads, n_tiles = _routing_tables(
        [route_t, route_ts], [counts, counts_s], EXPERT_TILE)
    sorted_rows = _dispatch(h2.reshape(B * S, D), pos, pads, n_tiles * EXPERT_TILE, tm, EXPERT_TILE)
    sorted_rows = _dispatch(h2s.reshape(TS, D), pos_s, pads, n_tiles * EXPERT_TILE, TS, EXPERT_TILE,
                            xs_prev=sorted_rows)
    ys = _experts(sorted_rows, tile_expert, n_used, wgu, wd, EXPERT_TILE)
    y_prompt = _combine(ys, pos, route, x1, mod_p, gf, tm=tm)
    y_sample = _combine(ys, pos_s, route_s, x1s, mod_s, gf, tm=TS).reshape(N, T, D)

    def kv_shape(a):
        a = a.reshape(a.shape[0], 2, HEADS_PER_GROUP, HEAD_DIM, a.shape[3])
        return jnp.transpose(a, (0, 4, 1, 2, 3))[None]

    return (y_prompt, y_sample, kv_shape(kv0), kv_shape(kv1), kv_shape(kv2), pool_prompt,
            kv_shape(ko0), kv_shape(ko1), kv_shape(ko2), pool_s[None])
```

```python
import functools

import jax
import jax.numpy as jnp
from jax import lax
from jax.experimental import pallas as pl
from jax.experimental.pallas import tpu as pltpu

F32 = jnp.float32
BF16 = jnp.bfloat16

HEAD_DIM = 64
HEADS_PER_GROUP = 4
HEAD_SHIFT = 2
LANE_HEAD_SHIFT = 6
GROUP_WIDTH = HEADS_PER_GROUP * HEAD_DIM
ATTN_WINDOWS = (128, 512, 2048)
ATTN_DILATIONS = (1, 4, 16)
N_GROUPS = 3
SPAN = 128
ATTN_WIDTH = N_GROUPS * GROUP_WIDTH
ROPE_THETA = 10000.0
PAST_LEN = 8192
POOL_WINDOWS = (2, 4, 8, 16)
POOL_GROUP_WIDTH = 128
POOL_WIDTH = 512
POOL_STATE_LEN = 15
POOL_HALO = 16
MOE_GROUPS = 4
EXPERTS_PER_GROUP = 8
N_EXPERTS = 32
N_COND = 6
EPS = 1e-6

LANE = 128
SUBLANE = 8
VMEM_LIMIT_BYTES = 56 * 1024 * 1024

TOKEN_TILE = 512
ATTN_BLOCKS_PER_STEP = 8

NEG = -1e30
BIG_LANE = 1e9

ROUTER_LANES = LANE
GROUP_LANE0 = N_EXPERTS


def _cparams(n_axes):
    return pltpu.CompilerParams(dimension_semantics=("arbitrary",) * n_axes,
                                vmem_limit_bytes=VMEM_LIMIT_BYTES)


def _full(shape):
    nd = len(shape)
    return pl.BlockSpec(tuple(shape), lambda *_: (0,) * nd)


def _norm_mod(x, g, scale, shift):
    var = jnp.mean(x * x, axis=-1, keepdims=True)
    return (x * lax.rsqrt(var + EPS) * g) * (1.0 + scale) + shift


def _rope(x, cos, sin):
    lane = lax.broadcasted_iota(jnp.int32, (x.shape[0], LANE), 1)
    first_half = (lane & (HEAD_DIM - 1)) < (HEAD_DIM // 2)
    outs = []
    for c in range(x.shape[1] // LANE):
        xc = x[:, c * LANE:(c + 1) * LANE]
        partner = jnp.where(first_half, pltpu.roll(xc, LANE - HEAD_DIM // 2, 1),
                            pltpu.roll(xc, HEAD_DIM // 2, 1))
        outs.append(xc * cos + partner * sin)
    return jnp.concatenate(outs, axis=1)


def _silu(x):
    return x * jax.nn.sigmoid(x)


def _ada_kernel(c_ref, w_ref, b_ref, o_ref):
    s = _silu(c_ref[...]).astype(BF16)
    o_ref[0] = jnp.dot(s, w_ref[...].astype(BF16), preferred_element_type=F32) + b_ref[...]


def _ada(c_all, w_ada, b_ada):
    rows, d = c_all.shape
    n_out = w_ada.shape[1]
    tn = n_out // N_COND
    return pl.pallas_call(
        _ada_kernel,
        grid=(N_COND,),
        in_specs=[_full((rows, d)),
                  pl.BlockSpec((d, tn), lambda j: (0, j)),
                  pl.BlockSpec((1, tn), lambda j: (0, j))],
        out_specs=pl.BlockSpec((1, rows, tn), lambda j: (j, 0, 0)),
        out_shape=jax.ShapeDtypeStruct((N_COND, rows, tn), F32),
        compiler_params=_cparams(1),
        name="ada",
    )(c_all, w_ada, b_ada.reshape(1, n_out))


def _in_proj_prompt_kernel(x_ref, mod_ref, g1_ref, w_ref, cos_ref, sin_ref, wpool_ref, pscale_ref, *rest,
                           tm, n_tiles, tails):
    qkv_refs = (rest[0:3], rest[3:6], rest[6:9])
    pooled_ref, kv0_ref, kv1_ref, kv2_ref, ptail_ref = rest[9:14]
    stage_refs = rest[14:17]
    ue_ref = rest[17]
    i = pl.program_id(1)
    x = x_ref[0]
    h = _norm_mod(x, g1_ref[...], mod_ref[0, 1], mod_ref[0, 0]).astype(BF16)
    cos = cos_ref[...]
    sin = sin_ref[...]
    y = jnp.dot(h, w_ref[...], preferred_element_type=F32)
    q = _rope(y[:, :ATTN_WIDTH], cos, sin) * (HEAD_DIM ** -0.5)
    k = _rope(y[:, ATTN_WIDTH:2 * ATTN_WIDTH], cos, sin)
    v = y[:, 2 * ATTN_WIDTH:3 * ATTN_WIDTH]
    u = y[:, 3 * ATTN_WIDTH:3 * ATTN_WIDTH + POOL_WIDTH]

    chunks = GROUP_WIDTH // LANE
    for val, out_refs, stage_ref in zip((q, k, v), qkv_refs, stage_refs):
        for g, d in enumerate(ATTN_DILATIONS):
            gl = slice(g * GROUP_WIDTH, (g + 1) * GROUP_WIDTH)
            if d == 1:
                out_refs[g][0] = val[:, gl].astype(BF16)
                continue
            for c in range(chunks):
                stage_ref[g * chunks + c] = val[:, g * GROUP_WIDTH + c * LANE:g * GROUP_WIDTH + (c + 1) * LANE]
            for r in range(d):
                for c in range(chunks):
                    out_refs[g][0, :, r * GROUP_WIDTH + c * LANE:r * GROUP_WIDTH + (c + 1) * LANE] = (
                        stage_ref[g * chunks + c, pl.ds(r, tm // d, stride=d), :].astype(BF16))

    for g, kv_ref in enumerate((kv0_ref, kv1_ref, kv2_ref)):
        first_tile, rows = tails[g]
        lo = tm - rows

        @pl.when(i >= first_tile)
        def _(kv_ref=kv_ref, g=g, lo=lo):
            kv_ref[0, 0] = k[lo:, g * GROUP_WIDTH:(g + 1) * GROUP_WIDTH].T
            kv_ref[0, 1] = v[lo:, g * GROUP_WIDTH:(g + 1) * GROUP_WIDTH].T

    @pl.when(i == 0)
    def _():
        ue_ref[0:POOL_HALO, :] = jnp.zeros((POOL_HALO, POOL_WIDTH), F32)

    ue_ref[POOL_HALO:, :] = u
    pos1 = (i * tm + 1 + lax.broadcasted_iota(jnp.int32, (tm, 1), 0)).astype(F32)
    for gi, w in enumerate(POOL_WINDOWS):
        lanes = slice(gi * POOL_GROUP_WIDTH, (gi + 1) * POOL_GROUP_WIDTH)
        acc = ue_ref[pl.ds(POOL_HALO, tm), lanes]
        for j in range(1, w):
            acc = acc + ue_ref[pl.ds(POOL_HALO - j, tm), lanes]
        mean = acc / jnp.minimum(float(w), pos1)
        z = jnp.dot((mean - u[:, lanes]).astype(BF16), wpool_ref[gi], preferred_element_type=F32)
        pooled_ref[0, :, lanes] = (z * pscale_ref[:, lanes]).astype(BF16)
    ue_ref[0:POOL_HALO, :] = u[tm - POOL_HALO:, :]

    @pl.when(i == n_tiles - 1)
    def _():
        ptail_ref[0] = u[tm - POOL_HALO:, :]


def _in_proj_prompt(x, mod, g1, w_qkvu, cos, sin, w_pool, pool_scale, tm):
    B, S, D = x.shape
    n_tiles = S // tm
    n_w = w_qkvu.shape[1]
    tails, kv_shapes, kv_specs = [], [], []
    for W in ATTN_WINDOWS:
        Wg = min(W, S)
        if Wg >= tm:
            first = n_tiles - Wg // tm
            rows = tm
        else:
            first = n_tiles - 1
            rows = Wg
        tails.append((first, rows))
        kv_shapes.append(jax.ShapeDtypeStruct((B, 2, GROUP_WIDTH, Wg), F32))
        kv_specs.append(pl.BlockSpec((1, 2, GROUP_WIDTH, rows),
                                     lambda b, i, first=first: (b, 0, 0, jnp.maximum(i - first, 0))))
    tok = lambda width: pl.BlockSpec((1, tm, width), lambda b, i: (b, i, 0))
    dil_specs = [pl.BlockSpec((1, tm // d, d * GROUP_WIDTH), lambda b, i: (b, i, 0)) for d in ATTN_DILATIONS]
    dil_shapes = [jax.ShapeDtypeStruct((B, S // d, d * GROUP_WIDTH), BF16) for d in ATTN_DILATIONS]
    outs = pl.pallas_call(
        functools.partial(_in_proj_prompt_kernel, tm=tm, n_tiles=n_tiles, tails=tuple(tails)),
        grid=(B, n_tiles),
        in_specs=[tok(D),
                  pl.BlockSpec((1, N_COND, 1, D), lambda b, i: (b, 0, 0, 0)),
                  _full((1, D)),
                  _full((D, n_w)),
                  pl.BlockSpec((tm, LANE), lambda b, i: (i, 0)),
                  pl.BlockSpec((tm, LANE), lambda b, i: (i, 0)),
                  _full(w_pool.shape),
                  _full((1, POOL_WIDTH))],
        out_specs=dil_specs * 3 + [tok(POOL_WIDTH)] + kv_specs
                  + [pl.BlockSpec((1, POOL_HALO, POOL_WIDTH), lambda b, i: (b, 0, 0))],
        out_shape=dil_shapes * 3
                  + [jax.ShapeDtypeStruct((B, S, POOL_WIDTH), BF16)] + kv_shapes
                  + [jax.ShapeDtypeStruct((B, POOL_HALO, POOL_WIDTH), F32)],
        scratch_shapes=[pltpu.VMEM((ATTN_WIDTH // LANE, tm, LANE), F32)] * 3
                       + [pltpu.VMEM((tm + POOL_HALO, POOL_WIDTH), F32)],
        compiler_params=_cparams(2),
        name="in_proj_prompt",
    )(x, mod, g1, w_qkvu, cos, sin, w_pool, pool_scale)
    return outs


def _in_proj_sample_kernel(x_ref, mod_ref, g1_ref, w_ref, cos_ref, sin_ref, q_ref, kvn_ref, u_ref):
    h = _norm_mod(x_ref[0], g1_ref[...], mod_ref[0, 1], mod_ref[0, 0]).astype(BF16)
    y = jnp.dot(h, w_ref[...], preferred_element_type=F32)
    cos = cos_ref[...]
    sin = sin_ref[...]
    q = _rope(y[:, :ATTN_WIDTH], cos, sin) * (HEAD_DIM ** -0.5)
    k = _rope(y[:, ATTN_WIDTH:2 * ATTN_WIDTH], cos, sin)
    v = y[:, 2 * ATTN_WIDTH:3 * ATTN_WIDTH]
    q_ref[...] = q.astype(BF16)
    for g in range(N_GROUPS):
        gl = slice(g * GROUP_WIDTH, (g + 1) * GROUP_WIDTH)
        kvn_ref[:, 2 * g * GROUP_WIDTH:(2 * g + 1) * GROUP_WIDTH] = k[:, gl]
        kvn_ref[:, (2 * g + 1) * GROUP_WIDTH:(2 * g + 2) * GROUP_WIDTH] = v[:, gl]
    u_ref[...] = y[:, 3 * ATTN_WIDTH:3 * ATTN_WIDTH + POOL_WIDTH]


def _in_proj_sample(x, mod, g1, w_qkvu, cos, sin):
    _, T, D = x.shape
    n_w = w_qkvu.shape[1]
    return pl.pallas_call(
        _in_proj_sample_kernel,
        grid=(1,),
        in_specs=[_full((1, T, D)), _full((1, N_COND, T, D)), _full((1, D)), _full((D, n_w)),
                  _full((T, LANE)), _full((T, LANE))],
        out_specs=[_full((T, ATTN_WIDTH)), _full((T, 2 * ATTN_WIDTH)), _full((T, POOL_WIDTH))],
        out_shape=[jax.ShapeDtypeStruct((T, ATTN_WIDTH), BF16),
                   jax.ShapeDtypeStruct((T, 2 * ATTN_WIDTH), F32),
                   jax.ShapeDtypeStruct((T, POOL_WIDTH), F32)],
        compiler_params=_cparams(1),
        name="in_proj_sample",
    )(x, mod, g1, w_qkvu, cos, sin)


def _attn_body(q_ref, kc_ref, kp_ref, vc_ref, vp_ref, o_ref, lse_ref, kbuf, vbuf, n0, nq):
    kbuf[0:SPAN, :] = kp_ref[0]
    kbuf[SPAN:, :] = kc_ref[0]
    vbuf[0:SPAN, :] = vp_ref[0]
    vbuf[SPAN:, :] = vc_ref[0]
    qi = lax.broadcasted_iota(jnp.int32, (SPAN, 2 * SPAN), 0)
    kj = lax.broadcasted_iota(jnp.int32, (SPAN, 2 * SPAN), 1)
    band = (kj >= qi) & (kj <= qi + SPAN)
    band_first = band & (kj >= jnp.where(n0 > 0, 0, SPAN))
    lane = lax.broadcasted_iota(jnp.int32, (SPAN, LANE), 1)
    low_head = lane < HEAD_DIM
    for j in range(nq):
        valid = band_first if j == 0 else band
        rows = slice(j * SPAN, (j + 1) * SPAN)
        krows = slice(j * SPAN, (j + 2) * SPAN)
        for c in range(GROUP_WIDTH // LANE):
            cl = slice(c * LANE, (c + 1) * LANE)
            q = q_ref[0, rows, cl]
            kk = kbuf[krows, cl]
            vv = vbuf[krows, cl]
            o_pair, lse_pair = [], []
            for hh in range(2):
                mask_h = low_head if hh == 0 else jnp.logical_not(low_head)
                qm = jnp.where(mask_h, q, jnp.zeros_like(q))
                s = lax.dot_general(qm, kk, (((1,), (1,)), ((), ())), preferred_element_type=F32)
                s = jnp.where(valid, s, NEG)
                m = jnp.max(s, axis=-1, keepdims=True)
                p = jnp.exp(s - m)
                den = jnp.sum(p, axis=-1, keepdims=True)
                o_pair.append(jnp.dot(p.astype(BF16), vv, preferred_element_type=F32) / den)
                lse_pair.append(m + jnp.log(den))
            o_ref[0, rows, cl] = jnp.where(low_head, o_pair[0], o_pair[1]).astype(BF16)
            lse_ref[0, rows, cl] = jnp.where(low_head, lse_pair[0], lse_pair[1])


def _sample_body(qbd_ref, kvn_ref, c0_ref, c1_ref, c2_ref, sp_ref, u_ref, wpool_ref, pscale_ref,
                 attn_ref, pooled_ref, o0_ref, o1_ref, o2_ref, po_ref, ue_ref, diff_ref, nt_ref, t_new):
    rows = HEADS_PER_GROUP * t_new
    caches = (c0_ref, c1_ref, c2_ref)
    outs = (o0_ref, o1_ref, o2_ref)

    s_cache, s_new, valid_new = [], [], []
    m = jnp.full((rows, 1), NEG, F32)
    for g in range(N_GROUPS):
        d = ATTN_DILATIONS[g]
        c_ref = caches[g]
        L = c_ref.shape[3]
        qb = qbd_ref[0, g]
        s = jnp.dot(qb, c_ref[0, 0].astype(BF16), preferred_element_type=F32)
        t_row = lax.broadcasted_iota(jnp.int32, (rows, L), 0) >> HEAD_SHIFT
        delta = L + t_row - lax.broadcasted_iota(jnp.int32, (rows, L), 1)
        ok = ((delta & (d - 1)) == 0) & (delta <= SPAN * d)
        s = jnp.where(ok, s, NEG)
        s_cache.append(s)
        m = jnp.maximum(m, jnp.max(s, axis=-1, keepdims=True))
        qf = qb.astype(F32)
        t_col = lax.broadcasted_iota(jnp.int32, (rows, 1), 0) >> HEAD_SHIFT
        sn, okn = [], []
        for tn in range(t_new):
            kn = kvn_ref[0, tn:tn + 1, 2 * g * GROUP_WIDTH:(2 * g + 1) * GROUP_WIDTH]
            kn = kn.astype(BF16).astype(F32)
            dn = t_col - tn
            ok_n = (dn >= 0) & ((dn & (d - 1)) == 0)
            s1 = jnp.where(ok_n, jnp.sum(qf * kn, axis=-1, keepdims=True), NEG)
            sn.append(s1)
            okn.append(ok_n)
            m = jnp.maximum(m, s1)
        s_new.append(sn)
        valid_new.append(okn)

    den = jnp.zeros((rows, 1), F32)
    acc = jnp.zeros((rows, GROUP_WIDTH), F32)
    for g in range(N_GROUPS):
        c_ref = caches[g]
        p = jnp.exp(s_cache[g] - m)
        den = den + jnp.sum(p, axis=-1, keepdims=True)
        acc = acc + lax.dot_general(p.astype(BF16), c_ref[0, 1].astype(BF16), (((1,), (1,)), ((), ())),
                                    preferred_element_type=F32)
        for tn in range(t_new):
            pn = jnp.exp(s_new[g][tn] - m)
            den = den + pn
            vn = kvn_ref[0, tn:tn + 1, (2 * g + 1) * GROUP_WIDTH:(2 * g + 2) * GROUP_WIDTH]
            acc = acc + pn * vn
    row_head = lax.broadcasted_iota(jnp.int32, (rows, GROUP_WIDTH), 0) & (HEADS_PER_GROUP - 1)
    lane_head = lax.broadcasted_iota(jnp.int32, (rows, GROUP_WIDTH), 1) >> LANE_HEAD_SHIFT
    o_diag = jnp.where(row_head == lane_head, acc / den, 0.0).astype(BF16)
    sel = ((lax.broadcasted_iota(jnp.int32, (rows, rows), 1) >> HEAD_SHIFT)
           == lax.broadcasted_iota(jnp.int32, (rows, rows), 0)).astype(BF16)
    attn = jnp.dot(sel, o_diag, preferred_element_type=F32)
    attn_ref[0] = attn[0:t_new].astype(BF16)

    tail_lane = lax.broadcasted_iota(jnp.int32, (GROUP_WIDTH, LANE), 1)
    nt_ref[...] = jnp.zeros(nt_ref.shape, F32)
    for g in range(N_GROUPS):
        c_ref, o_ref = caches[g], outs[g]
        L = c_ref.shape[3]
        for kv in range(2):
            col0 = (2 * g + kv) * GROUP_WIDTH
            nt_ref[LANE - t_new:LANE, :] = kvn_ref[0, :, col0:col0 + GROUP_WIDTH]
            new_t = nt_ref[...].T
            rolled = pltpu.roll(c_ref[0, kv], L - t_new, 1)
            if L > LANE:
                o_ref[0, kv, :, 0:L - LANE] = rolled[:, 0:L - LANE]
            o_ref[0, kv, :, L - LANE:L] = jnp.where(tail_lane >= LANE - t_new, new_t, rolled[:, L - LANE:L])

    ue_ref[0:POOL_STATE_LEN, :] = sp_ref[0]
    ue_ref[POOL_STATE_LEN:POOL_STATE_LEN + t_new, :] = u_ref[0]
    diff_ref[...] = jnp.zeros(diff_ref.shape, F32)
    for tn in range(t_new):
        r = POOL_STATE_LEN + tn
        for gi, w in enumerate(POOL_WINDOWS):
            lanes = slice(gi * POOL_GROUP_WIDTH, (gi + 1) * POOL_GROUP_WIDTH)
            win = jnp.sum(ue_ref[r - w + 1:r + 1, lanes], axis=0, keepdims=True)
            diff_ref[tn:tn + 1, lanes] = win / float(w) - ue_ref[r:r + 1, lanes]
    for gi in range(len(POOL_WINDOWS)):
        lanes = slice(gi * POOL_GROUP_WIDTH, (gi + 1) * POOL_GROUP_WIDTH)
        z = jnp.dot(diff_ref[:, lanes].astype(BF16), wpool_ref[gi], preferred_element_type=F32)
        pooled_ref[0, :, lanes] = (z[0:t_new] * pscale_ref[:, lanes]).astype(BF16)
    po_ref[0] = ue_ref[t_new:t_new + POOL_STATE_LEN, :]


N_ATTN_IN = 5
N_SAMPLE_IN = 9
N_SAMPLE_OUT = 6


def _attn_sample_kernel(*refs, plans, n_seq, t_new):
    n_att = N_GROUPS * N_ATTN_IN
    attn_in = refs[:n_att]
    sample_in = refs[n_att:n_att + N_SAMPLE_IN]
    outs = refs[n_att + N_SAMPLE_IN:]
    attn_out = outs[:2 * N_GROUPS]
    sample_out = outs[2 * N_GROUPS:2 * N_GROUPS + N_SAMPLE_OUT]
    scratch = outs[2 * N_GROUPS + N_SAMPLE_OUT:]
    kv_bufs, sample_scratch = scratch[:2 * N_GROUPS], scratch[2 * N_GROUPS:]
    s = pl.program_id(0)

    for g, (first, count, n_sup, d, nq) in enumerate(plans):
        @pl.when((s >= first) & (s < first + count))
        def _(g=g, first=first, n_sup=n_sup, d=d, nq=nq):
            n0 = ((s - first) // d) % n_sup
            _attn_body(*attn_in[g * N_ATTN_IN:(g + 1) * N_ATTN_IN], attn_out[2 * g], attn_out[2 * g + 1],
                       kv_bufs[2 * g], kv_bufs[2 * g + 1], n0, nq)

    @pl.when(s < n_seq)
    def _():
        _sample_body(*sample_in, *sample_out, *sample_scratch, t_new)


def _attn_and_sample(qs, ks, vs, qbd, kvn, caches, state_pool, u, w_pool, pool_scale, nq_max):
    N, t_new, _ = u.shape
    rows = HEADS_PER_GROUP * t_new
    plans, in_specs, out_specs, out_shape, scratch, args = [], [], [], [], [], []
    first = 0
    for g in range(N_GROUPS):
        B, M, width = qs[g].shape
        d = width // GROUP_WIDTH
        nq = min(nq_max, M // SPAN)
        n_sup = M // (nq * SPAN)
        count = B * n_sup * d
        plans.append((first, count, n_sup, d, nq))

        def block_index(s, first=first, count=count, n_sup=n_sup, d=d):
            local = jnp.clip(s - first, 0, count - 1)
            return local // (n_sup * d), (local // d) % n_sup, local % d

        def cur_map(s, block_index=block_index):
            b, n, r = block_index(s)
            return b, n, r

        def prev_map(s, block_index=block_index, nq=nq):
            b, n, r = block_index(s)
            return b, jnp.maximum(n * nq - 1, 0), r

        cur = pl.BlockSpec((1, nq * SPAN, GROUP_WIDTH), cur_map)
        prev = pl.BlockSpec((1, SPAN, GROUP_WIDTH), prev_map)
        in_specs += [cur, cur, prev, cur, prev]
        args += [qs[g], ks[g], ks[g], vs[g], vs[g]]
        out_specs += [cur, cur]
        out_shape += [jax.ShapeDtypeStruct((B, M, width), BF16), jax.ShapeDtypeStruct((B, M, width), F32)]
        scratch += [pltpu.VMEM(((nq + 1) * SPAN, GROUP_WIDTH), BF16)] * 2
        first += count
    n_steps = max(first, N)

    per_n = lambda shape: pl.BlockSpec((1,) + tuple(shape),
                                       lambda s: (jnp.minimum(s, N - 1),) + (0,) * len(shape))
    cache_specs = [per_n(c.shape[1:]) for c in caches]
    in_specs += ([per_n(qbd.shape[1:]), per_n(kvn.shape[1:])] + cache_specs
                 + [per_n(state_pool.shape[1:]), per_n(u.shape[1:]), _full(w_pool.shape), _full((1, POOL_WIDTH))])
    args += [qbd, kvn, *caches, state_pool, u, w_pool, pool_scale]
    out_specs += ([per_n((t_new, GROUP_WIDTH)), per_n((t_new, POOL_WIDTH))] + cache_specs
                  + [per_n(state_pool.shape[1:])])
    out_shape += ([jax.ShapeDtypeStruct((N, t_new, GROUP_WIDTH), BF16),
                   jax.ShapeDtypeStruct((N, t_new, POOL_WIDTH), BF16)]
                  + [jax.ShapeDtypeStruct(c.shape, F32) for c in caches]
                  + [jax.ShapeDtypeStruct(state_pool.shape, F32)])
    scratch += [pltpu.VMEM((POOL_STATE_LEN + t_new + 5, POOL_WIDTH), F32),
                pltpu.VMEM((rows, POOL_WIDTH), F32),
                pltpu.VMEM((LANE, GROUP_WIDTH), F32)]
    outs = pl.pallas_call(
        functools.partial(_attn_sample_kernel, plans=tuple(plans), n_seq=N, t_new=t_new),
        grid=(n_steps,),
        in_specs=in_specs,
        out_specs=out_specs,
        out_shape=out_shape,
        scratch_shapes=scratch,
        compiler_params=_cparams(1),
        name="attn_and_sample",
    )(*args)
    return [(outs[2 * g], outs[2 * g + 1]) for g in range(N_GROUPS)], outs[2 * N_GROUPS:]


def _route(logits):
    lane = lax.broadcasted_iota(jnp.int32, logits.shape, 1).astype(F32)
    is_grp = (lane >= GROUP_LANE0) & (lane < GROUP_LANE0 + MOE_GROUPS)
    gl = jnp.where(is_grp, logits, NEG)
    gmax = jnp.max(gl, axis=-1, keepdims=True)
    gidx = jnp.min(jnp.where(gl == gmax, lane, BIG_LANE), axis=-1, keepdims=True) - GROUP_LANE0
    gsum = jnp.sum(jnp.where(is_grp, jnp.exp(gl - gmax), 0.0), axis=-1, keepdims=True)
    grp_w = 1.0 / gsum
    lo = gidx * EXPERTS_PER_GROUP
    in_grp = (lane >= lo) & (lane < lo + EXPERTS_PER_GROUP)
    el = jnp.where(in_grp, logits, NEG)
    v1 = jnp.max(el, axis=-1, keepdims=True)
    i1 = jnp.min(jnp.where(el == v1, lane, BIG_LANE), axis=-1, keepdims=True)
    el2 = jnp.where(lane == i1, NEG, el)
    v2 = jnp.max(el2, axis=-1, keepdims=True)
    i2 = jnp.min(jnp.where(el2 == v2, lane, BIG_LANE), axis=-1, keepdims=True)
    e = jnp.exp(v2 - v1)
    w1 = grp_w / (1.0 + e)
    w2 = grp_w * e / (1.0 + e)
    return lane, i1, i2, w1, w2


def _merge_kernel(*refs, n_groups):
    x_ref, mod_ref, g1_ref, g2_ref = refs[0:4]
    n_attn = 2 * n_groups if n_groups > 1 else 1
    attn_refs = refs[4:4 + n_attn]
    rest = refs[4 + n_attn:]
    pooled_ref, wg_ref, wa_ref, wp_ref, wo_ref, wrh_ref, wrc_ref, br_ref = rest[:8]
    x1_ref, h2_ref, route_ref, counts_ref, route_t_ref = rest[8:13]
    scratch_refs = rest[13:]
    stage_refs = scratch_refs[:n_attn] if n_groups > 1 else ()
    carry_ref = scratch_refs[-1]

    x = x_ref[0]
    tm = x.shape[0]
    if n_groups > 1:
        vals = []
        for idx, ref in enumerate(attn_refs):
            d = ATTN_DILATIONS[idx % n_groups]
            if d == 1:
                vals.append(ref[0].astype(F32))
                continue
            stage_ref = stage_refs[idx]
            chunks = GROUP_WIDTH // LANE
            for r in range(d):
                for c in range(chunks):
                    stage_ref[c, pl.ds(r, tm // d, stride=d), :] = (
                        ref[0, :, r * GROUP_WIDTH + c * LANE:r * GROUP_WIDTH + (c + 1) * LANE].astype(F32))
            vals.append(jnp.concatenate([stage_ref[c] for c in range(chunks)], axis=1))
        os_, ls = vals[:n_groups], vals[n_groups:]
        lmax = functools.reduce(jnp.maximum, ls)
        es = [jnp.exp(l - lmax) for l in ls]
        attn = sum(e * o for e, o in zip(es, os_)) / sum(es)
    else:
        attn = attn_refs[0][0]
    a = jnp.dot(attn.astype(BF16), wa_ref[...], preferred_element_type=F32)
    p = jnp.dot(pooled_ref[0], wp_ref[...], preferred_element_type=F32)
    h = _norm_mod(x, g1_ref[...], mod_ref[0, 1], mod_ref[0, 0]).astype(BF16)
    gates = jnp.dot(h, wg_ref[...], preferred_element_type=F32)
    D = x.shape[1]
    merged = jax.nn.sigmoid(gates[:, :D]) * a + jax.nn.sigmoid(gates[:, D:]) * p
    y = jnp.dot(merged.astype(BF16), wo_ref[...], preferred_element_type=F32)
    x1 = x + mod_ref[0, 2] * y
    x1_ref[0] = x1
    h2 = _norm_mod(x1, g2_ref[...], mod_ref[0, 4], mod_ref[0, 3])
    h2_hi = h2.astype(BF16)
    h2_lo = (h2 - h2_hi.astype(F32)).astype(BF16)
    hi_terms = jnp.dot(h2_hi, wrc_ref[...], preferred_element_type=F32)
    logits = (hi_terms[:, :ROUTER_LANES] + jnp.dot(h2_lo, wrh_ref[...], preferred_element_type=F32)
              + hi_terms[:, ROUTER_LANES:]) + br_ref[...]
    lane, i1, i2, w1, w2 = _route(logits)
    h2_ref[0] = h2

    @pl.when((pl.program_id(0) == 0) & (pl.program_id(1) == 0))
    def _():
        carry_ref[...] = jnp.zeros(carry_ref.shape, F32)

    hit = ((lane == i1) | (lane == i2)).astype(BF16)
    ltri = (lax.broadcasted_iota(jnp.int32, (tm, tm), 0) >= lax.broadcasted_iota(jnp.int32, (tm, tm), 1))
    prefix = jnp.dot(ltri.astype(BF16), hit, preferred_element_type=F32) + carry_ref[...]
    rank1 = jnp.sum(jnp.where(lane == i1, prefix, 0.0), axis=-1, keepdims=True) - 1.0
    rank2 = jnp.sum(jnp.where(lane == i2, prefix, 0.0), axis=-1, keepdims=True) - 1.0
    carry_ref[...] = prefix[tm - 1:tm, :]
    counts_ref[...] = jnp.broadcast_to(prefix[tm - 1:tm, :], counts_ref.shape)
    cols = (i1, i2, w1, w2, rank1, rank2)
    route = jnp.zeros(logits.shape, F32)
    for c, col in enumerate(cols):
        route = jnp.where(lane == float(c), col, route)
    route_ref[0] = route
    route_t_ref[...] = route.T[0:SUBLANE, :]


def _merge(x, mod, g1, g2, attn_inputs, pooled, w_gates, w_attn_out, w_pool_out, w_o, wr_hi, wr_hilo, b_r, tm):
    B, S, D = x.shape
    R = mod.shape[2]
    n_groups = len(attn_inputs) // 2 if len(attn_inputs) > 1 else 1
    tok = lambda width: pl.BlockSpec((1, tm, width), lambda b, i: (b, i, 0))
    if R == 1:
        mod_spec = pl.BlockSpec((1, N_COND, 1, D), lambda b, i: (b, 0, 0, 0))
    else:
        mod_spec = pl.BlockSpec((1, N_COND, tm, D), lambda b, i: (b, 0, i, 0))
    weights = (w_gates, w_attn_out, w_pool_out, w_o, wr_hi, wr_hilo, b_r)
    if n_groups > 1:
        attn_specs = [pl.BlockSpec((1, tm // (a.shape[2] // GROUP_WIDTH), a.shape[2]), lambda b, i: (b, i, 0))
                      for a in attn_inputs]
        scratch = [pltpu.VMEM((GROUP_WIDTH // LANE, tm, LANE), F32)] * len(attn_inputs)
    else:
        attn_specs = [tok(GROUP_WIDTH)]
        scratch = []
    n_t = S // tm
    out_specs = [tok(D), tok(D), tok(ROUTER_LANES), _full((SUBLANE, ROUTER_LANES)),
                 pl.BlockSpec((SUBLANE, tm), lambda b, i: (0, b * n_t + i))]
    out_shape = [jax.ShapeDtypeStruct((B, S, D), F32),
                 jax.ShapeDtypeStruct((B, S, D), F32),
                 jax.ShapeDtypeStruct((B, S, ROUTER_LANES), F32),
                 jax.ShapeDtypeStruct((SUBLANE, ROUTER_LANES), F32),
                 jax.ShapeDtypeStruct((SUBLANE, B * S), F32)]
    scratch = scratch + [pltpu.VMEM((1, ROUTER_LANES), F32)]
    return pl.pallas_call(
        functools.partial(_merge_kernel, n_groups=n_groups),
        grid=(B, S // tm),
        in_specs=[tok(D), mod_spec, _full((1, D)), _full((1, D))]
                 + attn_specs + [tok(POOL_WIDTH)]
                 + [_full(w.shape) for w in weights],
        out_specs=out_specs,
        out_shape=out_shape,
        scratch_shapes=scratch,
        compiler_params=_cparams(2),
        name="merge",
    )(x, mod, g1, g2, *attn_inputs, pooled, *weights)


EXPERT_TILE = 512
PAD_CHUNK = 32


def _positions_kernel(base_ref, route_ref, pos_ref):
    cols = route_ref[...].astype(jnp.int32)
    start = jnp.zeros_like(cols)
    for e in range(N_EXPERTS):
        start = jnp.where(cols == e, base_ref[e], start)
    pos_ref[...] = start + pltpu.roll(cols, SUBLANE - 4, 0)


def _positions(route_t, base):
    rows, T = route_t.shape
    grid_spec = pltpu.PrefetchScalarGridSpec(
        num_scalar_prefetch=1, grid=(1,),
        in_specs=[pl.BlockSpec((rows, T), lambda i, base: (0, 0))],
        out_specs=pl.BlockSpec((rows, T), lambda i, base: (0, 0)))
    pos = pl.pallas_call(
        _positions_kernel,
        grid_spec=grid_spec,
        out_shape=jax.ShapeDtypeStruct((rows, T), jnp.int32),
        compiler_params=_cparams(1),
        name="moe_positions",
    )(base.astype(jnp.int32), route_t)
    return pos[0:2]


def _routing_tables(routes, counts, tile):
    cnts = [c[0, :N_EXPERTS].astype(jnp.int32) for c in counts]
    cnt = sum(cnts)
    padded = ((cnt + tile - 1) // tile) * tile
    ends = jnp.cumsum(padded)
    base = ends - padded
    poss = []
    for route in routes:
        poss.append(_positions(route, base))
        base = base + cnts[len(poss) - 1]
    n_pairs = 2 * sum(r.shape[1] for r in routes)
    n_tiles = -(-n_pairs // tile) + N_EXPERTS
    starts = jnp.arange(n_tiles, dtype=jnp.int32) * tile
    n_used = ends[-1] // tile
    tile_expert = jnp.sum(starts[:, None] >= ends[None, :], axis=1).astype(jnp.int32)
    last = jnp.take(tile_expert, n_used - 1)
    tile_expert = jnp.where(jnp.arange(n_tiles) < n_used, tile_expert, last)
    pads = jnp.stack([ends - padded + cnts[0], ends], axis=1).reshape(-1).astype(jnp.int32)
    return poss, tile_expert, n_used.reshape(1).astype(jnp.int32), pads, n_tiles


def _pos_steps(pos, tm):
    steps = pos.shape[1] // tm
    return jnp.transpose(pos.reshape(2, steps, tm), (1, 0, 2)).reshape(steps, 1, 2 * tm)


def _row_copy(src_ref, src_row, dst_ref, dst_row, sem):
    return pltpu.make_async_copy(src_ref.at[pl.ds(src_row, 1)], dst_ref.at[pl.ds(dst_row, 1)], sem)


def _dispatch_kernel(pads_ref, pos_ref, h_ref, *rest, tm, tile, first):
    xs_ref, zero_ref, sem = rest[-3:]

    def zero_fill():
        zero_ref[...] = jnp.zeros(zero_ref.shape, zero_ref.dtype)
        n_tiles = xs_ref.shape[0] // tile
        first_unused = pads_ref[2 * N_EXPERTS - 1] // tile

        def tile_copy(j):
            return pltpu.make_async_copy(zero_ref, xs_ref.at[pl.ds(pl.multiple_of(j * tile, tile), tile)], sem)

        def fill_tile(j, c):
            tile_copy(j).start()
            return c

        def drain_tile(j, c):
            tile_copy(j).wait()
            return c

        lax.fori_loop(first_unused, n_tiles, fill_tile, 0)
        lax.fori_loop(first_unused, n_tiles, drain_tile, 0)

        def chunk_copy(c):
            rows = pl.ds(pl.multiple_of(c * PAD_CHUNK, PAD_CHUNK), PAD_CHUNK)
            return pltpu.make_async_copy(zero_ref.at[pl.ds(0, PAD_CHUNK)], xs_ref.at[rows], sem)

        def for_each_pad_piece(on_row, on_chunk):
            def per_expert(e, carry):
                lo, hi = pads_ref[2 * e], pads_ref[2 * e + 1]
                first_chunk = (lo + PAD_CHUNK - 1) // PAD_CHUNK
                lax.fori_loop(lo, jnp.minimum(first_chunk * PAD_CHUNK, hi), on_row, 0)
                lax.fori_loop(first_chunk, hi // PAD_CHUNK, on_chunk, 0)
                return carry

            lax.fori_loop(0, N_EXPERTS, per_expert, 0)

        def start_row(r, c):
            _row_copy(zero_ref, 0, xs_ref, r, sem).start()
            return c

        def wait_row(r, c):
            _row_copy(zero_ref, 0, xs_ref, r, sem).wait()
            return c

        def start_chunk(c, carry):
            chunk_copy(c).start()
            return carry

        def wait_chunk(c, carry):
            chunk_copy(c).wait()
            return carry

        for_each_pad_piece(start_row, start_chunk)
        for_each_pad_piece(wait_row, wait_chunk)

    if first:
        pl.when(pl.program_id(0) == 0)(zero_fill)

    def issue(k, carry):
        base = pl.multiple_of(k * SUBLANE, SUBLANE)
        for j in range(SUBLANE):
            i = base + j
            _row_copy(h_ref, i, xs_ref, pos_ref[0, 0, i], sem).start()
            _row_copy(h_ref, i, xs_ref, pos_ref[0, 0, tm + i], sem).start()
        return carry

    lax.fori_loop(0, tm // SUBLANE, issue, 0)
    for _ in range(2):
        pltpu.make_async_copy(h_ref, xs_ref.at[pl.ds(0, tm)], sem).wait()


def _dispatch(h2, pos, pads, n_rows, tm, tile, xs_prev=None):
    T, width = h2.shape
    in_specs = [pl.BlockSpec((1, 1, 2 * tm), lambda i, pads: (i, 0, 0), memory_space=pltpu.SMEM),
                pl.BlockSpec((tm, width), lambda i, pads: (i, 0))]
    args = [pads, _pos_steps(pos, tm), h2]
    aliases = {}
    if xs_prev is not None:
        in_specs.append(pl.BlockSpec(memory_space=pl.ANY))
        args.append(xs_prev)
        aliases = {3: 0}
    grid_spec = pltpu.PrefetchScalarGridSpec(
        num_scalar_prefetch=1,
        grid=(T // tm,),
        in_specs=in_specs,
        out_specs=pl.BlockSpec(memory_space=pl.ANY),
        scratch_shapes=[pltpu.VMEM((tile, width), h2.dtype), pltpu.SemaphoreType.DMA(())],
    )
    return pl.pallas_call(
        functools.partial(_dispatch_kernel, tm=tm, tile=tile, first=xs_prev is None),
        grid_spec=grid_spec,
        out_shape=jax.ShapeDtypeStruct((n_rows, width), h2.dtype),
        input_output_aliases=aliases,
        compiler_params=_cparams(1),
        name="moe_dispatch",
    )(*args)


def _experts_kernel(te_ref, nu_ref, xs_ref, wgu_ref, wd_ref, ys_ref, wgu_bf_ref, wd_bf_ref, *, d_expert):
    j = pl.program_id(0)

    @pl.when(j >= nu_ref[0])
    def _():
        ys_ref[...] = jnp.zeros(ys_ref.shape, F32)

    @pl.when((j == 0) | (te_ref[j] != te_ref[jnp.maximum(j - 1, 0)]))
    def _():
        wgu_bf_ref[...] = wgu_ref[0].astype(BF16)
        wd_bf_ref[...] = wd_ref[0].astype(BF16)

    @pl.when(j < nu_ref[0])
    def _():
        gu = jnp.dot(xs_ref[...].astype(BF16), wgu_bf_ref[...], preferred_element_type=F32)
        act = _silu(gu[:, :d_expert]) * gu[:, d_expert:]
        ys_ref[...] = jnp.dot(act.astype(BF16), wd_bf_ref[...], preferred_element_type=F32)


def _experts(xs, tile_expert, n_used, w_gate_up, w_down, tile):
    n_rows, width = xs.shape
    _, D, two_f = w_gate_up.shape
    grid_spec = pltpu.PrefetchScalarGridSpec(
        num_scalar_prefetch=2,
        grid=(n_rows // tile,),
        in_specs=[pl.BlockSpec((tile, width), lambda j, te, nu: (j, 0)),
                  pl.BlockSpec((1, D, two_f), lambda j, te, nu: (te[j], 0, 0)),
                  pl.BlockSpec((1, two_f // 2, D), lambda j, te, nu: (te[j], 0, 0))],
        out_specs=pl.BlockSpec((tile, D), lambda j, te, nu: (j, 0)),
        scratch_shapes=[pltpu.VMEM((D, two_f), BF16), pltpu.VMEM((two_f // 2, D), BF16)],
    )
    return pl.pallas_call(
        functools.partial(_experts_kernel, d_expert=two_f // 2),
        grid_spec=grid_spec,
        out_shape=jax.ShapeDtypeStruct((n_rows, D), F32),
        compiler_params=_cparams(1),
        name="moe_experts",
    )(tile_expert, n_used, xs, w_gate_up, w_down)


def _combine_kernel(pos_ref, pos_next_ref, ys_ref, route_ref, x1_ref, mod_ref, gf_ref, y_ref,
                    ya_ref, yb_ref, sems, *, tm):
    step = pl.program_id(0) * pl.num_programs(1) + pl.program_id(1)
    n_steps = pl.num_programs(0) * pl.num_programs(1)
    slot = step % 2

    def start_gathers(p_ref, s):
        def issue(k, carry):
            base = pl.multiple_of(k * SUBLANE, SUBLANE)
            for j in range(SUBLANE):
                i = base + j
                _row_copy(ys_ref, p_ref[0, 0, i], ya_ref.at[s], i, sems.at[s]).start()
                _row_copy(ys_ref, p_ref[0, 0, tm + i], yb_ref.at[s], i, sems.at[s]).start()
            return carry

        lax.fori_loop(0, tm // SUBLANE, issue, 0)

    @pl.when(step == 0)
    def _():
        start_gathers(pos_ref, 0)

    @pl.when(step + 1 < n_steps)
    def _():
        start_gathers(pos_next_ref, 1 - slot)

    for buf in (ya_ref, yb_ref):
        pltpu.make_async_copy(ys_ref.at[pl.ds(0, tm)], buf.at[slot], sems.at[slot]).wait()
    route = route_ref[0]
    lane = lax.broadcasted_iota(jnp.int32, route.shape, 1)
    w1 = jnp.sum(jnp.where(lane == 2, route, 0.0), axis=-1, keepdims=True)
    w2 = jnp.sum(jnp.where(lane == 3, route, 0.0), axis=-1, keepdims=True)
    x2 = x1_ref[0] + mod_ref[0, 5] * (w1 * ya_ref[slot] + w2 * yb_ref[slot])
    var = jnp.mean(x2 * x2, axis=-1, keepdims=True)
    y_ref[0] = x2 * lax.rsqrt(var + EPS) * gf_ref[...]


def _combine(ys, pos, route, x1, mod, gf, tm):
    B, S, D = x1.shape
    n_t = S // tm
    tok = lambda width: pl.BlockSpec((1, tm, width), lambda b, i: (b, i, 0))
    last = B * n_t - 1
    pos_spec = lambda ahead: pl.BlockSpec(
        (1, 1, 2 * tm), lambda b, i: (jnp.minimum(b * n_t + i + ahead, last), 0, 0), memory_space=pltpu.SMEM)
    pos_steps = _pos_steps(pos, tm)
    if mod.shape[2] == 1:
        mod_spec = pl.BlockSpec((1, N_COND, 1, D), lambda b, i: (b, 0, 0, 0))
    else:
        mod_spec = pl.BlockSpec((1, N_COND, tm, D), lambda b, i: (b, 0, i, 0))
    return pl.pallas_call(
        functools.partial(_combine_kernel, tm=tm),
        grid=(B, n_t),
        in_specs=[pos_spec(0), pos_spec(1),
                  pl.BlockSpec(memory_space=pl.ANY),
                  tok(ROUTER_LANES), tok(D),
                  mod_spec,
                  _full((1, D))],
        out_specs=tok(D),
        out_shape=jax.ShapeDtypeStruct((B, S, D), F32),
        scratch_shapes=[pltpu.VMEM((2, tm, D), F32), pltpu.VMEM((2, tm, D), F32),
                        pltpu.SemaphoreType.DMA((2,))],
        compiler_params=_cparams(2),
        name="moe_combine",
    )(pos_steps, pos_steps, ys, route, x1, mod, gf)


def _rope_tables(pos):
    half = HEAD_DIM // 2
    inv = ROPE_THETA ** (-jnp.arange(half, dtype=F32) * 2.0 / HEAD_DIM)
    ang = pos.astype(F32)[:, None] * inv[None, :]
    cos, sin = jnp.cos(ang), jnp.sin(ang)
    reps = LANE // HEAD_DIM
    return (jnp.tile(jnp.concatenate([cos, cos], axis=-1), (1, reps)),
            jnp.tile(jnp.concatenate([-sin, sin], axis=-1), (1, reps)))


def kernel(x_prompt, x_sample, cache_kv_w128, cache_kv_w512, cache_kv_w2048, state_pool, c_prompt, c_sample, norm1_g, w_ada, b_ada, w_in, w_attn_out, w_pool, pool_scale, w_pool_out, w_o, norm2_g, w_grp, b_grp, w_exp_router, b_exp_router, w_gate_up, w_down, final_norm_g):
    B, S, D = x_prompt.shape
    N, T, _ = x_sample.shape
    depth = norm1_g.shape[0]
    assert depth == 1, "single trunk layer"
    tm = min(TOKEN_TILE, S)
    assert S % tm == 0 and all(S % (SPAN * d) == 0 for d in ATTN_DILATIONS)

    n_qkvu = 3 * ATTN_WIDTH + POOL_WIDTH
    w_qkvu = w_in[0, :, :n_qkvu].astype(BF16)
    w_gates = w_in[0, :, n_qkvu:].astype(BF16)
    wa, wpo, wo = w_attn_out[0].astype(BF16), w_pool_out[0].astype(BF16), w_o[0].astype(BF16)
    wpool = w_pool[0].astype(BF16)
    pscale = pool_scale[0].reshape(1, POOL_WIDTH)
    w_r = jnp.concatenate([w_exp_router[0], w_grp[0]], axis=1)
    w_r = jnp.pad(w_r, ((0, 0), (0, ROUTER_LANES - w_r.shape[1])))
    wr_hi = w_r.astype(BF16)
    wr_hilo = jnp.concatenate([wr_hi, (w_r - wr_hi.astype(F32)).astype(BF16)], axis=1)
    b_r = jnp.pad(jnp.concatenate([b_exp_router[0], b_grp[0]]), (0, ROUTER_LANES - N_EXPERTS - MOE_GROUPS))
    b_r = b_r.reshape(1, ROUTER_LANES)
    wgu, wd = w_gate_up[0], w_down[0]
    g1, g2, gf = norm1_g[0].reshape(1, D), norm2_g[0].reshape(1, D), final_norm_g.reshape(1, D)

    c_rows = jnp.concatenate([jnp.repeat(c_sample, T, axis=0), c_prompt], axis=0)
    mod = _ada(c_rows, w_ada[0], b_ada[0])
    mod_p = jnp.transpose(mod[:, N * T:], (1, 0, 2)).reshape(B, N_COND, 1, D)
    mod_s = mod[None]

    cos_p, sin_p = _rope_tables(jnp.arange(S, dtype=jnp.int32))
    outs = _in_proj_prompt(x_prompt, mod_p, g1, w_qkvu, cos_p, sin_p, wpool, pscale, tm)
    qs, ks, vs = outs[0:3], outs[3:6], outs[6:9]
    pooled, kv0, kv1, kv2, ptail = outs[9:14]
    pool_prompt = ptail[:, POOL_HALO - POOL_STATE_LEN:, :][None]

    TS = N * T
    pos_s = PAST_LEN + jnp.arange(T, dtype=jnp.int32)
    cos_s, sin_s = _rope_tables(pos_s)
    cos_s, sin_s = jnp.tile(cos_s, (N, 1)), jnp.tile(sin_s, (N, 1))
    xs = x_sample.reshape(1, TS, D)
    q_s, kvn, u_s = _in_proj_sample(xs, mod_s, g1, w_qkvu, cos_s, sin_s)
    eye = jnp.eye(HEADS_PER_GROUP, dtype=BF16)
    qbd = jnp.einsum('ntghe,hk->ngthke', q_s.reshape(N, T, N_GROUPS, HEADS_PER_GROUP, HEAD_DIM), eye)
    qbd = qbd.reshape(N, N_GROUPS, T * HEADS_PER_GROUP, GROUP_WIDTH)
    caches = [jnp.transpose(c[0], (0, 2, 3, 4, 1)).reshape(N, 2, GROUP_WIDTH, c.shape[2])
              for c in (cache_kv_w128, cache_kv_w512, cache_kv_w2048)]

    attn_parts, (attn_s, pooled_s, ko0, ko1, ko2, pool_s) = _attn_and_sample(
        qs, ks, vs, qbd, kvn.reshape(N, T, 2 * ATTN_WIDTH), caches, state_pool[0],
        u_s.reshape(N, T, POOL_WIDTH), wpool, pscale, nq_max=ATTN_BLOCKS_PER_STEP)
    attn_inputs = [o for o, _ in attn_parts] + [l for _, l in attn_parts]
    x1, h2, route, counts, route_t = _merge(x_prompt, mod_p, g1, g2, attn_inputs, pooled, w_gates, wa, wpo, wo,
                                            wr_hi, wr_hilo, b_r, tm)
    x1s, h2s, route_s, counts_s, route_ts = _merge(xs, mod_s, g1, g2, [attn_s.reshape(1, TS, GROUP_WIDTH)],
                                                   pooled_s.reshape(1, TS, POOL_WIDTH), w_gates, wa, wpo, wo,
                                                   wr_hi, wr_hilo, b_r, TS)

    (pos, pos_s), tile_expert, n_used, pads, n_tiles = _routing_tables(
        [route_t, route_ts], [counts, counts_s], EXPERT_TILE)
    sorted_rows = _dispatch(h2.reshape(B * S, D), pos, pads, n_tiles * EXPERT_TILE, tm, EXPERT_TILE)
    sorted_rows = _dispatch(h2s.reshape(TS, D), pos_s, pads, n_tiles * EXPERT_TILE, TS, EXPERT_TILE,
                            xs_prev=sorted_rows)
    ys = _experts(sorted_rows, tile_expert, n_used, wgu, wd, EXPERT_TILE)
    y_prompt = _combine(ys, pos, route, x1, mod_p, gf, tm=tm)
    y_sample = _combine(ys, pos_s, route_s, x1s, mod_s, gf, tm=TS).reshape(N, T, D)

    def kv_shape(a):
        a = a.reshape(a.shape[0], 2, HEADS_PER_GROUP, HEAD_DIM, a.shape[3])
        return jnp.transpose(a, (0, 4, 1, 2, 3))[None]

    return (y_prompt, y_sample, kv_shape(kv0), kv_shape(kv1), kv_shape(kv2), pool_prompt,
            kv_shape(ko0), kv_shape(ko1), kv_shape(ko2), pool_s[None])
```

```python
import functools

import jax
import jax.numpy as jnp
from jax import lax
from jax.experimental import pallas as pl
from jax.experimental.pallas import tpu as pltpu

F32 = jnp.float32
BF16 = jnp.bfloat16

HEAD_DIM = 64
HEADS_PER_GROUP = 4
HEAD_SHIFT = 2
LANE_HEAD_SHIFT = 6
GROUP_WIDTH = HEADS_PER_GROUP * HEAD_DIM
ATTN_WINDOWS = (128, 512, 2048)
ATTN_DILATIONS = (1, 4, 16)
N_GROUPS = 3
SPAN = 128
ATTN_WIDTH = N_GROUPS * GROUP_WIDTH
ROPE_THETA = 10000.0
PAST_LEN = 8192
POOL_WINDOWS = (2, 4, 8, 16)
POOL_GROUP_WIDTH = 128
POOL_WIDTH = 512
POOL_STATE_LEN = 15
POOL_HALO = 16
MOE_GROUPS = 4
EXPERTS_PER_GROUP = 8
N_EXPERTS = 32
N_COND = 6
EPS = 1e-6

LANE = 128
SUBLANE = 8
VMEM_LIMIT_BYTES = 56 * 1024 * 1024

TOKEN_TILE = 512
ATTN_BLOCKS_PER_STEP = 8

NEG = -1e30
BIG_LANE = 1e9

ROUTER_LANES = LANE
GROUP_LANE0 = N_EXPERTS


def _cparams(n_axes):
    return pltpu.CompilerParams(dimension_semantics=("arbitrary",) * n_axes,
                                vmem_limit_bytes=VMEM_LIMIT_BYTES)


def _full(shape):
    nd = len(shape)
    return pl.BlockSpec(tuple(shape), lambda *_: (0,) * nd)


def _norm_mod(x, g, scale, shift):
    var = jnp.mean(x * x, axis=-1, keepdims=True)
    return (x * lax.rsqrt(var + EPS) * g) * (1.0 + scale) + shift


def _rope(x, cos, sin):
    lane = lax.broadcasted_iota(jnp.int32, (x.shape[0], LANE), 1)
    first_half = (lane & (HEAD_DIM - 1)) < (HEAD_DIM // 2)
    outs = []
    for c in range(x.shape[1] // LANE):
        xc = x[:, c * LANE:(c + 1) * LANE]
        partner = jnp.where(first_half, pltpu.roll(xc, LANE - HEAD_DIM // 2, 1),
                            pltpu.roll(xc, HEAD_DIM // 2, 1))
        outs.append(xc * cos + partner * sin)
    return jnp.concatenate(outs, axis=1)


def _silu(x):
    return x * jax.nn.sigmoid(x)


def _ada_kernel(c_ref, w_ref, b_ref, o_ref):
    s = _silu(c_ref[...]).astype(BF16)
    o_ref[0] = jnp.dot(s, w_ref[...].astype(BF16), preferred_element_type=F32) + b_ref[...]


def _ada(c_all, w_ada, b_ada):
    rows, d = c_all.shape
    n_out = w_ada.shape[1]
    tn = n_out // N_COND
    return pl.pallas_call(
        _ada_kernel,
        grid=(N_COND,),
        in_specs=[_full((rows, d)),
                  pl.BlockSpec((d, tn), lambda j: (0, j)),
                  pl.BlockSpec((1, tn), lambda j: (0, j))],
        out_specs=pl.BlockSpec((1, rows, tn), lambda j: (j, 0, 0)),
        out_shape=jax.ShapeDtypeStruct((N_COND, rows, tn), F32),
        compiler_params=_cparams(1),
        name="ada",
    )(c_all, w_ada, b_ada.reshape(1, n_out))


def _in_proj_prompt_kernel(x_ref, mod_ref, g1_ref, w_ref, cos_ref, sin_ref, wpool_ref, pscale_ref, *rest,
                           tm, n_tiles, tails):
    qkv_refs = (rest[0:3], rest[3:6], rest[6:9])
    pooled_ref, kv0_ref, kv1_ref, kv2_ref, ptail_ref = rest[9:14]
    stage_refs = rest[14:17]
    ue_ref = rest[17]
    i = pl.program_id(1)
    x = x_ref[0]
    h = _norm_mod(x, g1_ref[...], mod_ref[0, 1], mod_ref[0, 0]).astype(BF16)
    cos = cos_ref[...]
    sin = sin_ref[...]
    y = jnp.dot(h, w_ref[...], preferred_element_type=F32)
    q = _rope(y[:, :ATTN_WIDTH], cos, sin) * (HEAD_DIM ** -0.5)
    k = _rope(y[:, ATTN_WIDTH:2 * ATTN_WIDTH], cos, sin)
    v = y[:, 2 * ATTN_WIDTH:3 * ATTN_WIDTH]
    u = y[:, 3 * ATTN_WIDTH:3 * ATTN_WIDTH + POOL_WIDTH]

    chunks = GROUP_WIDTH // LANE
    for val, out_refs, stage_ref in zip((q, k, v), qkv_refs, stage_refs):
        for g, d in enumerate(ATTN_DILATIONS):
            gl = slice(g * GROUP_WIDTH, (g + 1) * GROUP_WIDTH)
            if d == 1:
                out_refs[g][0] = val[:, gl].astype(BF16)
                continue
            for c in range(chunks):
                stage_ref[g * chunks + c] = val[:, g * GROUP_WIDTH + c * LANE:g * GROUP_WIDTH + (c + 1) * LANE]
            for r in range(d):
                for c in range(chunks):
                    out_refs[g][0, :, r * GROUP_WIDTH + c * LANE:r * GROUP_WIDTH + (c + 1) * LANE] = (
                        stage_ref[g * chunks + c, pl.ds(r, tm // d, stride=d), :].astype(BF16))

    for g, kv_ref in enumerate((kv0_ref, kv1_ref, kv2_ref)):
        first_tile, rows = tails[g]
        lo = tm - rows

        @pl.when(i >= first_tile)
        def _(kv_ref=kv_ref, g=g, lo=lo):
            kv_ref[0, 0] = k[lo:, g * GROUP_WIDTH:(g + 1) * GROUP_WIDTH].T
            kv_ref[0, 1] = v[lo:, g * GROUP_WIDTH:(g + 1) * GROUP_WIDTH].T

    @pl.when(i == 0)
    def _():
        ue_ref[0:POOL_HALO, :] = jnp.zeros((POOL_HALO, POOL_WIDTH), F32)

    ue_ref[POOL_HALO:, :] = u
    pos1 = (i * tm + 1 + lax.broadcasted_iota(jnp.int32, (tm, 1), 0)).astype(F32)
    for gi, w in enumerate(POOL_WINDOWS):
        lanes = slice(gi * POOL_GROUP_WIDTH, (gi + 1) * POOL_GROUP_WIDTH)
        acc = ue_ref[pl.ds(POOL_HALO, tm), lanes]
        for j in range(1, w):
            acc = acc + ue_ref[pl.ds(POOL_HALO - j, tm), lanes]
        mean = acc / jnp.minimum(float(w), pos1)
        z = jnp.dot((mean - u[:, lanes]).astype(BF16), wpool_ref[gi], preferred_element_type=F32)
        pooled_ref[0, :, lanes] = (z * pscale_ref[:, lanes]).astype(BF16)
    ue_ref[0:POOL_HALO, :] = u[tm - POOL_HALO:, :]

    @pl.when(i == n_tiles - 1)
    def _():
        ptail_ref[0] = u[tm - POOL_HALO:, :]


def _in_proj_prompt(x, mod, g1, w_qkvu, cos, sin, w_pool, pool_scale, tm):
    B, S, D = x.shape
    n_tiles = S // tm
    n_w = w_qkvu.shape[1]
    tails, kv_shapes, kv_specs = [], [], []
    for W in ATTN_WINDOWS:
        Wg = min(W, S)
        if Wg >= tm:
            first = n_tiles - Wg // tm
            rows = tm
        else:
            first = n_tiles - 1
            rows = Wg
        tails.append((first, rows))
        kv_shapes.append(jax.ShapeDtypeStruct((B, 2, GROUP_WIDTH, Wg), F32))
        kv_specs.append(pl.BlockSpec((1, 2, GROUP_WIDTH, rows),
                                     lambda b, i, first=first: (b, 0, 0, jnp.maximum(i - first, 0))))
    tok = lambda width: pl.BlockSpec((1, tm, width), lambda b, i: (b, i, 0))
    dil_specs = [pl.BlockSpec((1, tm // d, d * GROUP_WIDTH), lambda b, i: (b, i, 0)) for d in ATTN_DILATIONS]
    dil_shapes = [jax.ShapeDtypeStruct((B, S // d, d * GROUP_WIDTH), BF16) for d in ATTN_DILATIONS]
    outs = pl.pallas_call(
        functools.partial(_in_proj_prompt_kernel, tm=tm, n_tiles=n_tiles, tails=tuple(tails)),
        grid=(B, n_tiles),
        in_specs=[tok(D),
                  pl.BlockSpec((1, N_COND, 1, D), lambda b, i: (b, 0, 0, 0)),
                  _full((1, D)),
                  _full((D, n_w)),
                  pl.BlockSpec((tm, LANE), lambda b, i: (i, 0)),
                  pl.BlockSpec((tm, LANE), lambda b, i: (i, 0)),
                  _full(w_pool.shape),
                  _full((1, POOL_WIDTH))],
        out_specs=dil_specs * 3 + [tok(POOL_WIDTH)] + kv_specs
                  + [pl.BlockSpec((1, POOL_HALO, POOL_WIDTH), lambda b, i: (b, 0, 0))],
        out_shape=dil_shapes * 3
                  + [jax.ShapeDtypeStruct((B, S, POOL_WIDTH), BF16)] + kv_shapes
                  + [jax.ShapeDtypeStruct((B, POOL_HALO, POOL_WIDTH), F32)],
        scratch_shapes=[pltpu.VMEM((ATTN_WIDTH // LANE, tm, LANE), F32)] * 3
                       + [pltpu.VMEM((tm + POOL_HALO, POOL_WIDTH), F32)],
        compiler_params=_cparams(2),
        name="in_proj_prompt",
    )(x, mod, g1, w_qkvu, cos, sin, w_pool, pool_scale)
    return outs


def _in_proj_sample_kernel(x_ref, mod_ref, g1_ref, w_ref, cos_ref, sin_ref, q_ref, kvn_ref, u_ref):
    h = _norm_mod(x_ref[0], g1_ref[...], mod_ref[0, 1], mod_ref[0, 0]).astype(BF16)
    y = jnp.dot(h, w_ref[...], preferred_element_type=F32)
    cos = cos_ref[...]
    sin = sin_ref[...]
    q = _rope(y[:, :ATTN_WIDTH], cos, sin) * (HEAD_DIM ** -0.5)
    k = _rope(y[:, ATTN_WIDTH:2 * ATTN_WIDTH], cos, sin)
    v = y[:, 2 * ATTN_WIDTH:3 * ATTN_WIDTH]
    q_ref[...] = q.astype(BF16)
    for g in range(N_GROUPS):
        gl = slice(g * GROUP_WIDTH, (g + 1) * GROUP_WIDTH)
        kvn_ref[:, 2 * g * GROUP_WIDTH:(2 * g + 1) * GROUP_WIDTH] = k[:, gl]
        kvn_ref[:, (2 * g + 1) * GROUP_WIDTH:(2 * g + 2) * GROUP_WIDTH] = v[:, gl]
    u_ref[...] = y[:, 3 * ATTN_WIDTH:3 * ATTN_WIDTH + POOL_WIDTH]


def _in_proj_sample(x, mod, g1, w_qkvu, cos, sin):
    _, T, D = x.shape
    n_w = w_qkvu.shape[1]
    return pl.pallas_call(
        _in_proj_sample_kernel,
        grid=(1,),
        in_specs=[_full((1, T, D)), _full((1, N_COND, T, D)), _full((1, D)), _full((D, n_w)),
                  _full((T, LANE)), _full((T, LANE))],
        out_specs=[_full((T, ATTN_WIDTH)), _full((T, 2 * ATTN_WIDTH)), _full((T, POOL_WIDTH))],
        out_shape=[jax.ShapeDtypeStruct((T, ATTN_WIDTH), BF16),
                   jax.ShapeDtypeStruct((T, 2 * ATTN_WIDTH), F32),
                   jax.ShapeDtypeStruct((T, POOL_WIDTH), F32)],
        compiler_params=_cparams(1),
        name="in_proj_sample",
    )(x, mod, g1, w_qkvu, cos, sin)


def _attn_body(q_ref, kc_ref, kp_ref, vc_ref, vp_ref, o_ref, lse_ref, kbuf, vbuf, n0, nq):
    kbuf[0:SPAN, :] = kp_ref[0]
    kbuf[SPAN:, :] = kc_ref[0]
    vbuf[0:SPAN, :] = vp_ref[0]
    vbuf[SPAN:, :] = vc_ref[0]
    qi = lax.broadcasted_iota(jnp.int32, (SPAN, 2 * SPAN), 0)
    kj = lax.broadcasted_iota(jnp.int32, (SPAN, 2 * SPAN), 1)
    band = (kj >= qi) & (kj <= qi + SPAN)
    band_first = band & (kj >= jnp.where(n0 > 0, 0, SPAN))
    lane = lax.broadcasted_iota(jnp.int32, (SPAN, LANE), 1)
    low_head = lane < HEAD_DIM
    for j in range(nq):
        valid = band_first if j == 0 else band
        rows = slice(j * SPAN, (j + 1) * SPAN)
        krows = slice(j * SPAN, (j + 2) * SPAN)
        for c in range(GROUP_WIDTH // LANE):
            cl = slice(c * LANE, (c + 1) * LANE)
            q = q_ref[0, rows, cl]
            kk = kbuf[krows, cl]
            vv = vbuf[krows, cl]
            o_pair, lse_pair = [], []
            for hh in range(2):
                mask_h = low_head if hh == 0 else jnp.logical_not(low_head)
                qm = jnp.where(mask_h, q, jnp.zeros_like(q))
                s = lax.dot_general(qm, kk, (((1,), (1,)), ((), ())), preferred_element_type=F32)
                s = jnp.where(valid, s, NEG)
                m = jnp.max(s, axis=-1, keepdims=True)
                p = jnp.exp(s - m)
                den = jnp.sum(p, axis=-1, keepdims=True)
                o_pair.append(jnp.dot(p.astype(BF16), vv, preferred_element_type=F32) / den)
                lse_pair.append(m + jnp.log(den))
            o_ref[0, rows, cl] = jnp.where(low_head, o_pair[0], o_pair[1]).astype(BF16)
            lse_ref[0, rows, cl] = jnp.where(low_head, lse_pair[0], lse_pair[1])


def _sample_body(qbd_ref, kvn_ref, c0_ref, c1_ref, c2_ref, sp_ref, u_ref, wpool_ref, pscale_ref,
                 attn_ref, pooled_ref, o0_ref, o1_ref, o2_ref, po_ref, ue_ref, diff_ref, nt_ref, t_new):
    rows = HEADS_PER_GROUP * t_new
    caches = (c0_ref, c1_ref, c2_ref)
    outs = (o0_ref, o1_ref, o2_ref)

    s_cache, s_new, valid_new = [], [], []
    m = jnp.full((rows, 1), NEG, F32)
    for g in range(N_GROUPS):
        d = ATTN_DILATIONS[g]
        c_ref = caches[g]
        L = c_ref.shape[3]
        qb = qbd_ref[0, g]
        s = jnp.dot(qb, c_ref[0, 0].astype(BF16), preferred_element_type=F32)
        t_row = lax.broadcasted_iota(jnp.int32, (rows, L), 0) >> HEAD_SHIFT
        delta = L + t_row - lax.broadcasted_iota(jnp.int32, (rows, L), 1)
        ok = ((delta & (d - 1)) == 0) & (delta <= SPAN * d)
        s = jnp.where(ok, s, NEG)
        s_cache.append(s)
        m = jnp.maximum(m, jnp.max(s, axis=-1, keepdims=True))
        qf = qb.astype(F32)
        t_col = lax.broadcasted_iota(jnp.int32, (rows, 1), 0) >> HEAD_SHIFT
        sn, okn = [], []
        for tn in range(t_new):
            kn = kvn_ref[0, tn:tn + 1, 2 * g * GROUP_WIDTH:(2 * g + 1) * GROUP_WIDTH]
            kn = kn.astype(BF16).astype(F32)
            dn = t_col - tn
            ok_n = (dn >= 0) & ((dn & (d - 1)) == 0)
            s1 = jnp.where(ok_n, jnp.sum(qf * kn, axis=-1, keepdims=True), NEG)
            sn.append(s1)
            okn.append(ok_n)
            m = jnp.maximum(m, s1)
        s_new.append(sn)
        valid_new.append(okn)

    den = jnp.zeros((rows, 1), F32)
    acc = jnp.zeros((rows, GROUP_WIDTH), F32)
    for g in range(N_GROUPS):
        c_ref = caches[g]
        p = jnp.exp(s_cache[g] - m)
        den = den + jnp.sum(p, axis=-1, keepdims=True)
        acc = acc + lax.dot_general(p.astype(BF16), c_ref[0, 1].astype(BF16), (((1,), (1,)), ((), ())),
                                    preferred_element_type=F32)
        for tn in range(t_new):
            pn = jnp.exp(s_new[g][tn] - m)
            den = den + pn
            vn = kvn_ref[0, tn:tn + 1, (2 * g + 1) * GROUP_WIDTH:(2 * g + 2) * GROUP_WIDTH]
            acc = acc + pn * vn
    row_head = lax.broadcasted_iota(jnp.int32, (rows, GROUP_WIDTH), 0) & (HEADS_PER_GROUP - 1)
    lane_head = lax.broadcasted_iota(jnp.int32, (rows, GROUP_WIDTH), 1) >> LANE_HEAD_SHIFT
    o_diag = jnp.where(row_head == lane_head, acc / den, 0.0).astype(BF16)
    sel = ((lax.broadcasted_iota(jnp.int32, (rows, rows), 1) >> HEAD_SHIFT)
           == lax.broadcasted_iota(jnp.int32, (rows, rows), 0)).astype(BF16)
    attn = jnp.dot(sel, o_diag, preferred_element_type=F32)
    attn_ref[0] = attn[0:t_new].astype(BF16)

    tail_lane = lax.broadcasted_iota(jnp.int32, (GROUP_WIDTH, LANE), 1)
    nt_ref[...] = jnp.zeros(nt_ref.shape, F32)
    for g in range(N_GROUPS):
        c_ref, o_ref = caches[g], outs[g]
        L = c_ref.shape[3]
        for kv in range(2):
            col0 = (2 * g + kv) * GROUP_WIDTH
            nt_ref[LANE - t_new:LANE, :] = kvn_ref[0, :, col0:col0 + GROUP_WIDTH]
            new_t = nt_ref[...].T
            rolled = pltpu.roll(c_ref[0, kv], L - t_new, 1)
            if L > LANE:
                o_ref[0, kv, :, 0:L - LANE] = rolled[:, 0:L - LANE]
            o_ref[0, kv, :, L - LANE:L] = jnp.where(tail_lane >= LANE - t_new, new_t, rolled[:, L - LANE:L])

    ue_ref[0:POOL_STATE_LEN, :] = sp_ref[0]
    ue_ref[POOL_STATE_LEN:POOL_STATE_LEN + t_new, :] = u_ref[0]
    diff_ref[...] = jnp.zeros(diff_ref.shape, F32)
    for tn in range(t_new):
        r = POOL_STATE_LEN + tn
        for gi, w in enumerate(POOL_WINDOWS):
            lanes = slice(gi * POOL_GROUP_WIDTH, (gi + 1) * POOL_GROUP_WIDTH)
            win = jnp.sum(ue_ref[r - w + 1:r + 1, lanes], axis=0, keepdims=True)
            diff_ref[tn:tn + 1, lanes] = win / float(w) - ue_ref[r:r + 1, lanes]
    for gi in range(len(POOL_WINDOWS)):
        lanes = slice(gi * POOL_GROUP_WIDTH, (gi + 1) * POOL_GROUP_WIDTH)
        z = jnp.dot(diff_ref[:, lanes].astype(BF16), wpool_ref[gi], preferred_element_type=F32)
        pooled_ref[0, :, lanes] = (z[0:t_new] * pscale_ref[:, lanes]).astype(BF16)
    po_ref[0] = ue_ref[t_new:t_new + POOL_STATE_LEN, :]


N_ATTN_IN = 5
N_SAMPLE_IN = 9
N_SAMPLE_OUT = 6


def _attn_sample_kernel(*refs, plans, n_seq, t_new):
    n_att = N_GROUPS * N_ATTN_IN
    attn_in = refs[:n_att]
    sample_in = refs[n_att:n_att + N_SAMPLE_IN]
    outs = refs[n_att + N_SAMPLE_IN:]
    attn_out = outs[:2 * N_GROUPS]
    sample_out = outs[2 * N_GROUPS:2 * N_GROUPS + N_SAMPLE_OUT]
    scratch = outs[2 * N_GROUPS + N_SAMPLE_OUT:]
    kv_bufs, sample_scratch = scratch[:2 * N_GROUPS], scratch[2 * N_GROUPS:]
    s = pl.program_id(0)

    for g, (first, count, n_sup, d, nq) in enumerate(plans):
        @pl.when((s >= first) & (s < first + count))
        def _(g=g, first=first, n_sup=n_sup, d=d, nq=nq):
            n0 = ((s - first) // d) % n_sup
            _attn_body(*attn_in[g * N_ATTN_IN:(g + 1) * N_ATTN_IN], attn_out[2 * g], attn_out[2 * g + 1],
                       kv_bufs[2 * g], kv_bufs[2 * g + 1], n0, nq)

    @pl.when(s < n_seq)
    def _():
        _sample_body(*sample_in, *sample_out, *sample_scratch, t_new)


def _attn_and_sample(qs, ks, vs, qbd, kvn, caches, state_pool, u, w_pool, pool_scale, nq_max):
    N, t_new, _ = u.shape
    rows = HEADS_PER_GROUP * t_new
    plans, in_specs, out_specs, out_shape, scratch, args = [], [], [], [], [], []
    first = 0
    for g in range(N_GROUPS):
        B, M, width = qs[g].shape
        d = width // GROUP_WIDTH
        nq = min(nq_max, M // SPAN)
        n_sup = M // (nq * SPAN)
        count = B * n_sup * d
        plans.append((first, count, n_sup, d, nq))

        def block_index(s, first=first, count=count, n_sup=n_sup, d=d):
            local = jnp.clip(s - first, 0, count - 1)
            return local // (n_sup * d), (local // d) % n_sup, local % d

        def cur_map(s, block_index=block_index):
            b, n, r = block_index(s)
            return b, n, r

        def prev_map(s, block_index=block_index, nq=nq):
            b, n, r = block_index(s)
            return b, jnp.maximum(n * nq - 1, 0), r

        cur = pl.BlockSpec((1, nq * SPAN, GROUP_WIDTH), cur_map)
        prev = pl.BlockSpec((1, SPAN, GROUP_WIDTH), prev_map)
        in_specs += [cur, cur, prev, cur, prev]
        args += [qs[g], ks[g], ks[g], vs[g], vs[g]]
        out_specs += [cur, cur]
        out_shape += [jax.ShapeDtypeStruct((B, M, width), BF16), jax.ShapeDtypeStruct((B, M, width), F32)]
        scratch += [pltpu.VMEM(((nq + 1) * SPAN, GROUP_WIDTH), BF16)] * 2
        first += count
    n_steps = max(first, N)

    per_n = lambda shape: pl.BlockSpec((1,) + tuple(shape),
                                       lambda s: (jnp.minimum(s, N - 1),) + (0,) * len(shape))
    cache_specs = [per_n(c.shape[1:]) for c in caches]
    in_specs += ([per_n(qbd.shape[1:]), per_n(kvn.shape[1:])] + cache_specs
                 + [per_n(state_pool.shape[1:]), per_n(u.shape[1:]), _full(w_pool.shape), _full((1, POOL_WIDTH))])
    args += [qbd, kvn, *caches, state_pool, u, w_pool, pool_scale]
    out_specs += ([per_n((t_new, GROUP_WIDTH)), per_n((t_new, POOL_WIDTH))] + cache_specs
                  + [per_n(state_pool.shape[1:])])
    out_shape += ([jax.ShapeDtypeStruct((N, t_new, GROUP_WIDTH), BF16),
                   jax.ShapeDtypeStruct((N, t_new, POOL_WIDTH), BF16)]
                  + [jax.ShapeDtypeStruct(c.shape, F32) for c in caches]
                  + [jax.ShapeDtypeStruct(state_pool.shape, F32)])
    scratch += [pltpu.VMEM((POOL_STATE_LEN + t_new + 5, POOL_WIDTH), F32),
                pltpu.VMEM((rows, POOL_WIDTH), F32),
                pltpu.VMEM((LANE, GROUP_WIDTH), F32)]
    outs = pl.pallas_call(
        functools.partial(_attn_sample_kernel, plans=tuple(plans), n_seq=N, t_new=t_new),
        grid=(n_steps,),
        in_specs=in_specs,
        out_specs=out_specs,
        out_shape=out_shape,
        scratch_shapes=scratch,
        compiler_params=_cparams(1),
        name="attn_and_sample",
    )(*args)
    return [(outs[2 * g], outs[2 * g + 1]) for g in range(N_GROUPS)], outs[2 * N_GROUPS:]


def _route(logits):
    lane = lax.broadcasted_iota(jnp.int32, logits.shape, 1).astype(F32)
    is_grp = (lane >= GROUP_LANE0) & (lane < GROUP_LANE0 + MOE_GROUPS)
    gl = jnp.where(is_grp, logits, NEG)
    gmax = jnp.max(gl, axis=-1, keepdims=True)
    gidx = jnp.min(jnp.where(gl == gmax, lane, BIG_LANE), axis=-1, keepdims=True) - GROUP_LANE0
    gsum = jnp.sum(jnp.where(is_grp, jnp.exp(gl - gmax), 0.0), axis=-1, keepdims=True)
    grp_w = 1.0 / gsum
    lo = gidx * EXPERTS_PER_GROUP
    in_grp = (lane >= lo) & (lane < lo + EXPERTS_PER_GROUP)
    el = jnp.where(in_grp, logits, NEG)
    v1 = jnp.max(el, axis=-1, keepdims=True)
    i1 = jnp.min(jnp.where(el == v1, lane, BIG_LANE), axis=-1, keepdims=True)
    el2 = jnp.where(lane == i1, NEG, el)
    v2 = jnp.max(el2, axis=-1, keepdims=True)
    i2 = jnp.min(jnp.where(el2 == v2, lane, BIG_LANE), axis=-1, keepdims=True)
    e = jnp.exp(v2 - v1)
    w1 = grp_w / (1.0 + e)
    w2 = grp_w * e / (1.0 + e)
    return lane, i1, i2, w1, w2


def _merge_kernel(*refs, n_groups):
    x_ref, mod_ref, g1_ref, g2_ref = refs[0:4]
    n_attn = 2 * n_groups if n_groups > 1 else 1
    attn_refs = refs[4:4 + n_attn]
    rest = refs[4 + n_attn:]
    pooled_ref, wg_ref, wa_ref, wp_ref, wo_ref, wrh_ref, wrc_ref, br_ref = rest[:8]
    x1_ref, h2_ref, route_ref, counts_ref, route_t_ref = rest[8:13]
    scratch_refs = rest[13:]
    stage_refs = scratch_refs[:n_attn] if n_groups > 1 else ()
    carry_ref = scratch_refs[-1]

    x = x_ref[0]
    tm = x.shape[0]
    if n_groups > 1:
        vals = []
        for idx, ref in enumerate(attn_refs):
            d = ATTN_DILATIONS[idx % n_groups]
            if d == 1:
                vals.append(ref[0].astype(F32))
                continue
            stage_ref = stage_refs[idx]
            chunks = GROUP_WIDTH // LANE
            for r in range(d):
                for c in range(chunks):
                    stage_ref[c, pl.ds(r, tm // d, stride=d), :] = (
                        ref[0, :, r * GROUP_WIDTH + c * LANE:r * GROUP_WIDTH + (c + 1) * LANE].astype(F32))
            vals.append(jnp.concatenate([stage_ref[c] for c in range(chunks)], axis=1))
        os_, ls = vals[:n_groups], vals[n_groups:]
        lmax = functools.reduce(jnp.maximum, ls)
        es = [jnp.exp(l - lmax) for l in ls]
        attn = sum(e * o for e, o in zip(es, os_)) / sum(es)
    else:
        attn = attn_refs[0][0]
    a = jnp.dot(attn.astype(BF16), wa_ref[...], preferred_element_type=F32)
    p = jnp.dot(pooled_ref[0], wp_ref[...], preferred_element_type=F32)
    h = _norm_mod(x, g1_ref[...], mod_ref[0, 1], mod_ref[0, 0]).astype(BF16)
    gates = jnp.dot(h, wg_ref[...], preferred_element_type=F32)
    D = x.shape[1]
    merged = jax.nn.sigmoid(gates[:, :D]) * a + jax.nn.sigmoid(gates[:, D:]) * p
    y = jnp.dot(merged.astype(BF16), wo_ref[...], preferred_element_type=F32)
    x1 = x + mod_ref[0, 2] * y
    x1_ref[0] = x1
    h2 = _norm_mod(x1, g2_ref[...], mod_ref[0, 4], mod_ref[0, 3])
    h2_hi = h2.astype(BF16)
    h2_lo = (h2 - h2_hi.astype(F32)).astype(BF16)
    hi_terms = jnp.dot(h2_hi, wrc_ref[...], preferred_element_type=F32)
    logits = (hi_terms[:, :ROUTER_LANES] + jnp.dot(h2_lo, wrh_ref[...], preferred_element_type=F32)
              + hi_terms[:, ROUTER_LANES:]) + br_ref[...]
    lane, i1, i2, w1, w2 = _route(logits)
    h2_ref[0] = h2

    @pl.when((pl.program_id(0) == 0) & (pl.program_id(1) == 0))
    def _():
        carry_ref[...] = jnp.zeros(carry_ref.shape, F32)

    hit = ((lane == i1) | (lane == i2)).astype(BF16)
    ltri = (lax.broadcasted_iota(jnp.int32, (tm, tm), 0) >= lax.broadcasted_iota(jnp.int32, (tm, tm), 1))
    prefix = jnp.dot(ltri.astype(BF16), hit, preferred_element_type=F32) + carry_ref[...]
    rank1 = jnp.sum(jnp.where(lane == i1, prefix, 0.0), axis=-1, keepdims=True) - 1.0
    rank2 = jnp.sum(jnp.where(lane == i2, prefix, 0.0), axis=-1, keepdims=True) - 1.0
    carry_ref[...] = prefix[tm - 1:tm, :]
    counts_ref[...] = jnp.broadcast_to(prefix[tm - 1:tm, :], counts_ref.shape)
    cols = (i1, i2, w1, w2, rank1, rank2)
    route = jnp.zeros(logits.shape, F32)
    for c, col in enumerate(cols):
        route = jnp.where(lane == float(c), col, route)
    route_ref[0] = route
    route_t_ref[...] = route.T[0:SUBLANE, :]


def _merge(x, mod, g1, g2, attn_inputs, pooled, w_gates, w_attn_out, w_pool_out, w_o, wr_hi, wr_hilo, b_r, tm):
    B, S, D = x.shape
    R = mod.shape[2]
    n_groups = len(attn_inputs) // 2 if len(attn_inputs) > 1 else 1
    tok = lambda width: pl.BlockSpec((1, tm, width), lambda b, i: (b, i, 0))
    if R == 1:
        mod_spec = pl.BlockSpec((1, N_COND, 1, D), lambda b, i: (b, 0, 0, 0))
    else:
        mod_spec = pl.BlockSpec((1, N_COND, tm, D), lambda b, i: (b, 0, i, 0))
    weights = (w_gates, w_attn_out, w_pool_out, w_o, wr_hi, wr_hilo, b_r)
    if n_groups > 1:
        attn_specs = [pl.BlockSpec((1, tm // (a.shape[2] // GROUP_WIDTH), a.shape[2]), lambda b, i: (b, i, 0))
                      for a in attn_inputs]
        scratch = [pltpu.VMEM((GROUP_WIDTH // LANE, tm, LANE), F32)] * len(attn_inputs)
    else:
        attn_specs = [tok(GROUP_WIDTH)]
        scratch = []
    n_t = S // tm
    out_specs = [tok(D), tok(D), tok(ROUTER_LANES), _full((SUBLANE, ROUTER_LANES)),
                 pl.BlockSpec((SUBLANE, tm), lambda b, i: (0, b * n_t + i))]
    out_shape = [jax.ShapeDtypeStruct((B, S, D), F32),
                 jax.ShapeDtypeStruct((B, S, D), F32),
                 jax.ShapeDtypeStruct((B, S, ROUTER_LANES), F32),
                 jax.ShapeDtypeStruct((SUBLANE, ROUTER_LANES), F32),
                 jax.ShapeDtypeStruct((SUBLANE, B * S), F32)]
    scratch = scratch + [pltpu.VMEM((1, ROUTER_LANES), F32)]
    return pl.pallas_call(
        functools.partial(_merge_kernel, n_groups=n_groups),
        grid=(B, S // tm),
        in_specs=[tok(D), mod_spec, _full((1, D)), _full((1, D))]
                 + attn_specs + [tok(POOL_WIDTH)]
                 + [_full(w.shape) for w in weights],
        out_specs=out_specs,
        out_shape=out_shape,
        scratch_shapes=scratch,
        compiler_params=_cparams(2),
        name="merge",
    )(x, mod, g1, g2, *attn_inputs, pooled, *weights)


EXPERT_TILE = 512
PAD_CHUNK = 32


def _positions_kernel(base_ref, route_ref, pos_ref):
    cols = route_ref[...].astype(jnp.int32)
    start = jnp.zeros_like(cols)
    for e in range(N_EXPERTS):
        start = jnp.where(cols == e, base_ref[e], start)
    pos_ref[...] = start + pltpu.roll(cols, SUBLANE - 4, 0)


def _positions(route_t, base):
    rows, T = route_t.shape
    grid_spec = pltpu.PrefetchScalarGridSpec(
        num_scalar_prefetch=1, grid=(1,),
        in_specs=[pl.BlockSpec((rows, T), lambda i, base: (0, 0))],
        out_specs=pl.BlockSpec((rows, T), lambda i, base: (0, 0)))
    pos = pl.pallas_call(
        _positions_kernel,
        grid_spec=grid_spec,
        out_shape=jax.ShapeDtypeStruct((rows, T), jnp.int32),
        compiler_params=_cparams(1),
        name="moe_positions",
    )(base.astype(jnp.int32), route_t)
    return pos[0:2]


def _routing_tables(routes, counts, tile):
    cnts = [c[0, :N_EXPERTS].astype(jnp.int32) for c in counts]
    cnt = sum(cnts)
    padded = ((cnt + tile - 1) // tile) * tile
    ends = jnp.cumsum(padded)
    base = ends - padded
    poss = []
    for route in routes:
        poss.append(_positions(route, base))
        base = base + cnts[len(poss) - 1]
    n_pairs = 2 * sum(r.shape[1] for r in routes)
    n_tiles = -(-n_pairs // tile) + N_EXPERTS
    starts = jnp.arange(n_tiles, dtype=jnp.int32) * tile
    n_used = ends[-1] // tile
    tile_expert = jnp.sum(starts[:, None] >= ends[None, :], axis=1).astype(jnp.int32)
    last = jnp.take(tile_expert, n_used - 1)
    tile_expert = jnp.where(jnp.arange(n_tiles) < n_used, tile_expert, last)
    pads = jnp.stack([ends - padded + cnts[0], ends], axis=1).reshape(-1).astype(jnp.int32)
    return poss, tile_expert, n_used.reshape(1).astype(jnp.int32), pads, n_tiles


def _pos_steps(pos, tm):
    steps = pos.shape[1] // tm
    return jnp.transpose(pos.reshape(2, steps, tm), (1, 0, 2)).reshape(steps, 1, 2 * tm)


def _row_copy(src_ref, src_row, dst_ref, dst_row, sem):
    return pltpu.make_async_copy(src_ref.at[pl.ds(src_row, 1)], dst_ref.at[pl.ds(dst_row, 1)], sem)


def _dispatch_kernel(pads_ref, pos_ref, h_ref, *rest, tm, tile, first):
    xs_ref, zero_ref, sem = rest[-3:]

    def zero_fill():
        zero_ref[...] = jnp.zeros(zero_ref.shape, zero_ref.dtype)
        n_tiles = xs_ref.shape[0] // tile
        first_unused = pads_ref[2 * N_EXPERTS - 1] // tile

        def tile_copy(j):
            return pltpu.make_async_copy(zero_ref, xs_ref.at[pl.ds(pl.multiple_of(j * tile, tile), tile)], sem)

        def fill_tile(j, c):
            tile_copy(j).start()
            return c

        def drain_tile(j, c):
            tile_copy(j).wait()
            return c

        lax.fori_loop(first_unused, n_tiles, fill_tile, 0)
        lax.fori_loop(first_unused, n_tiles, drain_tile, 0)

        def chunk_copy(c):
            rows = pl.ds(pl.multiple_of(c * PAD_CHUNK, PAD_CHUNK), PAD_CHUNK)
            return pltpu.make_async_copy(zero_ref.at[pl.ds(0, PAD_CHUNK)], xs_ref.at[rows], sem)

        def for_each_pad_piece(on_row, on_chunk):
            def per_expert(e, carry):
                lo, hi = pads_ref[2 * e], pads_ref[2 * e + 1]
                first_chunk = (lo + PAD_CHUNK - 1) // PAD_CHUNK
                lax.fori_loop(lo, jnp.minimum(first_chunk * PAD_CHUNK, hi), on_row, 0)
                lax.fori_loop(first_chunk, hi // PAD_CHUNK, on_chunk, 0)
                return carry

            lax.fori_loop(0, N_EXPERTS, per_expert, 0)

        def start_row(r, c):
            _row_copy(zero_ref, 0, xs_ref, r, sem).start()
            return c

        def wait_row(r, c):
            _row_copy(zero_ref, 0, xs_ref, r, sem).wait()
            return c

        def start_chunk(c, carry):
            chunk_copy(c).start()
            return carry

        def wait_chunk(c, carry):
            chunk_copy(c).wait()
            return carry

        for_each_pad_piece(start_row, start_chunk)
        for_each_pad_piece(wait_row, wait_chunk)

    if first:
        pl.when(pl.program_id(0) == 0)(zero_fill)

    def issue(k, carry):
        for j in range(SUBLANE):
            i = k * SUBLANE + j
            for p in (pos_ref[0, 0, i], pos_ref[0, 0, tm + i]):
                pltpu.make_async_copy(h_ref.at[k, pl.ds(j, 1)], xs_ref.at[pl.ds(p, 1)], sem).start()
        return carry

    lax.fori_loop(0, tm // SUBLANE, issue, 0)
    for _ in range(2):
        pltpu.make_async_copy(h_ref, h_ref, sem).wait()


def _dispatch(h2, pos, pads, n_rows, tm, tile, xs_prev=None):
    T, width = h2.shape
    in_specs = [pl.BlockSpec((1, 1, 2 * tm), lambda i, pads: (i, 0, 0), memory_space=pltpu.SMEM),
                pl.BlockSpec((tm // SUBLANE, SUBLANE, width), lambda i, pads: (i, 0, 0))]
    args = [pads, _pos_steps(pos, tm), h2.reshape(T // SUBLANE, SUBLANE, width)]
    aliases = {}
    if xs_prev is not None:
        in_specs.append(pl.BlockSpec(memory_space=pl.ANY))
        args.append(xs_prev)
        aliases = {3: 0}
    grid_spec = pltpu.PrefetchScalarGridSpec(
        num_scalar_prefetch=1,
        grid=(T // tm,),
        in_specs=in_specs,
        out_specs=pl.BlockSpec(memory_space=pl.ANY),
        scratch_shapes=[pltpu.VMEM((tile, width), h2.dtype), pltpu.SemaphoreType.DMA(())],
    )
    return pl.pallas_call(
        functools.partial(_dispatch_kernel, tm=tm, tile=tile, first=xs_prev is None),
        grid_spec=grid_spec,
        out_shape=jax.ShapeDtypeStruct((n_rows, width), h2.dtype),
        input_output_aliases=aliases,
        compiler_params=_cparams(1),
        name="moe_dispatch",
    )(*args)


def _experts_kernel(te_ref, nu_ref, xs_ref, wgu_ref, wd_ref, ys_ref, wgu_bf_ref, wd_bf_ref, *, d_expert):
    j = pl.program_id(0)

    @pl.when(j >= nu_ref[0])
    def _():
        ys_ref[...] = jnp.zeros(ys_ref.shape, F32)

    @pl.when((j == 0) | (te_ref[j] != te_ref[jnp.maximum(j - 1, 0)]))
    def _():
        wgu_bf_ref[...] = wgu_ref[0].astype(BF16)
        wd_bf_ref[...] = wd_ref[0].astype(BF16)

    @pl.when(j < nu_ref[0])
    def _():
        gu = jnp.dot(xs_ref[...].astype(BF16), wgu_bf_ref[...], preferred_element_type=F32)
        act = _silu(gu[:, :d_expert]) * gu[:, d_expert:]
        ys_ref[...] = jnp.dot(act.astype(BF16), wd_bf_ref[...], preferred_element_type=F32)


def _experts(xs, tile_expert, n_used, w_gate_up, w_down, tile):
    n_rows, width = xs.shape
    _, D, two_f = w_gate_up.shape
    grid_spec = pltpu.PrefetchScalarGridSpec(
        num_scalar_prefetch=2,
        grid=(n_rows // tile,),
        in_specs=[pl.BlockSpec((tile, width), lambda j, te, nu: (jnp.minimum(j, nu[0] - 1), 0)),
                  pl.BlockSpec((1, D, two_f), lambda j, te, nu: (te[j], 0, 0)),
                  pl.BlockSpec((1, two_f // 2, D), lambda j, te, nu: (te[j], 0, 0))],
        out_specs=pl.BlockSpec((tile, D), lambda j, te, nu: (j, 0)),
        scratch_shapes=[pltpu.VMEM((D, two_f), BF16), pltpu.VMEM((two_f // 2, D), BF16)],
    )
    return pl.pallas_call(
        functools.partial(_experts_kernel, d_expert=two_f // 2),
        grid_spec=grid_spec,
        out_shape=jax.ShapeDtypeStruct((n_rows, D), F32),
        compiler_params=_cparams(1),
        name="moe_experts",
    )(tile_expert, n_used, xs, w_gate_up, w_down)


def _combine_kernel(pos_ref, pos_next_ref, ys_ref, route_ref, x1_ref, mod_ref, gf_ref, y_ref,
                    ya_ref, yb_ref, sems, *, tm):
    step = pl.program_id(0) * pl.num_programs(1) + pl.program_id(1)
    n_steps = pl.num_programs(0) * pl.num_programs(1)
    slot = step % 2

    def start_gathers(p_ref, s):
        def issue(k, carry):
            for j in range(SUBLANE):
                i = k * SUBLANE + j
                for buf, p in ((ya_ref, p_ref[0, 0, i]), (yb_ref, p_ref[0, 0, tm + i])):
                    pltpu.make_async_copy(ys_ref.at[pl.ds(p, 1)], buf.at[s, k, pl.ds(j, 1)], sems.at[s]).start()
            return carry

        lax.fori_loop(0, tm // SUBLANE, issue, 0)

    @pl.when(step == 0)
    def _():
        start_gathers(pos_ref, 0)

    @pl.when(step + 1 < n_steps)
    def _():
        start_gathers(pos_next_ref, 1 - slot)

    for buf in (ya_ref, yb_ref):
        pltpu.make_async_copy(buf.at[slot], buf.at[slot], sems.at[slot]).wait()
    route = route_ref[0]
    lane = lax.broadcasted_iota(jnp.int32, route.shape, 1)
    w1 = jnp.sum(jnp.where(lane == 2, route, 0.0), axis=-1, keepdims=True)
    w2 = jnp.sum(jnp.where(lane == 3, route, 0.0), axis=-1, keepdims=True)
    ya = ya_ref[slot].reshape(tm, ya_ref.shape[-1])
    yb = yb_ref[slot].reshape(tm, yb_ref.shape[-1])
    x2 = x1_ref[0] + mod_ref[0, 5] * (w1 * ya + w2 * yb)
    var = jnp.mean(x2 * x2, axis=-1, keepdims=True)
    y_ref[0] = x2 * lax.rsqrt(var + EPS) * gf_ref[...]


def _combine(ys, pos, route, x1, mod, gf, tm):
    B, S, D = x1.shape
    n_t = S // tm
    tok = lambda width: pl.BlockSpec((1, tm, width), lambda b, i: (b, i, 0))
    last = B * n_t - 1
    pos_spec = lambda ahead: pl.BlockSpec(
        (1, 1, 2 * tm), lambda b, i: (jnp.minimum(b * n_t + i + ahead, last), 0, 0), memory_space=pltpu.SMEM)
    pos_steps = _pos_steps(pos, tm)
    if mod.shape[2] == 1:
        mod_spec = pl.BlockSpec((1, N_COND, 1, D), lambda b, i: (b, 0, 0, 0))
    else:
        mod_spec = pl.BlockSpec((1, N_COND, tm, D), lambda b, i: (b, 0, i, 0))
    return pl.pallas_call(
        functools.partial(_combine_kernel, tm=tm),
        grid=(B, n_t),
        in_specs=[pos_spec(0), pos_spec(1),
                  pl.BlockSpec(memory_space=pl.ANY),
                  tok(ROUTER_LANES), tok(D),
                  mod_spec,
                  _full((1, D))],
        out_specs=tok(D),
        out_shape=jax.ShapeDtypeStruct((B, S, D), F32),
        scratch_shapes=[pltpu.VMEM((2, tm // SUBLANE, SUBLANE, D), F32)] * 2
                       + [pltpu.SemaphoreType.DMA((2,))],
        compiler_params=_cparams(2),
        name="moe_combine",
    )(pos_steps, pos_steps, ys, route, x1, mod, gf)


def _rope_tables(pos):
    half = HEAD_DIM // 2
    inv = ROPE_THETA ** (-jnp.arange(half, dtype=F32) * 2.0 / HEAD_DIM)
    ang = pos.astype(F32)[:, None] * inv[None, :]
    cos, sin = jnp.cos(ang), jnp.sin(ang)
    reps = LANE // HEAD_DIM
    return (jnp.tile(jnp.concatenate([cos, cos], axis=-1), (1, reps)),
            jnp.tile(jnp.concatenate([-sin, sin], axis=-1), (1, reps)))


def kernel(x_prompt, x_sample, cache_kv_w128, cache_kv_w512, cache_kv_w2048, state_pool, c_prompt, c_sample, norm1_g, w_ada, b_ada, w_in, w_attn_out, w_pool, pool_scale, w_pool_out, w_o, norm2_g, w_grp, b_grp, w_exp_router, b_exp_router, w_gate_up, w_down, final_norm_g):
    B, S, D = x_prompt.shape
    N, T, _ = x_sample.shape
    depth = norm1_g.shape[0]
    assert depth == 1, "single trunk layer"
    tm = min(TOKEN_TILE, S)
    assert S % tm == 0 and all(S % (SPAN * d) == 0 for d in ATTN_DILATIONS)

    n_qkvu = 3 * ATTN_WIDTH + POOL_WIDTH
    w_qkvu = w_in[0, :, :n_qkvu].astype(BF16)
    w_gates = w_in[0, :, n_qkvu:].astype(BF16)
    wa, wpo, wo = w_attn_out[0].astype(BF16), w_pool_out[0].astype(BF16), w_o[0].astype(BF16)
    wpool = w_pool[0].astype(BF16)
    pscale = pool_scale[0].reshape(1, POOL_WIDTH)
    w_r = jnp.concatenate([w_exp_router[0], w_grp[0]], axis=1)
    w_r = jnp.pad(w_r, ((0, 0), (0, ROUTER_LANES - w_r.shape[1])))
    wr_hi = w_r.astype(BF16)
    wr_hilo = jnp.concatenate([wr_hi, (w_r - wr_hi.astype(F32)).astype(BF16)], axis=1)
    b_r = jnp.pad(jnp.concatenate([b_exp_router[0], b_grp[0]]), (0, ROUTER_LANES - N_EXPERTS - MOE_GROUPS))
    b_r = b_r.reshape(1, ROUTER_LANES)
    wgu, wd = w_gate_up[0], w_down[0]
    g1, g2, gf = norm1_g[0].reshape(1, D), norm2_g[0].reshape(1, D), final_norm_g.reshape(1, D)

    c_rows = jnp.concatenate([jnp.repeat(c_sample, T, axis=0), c_prompt], axis=0)
    mod = _ada(c_rows, w_ada[0], b_ada[0])
    mod_p = jnp.transpose(mod[:, N * T:], (1, 0, 2)).reshape(B, N_COND, 1, D)
    mod_s = mod[None]

    cos_p, sin_p = _rope_tables(jnp.arange(S, dtype=jnp.int32))
    outs = _in_proj_prompt(x_prompt, mod_p, g1, w_qkvu, cos_p, sin_p, wpool, pscale, tm)
    qs, ks, vs = outs[0:3], outs[3:6], outs[6:9]
    pooled, kv0, kv1, kv2, ptail = outs[9:14]
    pool_prompt = ptail[:, POOL_HALO - POOL_STATE_LEN:, :][None]

    TS = N * T
    pos_s = PAST_LEN + jnp.arange(T, dtype=jnp.int32)
    cos_s, sin_s = _rope_tables(pos_s)
    cos_s, sin_s = jnp.tile(cos_s, (N, 1)), jnp.tile(sin_s, (N, 1))
    xs = x_sample.reshape(1, TS, D)
    q_s, kvn, u_s = _in_proj_sample(xs, mod_s, g1, w_qkvu, cos_s, sin_s)
    eye = jnp.eye(HEADS_PER_GROUP, dtype=BF16)
    qbd = jnp.einsum('ntghe,hk->ngthke', q_s.reshape(N, T, N_GROUPS, HEADS_PER_GROUP, HEAD_DIM), eye)
    qbd = qbd.reshape(N, N_GROUPS, T * HEADS_PER_GROUP, GROUP_WIDTH)
    caches = [jnp.transpose(c[0], (0, 2, 3, 4, 1)).reshape(N, 2, GROUP_WIDTH, c.shape[2])
              for c in (cache_kv_w128, cache_kv_w512, cache_kv_w2048)]

    attn_parts, (attn_s, pooled_s, ko0, ko1, ko2, pool_s) = _attn_and_sample(
        qs, ks, vs, qbd, kvn.reshape(N, T, 2 * ATTN_WIDTH), caches, state_pool[0],
        u_s.reshape(N, T, POOL_WIDTH), wpool, pscale, nq_max=ATTN_BLOCKS_PER_STEP)
    attn_inputs = [o for o, _ in attn_parts] + [l for _, l in attn_parts]
    x1, h2, route, counts, route_t = _merge(x_prompt, mod_p, g1, g2, attn_inputs, pooled, w_gates, wa, wpo, wo,
                                            wr_hi, wr_hilo, b_r, tm)
    x1s, h2s, route_s, counts_s, route_ts = _merge(xs, mod_s, g1, g2, [attn_s.reshape(1, TS, GROUP_WIDTH)],
                                                   pooled_s.reshape(1, TS, POOL_WIDTH), w_gates, wa, wpo, wo,
                                                   wr_hi, wr_hilo, b_r, TS)

    (pos, pos_s), tile_expert, n_used, pads, n_tiles = _routing_tables(
        [route_t, route_ts], [counts, counts_s], EXPERT_TILE)
    sorted_rows = _dispatch(h2.reshape(B * S, D), pos, pads, n_tiles * EXPERT_TILE, tm, EXPERT_TILE)
    sorted_rows = _dispatch(h2s.reshape(TS, D), pos_s, pads, n_tiles * EXPERT_TILE, TS, EXPERT_TILE,
                            xs_prev=sorted_rows)
    ys = _experts(sorted_rows, tile_expert, n_used, wgu, wd, EXPERT_TILE)
    y_prompt = _combine(ys, pos, route, x1, mod_p, gf, tm=tm)
    y_sample = _combine(ys, pos_s, route_s, x1s, mod_s, gf, tm=TS).reshape(N, T, D)

    def kv_shape(a):
        a = a.reshape(a.shape[0], 2, HEADS_PER_GROUP, HEAD_DIM, a.shape[3])
        return jnp.transpose(a, (0, 4, 1, 2, 3))[None]

    return (y_prompt, y_sample, kv_shape(kv0), kv_shape(kv1), kv_shape(kv2), pool_prompt,
            kv_shape(ko0), kv_shape(ko1), kv_shape(ko2), pool_s[None])
```

```python
import functools

import jax
import jax.numpy as jnp
from jax import lax
from jax.experimental import pallas as pl
from jax.experimental.pallas import tpu as pltpu

F32 = jnp.float32
BF16 = jnp.bfloat16

HEAD_DIM = 64
HEADS_PER_GROUP = 4
HEAD_SHIFT = 2
LANE_HEAD_SHIFT = 6
GROUP_WIDTH = HEADS_PER_GROUP * HEAD_DIM
ATTN_WINDOWS = (128, 512, 2048)
ATTN_DILATIONS = (1, 4, 16)
N_GROUPS = 3
SPAN = 128
ATTN_WIDTH = N_GROUPS * GROUP_WIDTH
ROPE_THETA = 10000.0
PAST_LEN = 8192
POOL_WINDOWS = (2, 4, 8, 16)
POOL_GROUP_WIDTH = 128
POOL_WIDTH = 512
POOL_STATE_LEN = 15
POOL_HALO = 16
MOE_GROUPS = 4
EXPERTS_PER_GROUP = 8
N_EXPERTS = 32
N_COND = 6
EPS = 1e-6

LANE = 128
SUBLANE = 8
VMEM_LIMIT_BYTES = 56 * 1024 * 1024

TOKEN_TILE = 512
ATTN_BLOCKS_PER_STEP = 8

NEG = -1e30
BIG_LANE = 1e9

ROUTER_LANES = LANE
GROUP_LANE0 = N_EXPERTS


def _cparams(n_axes):
    return pltpu.CompilerParams(dimension_semantics=("arbitrary",) * n_axes,
                                vmem_limit_bytes=VMEM_LIMIT_BYTES)


def _full(shape):
    nd = len(shape)
    return pl.BlockSpec(tuple(shape), lambda *_: (0,) * nd)


def _norm_mod(x, g, scale, shift):
    var = jnp.mean(x * x, axis=-1, keepdims=True)
    return (x * lax.rsqrt(var + EPS) * g) * (1.0 + scale) + shift


def _rope(x, cos, sin):
    lane = lax.broadcasted_iota(jnp.int32, (x.shape[0], LANE), 1)
    first_half = (lane & (HEAD_DIM - 1)) < (HEAD_DIM // 2)
    outs = []
    for c in range(x.shape[1] // LANE):
        xc = x[:, c * LANE:(c + 1) * LANE]
        partner = jnp.where(first_half, pltpu.roll(xc, LANE - HEAD_DIM // 2, 1),
                            pltpu.roll(xc, HEAD_DIM // 2, 1))
        outs.append(xc * cos + partner * sin)
    return jnp.concatenate(outs, axis=1)


def _silu(x):
    return x * jax.nn.sigmoid(x)


def _ada_kernel(c_ref, w_ref, b_ref, o_ref):
    s = _silu(c_ref[...]).astype(BF16)
    o_ref[0] = jnp.dot(s, w_ref[...].astype(BF16), preferred_element_type=F32) + b_ref[...]


def _ada(c_all, w_ada, b_ada):
    rows, d = c_all.shape
    n_out = w_ada.shape[1]
    tn = n_out // N_COND
    return pl.pallas_call(
        _ada_kernel,
        grid=(N_COND,),
        in_specs=[_full((rows, d)),
                  pl.BlockSpec((d, tn), lambda j: (0, j)),
                  pl.BlockSpec((1, tn), lambda j: (0, j))],
        out_specs=pl.BlockSpec((1, rows, tn), lambda j: (j, 0, 0)),
        out_shape=jax.ShapeDtypeStruct((N_COND, rows, tn), F32),
        compiler_params=_cparams(1),
        name="ada",
    )(c_all, w_ada, b_ada.reshape(1, n_out))


def _in_proj_prompt_kernel(x_ref, mod_ref, g1_ref, w_ref, cos_ref, sin_ref, wpool_ref, pscale_ref, *rest,
                           tm, n_tiles, tails):
    qkv_refs = (rest[0:3], rest[3:6], rest[6:9])
    pooled_ref, kv0_ref, kv1_ref, kv2_ref, ptail_ref = rest[9:14]
    stage_refs = rest[14:17]
    ue_ref = rest[17]
    i = pl.program_id(1)
    x = x_ref[0]
    h = _norm_mod(x, g1_ref[...], mod_ref[0, 1], mod_ref[0, 0]).astype(BF16)
    cos = cos_ref[...]
    sin = sin_ref[...]
    y = jnp.dot(h, w_ref[...], preferred_element_type=F32)
    q = _rope(y[:, :ATTN_WIDTH], cos, sin) * (HEAD_DIM ** -0.5)
    k = _rope(y[:, ATTN_WIDTH:2 * ATTN_WIDTH], cos, sin)
    v = y[:, 2 * ATTN_WIDTH:3 * ATTN_WIDTH]
    u = y[:, 3 * ATTN_WIDTH:3 * ATTN_WIDTH + POOL_WIDTH]

    chunks = GROUP_WIDTH // LANE
    for val, out_refs, stage_ref in zip((q, k, v), qkv_refs, stage_refs):
        for g, d in enumerate(ATTN_DILATIONS):
            gl = slice(g * GROUP_WIDTH, (g + 1) * GROUP_WIDTH)
            if d == 1:
                out_refs[g][0] = val[:, gl].astype(BF16)
                continue
            for c in range(chunks):
                stage_ref[g * chunks + c] = val[:, g * GROUP_WIDTH + c * LANE:g * GROUP_WIDTH + (c + 1) * LANE]
            for r in range(d):
                for c in range(chunks):
                    out_refs[g][0, :, r * GROUP_WIDTH + c * LANE:r * GROUP_WIDTH + (c + 1) * LANE] = (
                        stage_ref[g * chunks + c, pl.ds(r, tm // d, stride=d), :].astype(BF16))

    for g, kv_ref in enumerate((kv0_ref, kv1_ref, kv2_ref)):
        first_tile, rows = tails[g]
        lo = tm - rows

        @pl.when(i >= first_tile)
        def _(kv_ref=kv_ref, g=g, lo=lo):
            kv_ref[0, 0] = k[lo:, g * GROUP_WIDTH:(g + 1) * GROUP_WIDTH].T
            kv_ref[0, 1] = v[lo:, g * GROUP_WIDTH:(g + 1) * GROUP_WIDTH].T

    @pl.when(i == 0)
    def _():
        ue_ref[0:POOL_HALO, :] = jnp.zeros((POOL_HALO, POOL_WIDTH), F32)

    ue_ref[POOL_HALO:, :] = u
    pos1 = (i * tm + 1 + lax.broadcasted_iota(jnp.int32, (tm, 1), 0)).astype(F32)
    for gi, w in enumerate(POOL_WINDOWS):
        lanes = slice(gi * POOL_GROUP_WIDTH, (gi + 1) * POOL_GROUP_WIDTH)
        acc = ue_ref[pl.ds(POOL_HALO, tm), lanes]
        for j in range(1, w):
            acc = acc + ue_ref[pl.ds(POOL_HALO - j, tm), lanes]
        mean = acc / jnp.minimum(float(w), pos1)
        z = jnp.dot((mean - u[:, lanes]).astype(BF16), wpool_ref[gi], preferred_element_type=F32)
        pooled_ref[0, :, lanes] = (z * pscale_ref[:, lanes]).astype(BF16)
    ue_ref[0:POOL_HALO, :] = u[tm - POOL_HALO:, :]

    @pl.when(i == n_tiles - 1)
    def _():
        ptail_ref[0] = u[tm - POOL_HALO:, :]


def _in_proj_prompt(x, mod, g1, w_qkvu, cos, sin, w_pool, pool_scale, tm):
    B, S, D = x.shape
    n_tiles = S // tm
    n_w = w_qkvu.shape[1]
    tails, kv_shapes, kv_specs = [], [], []
    for W in ATTN_WINDOWS:
        Wg = min(W, S)
        if Wg >= tm:
            first = n_tiles - Wg // tm
            rows = tm
        else:
            first = n_tiles - 1
            rows = Wg
        tails.append((first, rows))
        kv_shapes.append(jax.ShapeDtypeStruct((B, 2, GROUP_WIDTH, Wg), F32))
        kv_specs.append(pl.BlockSpec((1, 2, GROUP_WIDTH, rows),
                                     lambda b, i, first=first: (b, 0, 0, jnp.maximum(i - first, 0))))
    tok = lambda width: pl.BlockSpec((1, tm, width), lambda b, i: (b, i, 0))
    dil_specs = [pl.BlockSpec((1, tm // d, d * GROUP_WIDTH), lambda b, i: (b, i, 0)) for d in ATTN_DILATIONS]
    dil_shapes = [jax.ShapeDtypeStruct((B, S // d, d * GROUP_WIDTH), BF16) for d in ATTN_DILATIONS]
    outs = pl.pallas_call(
        functools.partial(_in_proj_prompt_kernel, tm=tm, n_tiles=n_tiles, tails=tuple(tails)),
        grid=(B, n_tiles),
        in_specs=[tok(D),
                  pl.BlockSpec((1, N_COND, 1, D), lambda b, i: (b, 0, 0, 0)),
                  _full((1, D)),
                  _full((D, n_w)),
                  pl.BlockSpec((tm, LANE), lambda b, i: (i, 0)),
                  pl.BlockSpec((tm, LANE), lambda b, i: (i, 0)),
                  _full(w_pool.shape),
                  _full((1, POOL_WIDTH))],
        out_specs=dil_specs * 3 + [tok(POOL_WIDTH)] + kv_specs
                  + [pl.BlockSpec((1, POOL_HALO, POOL_WIDTH), lambda b, i: (b, 0, 0))],
        out_shape=dil_shapes * 3
                  + [jax.ShapeDtypeStruct((B, S, POOL_WIDTH), BF16)] + kv_shapes
                  + [jax.ShapeDtypeStruct((B, POOL_HALO, POOL_WIDTH), F32)],
        scratch_shapes=[pltpu.VMEM((ATTN_WIDTH // LANE, tm, LANE), F32)] * 3
                       + [pltpu.VMEM((tm + POOL_HALO, POOL_WIDTH), F32)],
        compiler_params=_cparams(2),
        name="in_proj_prompt",
    )(x, mod, g1, w_qkvu, cos, sin, w_pool, pool_scale)
    return outs


def _in_proj_sample_kernel(x_ref, mod_ref, g1_ref, w_ref, cos_ref, sin_ref, q_ref, kvn_ref, u_ref):
    h = _norm_mod(x_ref[0], g1_ref[...], mod_ref[0, 1], mod_ref[0, 0]).astype(BF16)
    y = jnp.dot(h, w_ref[...], preferred_element_type=F32)
    cos = cos_ref[...]
    sin = sin_ref[...]
    q = _rope(y[:, :ATTN_WIDTH], cos, sin) * (HEAD_DIM ** -0.5)
    k = _rope(y[:, ATTN_WIDTH:2 * ATTN_WIDTH], cos, sin)
    v = y[:, 2 * ATTN_WIDTH:3 * ATTN_WIDTH]
    q_ref[...] = q.astype(BF16)
    for g in range(N_GROUPS):
        gl = slice(g * GROUP_WIDTH, (g + 1) * GROUP_WIDTH)
        kvn_ref[:, 2 * g * GROUP_WIDTH:(2 * g + 1) * GROUP_WIDTH] = k[:, gl]
        kvn_ref[:, (2 * g + 1) * GROUP_WIDTH:(2 * g + 2) * GROUP_WIDTH] = v[:, gl]
    u_ref[...] = y[:, 3 * ATTN_WIDTH:3 * ATTN_WIDTH + POOL_WIDTH]


def _in_proj_sample(x, mod, g1, w_qkvu, cos, sin):
    _, T, D = x.shape
    n_w = w_qkvu.shape[1]
    return pl.pallas_call(
        _in_proj_sample_kernel,
        grid=(1,),
        in_specs=[_full((1, T, D)), _full((1, N_COND, T, D)), _full((1, D)), _full((D, n_w)),
                  _full((T, LANE)), _full((T, LANE))],
        out_specs=[_full((T, ATTN_WIDTH)), _full((T, 2 * ATTN_WIDTH)), _full((T, POOL_WIDTH))],
        out_shape=[jax.ShapeDtypeStruct((T, ATTN_WIDTH), BF16),
                   jax.ShapeDtypeStruct((T, 2 * ATTN_WIDTH), F32),
                   jax.ShapeDtypeStruct((T, POOL_WIDTH), F32)],
        compiler_params=_cparams(1),
        name="in_proj_sample",
    )(x, mod, g1, w_qkvu, cos, sin)


def _attn_body(q_ref, kc_ref, kp_ref, vc_ref, vp_ref, o_ref, lse_ref, kbuf, vbuf, n0, nq):
    kbuf[0:SPAN, :] = kp_ref[0]
    kbuf[SPAN:, :] = kc_ref[0]
    vbuf[0:SPAN, :] = vp_ref[0]
    vbuf[SPAN:, :] = vc_ref[0]
    qi = lax.broadcasted_iota(jnp.int32, (SPAN, 2 * SPAN), 0)
    kj = lax.broadcasted_iota(jnp.int32, (SPAN, 2 * SPAN), 1)
    band = (kj >= qi) & (kj <= qi + SPAN)
    band_first = band & (kj >= jnp.where(n0 > 0, 0, SPAN))
    lane = lax.broadcasted_iota(jnp.int32, (SPAN, LANE), 1)
    low_head = lane < HEAD_DIM
    for j in range(nq):
        valid = band_first if j == 0 else band
        rows = slice(j * SPAN, (j + 1) * SPAN)
        krows = slice(j * SPAN, (j + 2) * SPAN)
        for c in range(GROUP_WIDTH // LANE):
            cl = slice(c * LANE, (c + 1) * LANE)
            q = q_ref[0, rows, cl]
            kk = kbuf[krows, cl]
            vv = vbuf[krows, cl]
            o_pair, lse_pair = [], []
            for hh in range(2):
                mask_h = low_head if hh == 0 else jnp.logical_not(low_head)
                qm = jnp.where(mask_h, q, jnp.zeros_like(q))
                s = lax.dot_general(qm, kk, (((1,), (1,)), ((), ())), preferred_element_type=F32)
                s = jnp.where(valid, s, NEG)
                m = jnp.max(s, axis=-1, keepdims=True)
                p = jnp.exp(s - m)
                den = jnp.sum(p, axis=-1, keepdims=True)
                o_pair.append(jnp.dot(p.astype(BF16), vv, preferred_element_type=F32) / den)
                lse_pair.append(m + jnp.log(den))
            o_ref[0, rows, cl] = jnp.where(low_head, o_pair[0], o_pair[1]).astype(BF16)
            lse_ref[0, rows, cl] = jnp.where(low_head, lse_pair[0], lse_pair[1])


def _sample_body(qbd_ref, kvn_ref, c0_ref, c1_ref, c2_ref, sp_ref, u_ref, wpool_ref, pscale_ref,
                 attn_ref, pooled_ref, o0_ref, o1_ref, o2_ref, po_ref, ue_ref, diff_ref, nt_ref, t_new):
    rows = HEADS_PER_GROUP * t_new
    caches = (c0_ref, c1_ref, c2_ref)
    outs = (o0_ref, o1_ref, o2_ref)

    s_cache, s_new, valid_new = [], [], []
    m = jnp.full((rows, 1), NEG, F32)
    for g in range(N_GROUPS):
        d = ATTN_DILATIONS[g]
        c_ref = caches[g]
        L = c_ref.shape[3]
        qb = qbd_ref[0, g]
        s = jnp.dot(qb, c_ref[0, 0].astype(BF16), preferred_element_type=F32)
        t_row = lax.broadcasted_iota(jnp.int32, (rows, L), 0) >> HEAD_SHIFT
        delta = L + t_row - lax.broadcasted_iota(jnp.int32, (rows, L), 1)
        ok = ((delta & (d - 1)) == 0) & (delta <= SPAN * d)
        s = jnp.where(ok, s, NEG)
        s_cache.append(s)
        m = jnp.maximum(m, jnp.max(s, axis=-1, keepdims=True))
        qf = qb.astype(F32)
        t_col = lax.broadcasted_iota(jnp.int32, (rows, 1), 0) >> HEAD_SHIFT
        sn, okn = [], []
        for tn in range(t_new):
            kn = kvn_ref[0, tn:tn + 1, 2 * g * GROUP_WIDTH:(2 * g + 1) * GROUP_WIDTH]
            kn = kn.astype(BF16).astype(F32)
            dn = t_col - tn
            ok_n = (dn >= 0) & ((dn & (d - 1)) == 0)
            s1 = jnp.where(ok_n, jnp.sum(qf * kn, axis=-1, keepdims=True), NEG)
            sn.append(s1)
            okn.append(ok_n)
            m = jnp.maximum(m, s1)
        s_new.append(sn)
        valid_new.append(okn)

    den = jnp.zeros((rows, 1), F32)
    acc = jnp.zeros((rows, GROUP_WIDTH), F32)
    for g in range(N_GROUPS):
        c_ref = caches[g]
        p = jnp.exp(s_cache[g] - m)
        den = den + jnp.sum(p, axis=-1, keepdims=True)
        acc = acc + lax.dot_general(p.astype(BF16), c_ref[0, 1].astype(BF16), (((1,), (1,)), ((), ())),
                                    preferred_element_type=F32)
        for tn in range(t_new):
            pn = jnp.exp(s_new[g][tn] - m)
            den = den + pn
            vn = kvn_ref[0, tn:tn + 1, (2 * g + 1) * GROUP_WIDTH:(2 * g + 2) * GROUP_WIDTH]
            acc = acc + pn * vn
    row_head = lax.broadcasted_iota(jnp.int32, (rows, GROUP_WIDTH), 0) & (HEADS_PER_GROUP - 1)
    lane_head = lax.broadcasted_iota(jnp.int32, (rows, GROUP_WIDTH), 1) >> LANE_HEAD_SHIFT
    o_diag = jnp.where(row_head == lane_head, acc / den, 0.0).astype(BF16)
    sel = ((lax.broadcasted_iota(jnp.int32, (rows, rows), 1) >> HEAD_SHIFT)
           == lax.broadcasted_iota(jnp.int32, (rows, rows), 0)).astype(BF16)
    attn = jnp.dot(sel, o_diag, preferred_element_type=F32)
    attn_ref[0] = attn[0:t_new].astype(BF16)

    tail_lane = lax.broadcasted_iota(jnp.int32, (GROUP_WIDTH, LANE), 1)
    nt_ref[...] = jnp.zeros(nt_ref.shape, F32)
    for g in range(N_GROUPS):
        c_ref, o_ref = caches[g], outs[g]
        L = c_ref.shape[3]
        for kv in range(2):
            col0 = (2 * g + kv) * GROUP_WIDTH
            nt_ref[LANE - t_new:LANE, :] = kvn_ref[0, :, col0:col0 + GROUP_WIDTH]
            new_t = nt_ref[...].T
            rolled = pltpu.roll(c_ref[0, kv], L - t_new, 1)
            if L > LANE:
                o_ref[0, kv, :, 0:L - LANE] = rolled[:, 0:L - LANE]
            o_ref[0, kv, :, L - LANE:L] = jnp.where(tail_lane >= LANE - t_new, new_t, rolled[:, L - LANE:L])

    ue_ref[0:POOL_STATE_LEN, :] = sp_ref[0]
    ue_ref[POOL_STATE_LEN:POOL_STATE_LEN + t_new, :] = u_ref[0]
    diff_ref[...] = jnp.zeros(diff_ref.shape, F32)
    for tn in range(t_new):
        r = POOL_STATE_LEN + tn
        for gi, w in enumerate(POOL_WINDOWS):
            lanes = slice(gi * POOL_GROUP_WIDTH, (gi + 1) * POOL_GROUP_WIDTH)
            win = jnp.sum(ue_ref[r - w + 1:r + 1, lanes], axis=0, keepdims=True)
            diff_ref[tn:tn + 1, lanes] = win / float(w) - ue_ref[r:r + 1, lanes]
    for gi in range(len(POOL_WINDOWS)):
        lanes = slice(gi * POOL_GROUP_WIDTH, (gi + 1) * POOL_GROUP_WIDTH)
        z = jnp.dot(diff_ref[:, lanes].astype(BF16), wpool_ref[gi], preferred_element_type=F32)
        pooled_ref[0, :, lanes] = (z[0:t_new] * pscale_ref[:, lanes]).astype(BF16)
    po_ref[0] = ue_ref[t_new:t_new + POOL_STATE_LEN, :]


N_ATTN_IN = 5
N_SAMPLE_IN = 9
N_SAMPLE_OUT = 6


def _attn_sample_kernel(*refs, plans, n_seq, t_new):
    n_att = N_GROUPS * N_ATTN_IN
    attn_in = refs[:n_att]
    sample_in = refs[n_att:n_att + N_SAMPLE_IN]
    outs = refs[n_att + N_SAMPLE_IN:]
    attn_out = outs[:2 * N_GROUPS]
    sample_out = outs[2 * N_GROUPS:2 * N_GROUPS + N_SAMPLE_OUT]
    scratch = outs[2 * N_GROUPS + N_SAMPLE_OUT:]
    kv_bufs, sample_scratch = scratch[:2 * N_GROUPS], scratch[2 * N_GROUPS:]
    s = pl.program_id(0)

    for g, (first, count, n_sup, d, nq) in enumerate(plans):
        @pl.when((s >= first) & (s < first + count))
        def _(g=g, first=first, n_sup=n_sup, d=d, nq=nq):
            n0 = ((s - first) // d) % n_sup
            _attn_body(*attn_in[g * N_ATTN_IN:(g + 1) * N_ATTN_IN], attn_out[2 * g], attn_out[2 * g + 1],
                       kv_bufs[2 * g], kv_bufs[2 * g + 1], n0, nq)

    @pl.when(s < n_seq)
    def _():
        _sample_body(*sample_in, *sample_out, *sample_scratch, t_new)


def _attn_and_sample(qs, ks, vs, qbd, kvn, caches, state_pool, u, w_pool, pool_scale, nq_max):
    N, t_new, _ = u.shape
    rows = HEADS_PER_GROUP * t_new
    plans, in_specs, out_specs, out_shape, scratch, args = [], [], [], [], [], []
    first = 0
    for g in range(N_GROUPS):
        B, M, width = qs[g].shape
        d = width // GROUP_WIDTH
        nq = min(nq_max, M // SPAN)
        n_sup = M // (nq * SPAN)
        count = B * n_sup * d
        plans.append((first, count, n_sup, d, nq))

        def block_index(s, first=first, count=count, n_sup=n_sup, d=d):
            local = jnp.clip(s - first, 0, count - 1)
            return local // (n_sup * d), (local // d) % n_sup, local % d

        def cur_map(s, block_index=block_index):
            b, n, r = block_index(s)
            return b, n, r

        def prev_map(s, block_index=block_index, nq=nq):
            b, n, r = block_index(s)
            return b, jnp.maximum(n * nq - 1, 0), r

        cur = pl.BlockSpec((1, nq * SPAN, GROUP_WIDTH), cur_map)
        prev = pl.BlockSpec((1, SPAN, GROUP_WIDTH), prev_map)
        in_specs += [cur, cur, prev, cur, prev]
        args += [qs[g], ks[g], ks[g], vs[g], vs[g]]
        out_specs += [cur, cur]
        out_shape += [jax.ShapeDtypeStruct((B, M, width), BF16), jax.ShapeDtypeStruct((B, M, width), F32)]
        scratch += [pltpu.VMEM(((nq + 1) * SPAN, GROUP_WIDTH), BF16)] * 2
        first += count
    n_steps = max(first, N)

    per_n = lambda shape: pl.BlockSpec((1,) + tuple(shape),
                                       lambda s: (jnp.minimum(s, N - 1),) + (0,) * len(shape))
    cache_specs = [per_n(c.shape[1:]) for c in caches]
    in_specs += ([per_n(qbd.shape[1:]), per_n(kvn.shape[1:])] + cache_specs
                 + [per_n(state_pool.shape[1:]), per_n(u.shape[1:]), _full(w_pool.shape), _full((1, POOL_WIDTH))])
    args += [qbd, kvn, *caches, state_pool, u, w_pool, pool_scale]
    out_specs += ([per_n((t_new, GROUP_WIDTH)), per_n((t_new, POOL_WIDTH))] + cache_specs
                  + [per_n(state_pool.shape[1:])])
    out_shape += ([jax.ShapeDtypeStruct((N, t_new, GROUP_WIDTH), BF16),
                   jax.ShapeDtypeStruct((N, t_new, POOL_WIDTH), BF16)]
                  + [jax.ShapeDtypeStruct(c.shape, F32) for c in caches]
                  + [jax.ShapeDtypeStruct(state_pool.shape, F32)])
    scratch += [pltpu.VMEM((POOL_STATE_LEN + t_new + 5, POOL_WIDTH), F32),
                pltpu.VMEM((rows, POOL_WIDTH), F32),
                pltpu.VMEM((LANE, GROUP_WIDTH), F32)]
    outs = pl.pallas_call(
        functools.partial(_attn_sample_kernel, plans=tuple(plans), n_seq=N, t_new=t_new),
        grid=(n_steps,),
        in_specs=in_specs,
        out_specs=out_specs,
        out_shape=out_shape,
        scratch_shapes=scratch,
        compiler_params=_cparams(1),
        name="attn_and_sample",
    )(*args)
    return [(outs[2 * g], outs[2 * g + 1]) for g in range(N_GROUPS)], outs[2 * N_GROUPS:]


def _route(logits):
    lane = lax.broadcasted_iota(jnp.int32, logits.shape, 1).astype(F32)
    is_grp = (lane >= GROUP_LANE0) & (lane < GROUP_LANE0 + MOE_GROUPS)
    gl = jnp.where(is_grp, logits, NEG)
    gmax = jnp.max(gl, axis=-1, keepdims=True)
    gidx = jnp.min(jnp.where(gl == gmax, lane, BIG_LANE), axis=-1, keepdims=True) - GROUP_LANE0
    gsum = jnp.sum(jnp.where(is_grp, jnp.exp(gl - gmax), 0.0), axis=-1, keepdims=True)
    grp_w = 1.0 / gsum
    lo = gidx * EXPERTS_PER_GROUP
    in_grp = (lane >= lo) & (lane < lo + EXPERTS_PER_GROUP)
    el = jnp.where(in_grp, logits, NEG)
    v1 = jnp.max(el, axis=-1, keepdims=True)
    i1 = jnp.min(jnp.where(el == v1, lane, BIG_LANE), axis=-1, keepdims=True)
    el2 = jnp.where(lane == i1, NEG, el)
    v2 = jnp.max(el2, axis=-1, keepdims=True)
    i2 = jnp.min(jnp.where(el2 == v2, lane, BIG_LANE), axis=-1, keepdims=True)
    e = jnp.exp(v2 - v1)
    w1 = grp_w / (1.0 + e)
    w2 = grp_w * e / (1.0 + e)
    return lane, i1, i2, w1, w2


def _merge_kernel(*refs, n_groups):
    x_ref, mod_ref, g1_ref, g2_ref = refs[0:4]
    n_attn = 2 * n_groups if n_groups > 1 else 1
    attn_refs = refs[4:4 + n_attn]
    rest = refs[4 + n_attn:]
    pooled_ref, wg_ref, wa_ref, wp_ref, wo_ref, wrh_ref, wrc_ref, br_ref = rest[:8]
    x1_ref, h2_ref, route_ref, counts_ref, route_t_ref = rest[8:13]
    scratch_refs = rest[13:]
    stage_refs = scratch_refs[:n_attn] if n_groups > 1 else ()
    carry_ref = scratch_refs[-1]

    x = x_ref[0]
    tm = x.shape[0]
    if n_groups > 1:
        vals = []
        for idx, ref in enumerate(attn_refs):
            d = ATTN_DILATIONS[idx % n_groups]
            if d == 1:
                vals.append(ref[0].astype(F32))
                continue
            stage_ref = stage_refs[idx]
            chunks = GROUP_WIDTH // LANE
            for r in range(d):
                for c in range(chunks):
                    stage_ref[c, pl.ds(r, tm // d, stride=d), :] = (
                        ref[0, :, r * GROUP_WIDTH + c * LANE:r * GROUP_WIDTH + (c + 1) * LANE].astype(F32))
            vals.append(jnp.concatenate([stage_ref[c] for c in range(chunks)], axis=1))
        os_, ls = vals[:n_groups], vals[n_groups:]
        lmax = functools.reduce(jnp.maximum, ls)
        es = [jnp.exp(l - lmax) for l in ls]
        attn = sum(e * o for e, o in zip(es, os_)) / sum(es)
    else:
        attn = attn_refs[0][0]
    a = jnp.dot(attn.astype(BF16), wa_ref[...], preferred_element_type=F32)
    p = jnp.dot(pooled_ref[0], wp_ref[...], preferred_element_type=F32)
    h = _norm_mod(x, g1_ref[...], mod_ref[0, 1], mod_ref[0, 0]).astype(BF16)
    gates = jnp.dot(h, wg_ref[...], preferred_element_type=F32)
    D = x.shape[1]
    merged = jax.nn.sigmoid(gates[:, :D]) * a + jax.nn.sigmoid(gates[:, D:]) * p
    y = jnp.dot(merged.astype(BF16), wo_ref[...], preferred_element_type=F32)
    x1 = x + mod_ref[0, 2] * y
    x1_ref[0] = x1
    h2 = _norm_mod(x1, g2_ref[...], mod_ref[0, 4], mod_ref[0, 3])
    h2_hi = h2.astype(BF16)
    h2_lo = (h2 - h2_hi.astype(F32)).astype(BF16)
    hi_terms = jnp.dot(h2_hi, wrc_ref[...], preferred_element_type=F32)
    logits = (hi_terms[:, :ROUTER_LANES] + jnp.dot(h2_lo, wrh_ref[...], preferred_element_type=F32)
              + hi_terms[:, ROUTER_LANES:]) + br_ref[...]
    lane, i1, i2, w1, w2 = _route(logits)
    h2_ref[0] = h2

    @pl.when((pl.program_id(0) == 0) & (pl.program_id(1) == 0))
    def _():
        carry_ref[...] = jnp.zeros(carry_ref.shape, F32)

    hit = ((lane == i1) | (lane == i2)).astype(BF16)
    ltri = (lax.broadcasted_iota(jnp.int32, (tm, tm), 0) >= lax.broadcasted_iota(jnp.int32, (tm, tm), 1))
    prefix = jnp.dot(ltri.astype(BF16), hit, preferred_element_type=F32) + carry_ref[...]
    rank1 = jnp.sum(jnp.where(lane == i1, prefix, 0.0), axis=-1, keepdims=True) - 1.0
    rank2 = jnp.sum(jnp.where(lane == i2, prefix, 0.0), axis=-1, keepdims=True) - 1.0
    carry_ref[...] = prefix[tm - 1:tm, :]
    counts_ref[...] = jnp.broadcast_to(prefix[tm - 1:tm, :], counts_ref.shape)
    cols = (i1, i2, w1, w2, rank1, rank2)
    route = jnp.zeros(logits.shape, F32)
    for c, col in enumerate(cols):
        route = jnp.where(lane == float(c), col, route)
    route_ref[0] = route
    route_t_ref[...] = route.T[0:SUBLANE, :]


def _merge(x, mod, g1, g2, attn_inputs, pooled, w_gates, w_attn_out, w_pool_out, w_o, wr_hi, wr_hilo, b_r, tm):
    B, S, D = x.shape
    R = mod.shape[2]
    n_groups = len(attn_inputs) // 2 if len(attn_inputs) > 1 else 1
    tok = lambda width: pl.BlockSpec((1, tm, width), lambda b, i: (b, i, 0))
    if R == 1:
        mod_spec = pl.BlockSpec((1, N_COND, 1, D), lambda b, i: (b, 0, 0, 0))
    else:
        mod_spec = pl.BlockSpec((1, N_COND, tm, D), lambda b, i: (b, 0, i, 0))
    weights = (w_gates, w_attn_out, w_pool_out, w_o, wr_hi, wr_hilo, b_r)
    if n_groups > 1:
        attn_specs = [pl.BlockSpec((1, tm // (a.shape[2] // GROUP_WIDTH), a.shape[2]), lambda b, i: (b, i, 0))
                      for a in attn_inputs]
        scratch = [pltpu.VMEM((GROUP_WIDTH // LANE, tm, LANE), F32)] * len(attn_inputs)
    else:
        attn_specs = [tok(GROUP_WIDTH)]
        scratch = []
    n_t = S // tm
    out_specs = [tok(D), tok(D), tok(ROUTER_LANES), _full((SUBLANE, ROUTER_LANES)),
                 pl.BlockSpec((SUBLANE, tm), lambda b, i: (0, b * n_t + i))]
    out_shape = [jax.ShapeDtypeStruct((B, S, D), F32),
                 jax.ShapeDtypeStruct((B, S, D), F32),
                 jax.ShapeDtypeStruct((B, S, ROUTER_LANES), F32),
                 jax.ShapeDtypeStruct((SUBLANE, ROUTER_LANES), F32),
                 jax.ShapeDtypeStruct((SUBLANE, B * S), F32)]
    scratch = scratch + [pltpu.VMEM((1, ROUTER_LANES), F32)]
    return pl.pallas_call(
        functools.partial(_merge_kernel, n_groups=n_groups),
        grid=(B, S // tm),
        in_specs=[tok(D), mod_spec, _full((1, D)), _full((1, D))]
                 + attn_specs + [tok(POOL_WIDTH)]
                 + [_full(w.shape) for w in weights],
        out_specs=out_specs,
        out_shape=out_shape,
        scratch_shapes=scratch,
        compiler_params=_cparams(2),
        name="merge",
    )(x, mod, g1, g2, *attn_inputs, pooled, *weights)


EXPERT_TILE = 512
PAD_CHUNK = 32


def _positions_kernel(base_ref, route_ref, pos_ref):
    cols = route_ref[...].astype(jnp.int32)
    start = jnp.zeros_like(cols)
    for e in range(N_EXPERTS):
        start = jnp.where(cols == e, base_ref[e], start)
    pos_ref[...] = start + pltpu.roll(cols, SUBLANE - 4, 0)


def _positions(route_t, base):
    rows, T = route_t.shape
    grid_spec = pltpu.PrefetchScalarGridSpec(
        num_scalar_prefetch=1, grid=(1,),
        in_specs=[pl.BlockSpec((rows, T), lambda i, base: (0, 0))],
        out_specs=pl.BlockSpec((rows, T), lambda i, base: (0, 0)))
    pos = pl.pallas_call(
        _positions_kernel,
        grid_spec=grid_spec,
        out_shape=jax.ShapeDtypeStruct((rows, T), jnp.int32),
        compiler_params=_cparams(1),
        name="moe_positions",
    )(base.astype(jnp.int32), route_t)
    return pos[0:2]


def _routing_tables(routes, counts, tile):
    cnts = [c[0, :N_EXPERTS].astype(jnp.int32) for c in counts]
    cnt = sum(cnts)
    padded = ((cnt + tile - 1) // tile) * tile
    ends = jnp.cumsum(padded)
    base = ends - padded
    poss = []
    for route in routes:
        poss.append(_positions(route, base))
        base = base + cnts[len(poss) - 1]
    n_pairs = 2 * sum(r.shape[1] for r in routes)
    n_tiles = -(-n_pairs // tile) + N_EXPERTS
    starts = jnp.arange(n_tiles, dtype=jnp.int32) * tile
    n_used = ends[-1] // tile
    tile_expert = jnp.sum(starts[:, None] >= ends[None, :], axis=1).astype(jnp.int32)
    last = jnp.take(tile_expert, n_used - 1)
    tile_expert = jnp.where(jnp.arange(n_tiles) < n_used, tile_expert, last)
    pads = jnp.stack([ends - padded + cnts[0], ends], axis=1).reshape(-1).astype(jnp.int32)
    return poss, tile_expert, n_used.reshape(1).astype(jnp.int32), pads, n_tiles


def _pos_steps(pos, tm):
    steps = pos.shape[1] // tm
    return jnp.transpose(pos.reshape(2, steps, tm), (1, 0, 2)).reshape(steps, 1, 2 * tm)


def _row_copy(src_ref, src_row, dst_ref, dst_row, sem):
    return pltpu.make_async_copy(src_ref.at[pl.ds(src_row, 1)], dst_ref.at[pl.ds(dst_row, 1)], sem)


def _dispatch_kernel(pads_ref, pos_ref, h_ref, *rest, tm, tile, first):
    xs_ref, zero_ref, sem = rest[-3:]

    def zero_fill():
        zero_ref[...] = jnp.zeros(zero_ref.shape, zero_ref.dtype)
        n_tiles = xs_ref.shape[0] // tile
        first_unused = pads_ref[2 * N_EXPERTS - 1] // tile

        def tile_copy(j):
            return pltpu.make_async_copy(zero_ref, xs_ref.at[pl.ds(pl.multiple_of(j * tile, tile), tile)], sem)

        def fill_tile(j, c):
            tile_copy(j).start()
            return c

        def drain_tile(j, c):
            tile_copy(j).wait()
            return c

        lax.fori_loop(first_unused, n_tiles, fill_tile, 0)
        lax.fori_loop(first_unused, n_tiles, drain_tile, 0)

        def chunk_copy(c):
            rows = pl.ds(pl.multiple_of(c * PAD_CHUNK, PAD_CHUNK), PAD_CHUNK)
            return pltpu.make_async_copy(zero_ref.at[pl.ds(0, PAD_CHUNK)], xs_ref.at[rows], sem)

        def for_each_pad_piece(on_row, on_chunk):
            def per_expert(e, carry):
                lo, hi = pads_ref[2 * e], pads_ref[2 * e + 1]
                first_chunk = (lo + PAD_CHUNK - 1) // PAD_CHUNK
                lax.fori_loop(lo, jnp.minimum(first_chunk * PAD_CHUNK, hi), on_row, 0)
                lax.fori_loop(first_chunk, hi // PAD_CHUNK, on_chunk, 0)
                return carry

            lax.fori_loop(0, N_EXPERTS, per_expert, 0)

        def start_row(r, c):
            _row_copy(zero_ref, 0, xs_ref, r, sem).start()
            return c

        def wait_row(r, c):
            _row_copy(zero_ref, 0, xs_ref, r, sem).wait()
            return c

        def start_chunk(c, carry):
            chunk_copy(c).start()
            return carry

        def wait_chunk(c, carry):
            chunk_copy(c).wait()
            return carry

        for_each_pad_piece(start_row, start_chunk)
        for_each_pad_piece(wait_row, wait_chunk)

    if first:
        pl.when(pl.program_id(0) == 0)(zero_fill)

    def issue(k, carry):
        for j in range(SUBLANE):
            i = k * SUBLANE + j
            for prio, p in enumerate((pos_ref[0, 0, i], pos_ref[0, 0, tm + i])):
                pltpu.make_async_copy(h_ref.at[k, pl.ds(j, 1)], xs_ref.at[pl.ds(p, 1)], sem).start(priority=prio)
        return carry

    lax.fori_loop(0, tm // SUBLANE, issue, 0)
    for _ in range(2):
        pltpu.make_async_copy(h_ref, h_ref, sem).wait()


def _dispatch(h2, pos, pads, n_rows, tm, tile, xs_prev=None):
    T, width = h2.shape
    in_specs = [pl.BlockSpec((1, 1, 2 * tm), lambda i, pads: (i, 0, 0), memory_space=pltpu.SMEM),
                pl.BlockSpec((tm // SUBLANE, SUBLANE, width), lambda i, pads: (i, 0, 0))]
    args = [pads, _pos_steps(pos, tm), h2.reshape(T // SUBLANE, SUBLANE, width)]
    aliases = {}
    if xs_prev is not None:
        in_specs.append(pl.BlockSpec(memory_space=pl.ANY))
        args.append(xs_prev)
        aliases = {3: 0}
    grid_spec = pltpu.PrefetchScalarGridSpec(
        num_scalar_prefetch=1,
        grid=(T // tm,),
        in_specs=in_specs,
        out_specs=pl.BlockSpec(memory_space=pl.ANY),
        scratch_shapes=[pltpu.VMEM((tile, width), h2.dtype), pltpu.SemaphoreType.DMA(())],
    )
    return pl.pallas_call(
        functools.partial(_dispatch_kernel, tm=tm, tile=tile, first=xs_prev is None),
        grid_spec=grid_spec,
        out_shape=jax.ShapeDtypeStruct((n_rows, width), h2.dtype),
        input_output_aliases=aliases,
        compiler_params=_cparams(1),
        name="moe_dispatch",
    )(*args)


def _experts_kernel(te_ref, nu_ref, xs_ref, wgu_ref, wd_ref, ys_ref, wgu_bf_ref, wd_bf_ref, *, d_expert):
    j = pl.program_id(0)

    @pl.when(j >= nu_ref[0])
    def _():
        ys_ref[...] = jnp.zeros(ys_ref.shape, F32)

    @pl.when((j == 0) | (te_ref[j] != te_ref[jnp.maximum(j - 1, 0)]))
    def _():
        wgu_bf_ref[...] = wgu_ref[0].astype(BF16)
        wd_bf_ref[...] = wd_ref[0].astype(BF16)

    @pl.when(j < nu_ref[0])
    def _():
        gu = jnp.dot(xs_ref[...].astype(BF16), wgu_bf_ref[...], preferred_element_type=F32)
        act = _silu(gu[:, :d_expert]) * gu[:, d_expert:]
        ys_ref[...] = jnp.dot(act.astype(BF16), wd_bf_ref[...], preferred_element_type=F32)


def _experts(xs, tile_expert, n_used, w_gate_up, w_down, tile):
    n_rows, width = xs.shape
    _, D, two_f = w_gate_up.shape
    grid_spec = pltpu.PrefetchScalarGridSpec(
        num_scalar_prefetch=2,
        grid=(n_rows // tile,),
        in_specs=[pl.BlockSpec((tile, width), lambda j, te, nu: (jnp.minimum(j, nu[0] - 1), 0)),
                  pl.BlockSpec((1, D, two_f), lambda j, te, nu: (te[j], 0, 0)),
                  pl.BlockSpec((1, two_f // 2, D), lambda j, te, nu: (te[j], 0, 0))],
        out_specs=pl.BlockSpec((tile, D), lambda j, te, nu: (j, 0)),
        scratch_shapes=[pltpu.VMEM((D, two_f), BF16), pltpu.VMEM((two_f // 2, D), BF16)],
    )
    return pl.pallas_call(
        functools.partial(_experts_kernel, d_expert=two_f // 2),
        grid_spec=grid_spec,
        out_shape=jax.ShapeDtypeStruct((n_rows, D), F32),
        compiler_params=_cparams(1),
        name="moe_experts",
    )(tile_expert, n_used, xs, w_gate_up, w_down)


def _combine_kernel(pos_ref, pos_next_ref, ys_ref, route_ref, x1_ref, mod_ref, gf_ref, y_ref,
                    ya_ref, yb_ref, sems, *, tm):
    step = pl.program_id(0) * pl.num_programs(1) + pl.program_id(1)
    n_steps = pl.num_programs(0) * pl.num_programs(1)
    slot = step % 2

    def start_gathers(p_ref, s):
        def issue(k, carry):
            for j in range(SUBLANE):
                i = k * SUBLANE + j
                for prio, (buf, p) in enumerate(((ya_ref, p_ref[0, 0, i]), (yb_ref, p_ref[0, 0, tm + i]))):
                    pltpu.make_async_copy(ys_ref.at[pl.ds(p, 1)], buf.at[s, k, pl.ds(j, 1)],
                                          sems.at[s]).start(priority=prio)
            return carry

        lax.fori_loop(0, tm // SUBLANE, issue, 0)

    @pl.when(step == 0)
    def _():
        start_gathers(pos_ref, 0)

    @pl.when(step + 1 < n_steps)
    def _():
        start_gathers(pos_next_ref, 1 - slot)

    for buf in (ya_ref, yb_ref):
        pltpu.make_async_copy(buf.at[slot], buf.at[slot], sems.at[slot]).wait()
    route = route_ref[0]
    lane = lax.broadcasted_iota(jnp.int32, route.shape, 1)
    w1 = jnp.sum(jnp.where(lane == 2, route, 0.0), axis=-1, keepdims=True)
    w2 = jnp.sum(jnp.where(lane == 3, route, 0.0), axis=-1, keepdims=True)
    ya = ya_ref[slot].reshape(tm, ya_ref.shape[-1])
    yb = yb_ref[slot].reshape(tm, yb_ref.shape[-1])
    x2 = x1_ref[0] + mod_ref[0, 5] * (w1 * ya + w2 * yb)
    var = jnp.mean(x2 * x2, axis=-1, keepdims=True)
    y_ref[0] = x2 * lax.rsqrt(var + EPS) * gf_ref[...]


def _combine(ys, pos, route, x1, mod, gf, tm):
    B, S, D = x1.shape
    n_t = S // tm
    tok = lambda width: pl.BlockSpec((1, tm, width), lambda b, i: (b, i, 0))
    last = B * n_t - 1
    pos_spec = lambda ahead: pl.BlockSpec(
        (1, 1, 2 * tm), lambda b, i: (jnp.minimum(b * n_t + i + ahead, last), 0, 0), memory_space=pltpu.SMEM)
    pos_steps = _pos_steps(pos, tm)
    if mod.shape[2] == 1:
        mod_spec = pl.BlockSpec((1, N_COND, 1, D), lambda b, i: (b, 0, 0, 0))
    else:
        mod_spec = pl.BlockSpec((1, N_COND, tm, D), lambda b, i: (b, 0, i, 0))
    return pl.pallas_call(
        functools.partial(_combine_kernel, tm=tm),
        grid=(B, n_t),
        in_specs=[pos_spec(0), pos_spec(1),
                  pl.BlockSpec(memory_space=pl.ANY),
                  tok(ROUTER_LANES), tok(D),
                  mod_spec,
                  _full((1, D))],
        out_specs=tok(D),
        out_shape=jax.ShapeDtypeStruct((B, S, D), F32),
        scratch_shapes=[pltpu.VMEM((2, tm // SUBLANE, SUBLANE, D), F32)] * 2
                       + [pltpu.SemaphoreType.DMA((2,))],
        compiler_params=_cparams(2),
        name="moe_combine",
    )(pos_steps, pos_steps, ys, route, x1, mod, gf)


def _rope_tables(pos):
    half = HEAD_DIM // 2
    inv = ROPE_THETA ** (-jnp.arange(half, dtype=F32) * 2.0 / HEAD_DIM)
    ang = pos.astype(F32)[:, None] * inv[None, :]
    cos, sin = jnp.cos(ang), jnp.sin(ang)
    reps = LANE // HEAD_DIM
    return (jnp.tile(jnp.concatenate([cos, cos], axis=-1), (1, reps)),
            jnp.tile(jnp.concatenate([-sin, sin], axis=-1), (1, reps)))


def kernel(x_prompt, x_sample, cache_kv_w128, cache_kv_w512, cache_kv_w2048, state_pool, c_prompt, c_sample, norm1_g, w_ada, b_ada, w_in, w_attn_out, w_pool, pool_scale, w_pool_out, w_o, norm2_g, w_grp, b_grp, w_exp_router, b_exp_router, w_gate_up, w_down, final_norm_g):
    B, S, D = x_prompt.shape
    N, T, _ = x_sample.shape
    depth = norm1_g.shape[0]
    assert depth == 1, "single trunk layer"
    tm = min(TOKEN_TILE, S)
    assert S % tm == 0 and all(S % (SPAN * d) == 0 for d in ATTN_DILATIONS)

    n_qkvu = 3 * ATTN_WIDTH + POOL_WIDTH
    w_qkvu = w_in[0, :, :n_qkvu].astype(BF16)
    w_gates = w_in[0, :, n_qkvu:].astype(BF16)
    wa, wpo, wo = w_attn_out[0].astype(BF16), w_pool_out[0].astype(BF16), w_o[0].astype(BF16)
    wpool = w_pool[0].astype(BF16)
    pscale = pool_scale[0].reshape(1, POOL_WIDTH)
    w_r = jnp.concatenate([w_exp_router[0], w_grp[0]], axis=1)
    w_r = jnp.pad(w_r, ((0, 0), (0, ROUTER_LANES - w_r.shape[1])))
    wr_hi = w_r.astype(BF16)
    wr_hilo = jnp.concatenate([wr_hi, (w_r - wr_hi.astype(F32)).astype(BF16)], axis=1)
    b_r = jnp.pad(jnp.concatenate([b_exp_router[0], b_grp[0]]), (0, ROUTER_LANES - N_EXPERTS - MOE_GROUPS))
    b_r = b_r.reshape(1, ROUTER_LANES)
    wgu, wd = w_gate_up[0], w_down[0]
    g1, g2, gf = norm1_g[0].reshape(1, D), norm2_g[0].reshape(1, D), final_norm_g.reshape(1, D)

    c_rows = jnp.concatenate([jnp.repeat(c_sample, T, axis=0), c_prompt], axis=0)
    mod = _ada(c_rows, w_ada[0], b_ada[0])
    mod_p = jnp.transpose(mod[:, N * T:], (1, 0, 2)).reshape(B, N_COND, 1, D)
    mod_s = mod[None]

    cos_p, sin_p = _rope_tables(jnp.arange(S, dtype=jnp.int32))
    outs = _in_proj_prompt(x_prompt, mod_p, g1, w_qkvu, cos_p, sin_p, wpool, pscale, tm)
    qs, ks, vs = outs[0:3], outs[3:6], outs[6:9]
    pooled, kv0, kv1, kv2, ptail = outs[9:14]
    pool_prompt = ptail[:, POOL_HALO - POOL_STATE_LEN:, :][None]

    TS = N * T
    pos_s = PAST_LEN + jnp.arange(T, dtype=jnp.int32)
    cos_s, sin_s = _rope_tables(pos_s)
    cos_s, sin_s = jnp.tile(cos_s, (N, 1)), jnp.tile(sin_s, (N, 1))
    xs = x_sample.reshape(1, TS, D)
    q_s, kvn, u_s = _in_proj_sample(xs, mod_s, g1, w_qkvu, cos_s, sin_s)
    eye = jnp.eye(HEADS_PER_GROUP, dtype=BF16)
    qbd = jnp.einsum('ntghe,hk->ngthke', q_s.reshape(N, T, N_GROUPS, HEADS_PER_GROUP, HEAD_DIM), eye)
    qbd = qbd.reshape(N, N_GROUPS, T * HEADS_PER_GROUP, GROUP_WIDTH)
    caches = [jnp.transpose(c[0], (0, 2, 3, 4, 1)).reshape(N, 2, GROUP_WIDTH, c.shape[2])
              for c in (cache_kv_w128, cache_kv_w512, cache_kv_w2048)]

    attn_parts, (attn_s, pooled_s, ko0, ko1, ko2, pool_s) = _attn_and_sample(
        qs, ks, vs, qbd, kvn.reshape(N, T, 2 * ATTN_WIDTH), caches, state_pool[0],
        u_s.reshape(N, T, POOL_WIDTH), wpool, pscale, nq_max=ATTN_BLOCKS_PER_STEP)
    attn_inputs = [o for o, _ in attn_parts] + [l for _, l in attn_parts]
    x1, h2, route, counts, route_t = _merge(x_prompt, mod_p, g1, g2, attn_inputs, pooled, w_gates, wa, wpo, wo,
                                            wr_hi, wr_hilo, b_r, tm)
    x1s, h2s, route_s, counts_s, route_ts = _merge(xs, mod_s, g1, g2, [attn_s.reshape(1, TS, GROUP_WIDTH)],
                                                   pooled_s.reshape(1, TS, POOL_WIDTH), w_gates, wa, wpo, wo,
                                                   wr_hi, wr_hilo, b_r, TS)

    (pos, pos_s), tile_expert, n_used, pads, n_tiles = _routing_tables(
        [route_t, route_ts], [counts, counts_s], EXPERT_TILE)
    sorted_rows = _dispatch(h2.reshape(B * S, D), pos, pads, n_tiles * EXPERT_TILE, tm, EXPERT_TILE)
    sorted_rows = _dispatch(h2s.reshape(TS, D), pos_s, pads, n_tiles * EXPERT_TILE, TS, EXPERT_TILE,
                            xs_prev=sorted_rows)
    ys = _experts(sorted_rows, tile_expert, n_used, wgu, wd, EXPERT_TILE)
    y_prompt = _combine(ys, pos, route, x1, mod_p, gf, tm=tm)
    y_sample = _combine(ys, pos_s, route_s, x1s, mod_s, gf, tm=TS).reshape(N, T, D)

    def kv_shape(a):
        a = a.reshape(a.shape[0], 2, HEADS_PER_GROUP, HEAD_DIM, a.shape[3])
        return jnp.transpose(a, (0, 4, 1, 2, 3))[None]

    return (y_prompt, y_sample, kv_shape(kv0), kv_shape(kv1), kv_shape(kv2), pool_prompt,
            kv_shape(ko0), kv_shape(ko1), kv_shape(ko2), pool_s[None])
```

```python
import functools

import jax
import jax.numpy as jnp
from jax import lax
from jax.experimental import pallas as pl
from jax.experimental.pallas import tpu as pltpu

F32 = jnp.float32
BF16 = jnp.bfloat16

HEAD_DIM = 64
HEADS_PER_GROUP = 4
HEAD_SHIFT = 2
LANE_HEAD_SHIFT = 6
GROUP_WIDTH = HEADS_PER_GROUP * HEAD_DIM
ATTN_WINDOWS = (128, 512, 2048)
ATTN_DILATIONS = (1, 4, 16)
N_GROUPS = 3
SPAN = 128
ATTN_WIDTH = N_GROUPS * GROUP_WIDTH
ROPE_THETA = 10000.0
PAST_LEN = 8192
POOL_WINDOWS = (2, 4, 8, 16)
POOL_GROUP_WIDTH = 128
POOL_WIDTH = 512
POOL_STATE_LEN = 15
POOL_HALO = 16
MOE_GROUPS = 4
EXPERTS_PER_GROUP = 8
N_EXPERTS = 32
N_COND = 6
EPS = 1e-6

LANE = 128
SUBLANE = 8
VMEM_LIMIT_BYTES = 56 * 1024 * 1024

TOKEN_TILE = 512
ATTN_BLOCKS_PER_STEP = 8
ROW_MOVE_TILE = 1024

NEG = -1e30
BIG_LANE = 1e9

ROUTER_LANES = LANE
GROUP_LANE0 = N_EXPERTS


def _cparams(n_axes):
    return pltpu.CompilerParams(dimension_semantics=("arbitrary",) * n_axes,
                                vmem_limit_bytes=VMEM_LIMIT_BYTES)


def _full(shape):
    nd = len(shape)
    return pl.BlockSpec(tuple(shape), lambda *_: (0,) * nd)


def _norm_mod(x, g, scale, shift):
    var = jnp.mean(x * x, axis=-1, keepdims=True)
    return (x * lax.rsqrt(var + EPS) * g) * (1.0 + scale) + shift


def _rope(x, cos, sin):
    lane = lax.broadcasted_iota(jnp.int32, (x.shape[0], LANE), 1)
    first_half = (lane & (HEAD_DIM - 1)) < (HEAD_DIM // 2)
    outs = []
    for c in range(x.shape[1] // LANE):
        xc = x[:, c * LANE:(c + 1) * LANE]
        partner = jnp.where(first_half, pltpu.roll(xc, LANE - HEAD_DIM // 2, 1),
                            pltpu.roll(xc, HEAD_DIM // 2, 1))
        outs.append(xc * cos + partner * sin)
    return jnp.concatenate(outs, axis=1)


def _silu(x):
    return x * jax.nn.sigmoid(x)


def _ada_kernel(c_ref, w_ref, b_ref, o_ref):
    s = _silu(c_ref[...]).astype(BF16)
    o_ref[0] = jnp.dot(s, w_ref[...].astype(BF16), preferred_element_type=F32) + b_ref[...]


def _ada(c_all, w_ada, b_ada):
    rows, d = c_all.shape
    n_out = w_ada.shape[1]
    tn = n_out // N_COND
    return pl.pallas_call(
        _ada_kernel,
        grid=(N_COND,),
        in_specs=[_full((rows, d)),
                  pl.BlockSpec((d, tn), lambda j: (0, j)),
                  pl.BlockSpec((1, tn), lambda j: (0, j))],
        out_specs=pl.BlockSpec((1, rows, tn), lambda j: (j, 0, 0)),
        out_shape=jax.ShapeDtypeStruct((N_COND, rows, tn), F32),
        compiler_params=_cparams(1),
        name="ada",
    )(c_all, w_ada, b_ada.reshape(1, n_out))


def _in_proj_prompt_kernel(x_ref, mod_ref, g1_ref, w_ref, cos_ref, sin_ref, wpool_ref, pscale_ref, *rest,
                           tm, n_tiles, tails):
    qkv_refs = (rest[0:3], rest[3:6], rest[6:9])
    pooled_ref, kv0_ref, kv1_ref, kv2_ref, ptail_ref = rest[9:14]
    stage_refs = rest[14:17]
    ue_ref = rest[17]
    i = pl.program_id(1)
    x = x_ref[0]
    h = _norm_mod(x, g1_ref[...], mod_ref[0, 1], mod_ref[0, 0]).astype(BF16)
    cos = cos_ref[...]
    sin = sin_ref[...]
    y = jnp.dot(h, w_ref[...], preferred_element_type=F32)
    q = _rope(y[:, :ATTN_WIDTH], cos, sin) * (HEAD_DIM ** -0.5)
    k = _rope(y[:, ATTN_WIDTH:2 * ATTN_WIDTH], cos, sin)
    v = y[:, 2 * ATTN_WIDTH:3 * ATTN_WIDTH]
    u = y[:, 3 * ATTN_WIDTH:3 * ATTN_WIDTH + POOL_WIDTH]

    chunks = GROUP_WIDTH // LANE
    for val, out_refs, stage_ref in zip((q, k, v), qkv_refs, stage_refs):
        for g, d in enumerate(ATTN_DILATIONS):
            gl = slice(g * GROUP_WIDTH, (g + 1) * GROUP_WIDTH)
            if d == 1:
                out_refs[g][0] = val[:, gl].astype(BF16)
                continue
            for c in range(chunks):
                stage_ref[g * chunks + c] = val[:, g * GROUP_WIDTH + c * LANE:g * GROUP_WIDTH + (c + 1) * LANE]
            for r in range(d):
                for c in range(chunks):
                    out_refs[g][0, :, r * GROUP_WIDTH + c * LANE:r * GROUP_WIDTH + (c + 1) * LANE] = (
                        stage_ref[g * chunks + c, pl.ds(r, tm // d, stride=d), :].astype(BF16))

    for g, kv_ref in enumerate((kv0_ref, kv1_ref, kv2_ref)):
        first_tile, rows = tails[g]
        lo = tm - rows

        @pl.when(i >= first_tile)
        def _(kv_ref=kv_ref, g=g, lo=lo):
            kv_ref[0, 0] = k[lo:, g * GROUP_WIDTH:(g + 1) * GROUP_WIDTH].T
            kv_ref[0, 1] = v[lo:, g * GROUP_WIDTH:(g + 1) * GROUP_WIDTH].T

    @pl.when(i == 0)
    def _():
        ue_ref[0:POOL_HALO, :] = jnp.zeros((POOL_HALO, POOL_WIDTH), F32)

    ue_ref[POOL_HALO:, :] = u
    pos1 = (i * tm + 1 + lax.broadcasted_iota(jnp.int32, (tm, 1), 0)).astype(F32)
    for gi, w in enumerate(POOL_WINDOWS):
        lanes = slice(gi * POOL_GROUP_WIDTH, (gi + 1) * POOL_GROUP_WIDTH)
        acc = ue_ref[pl.ds(POOL_HALO, tm), lanes]
        for j in range(1, w):
            acc = acc + ue_ref[pl.ds(POOL_HALO - j, tm), lanes]
        mean = acc / jnp.minimum(float(w), pos1)
        z = jnp.dot((mean - u[:, lanes]).astype(BF16), wpool_ref[gi], preferred_element_type=F32)
        pooled_ref[0, :, lanes] = (z * pscale_ref[:, lanes]).astype(BF16)
    ue_ref[0:POOL_HALO, :] = u[tm - POOL_HALO:, :]

    @pl.when(i == n_tiles - 1)
    def _():
        ptail_ref[0] = u[tm - POOL_HALO:, :]


def _in_proj_prompt(x, mod, g1, w_qkvu, cos, sin, w_pool, pool_scale, tm):
    B, S, D = x.shape
    n_tiles = S // tm
    n_w = w_qkvu.shape[1]
    tails, kv_shapes, kv_specs = [], [], []
    for W in ATTN_WINDOWS:
        Wg = min(W, S)
        if Wg >= tm:
            first = n_tiles - Wg // tm
            rows = tm
        else:
            first = n_tiles - 1
            rows = Wg
        tails.append((first, rows))
        kv_shapes.append(jax.ShapeDtypeStruct((B, 2, GROUP_WIDTH, Wg), F32))
        kv_specs.append(pl.BlockSpec((1, 2, GROUP_WIDTH, rows),
                                     lambda b, i, first=first: (b, 0, 0, jnp.maximum(i - first, 0))))
    tok = lambda width: pl.BlockSpec((1, tm, width), lambda b, i: (b, i, 0))
    dil_specs = [pl.BlockSpec((1, tm // d, d * GROUP_WIDTH), lambda b, i: (b, i, 0)) for d in ATTN_DILATIONS]
    dil_shapes = [jax.ShapeDtypeStruct((B, S // d, d * GROUP_WIDTH), BF16) for d in ATTN_DILATIONS]
    outs = pl.pallas_call(
        functools.partial(_in_proj_prompt_kernel, tm=tm, n_tiles=n_tiles, tails=tuple(tails)),
        grid=(B, n_tiles),
        in_specs=[tok(D),
                  pl.BlockSpec((1, N_COND, 1, D), lambda b, i: (b, 0, 0, 0)),
                  _full((1, D)),
                  _full((D, n_w)),
                  pl.BlockSpec((tm, LANE), lambda b, i: (i, 0)),
                  pl.BlockSpec((tm, LANE), lambda b, i: (i, 0)),
                  _full(w_pool.shape),
                  _full((1, POOL_WIDTH))],
        out_specs=dil_specs * 3 + [tok(POOL_WIDTH)] + kv_specs
                  + [pl.BlockSpec((1, POOL_HALO, POOL_WIDTH), lambda b, i: (b, 0, 0))],
        out_shape=dil_shapes * 3
                  + [jax.ShapeDtypeStruct((B, S, POOL_WIDTH), BF16)] + kv_shapes
                  + [jax.ShapeDtypeStruct((B, POOL_HALO, POOL_WIDTH), F32)],
        scratch_shapes=[pltpu.VMEM((ATTN_WIDTH // LANE, tm, LANE), F32)] * 3
                       + [pltpu.VMEM((tm + POOL_HALO, POOL_WIDTH), F32)],
        compiler_params=_cparams(2),
        name="in_proj_prompt",
    )(x, mod, g1, w_qkvu, cos, sin, w_pool, pool_scale)
    return outs


def _in_proj_sample_kernel(x_ref, mod_ref, g1_ref, w_ref, cos_ref, sin_ref, q_ref, kvn_ref, u_ref):
    h = _norm_mod(x_ref[0], g1_ref[...], mod_ref[0, 1], mod_ref[0, 0]).astype(BF16)
    y = jnp.dot(h, w_ref[...], preferred_element_type=F32)
    cos = cos_ref[...]
    sin = sin_ref[...]
    q = _rope(y[:, :ATTN_WIDTH], cos, sin) * (HEAD_DIM ** -0.5)
    k = _rope(y[:, ATTN_WIDTH:2 * ATTN_WIDTH], cos, sin)
    v = y[:, 2 * ATTN_WIDTH:3 * ATTN_WIDTH]
    q_ref[...] = q.astype(BF16)
    for g in range(N_GROUPS):
        gl = slice(g * GROUP_WIDTH, (g + 1) * GROUP_WIDTH)
        kvn_ref[:, 2 * g * GROUP_WIDTH:(2 * g + 1) * GROUP_WIDTH] = k[:, gl]
        kvn_ref[:, (2 * g + 1) * GROUP_WIDTH:(2 * g + 2) * GROUP_WIDTH] = v[:, gl]
    u_ref[...] = y[:, 3 * ATTN_WIDTH:3 * ATTN_WIDTH + POOL_WIDTH]


def _in_proj_sample(x, mod, g1, w_qkvu, cos, sin):
    _, T, D = x.shape
    n_w = w_qkvu.shape[1]
    return pl.pallas_call(
        _in_proj_sample_kernel,
        grid=(1,),
        in_specs=[_full((1, T, D)), _full((1, N_COND, T, D)), _full((1, D)), _full((D, n_w)),
                  _full((T, LANE)), _full((T, LANE))],
        out_specs=[_full((T, ATTN_WIDTH)), _full((T, 2 * ATTN_WIDTH)), _full((T, POOL_WIDTH))],
        out_shape=[jax.ShapeDtypeStruct((T, ATTN_WIDTH), BF16),
                   jax.ShapeDtypeStruct((T, 2 * ATTN_WIDTH), F32),
                   jax.ShapeDtypeStruct((T, POOL_WIDTH), F32)],
        compiler_params=_cparams(1),
        name="in_proj_sample",
    )(x, mod, g1, w_qkvu, cos, sin)


def _attn_body(q_ref, kc_ref, kp_ref, vc_ref, vp_ref, o_ref, lse_ref, kbuf, vbuf, n0, nq):
    kbuf[0:SPAN, :] = kp_ref[0]
    kbuf[SPAN:, :] = kc_ref[0]
    vbuf[0:SPAN, :] = vp_ref[0]
    vbuf[SPAN:, :] = vc_ref[0]
    qi = lax.broadcasted_iota(jnp.int32, (SPAN, 2 * SPAN), 0)
    kj = lax.broadcasted_iota(jnp.int32, (SPAN, 2 * SPAN), 1)
    band = (kj >= qi) & (kj <= qi + SPAN)
    band_first = band & (kj >= jnp.where(n0 > 0, 0, SPAN))
    lane = lax.broadcasted_iota(jnp.int32, (SPAN, LANE), 1)
    low_head = lane < HEAD_DIM
    for j in range(nq):
        valid = band_first if j == 0 else band
        rows = slice(j * SPAN, (j + 1) * SPAN)
        krows = slice(j * SPAN, (j + 2) * SPAN)
        for c in range(GROUP_WIDTH // LANE):
            cl = slice(c * LANE, (c + 1) * LANE)
            q = q_ref[0, rows, cl]
            kk = kbuf[krows, cl]
            vv = vbuf[krows, cl]
            o_pair, lse_pair = [], []
            for hh in range(2):
                mask_h = low_head if hh == 0 else jnp.logical_not(low_head)
                qm = jnp.where(mask_h, q, jnp.zeros_like(q))
                s = lax.dot_general(qm, kk, (((1,), (1,)), ((), ())), preferred_element_type=F32)
                s = jnp.where(valid, s, NEG)
                m = jnp.max(s, axis=-1, keepdims=True)
                p = jnp.exp(s - m)
                den = jnp.sum(p, axis=-1, keepdims=True)
                o_pair.append(jnp.dot(p.astype(BF16), vv, preferred_element_type=F32) / den)
                lse_pair.append(m + jnp.log(den))
            o_ref[0, rows, cl] = jnp.where(low_head, o_pair[0], o_pair[1]).astype(BF16)
            lse_ref[0, rows, cl] = jnp.where(low_head, lse_pair[0], lse_pair[1])


def _sample_body(qbd_ref, kvn_ref, c0_ref, c1_ref, c2_ref, sp_ref, u_ref, wpool_ref, pscale_ref,
                 attn_ref, pooled_ref, o0_ref, o1_ref, o2_ref, po_ref, ue_ref, diff_ref, nt_ref, t_new):
    rows = HEADS_PER_GROUP * t_new
    caches = (c0_ref, c1_ref, c2_ref)
    outs = (o0_ref, o1_ref, o2_ref)

    s_cache, s_new, valid_new = [], [], []
    m = jnp.full((rows, 1), NEG, F32)
    for g in range(N_GROUPS):
        d = ATTN_DILATIONS[g]
        c_ref = caches[g]
        L = c_ref.shape[3]
        qb = qbd_ref[0, g]
        s = jnp.dot(qb, c_ref[0, 0].astype(BF16), preferred_element_type=F32)
        t_row = lax.broadcasted_iota(jnp.int32, (rows, L), 0) >> HEAD_SHIFT
        delta = L + t_row - lax.broadcasted_iota(jnp.int32, (rows, L), 1)
        ok = ((delta & (d - 1)) == 0) & (delta <= SPAN * d)
        s = jnp.where(ok, s, NEG)
        s_cache.append(s)
        m = jnp.maximum(m, jnp.max(s, axis=-1, keepdims=True))
        qf = qb.astype(F32)
        t_col = lax.broadcasted_iota(jnp.int32, (rows, 1), 0) >> HEAD_SHIFT
        sn, okn = [], []
        for tn in range(t_new):
            kn = kvn_ref[0, tn:tn + 1, 2 * g * GROUP_WIDTH:(2 * g + 1) * GROUP_WIDTH]
            kn = kn.astype(BF16).astype(F32)
            dn = t_col - tn
            ok_n = (dn >= 0) & ((dn & (d - 1)) == 0)
            s1 = jnp.where(ok_n, jnp.sum(qf * kn, axis=-1, keepdims=True), NEG)
            sn.append(s1)
            okn.append(ok_n)
            m = jnp.maximum(m, s1)
        s_new.append(sn)
        valid_new.append(okn)

    den = jnp.zeros((rows, 1), F32)
    acc = jnp.zeros((rows, GROUP_WIDTH), F32)
    for g in range(N_GROUPS):
        c_ref = caches[g]
        p = jnp.exp(s_cache[g] - m)
        den = den + jnp.sum(p, axis=-1, keepdims=True)
        acc = acc + lax.dot_general(p.astype(BF16), c_ref[0, 1].astype(BF16), (((1,), (1,)), ((), ())),
                                    preferred_element_type=F32)
        for tn in range(t_new):
            pn = jnp.exp(s_new[g][tn] - m)
            den = den + pn
            vn = kvn_ref[0, tn:tn + 1, (2 * g + 1) * GROUP_WIDTH:(2 * g + 2) * GROUP_WIDTH]
            acc = acc + pn * vn
    row_head = lax.broadcasted_iota(jnp.int32, (rows, GROUP_WIDTH), 0) & (HEADS_PER_GROUP - 1)
    lane_head = lax.broadcasted_iota(jnp.int32, (rows, GROUP_WIDTH), 1) >> LANE_HEAD_SHIFT
    o_diag = jnp.where(row_head == lane_head, acc / den, 0.0).astype(BF16)
    sel = ((lax.broadcasted_iota(jnp.int32, (rows, rows), 1) >> HEAD_SHIFT)
           == lax.broadcasted_iota(jnp.int32, (rows, rows), 0)).astype(BF16)
    attn = jnp.dot(sel, o_diag, preferred_element_type=F32)
    attn_ref[0] = attn[0:t_new].astype(BF16)

    tail_lane = lax.broadcasted_iota(jnp.int32, (GROUP_WIDTH, LANE), 1)
    nt_ref[...] = jnp.zeros(nt_ref.shape, F32)
    for g in range(N_GROUPS):
        c_ref, o_ref = caches[g], outs[g]
        L = c_ref.shape[3]
        for kv in range(2):
            col0 = (2 * g + kv) * GROUP_WIDTH
            nt_ref[LANE - t_new:LANE, :] = kvn_ref[0, :, col0:col0 + GROUP_WIDTH]
            new_t = nt_ref[...].T
            rolled = pltpu.roll(c_ref[0, kv], L - t_new, 1)
            if L > LANE:
                o_ref[0, kv, :, 0:L - LANE] = rolled[:, 0:L - LANE]
            o_ref[0, kv, :, L - LANE:L] = jnp.where(tail_lane >= LANE - t_new, new_t, rolled[:, L - LANE:L])

    ue_ref[0:POOL_STATE_LEN, :] = sp_ref[0]
    ue_ref[POOL_STATE_LEN:POOL_STATE_LEN + t_new, :] = u_ref[0]
    diff_ref[...] = jnp.zeros(diff_ref.shape, F32)
    for tn in range(t_new):
        r = POOL_STATE_LEN + tn
        for gi, w in enumerate(POOL_WINDOWS):
            lanes = slice(gi * POOL_GROUP_WIDTH, (gi + 1) * POOL_GROUP_WIDTH)
            win = jnp.sum(ue_ref[r - w + 1:r + 1, lanes], axis=0, keepdims=True)
            diff_ref[tn:tn + 1, lanes] = win / float(w) - ue_ref[r:r + 1, lanes]
    for gi in range(len(POOL_WINDOWS)):
        lanes = slice(gi * POOL_GROUP_WIDTH, (gi + 1) * POOL_GROUP_WIDTH)
        z = jnp.dot(diff_ref[:, lanes].astype(BF16), wpool_ref[gi], preferred_element_type=F32)
        pooled_ref[0, :, lanes] = (z[0:t_new] * pscale_ref[:, lanes]).astype(BF16)
    po_ref[0] = ue_ref[t_new:t_new + POOL_STATE_LEN, :]


N_ATTN_IN = 5
N_SAMPLE_IN = 9
N_SAMPLE_OUT = 6


def _attn_sample_kernel(*refs, plans, n_seq, t_new):
    n_att = N_GROUPS * N_ATTN_IN
    attn_in = refs[:n_att]
    sample_in = refs[n_att:n_att + N_SAMPLE_IN]
    outs = refs[n_att + N_SAMPLE_IN:]
    attn_out = outs[:2 * N_GROUPS]
    sample_out = outs[2 * N_GROUPS:2 * N_GROUPS + N_SAMPLE_OUT]
    scratch = outs[2 * N_GROUPS + N_SAMPLE_OUT:]
    kv_bufs, sample_scratch = scratch[:2 * N_GROUPS], scratch[2 * N_GROUPS:]
    s = pl.program_id(0)

    for g, (first, count, n_sup, d, nq) in enumerate(plans):
        @pl.when((s >= first) & (s < first + count))
        def _(g=g, first=first, n_sup=n_sup, d=d, nq=nq):
            n0 = ((s - first) // d) % n_sup
            _attn_body(*attn_in[g * N_ATTN_IN:(g + 1) * N_ATTN_IN], attn_out[2 * g], attn_out[2 * g + 1],
                       kv_bufs[2 * g], kv_bufs[2 * g + 1], n0, nq)

    @pl.when(s < n_seq)
    def _():
        _sample_body(*sample_in, *sample_out, *sample_scratch, t_new)


def _attn_and_sample(qs, ks, vs, qbd, kvn, caches, state_pool, u, w_pool, pool_scale, nq_max):
    N, t_new, _ = u.shape
    rows = HEADS_PER_GROUP * t_new
    plans, in_specs, out_specs, out_shape, scratch, args = [], [], [], [], [], []
    first = 0
    for g in range(N_GROUPS):
        B, M, width = qs[g].shape
        d = width // GROUP_WIDTH
        nq = min(nq_max, M // SPAN)
        n_sup = M // (nq * SPAN)
        count = B * n_sup * d
        plans.append((first, count, n_sup, d, nq))

        def block_index(s, first=first, count=count, n_sup=n_sup, d=d):
            local = jnp.clip(s - first, 0, count - 1)
            return local // (n_sup * d), (local // d) % n_sup, local % d

        def cur_map(s, block_index=block_index):
            b, n, r = block_index(s)
            return b, n, r

        def prev_map(s, block_index=block_index, nq=nq):
            b, n, r = block_index(s)
            return b, jnp.maximum(n * nq - 1, 0), r

        cur = pl.BlockSpec((1, nq * SPAN, GROUP_WIDTH), cur_map)
        prev = pl.BlockSpec((1, SPAN, GROUP_WIDTH), prev_map)
        in_specs += [cur, cur, prev, cur, prev]
        args += [qs[g], ks[g], ks[g], vs[g], vs[g]]
        out_specs += [cur, cur]
        out_shape += [jax.ShapeDtypeStruct((B, M, width), BF16), jax.ShapeDtypeStruct((B, M, width), F32)]
        scratch += [pltpu.VMEM(((nq + 1) * SPAN, GROUP_WIDTH), BF16)] * 2
        first += count
    n_steps = max(first, N)

    per_n = lambda shape: pl.BlockSpec((1,) + tuple(shape),
                                       lambda s: (jnp.minimum(s, N - 1),) + (0,) * len(shape))
    cache_specs = [per_n(c.shape[1:]) for c in caches]
    in_specs += ([per_n(qbd.shape[1:]), per_n(kvn.shape[1:])] + cache_specs
                 + [per_n(state_pool.shape[1:]), per_n(u.shape[1:]), _full(w_pool.shape), _full((1, POOL_WIDTH))])
    args += [qbd, kvn, *caches, state_pool, u, w_pool, pool_scale]
    out_specs += ([per_n((t_new, GROUP_WIDTH)), per_n((t_new, POOL_WIDTH))] + cache_specs
                  + [per_n(state_pool.shape[1:])])
    out_shape += ([jax.ShapeDtypeStruct((N, t_new, GROUP_WIDTH), BF16),
                   jax.ShapeDtypeStruct((N, t_new, POOL_WIDTH), BF16)]
                  + [jax.ShapeDtypeStruct(c.shape, F32) for c in caches]
                  + [jax.ShapeDtypeStruct(state_pool.shape, F32)])
    scratch += [pltpu.VMEM((POOL_STATE_LEN + t_new + 5, POOL_WIDTH), F32),
                pltpu.VMEM((rows, POOL_WIDTH), F32),
                pltpu.VMEM((LANE, GROUP_WIDTH), F32)]
    outs = pl.pallas_call(
        functools.partial(_attn_sample_kernel, plans=tuple(plans), n_seq=N, t_new=t_new),
        grid=(n_steps,),
        in_specs=in_specs,
        out_specs=out_specs,
        out_shape=out_shape,
        scratch_shapes=scratch,
        compiler_params=_cparams(1),
        name="attn_and_sample",
    )(*args)
    return [(outs[2 * g], outs[2 * g + 1]) for g in range(N_GROUPS)], outs[2 * N_GROUPS:]


def _route(logits):
    lane = lax.broadcasted_iota(jnp.int32, logits.shape, 1).astype(F32)
    is_grp = (lane >= GROUP_LANE0) & (lane < GROUP_LANE0 + MOE_GROUPS)
    gl = jnp.where(is_grp, logits, NEG)
    gmax = jnp.max(gl, axis=-1, keepdims=True)
    gidx = jnp.min(jnp.where(gl == gmax, lane, BIG_LANE), axis=-1, keepdims=True) - GROUP_LANE0
    gsum = jnp.sum(jnp.where(is_grp, jnp.exp(gl - gmax), 0.0), axis=-1, keepdims=True)
    grp_w = 1.0 / gsum
    lo = gidx * EXPERTS_PER_GROUP
    in_grp = (lane >= lo) & (lane < lo + EXPERTS_PER_GROUP)
    el = jnp.where(in_grp, logits, NEG)
    v1 = jnp.max(el, axis=-1, keepdims=True)
    i1 = jnp.min(jnp.where(el == v1, lane, BIG_LANE), axis=-1, keepdims=True)
    el2 = jnp.where(lane == i1, NEG, el)
    v2 = jnp.max(el2, axis=-1, keepdims=True)
    i2 = jnp.min(jnp.where(el2 == v2, lane, BIG_LANE), axis=-1, keepdims=True)
    e = jnp.exp(v2 - v1)
    w1 = grp_w / (1.0 + e)
    w2 = grp_w * e / (1.0 + e)
    return lane, i1, i2, w1, w2


def _merge_kernel(*refs, n_groups):
    x_ref, mod_ref, g1_ref, g2_ref = refs[0:4]
    n_attn = 2 * n_groups if n_groups > 1 else 1
    attn_refs = refs[4:4 + n_attn]
    rest = refs[4 + n_attn:]
    pooled_ref, wg_ref, wa_ref, wp_ref, wo_ref, wrh_ref, wrc_ref, br_ref = rest[:8]
    x1_ref, h2_ref, route_ref, counts_ref, route_t_ref = rest[8:13]
    scratch_refs = rest[13:]
    stage_refs = scratch_refs[:n_attn] if n_groups > 1 else ()
    carry_ref = scratch_refs[-1]

    x = x_ref[0]
    tm = x.shape[0]
    if n_groups > 1:
        vals = []
        for idx, ref in enumerate(attn_refs):
            d = ATTN_DILATIONS[idx % n_groups]
            if d == 1:
                vals.append(ref[0].astype(F32))
                continue
            stage_ref = stage_refs[idx]
            chunks = GROUP_WIDTH // LANE
            for r in range(d):
                for c in range(chunks):
                    stage_ref[c, pl.ds(r, tm // d, stride=d), :] = (
                        ref[0, :, r * GROUP_WIDTH + c * LANE:r * GROUP_WIDTH + (c + 1) * LANE].astype(F32))
            vals.append(jnp.concatenate([stage_ref[c] for c in range(chunks)], axis=1))
        os_, ls = vals[:n_groups], vals[n_groups:]
        lmax = functools.reduce(jnp.maximum, ls)
        es = [jnp.exp(l - lmax) for l in ls]
        attn = sum(e * o for e, o in zip(es, os_)) / sum(es)
    else:
        attn = attn_refs[0][0]
    a = jnp.dot(attn.astype(BF16), wa_ref[...], preferred_element_type=F32)
    p = jnp.dot(pooled_ref[0], wp_ref[...], preferred_element_type=F32)
    h = _norm_mod(x, g1_ref[...], mod_ref[0, 1], mod_ref[0, 0]).astype(BF16)
    gates = jnp.dot(h, wg_ref[...], preferred_element_type=F32)
    D = x.shape[1]
    merged = jax.nn.sigmoid(gates[:, :D]) * a + jax.nn.sigmoid(gates[:, D:]) * p
    y = jnp.dot(merged.astype(BF16), wo_ref[...], preferred_element_type=F32)
    x1 = x + mod_ref[0, 2] * y
    x1_ref[0] = x1
    h2 = _norm_mod(x1, g2_ref[...], mod_ref[0, 4], mod_ref[0, 3])
    h2_hi = h2.astype(BF16)
    h2_lo = (h2 - h2_hi.astype(F32)).astype(BF16)
    hi_terms = jnp.dot(h2_hi, wrc_ref[...], preferred_element_type=F32)
    logits = (hi_terms[:, :ROUTER_LANES] + jnp.dot(h2_lo, wrh_ref[...], preferred_element_type=F32)
              + hi_terms[:, ROUTER_LANES:]) + br_ref[...]
    lane, i1, i2, w1, w2 = _route(logits)
    h2_ref[0] = h2

    @pl.when((pl.program_id(0) == 0) & (pl.program_id(1) == 0))
    def _():
        carry_ref[...] = jnp.zeros(carry_ref.shape, F32)

    hit = ((lane == i1) | (lane == i2)).astype(BF16)
    ltri = (lax.broadcasted_iota(jnp.int32, (tm, tm), 0) >= lax.broadcasted_iota(jnp.int32, (tm, tm), 1))
    prefix = jnp.dot(ltri.astype(BF16), hit, preferred_element_type=F32) + carry_ref[...]
    rank1 = jnp.sum(jnp.where(lane == i1, prefix, 0.0), axis=-1, keepdims=True) - 1.0
    rank2 = jnp.sum(jnp.where(lane == i2, prefix, 0.0), axis=-1, keepdims=True) - 1.0
    carry_ref[...] = prefix[tm - 1:tm, :]
    counts_ref[...] = jnp.broadcast_to(prefix[tm - 1:tm, :], counts_ref.shape)
    cols = (i1, i2, w1, w2, rank1, rank2)
    route = jnp.zeros(logits.shape, F32)
    for c, col in enumerate(cols):
        route = jnp.where(lane == float(c), col, route)
    route_ref[0] = route
    route_t_ref[...] = route.T[0:SUBLANE, :]


def _merge(x, mod, g1, g2, attn_inputs, pooled, w_gates, w_attn_out, w_pool_out, w_o, wr_hi, wr_hilo, b_r, tm):
    B, S, D = x.shape
    R = mod.shape[2]
    n_groups = len(attn_inputs) // 2 if len(attn_inputs) > 1 else 1
    tok = lambda width: pl.BlockSpec((1, tm, width), lambda b, i: (b, i, 0))
    if R == 1:
        mod_spec = pl.BlockSpec((1, N_COND, 1, D), lambda b, i: (b, 0, 0, 0))
    else:
        mod_spec = pl.BlockSpec((1, N_COND, tm, D), lambda b, i: (b, 0, i, 0))
    weights = (w_gates, w_attn_out, w_pool_out, w_o, wr_hi, wr_hilo, b_r)
    if n_groups > 1:
        attn_specs = [pl.BlockSpec((1, tm // (a.shape[2] // GROUP_WIDTH), a.shape[2]), lambda b, i: (b, i, 0))
                      for a in attn_inputs]
        scratch = [pltpu.VMEM((GROUP_WIDTH // LANE, tm, LANE), F32)] * len(attn_inputs)
    else:
        attn_specs = [tok(GROUP_WIDTH)]
        scratch = []
    n_t = S // tm
    out_specs = [tok(D), tok(D), tok(ROUTER_LANES), _full((SUBLANE, ROUTER_LANES)),
                 pl.BlockSpec((SUBLANE, tm), lambda b, i: (0, b * n_t + i))]
    out_shape = [jax.ShapeDtypeStruct((B, S, D), F32),
                 jax.ShapeDtypeStruct((B, S, D), F32),
                 jax.ShapeDtypeStruct((B, S, ROUTER_LANES), F32),
                 jax.ShapeDtypeStruct((SUBLANE, ROUTER_LANES), F32),
                 jax.ShapeDtypeStruct((SUBLANE, B * S), F32)]
    scratch = scratch + [pltpu.VMEM((1, ROUTER_LANES), F32)]
    return pl.pallas_call(
        functools.partial(_merge_kernel, n_groups=n_groups),
        grid=(B, S // tm),
        in_specs=[tok(D), mod_spec, _full((1, D)), _full((1, D))]
                 + attn_specs + [tok(POOL_WIDTH)]
                 + [_full(w.shape) for w in weights],
        out_specs=out_specs,
        out_shape=out_shape,
        scratch_shapes=scratch,
        compiler_params=_cparams(2),
        name="merge",
    )(x, mod, g1, g2, *attn_inputs, pooled, *weights)


EXPERT_TILE = 512
PAD_CHUNK = 32


def _positions_kernel(base_ref, route_ref, pos_ref):
    cols = route_ref[...].astype(jnp.int32)
    start = jnp.zeros_like(cols)
    for e in range(N_EXPERTS):
        start = jnp.where(cols == e, base_ref[e], start)
    pos_ref[...] = start + pltpu.roll(cols, SUBLANE - 4, 0)


def _positions(route_t, base):
    rows, T = route_t.shape
    grid_spec = pltpu.PrefetchScalarGridSpec(
        num_scalar_prefetch=1, grid=(1,),
        in_specs=[pl.BlockSpec((rows, T), lambda i, base: (0, 0))],
        out_specs=pl.BlockSpec((rows, T), lambda i, base: (0, 0)))
    pos = pl.pallas_call(
        _positions_kernel,
        grid_spec=grid_spec,
        out_shape=jax.ShapeDtypeStruct((rows, T), jnp.int32),
        compiler_params=_cparams(1),
        name="moe_positions",
    )(base.astype(jnp.int32), route_t)
    return pos[0:2]


def _routing_tables(routes, counts, tile):
    cnts = [c[0, :N_EXPERTS].astype(jnp.int32) for c in counts]
    cnt = sum(cnts)
    padded = ((cnt + tile - 1) // tile) * tile
    ends = jnp.cumsum(padded)
    base = ends - padded
    poss = []
    for route in routes:
        poss.append(_positions(route, base))
        base = base + cnts[len(poss) - 1]
    n_pairs = 2 * sum(r.shape[1] for r in routes)
    n_tiles = -(-n_pairs // tile) + N_EXPERTS
    starts = jnp.arange(n_tiles, dtype=jnp.int32) * tile
    n_used = ends[-1] // tile
    tile_expert = jnp.sum(starts[:, None] >= ends[None, :], axis=1).astype(jnp.int32)
    last = jnp.take(tile_expert, n_used - 1)
    tile_expert = jnp.where(jnp.arange(n_tiles) < n_used, tile_expert, last)
    pads = jnp.stack([ends - padded + cnts[0], ends], axis=1).reshape(-1).astype(jnp.int32)
    return poss, tile_expert, n_used.reshape(1).astype(jnp.int32), pads, n_tiles


def _pos_steps(pos, tm):
    steps = pos.shape[1] // tm
    return jnp.transpose(pos.reshape(2, steps, tm), (1, 0, 2)).reshape(steps, 1, 2 * tm)


def _row_copy(src_ref, src_row, dst_ref, dst_row, sem):
    return pltpu.make_async_copy(src_ref.at[pl.ds(src_row, 1)], dst_ref.at[pl.ds(dst_row, 1)], sem)


def _dispatch_kernel(pads_ref, pos_ref, h_ref, *rest, tm, tile, first):
    xs_ref, zero_ref, sem = rest[-3:]

    def zero_fill():
        zero_ref[...] = jnp.zeros(zero_ref.shape, zero_ref.dtype)
        n_tiles = xs_ref.shape[0] // tile
        first_unused = pads_ref[2 * N_EXPERTS - 1] // tile

        def tile_copy(j):
            return pltpu.make_async_copy(zero_ref, xs_ref.at[pl.ds(pl.multiple_of(j * tile, tile), tile)], sem)

        def fill_tile(j, c):
            tile_copy(j).start()
            return c

        def drain_tile(j, c):
            tile_copy(j).wait()
            return c

        lax.fori_loop(first_unused, n_tiles, fill_tile, 0)
        lax.fori_loop(first_unused, n_tiles, drain_tile, 0)

        def chunk_copy(c):
            rows = pl.ds(pl.multiple_of(c * PAD_CHUNK, PAD_CHUNK), PAD_CHUNK)
            return pltpu.make_async_copy(zero_ref.at[pl.ds(0, PAD_CHUNK)], xs_ref.at[rows], sem)

        def for_each_pad_piece(on_row, on_chunk):
            def per_expert(e, carry):
                lo, hi = pads_ref[2 * e], pads_ref[2 * e + 1]
                first_chunk = (lo + PAD_CHUNK - 1) // PAD_CHUNK
                lax.fori_loop(lo, jnp.minimum(first_chunk * PAD_CHUNK, hi), on_row, 0)
                lax.fori_loop(first_chunk, hi // PAD_CHUNK, on_chunk, 0)
                return carry

            lax.fori_loop(0, N_EXPERTS, per_expert, 0)

        def start_row(r, c):
            _row_copy(zero_ref, 0, xs_ref, r, sem).start()
            return c

        def wait_row(r, c):
            _row_copy(zero_ref, 0, xs_ref, r, sem).wait()
            return c

        def start_chunk(c, carry):
            chunk_copy(c).start()
            return carry

        def wait_chunk(c, carry):
            chunk_copy(c).wait()
            return carry

        for_each_pad_piece(start_row, start_chunk)
        for_each_pad_piece(wait_row, wait_chunk)

    if first:
        pl.when(pl.program_id(0) == 0)(zero_fill)

    def issue(k, carry):
        for j in range(SUBLANE):
            i = k * SUBLANE + j
            for prio, p in enumerate((pos_ref[0, 0, i], pos_ref[0, 0, tm + i])):
                pltpu.make_async_copy(h_ref.at[k, pl.ds(j, 1)], xs_ref.at[pl.ds(p, 1)], sem).start(priority=prio)
        return carry

    lax.fori_loop(0, tm // SUBLANE, issue, 0)
    for _ in range(2):
        pltpu.make_async_copy(h_ref, h_ref, sem).wait()


def _dispatch(h2, pos, pads, n_rows, tm, tile, xs_prev=None):
    T, width = h2.shape
    in_specs = [pl.BlockSpec((1, 1, 2 * tm), lambda i, pads: (i, 0, 0), memory_space=pltpu.SMEM),
                pl.BlockSpec((tm // SUBLANE, SUBLANE, width), lambda i, pads: (i, 0, 0))]
    args = [pads, _pos_steps(pos, tm), h2.reshape(T // SUBLANE, SUBLANE, width)]
    aliases = {}
    if xs_prev is not None:
        in_specs.append(pl.BlockSpec(memory_space=pl.ANY))
        args.append(xs_prev)
        aliases = {3: 0}
    grid_spec = pltpu.PrefetchScalarGridSpec(
        num_scalar_prefetch=1,
        grid=(T // tm,),
        in_specs=in_specs,
        out_specs=pl.BlockSpec(memory_space=pl.ANY),
        scratch_shapes=[pltpu.VMEM((tile, width), h2.dtype), pltpu.SemaphoreType.DMA(())],
    )
    return pl.pallas_call(
        functools.partial(_dispatch_kernel, tm=tm, tile=tile, first=xs_prev is None),
        grid_spec=grid_spec,
        out_shape=jax.ShapeDtypeStruct((n_rows, width), h2.dtype),
        input_output_aliases=aliases,
        compiler_params=_cparams(1),
        name="moe_dispatch",
    )(*args)


def _experts_kernel(te_ref, nu_ref, xs_ref, wgu_ref, wd_ref, ys_ref, wgu_bf_ref, wd_bf_ref, *, d_expert):
    j = pl.program_id(0)

    @pl.when(j >= nu_ref[0])
    def _():
        ys_ref[...] = jnp.zeros(ys_ref.shape, F32)

    @pl.when((j == 0) | (te_ref[j] != te_ref[jnp.maximum(j - 1, 0)]))
    def _():
        wgu_bf_ref[...] = wgu_ref[0].astype(BF16)
        wd_bf_ref[...] = wd_ref[0].astype(BF16)

    @pl.when(j < nu_ref[0])
    def _():
        gu = jnp.dot(xs_ref[...].astype(BF16), wgu_bf_ref[...], preferred_element_type=F32)
        act = _silu(gu[:, :d_expert]) * gu[:, d_expert:]
        ys_ref[...] = jnp.dot(act.astype(BF16), wd_bf_ref[...], preferred_element_type=F32)


def _experts(xs, tile_expert, n_used, w_gate_up, w_down, tile):
    n_rows, width = xs.shape
    _, D, two_f = w_gate_up.shape
    grid_spec = pltpu.PrefetchScalarGridSpec(
        num_scalar_prefetch=2,
        grid=(n_rows // tile,),
        in_specs=[pl.BlockSpec((tile, width), lambda j, te, nu: (jnp.minimum(j, nu[0] - 1), 0)),
                  pl.BlockSpec((1, D, two_f), lambda j, te, nu: (te[j], 0, 0)),
                  pl.BlockSpec((1, two_f // 2, D), lambda j, te, nu: (te[j], 0, 0))],
        out_specs=pl.BlockSpec((tile, D), lambda j, te, nu: (j, 0)),
        scratch_shapes=[pltpu.VMEM((D, two_f), BF16), pltpu.VMEM((two_f // 2, D), BF16)],
    )
    return pl.pallas_call(
        functools.partial(_experts_kernel, d_expert=two_f // 2),
        grid_spec=grid_spec,
        out_shape=jax.ShapeDtypeStruct((n_rows, D), F32),
        compiler_params=_cparams(1),
        name="moe_experts",
    )(tile_expert, n_used, xs, w_gate_up, w_down)


def _combine_kernel(pos_ref, pos_next_ref, ys_ref, route_ref, x1_ref, mod_ref, gf_ref, y_ref,
                    ya_ref, yb_ref, sems, *, tm):
    step = pl.program_id(0) * pl.num_programs(1) + pl.program_id(1)
    n_steps = pl.num_programs(0) * pl.num_programs(1)
    slot = step % 2

    def start_gathers(p_ref, s):
        def issue(k, carry):
            for j in range(SUBLANE):
                i = k * SUBLANE + j
                for prio, (buf, p) in enumerate(((ya_ref, p_ref[0, 0, i]), (yb_ref, p_ref[0, 0, tm + i]))):
                    pltpu.make_async_copy(ys_ref.at[pl.ds(p, 1)], buf.at[s, k, pl.ds(j, 1)],
                                          sems.at[s]).start(priority=prio)
            return carry

        lax.fori_loop(0, tm // SUBLANE, issue, 0)

    @pl.when(step == 0)
    def _():
        start_gathers(pos_ref, 0)

    @pl.when(step + 1 < n_steps)
    def _():
        start_gathers(pos_next_ref, 1 - slot)

    for buf in (ya_ref, yb_ref):
        pltpu.make_async_copy(buf.at[slot], buf.at[slot], sems.at[slot]).wait()
    route = route_ref[0]
    lane = lax.broadcasted_iota(jnp.int32, route.shape, 1)
    w1 = jnp.sum(jnp.where(lane == 2, route, 0.0), axis=-1, keepdims=True)
    w2 = jnp.sum(jnp.where(lane == 3, route, 0.0), axis=-1, keepdims=True)
    ya = ya_ref[slot].reshape(tm, ya_ref.shape[-1])
    yb = yb_ref[slot].reshape(tm, yb_ref.shape[-1])
    x2 = x1_ref[0] + mod_ref[0, 5] * (w1 * ya + w2 * yb)
    var = jnp.mean(x2 * x2, axis=-1, keepdims=True)
    y_ref[0] = x2 * lax.rsqrt(var + EPS) * gf_ref[...]


def _combine(ys, pos, route, x1, mod, gf, tm):
    B, S, D = x1.shape
    n_t = S // tm
    tok = lambda width: pl.BlockSpec((1, tm, width), lambda b, i: (b, i, 0))
    last = B * n_t - 1
    pos_spec = lambda ahead: pl.BlockSpec(
        (1, 1, 2 * tm), lambda b, i: (jnp.minimum(b * n_t + i + ahead, last), 0, 0), memory_space=pltpu.SMEM)
    pos_steps = _pos_steps(pos, tm)
    if mod.shape[2] == 1:
        mod_spec = pl.BlockSpec((1, N_COND, 1, D), lambda b, i: (b, 0, 0, 0))
    else:
        mod_spec = pl.BlockSpec((1, N_COND, tm, D), lambda b, i: (b, 0, i, 0))
    return pl.pallas_call(
        functools.partial(_combine_kernel, tm=tm),
        grid=(B, n_t),
        in_specs=[pos_spec(0), pos_spec(1),
                  pl.BlockSpec(memory_space=pl.ANY),
                  tok(ROUTER_LANES), tok(D),
                  mod_spec,
                  _full((1, D))],
        out_specs=tok(D),
        out_shape=jax.ShapeDtypeStruct((B, S, D), F32),
        scratch_shapes=[pltpu.VMEM((2, tm // SUBLANE, SUBLANE, D), F32)] * 2
                       + [pltpu.SemaphoreType.DMA((2,))],
        compiler_params=_cparams(2),
        name="moe_combine",
    )(pos_steps, pos_steps, ys, route, x1, mod, gf)


def _rope_tables(pos):
    half = HEAD_DIM // 2
    inv = ROPE_THETA ** (-jnp.arange(half, dtype=F32) * 2.0 / HEAD_DIM)
    ang = pos.astype(F32)[:, None] * inv[None, :]
    cos, sin = jnp.cos(ang), jnp.sin(ang)
    reps = LANE // HEAD_DIM
    return (jnp.tile(jnp.concatenate([cos, cos], axis=-1), (1, reps)),
            jnp.tile(jnp.concatenate([-sin, sin], axis=-1), (1, reps)))


def kernel(x_prompt, x_sample, cache_kv_w128, cache_kv_w512, cache_kv_w2048, state_pool, c_prompt, c_sample, norm1_g, w_ada, b_ada, w_in, w_attn_out, w_pool, pool_scale, w_pool_out, w_o, norm2_g, w_grp, b_grp, w_exp_router, b_exp_router, w_gate_up, w_down, final_norm_g):
    B, S, D = x_prompt.shape
    N, T, _ = x_sample.shape
    depth = norm1_g.shape[0]
    assert depth == 1, "single trunk layer"
    tm = min(TOKEN_TILE, S)
    assert S % tm == 0 and all(S % (SPAN * d) == 0 for d in ATTN_DILATIONS)

    n_qkvu = 3 * ATTN_WIDTH + POOL_WIDTH
    w_qkvu = w_in[0, :, :n_qkvu].astype(BF16)
    w_gates = w_in[0, :, n_qkvu:].astype(BF16)
    wa, wpo, wo = w_attn_out[0].astype(BF16), w_pool_out[0].astype(BF16), w_o[0].astype(BF16)
    wpool = w_pool[0].astype(BF16)
    pscale = pool_scale[0].reshape(1, POOL_WIDTH)
    w_r = jnp.concatenate([w_exp_router[0], w_grp[0]], axis=1)
    w_r = jnp.pad(w_r, ((0, 0), (0, ROUTER_LANES - w_r.shape[1])))
    wr_hi = w_r.astype(BF16)
    wr_hilo = jnp.concatenate([wr_hi, (w_r - wr_hi.astype(F32)).astype(BF16)], axis=1)
    b_r = jnp.pad(jnp.concatenate([b_exp_router[0], b_grp[0]]), (0, ROUTER_LANES - N_EXPERTS - MOE_GROUPS))
    b_r = b_r.reshape(1, ROUTER_LANES)
    wgu, wd = w_gate_up[0], w_down[0]
    g1, g2, gf = norm1_g[0].reshape(1, D), norm2_g[0].reshape(1, D), final_norm_g.reshape(1, D)

    c_rows = jnp.concatenate([jnp.repeat(c_sample, T, axis=0), c_prompt], axis=0)
    mod = _ada(c_rows, w_ada[0], b_ada[0])
    mod_p = jnp.transpose(mod[:, N * T:], (1, 0, 2)).reshape(B, N_COND, 1, D)
    mod_s = mod[None]

    cos_p, sin_p = _rope_tables(jnp.arange(S, dtype=jnp.int32))
    outs = _in_proj_prompt(x_prompt, mod_p, g1, w_qkvu, cos_p, sin_p, wpool, pscale, tm)
    qs, ks, vs = outs[0:3], outs[3:6], outs[6:9]
    pooled, kv0, kv1, kv2, ptail = outs[9:14]
    pool_prompt = ptail[:, POOL_HALO - POOL_STATE_LEN:, :][None]

    TS = N * T
    pos_s = PAST_LEN + jnp.arange(T, dtype=jnp.int32)
    cos_s, sin_s = _rope_tables(pos_s)
    cos_s, sin_s = jnp.tile(cos_s, (N, 1)), jnp.tile(sin_s, (N, 1))
    xs = x_sample.reshape(1, TS, D)
    q_s, kvn, u_s = _in_proj_sample(xs, mod_s, g1, w_qkvu, cos_s, sin_s)
    eye = jnp.eye(HEADS_PER_GROUP, dtype=BF16)
    qbd = jnp.einsum('ntghe,hk->ngthke', q_s.reshape(N, T, N_GROUPS, HEADS_PER_GROUP, HEAD_DIM), eye)
    qbd = qbd.reshape(N, N_GROUPS, T * HEADS_PER_GROUP, GROUP_WIDTH)
    caches = [jnp.transpose(c[0], (0, 2, 3, 4, 1)).reshape(N, 2, GROUP_WIDTH, c.shape[2])
              for c in (cache_kv_w128, cache_kv_w512, cache_kv_w2048)]

    attn_parts, (attn_s, pooled_s, ko0, ko1, ko2, pool_s) = _attn_and_sample(
        qs, ks, vs, qbd, kvn.reshape(N, T, 2 * ATTN_WIDTH), caches, state_pool[0],
        u_s.reshape(N, T, POOL_WIDTH), wpool, pscale, nq_max=ATTN_BLOCKS_PER_STEP)
    attn_inputs = [o for o, _ in attn_parts] + [l for _, l in attn_parts]
    x1, h2, route, counts, route_t = _merge(x_prompt, mod_p, g1, g2, attn_inputs, pooled, w_gates, wa, wpo, wo,
                                            wr_hi, wr_hilo, b_r, tm)
    x1s, h2s, route_s, counts_s, route_ts = _merge(xs, mod_s, g1, g2, [attn_s.reshape(1, TS, GROUP_WIDTH)],
                                                   pooled_s.reshape(1, TS, POOL_WIDTH), w_gates, wa, wpo, wo,
                                                   wr_hi, wr_hilo, b_r, TS)

    (pos, pos_s), tile_expert, n_used, pads, n_tiles = _routing_tables(
        [route_t, route_ts], [counts, counts_s], EXPERT_TILE)
    tm_rows = min(ROW_MOVE_TILE, S)
    sorted_rows = _dispatch(h2.reshape(B * S, D), pos, pads, n_tiles * EXPERT_TILE, tm_rows, EXPERT_TILE)
    sorted_rows = _dispatch(h2s.reshape(TS, D), pos_s, pads, n_tiles * EXPERT_TILE, TS, EXPERT_TILE,
                            xs_prev=sorted_rows)
    ys = _experts(sorted_rows, tile_expert, n_used, wgu, wd, EXPERT_TILE)
    y_prompt = _combine(ys, pos, route, x1, mod_p, gf, tm=tm_rows)
    y_sample = _combine(ys, pos_s, route_s, x1s, mod_s, gf, tm=TS).reshape(N, T, D)

    def kv_shape(a):
        a = a.reshape(a.shape[0], 2, HEADS_PER_GROUP, HEAD_DIM, a.shape[3])
        return jnp.transpose(a, (0, 4, 1, 2, 3))[None]

    return (y_prompt, y_sample, kv_shape(kv0), kv_shape(kv1), kv_shape(kv2), pool_prompt,
            kv_shape(ko0), kv_shape(ko1), kv_shape(ko2), pool_s[None])
```

```python
import functools

import jax
import jax.numpy as jnp
from jax import lax
from jax.experimental import pallas as pl
from jax.experimental.pallas import tpu as pltpu

F32 = jnp.float32
BF16 = jnp.bfloat16

HEAD_DIM = 64
HEADS_PER_GROUP = 4
HEAD_SHIFT = 2
LANE_HEAD_SHIFT = 6
GROUP_WIDTH = HEADS_PER_GROUP * HEAD_DIM
ATTN_WINDOWS = (128, 512, 2048)
ATTN_DILATIONS = (1, 4, 16)
N_GROUPS = 3
SPAN = 128
ATTN_WIDTH = N_GROUPS * GROUP_WIDTH
ROPE_THETA = 10000.0
PAST_LEN = 8192
POOL_WINDOWS = (2, 4, 8, 16)
POOL_GROUP_WIDTH = 128
POOL_WIDTH = 512
POOL_STATE_LEN = 15
POOL_HALO = 16
MOE_GROUPS = 4
EXPERTS_PER_GROUP = 8
N_EXPERTS = 32
N_COND = 6
EPS = 1e-6

LANE = 128
SUBLANE = 8
VMEM_LIMIT_BYTES = 56 * 1024 * 1024

TOKEN_TILE = 512
ATTN_BLOCKS_PER_STEP = 8
ROW_MOVE_TILE = 1024

NEG = -1e30
BIG_LANE = 1e9

ROUTER_LANES = LANE
GROUP_LANE0 = N_EXPERTS


def _cparams(n_axes):
    return pltpu.CompilerParams(dimension_semantics=("arbitrary",) * n_axes,
                                vmem_limit_bytes=VMEM_LIMIT_BYTES)


def _full(shape):
    nd = len(shape)
    return pl.BlockSpec(tuple(shape), lambda *_: (0,) * nd)


def _norm_mod(x, g, scale, shift):
    var = jnp.mean(x * x, axis=-1, keepdims=True)
    return (x * lax.rsqrt(var + EPS) * g) * (1.0 + scale) + shift


def _rope(x, cos, sin):
    lane = lax.broadcasted_iota(jnp.int32, (x.shape[0], LANE), 1)
    first_half = (lane & (HEAD_DIM - 1)) < (HEAD_DIM // 2)
    outs = []
    for c in range(x.shape[1] // LANE):
        xc = x[:, c * LANE:(c + 1) * LANE]
        partner = jnp.where(first_half, pltpu.roll(xc, LANE - HEAD_DIM // 2, 1),
                            pltpu.roll(xc, HEAD_DIM // 2, 1))
        outs.append(xc * cos + partner * sin)
    return jnp.concatenate(outs, axis=1)


def _silu(x):
    return x * jax.nn.sigmoid(x)


def _ada_kernel(c_ref, w_ref, b_ref, o_ref):
    s = _silu(c_ref[...]).astype(BF16)
    o_ref[0] = jnp.dot(s, w_ref[...].astype(BF16), preferred_element_type=F32) + b_ref[...]


def _ada(c_all, w_ada, b_ada):
    rows, d = c_all.shape
    n_out = w_ada.shape[1]
    tn = n_out // N_COND
    return pl.pallas_call(
        _ada_kernel,
        grid=(N_COND,),
        in_specs=[_full((rows, d)),
                  pl.BlockSpec((d, tn), lambda j: (0, j)),
                  pl.BlockSpec((1, tn), lambda j: (0, j))],
        out_specs=pl.BlockSpec((1, rows, tn), lambda j: (j, 0, 0)),
        out_shape=jax.ShapeDtypeStruct((N_COND, rows, tn), F32),
        compiler_params=_cparams(1),
        name="ada",
    )(c_all, w_ada, b_ada.reshape(1, n_out))


def _in_proj_prompt_kernel(x_ref, mod_ref, g1_ref, w_ref, cos_ref, sin_ref, wpool_ref, pscale_ref, *rest,
                           tm, n_tiles, tails):
    qkv_refs = (rest[0:3], rest[3:6], rest[6:9])
    pooled_ref, kv0_ref, kv1_ref, kv2_ref, ptail_ref = rest[9:14]
    stage_refs = rest[14:17]
    ue_ref = rest[17]
    i = pl.program_id(1)
    x = x_ref[0]
    h = _norm_mod(x, g1_ref[...], mod_ref[0, 1], mod_ref[0, 0]).astype(BF16)
    cos = cos_ref[...]
    sin = sin_ref[...]
    y = jnp.dot(h, w_ref[...], preferred_element_type=F32)
    q = _rope(y[:, :ATTN_WIDTH], cos, sin) * (HEAD_DIM ** -0.5)
    k = _rope(y[:, ATTN_WIDTH:2 * ATTN_WIDTH], cos, sin)
    v = y[:, 2 * ATTN_WIDTH:3 * ATTN_WIDTH]
    u = y[:, 3 * ATTN_WIDTH:3 * ATTN_WIDTH + POOL_WIDTH]

    chunks = GROUP_WIDTH // LANE
    for val, out_refs, stage_ref in zip((q, k, v), qkv_refs, stage_refs):
        for g, d in enumerate(ATTN_DILATIONS):
            gl = slice(g * GROUP_WIDTH, (g + 1) * GROUP_WIDTH)
            if d == 1:
                out_refs[g][0] = val[:, gl].astype(BF16)
                continue
            for c in range(chunks):
                stage_ref[g * chunks + c] = val[:, g * GROUP_WIDTH + c * LANE:g * GROUP_WIDTH + (c + 1) * LANE]
            for r in range(d):
                for c in range(chunks):
                    out_refs[g][0, :, r * GROUP_WIDTH + c * LANE:r * GROUP_WIDTH + (c + 1) * LANE] = (
                        stage_ref[g * chunks + c, pl.ds(r, tm // d, stride=d), :].astype(BF16))

    for g, kv_ref in enumerate((kv0_ref, kv1_ref, kv2_ref)):
        first_tile, rows = tails[g]
        lo = tm - rows

        @pl.when(i >= first_tile)
        def _(kv_ref=kv_ref, g=g, lo=lo):
            kv_ref[0, 0] = k[lo:, g * GROUP_WIDTH:(g + 1) * GROUP_WIDTH].T
            kv_ref[0, 1] = v[lo:, g * GROUP_WIDTH:(g + 1) * GROUP_WIDTH].T

    @pl.when(i == 0)
    def _():
        ue_ref[0:POOL_HALO, :] = jnp.zeros((POOL_HALO, POOL_WIDTH), F32)

    ue_ref[POOL_HALO:, :] = u
    pos1 = (i * tm + 1 + lax.broadcasted_iota(jnp.int32, (tm, 1), 0)).astype(F32)
    for gi, w in enumerate(POOL_WINDOWS):
        lanes = slice(gi * POOL_GROUP_WIDTH, (gi + 1) * POOL_GROUP_WIDTH)
        acc = ue_ref[pl.ds(POOL_HALO, tm), lanes]
        for j in range(1, w):
            acc = acc + ue_ref[pl.ds(POOL_HALO - j, tm), lanes]
        mean = acc / jnp.minimum(float(w), pos1)
        z = jnp.dot((mean - u[:, lanes]).astype(BF16), wpool_ref[gi], preferred_element_type=F32)
        pooled_ref[0, :, lanes] = (z * pscale_ref[:, lanes]).astype(BF16)
    ue_ref[0:POOL_HALO, :] = u[tm - POOL_HALO:, :]

    @pl.when(i == n_tiles - 1)
    def _():
        ptail_ref[0] = u[tm - POOL_HALO:, :]


def _in_proj_prompt(x, mod, g1, w_qkvu, cos, sin, w_pool, pool_scale, tm):
    B, S, D = x.shape
    n_tiles = S // tm
    n_w = w_qkvu.shape[1]
    tails, kv_shapes, kv_specs = [], [], []
    for W in ATTN_WINDOWS:
        Wg = min(W, S)
        if Wg >= tm:
            first = n_tiles - Wg // tm
            rows = tm
        else:
            first = n_tiles - 1
            rows = Wg
        tails.append((first, rows))
        kv_shapes.append(jax.ShapeDtypeStruct((B, 2, GROUP_WIDTH, Wg), F32))
        kv_specs.append(pl.BlockSpec((1, 2, GROUP_WIDTH, rows),
                                     lambda b, i, first=first: (b, 0, 0, jnp.maximum(i - first, 0))))
    tok = lambda width: pl.BlockSpec((1, tm, width), lambda b, i: (b, i, 0))
    dil_specs = [pl.BlockSpec((1, tm // d, d * GROUP_WIDTH), lambda b, i: (b, i, 0)) for d in ATTN_DILATIONS]
    dil_shapes = [jax.ShapeDtypeStruct((B, S // d, d * GROUP_WIDTH), BF16) for d in ATTN_DILATIONS]
    outs = pl.pallas_call(
        functools.partial(_in_proj_prompt_kernel, tm=tm, n_tiles=n_tiles, tails=tuple(tails)),
        grid=(B, n_tiles),
        in_specs=[tok(D),
                  pl.BlockSpec((1, N_COND, 1, D), lambda b, i: (b, 0, 0, 0)),
                  _full((1, D)),
                  _full((D, n_w)),
                  pl.BlockSpec((tm, LANE), lambda b, i: (i, 0)),
                  pl.BlockSpec((tm, LANE), lambda b, i: (i, 0)),
                  _full(w_pool.shape),
                  _full((1, POOL_WIDTH))],
        out_specs=dil_specs * 3 + [tok(POOL_WIDTH)] + kv_specs
                  + [pl.BlockSpec((1, POOL_HALO, POOL_WIDTH), lambda b, i: (b, 0, 0))],
        out_shape=dil_shapes * 3
                  + [jax.ShapeDtypeStruct((B, S, POOL_WIDTH), BF16)] + kv_shapes
                  + [jax.ShapeDtypeStruct((B, POOL_HALO, POOL_WIDTH), F32)],
        scratch_shapes=[pltpu.VMEM((ATTN_WIDTH // LANE, tm, LANE), F32)] * 3
                       + [pltpu.VMEM((tm + POOL_HALO, POOL_WIDTH), F32)],
        compiler_params=_cparams(2),
        name="in_proj_prompt",
    )(x, mod, g1, w_qkvu, cos, sin, w_pool, pool_scale)
    return outs


def _in_proj_sample_kernel(x_ref, mod_ref, g1_ref, w_ref, cos_ref, sin_ref, q_ref, kvn_ref, u_ref):
    h = _norm_mod(x_ref[0], g1_ref[...], mod_ref[0, 1], mod_ref[0, 0]).astype(BF16)
    y = jnp.dot(h, w_ref[...], preferred_element_type=F32)
    cos = cos_ref[...]
    sin = sin_ref[...]
    q = _rope(y[:, :ATTN_WIDTH], cos, sin) * (HEAD_DIM ** -0.5)
    k = _rope(y[:, ATTN_WIDTH:2 * ATTN_WIDTH], cos, sin)
    v = y[:, 2 * ATTN_WIDTH:3 * ATTN_WIDTH]
    q_ref[...] = q.astype(BF16)
    for g in range(N_GROUPS):
        gl = slice(g * GROUP_WIDTH, (g + 1) * GROUP_WIDTH)
        kvn_ref[:, 2 * g * GROUP_WIDTH:(2 * g + 1) * GROUP_WIDTH] = k[:, gl]
        kvn_ref[:, (2 * g + 1) * GROUP_WIDTH:(2 * g + 2) * GROUP_WIDTH] = v[:, gl]
    u_ref[...] = y[:, 3 * ATTN_WIDTH:3 * ATTN_WIDTH + POOL_WIDTH]


def _in_proj_sample(x, mod, g1, w_qkvu, cos, sin):
    _, T, D = x.shape
    n_w = w_qkvu.shape[1]
    return pl.pallas_call(
        _in_proj_sample_kernel,
        grid=(1,),
        in_specs=[_full((1, T, D)), _full((1, N_COND, T, D)), _full((1, D)), _full((D, n_w)),
                  _full((T, LANE)), _full((T, LANE))],
        out_specs=[_full((T, ATTN_WIDTH)), _full((T, 2 * ATTN_WIDTH)), _full((T, POOL_WIDTH))],
        out_shape=[jax.ShapeDtypeStruct((T, ATTN_WIDTH), BF16),
                   jax.ShapeDtypeStruct((T, 2 * ATTN_WIDTH), F32),
                   jax.ShapeDtypeStruct((T, POOL_WIDTH), F32)],
        compiler_params=_cparams(1),
        name="in_proj_sample",
    )(x, mod, g1, w_qkvu, cos, sin)


def _attn_body(q_ref, kc_ref, kp_ref, vc_ref, vp_ref, o_ref, lse_ref, kbuf, vbuf, n0, nq):
    kbuf[0:SPAN, :] = kp_ref[0]
    kbuf[SPAN:, :] = kc_ref[0]
    vbuf[0:SPAN, :] = vp_ref[0]
    vbuf[SPAN:, :] = vc_ref[0]
    qi = lax.broadcasted_iota(jnp.int32, (SPAN, 2 * SPAN), 0)
    kj = lax.broadcasted_iota(jnp.int32, (SPAN, 2 * SPAN), 1)
    band = (kj >= qi) & (kj <= qi + SPAN)
    band_first = band & (kj >= jnp.where(n0 > 0, 0, SPAN))
    lane = lax.broadcasted_iota(jnp.int32, (SPAN, LANE), 1)
    low_head = lane < HEAD_DIM
    for j in range(nq):
        valid = band_first if j == 0 else band
        rows = slice(j * SPAN, (j + 1) * SPAN)
        krows = slice(j * SPAN, (j + 2) * SPAN)
        for c in range(GROUP_WIDTH // LANE):
            cl = slice(c * LANE, (c + 1) * LANE)
            q = q_ref[0, rows, cl]
            kk = kbuf[krows, cl]
            vv = vbuf[krows, cl]
            o_pair, lse_pair = [], []
            for hh in range(2):
                mask_h = low_head if hh == 0 else jnp.logical_not(low_head)
                qm = jnp.where(mask_h, q, jnp.zeros_like(q))
                s = lax.dot_general(qm, kk, (((1,), (1,)), ((), ())), preferred_element_type=F32)
                s = jnp.where(valid, s, NEG)
                m = jnp.max(s, axis=-1, keepdims=True)
                p = jnp.exp(s - m)
                den = jnp.sum(p, axis=-1, keepdims=True)
                o_pair.append(jnp.dot(p.astype(BF16), vv, preferred_element_type=F32) / den)
                lse_pair.append(m + jnp.log(den))
            o_ref[0, rows, cl] = jnp.where(low_head, o_pair[0], o_pair[1]).astype(BF16)
            lse_ref[0, rows, cl] = jnp.where(low_head, lse_pair[0], lse_pair[1])


def _sample_body(qbd_ref, kvn_ref, c0_ref, c1_ref, c2_ref, sp_ref, u_ref, wpool_ref, pscale_ref,
                 attn_ref, pooled_ref, o0_ref, o1_ref, o2_ref, po_ref, ue_ref, diff_ref, nt_ref, t_new):
    rows = HEADS_PER_GROUP * t_new
    caches = (c0_ref, c1_ref, c2_ref)
    outs = (o0_ref, o1_ref, o2_ref)

    s_cache, s_new, valid_new = [], [], []
    m = jnp.full((rows, 1), NEG, F32)
    for g in range(N_GROUPS):
        d = ATTN_DILATIONS[g]
        c_ref = caches[g]
        L = c_ref.shape[3]
        qb = qbd_ref[0, g]
        s = jnp.dot(qb, c_ref[0, 0].astype(BF16), preferred_element_type=F32)
        t_row = lax.broadcasted_iota(jnp.int32, (rows, L), 0) >> HEAD_SHIFT
        delta = L + t_row - lax.broadcasted_iota(jnp.int32, (rows, L), 1)
        ok = ((delta & (d - 1)) == 0) & (delta <= SPAN * d)
        s = jnp.where(ok, s, NEG)
        s_cache.append(s)
        m = jnp.maximum(m, jnp.max(s, axis=-1, keepdims=True))
        qf = qb.astype(F32)
        t_col = lax.broadcasted_iota(jnp.int32, (rows, 1), 0) >> HEAD_SHIFT
        sn, okn = [], []
        for tn in range(t_new):
            kn = kvn_ref[0, tn:tn + 1, 2 * g * GROUP_WIDTH:(2 * g + 1) * GROUP_WIDTH]
            kn = kn.astype(BF16).astype(F32)
            dn = t_col - tn
            ok_n = (dn >= 0) & ((dn & (d - 1)) == 0)
            s1 = jnp.where(ok_n, jnp.sum(qf * kn, axis=-1, keepdims=True), NEG)
            sn.append(s1)
            okn.append(ok_n)
            m = jnp.maximum(m, s1)
        s_new.append(sn)
        valid_new.append(okn)

    den = jnp.zeros((rows, 1), F32)
    acc = jnp.zeros((rows, GROUP_WIDTH), F32)
    for g in range(N_GROUPS):
        c_ref = caches[g]
        p = jnp.exp(s_cache[g] - m)
        den = den + jnp.sum(p, axis=-1, keepdims=True)
        acc = acc + lax.dot_general(p.astype(BF16), c_ref[0, 1].astype(BF16), (((1,), (1,)), ((), ())),
                                    preferred_element_type=F32)
        for tn in range(t_new):
            pn = jnp.exp(s_new[g][tn] - m)
            den = den + pn
            vn = kvn_ref[0, tn:tn + 1, (2 * g + 1) * GROUP_WIDTH:(2 * g + 2) * GROUP_WIDTH]
            acc = acc + pn * vn
    row_head = lax.broadcasted_iota(jnp.int32, (rows, GROUP_WIDTH), 0) & (HEADS_PER_GROUP - 1)
    lane_head = lax.broadcasted_iota(jnp.int32, (rows, GROUP_WIDTH), 1) >> LANE_HEAD_SHIFT
    o_diag = jnp.where(row_head == lane_head, acc / den, 0.0).astype(BF16)
    sel = ((lax.broadcasted_iota(jnp.int32, (rows, rows), 1) >> HEAD_SHIFT)
           == lax.broadcasted_iota(jnp.int32, (rows, rows), 0)).astype(BF16)
    attn = jnp.dot(sel, o_diag, preferred_element_type=F32)
    attn_ref[0] = attn[0:t_new].astype(BF16)

    tail_lane = lax.broadcasted_iota(jnp.int32, (GROUP_WIDTH, LANE), 1)
    nt_ref[...] = jnp.zeros(nt_ref.shape, F32)
    for g in range(N_GROUPS):
        c_ref, o_ref = caches[g], outs[g]
        L = c_ref.shape[3]
        for kv in range(2):
            col0 = (2 * g + kv) * GROUP_WIDTH
            nt_ref[LANE - t_new:LANE, :] = kvn_ref[0, :, col0:col0 + GROUP_WIDTH]
            new_t = nt_ref[...].T
            rolled = pltpu.roll(c_ref[0, kv], L - t_new, 1)
            if L > LANE:
                o_ref[0, kv, :, 0:L - LANE] = rolled[:, 0:L - LANE]
            o_ref[0, kv, :, L - LANE:L] = jnp.where(tail_lane >= LANE - t_new, new_t, rolled[:, L - LANE:L])

    ue_ref[0:POOL_STATE_LEN, :] = sp_ref[0]
    ue_ref[POOL_STATE_LEN:POOL_STATE_LEN + t_new, :] = u_ref[0]
    diff_ref[...] = jnp.zeros(diff_ref.shape, F32)
    for tn in range(t_new):
        r = POOL_STATE_LEN + tn
        for gi, w in enumerate(POOL_WINDOWS):
            lanes = slice(gi * POOL_GROUP_WIDTH, (gi + 1) * POOL_GROUP_WIDTH)
            win = jnp.sum(ue_ref[r - w + 1:r + 1, lanes], axis=0, keepdims=True)
            diff_ref[tn:tn + 1, lanes] = win / float(w) - ue_ref[r:r + 1, lanes]
    for gi in range(len(POOL_WINDOWS)):
        lanes = slice(gi * POOL_GROUP_WIDTH, (gi + 1) * POOL_GROUP_WIDTH)
        z = jnp.dot(diff_ref[:, lanes].astype(BF16), wpool_ref[gi], preferred_element_type=F32)
        pooled_ref[0, :, lanes] = (z[0:t_new] * pscale_ref[:, lanes]).astype(BF16)
    po_ref[0] = ue_ref[t_new:t_new + POOL_STATE_LEN, :]


N_ATTN_IN = 5
N_SAMPLE_IN = 9
N_SAMPLE_OUT = 6


def _attn_sample_kernel(*refs, plans, n_seq, t_new):
    n_att = N_GROUPS * N_ATTN_IN
    attn_in = refs[:n_att]
    sample_in = refs[n_att:n_att + N_SAMPLE_IN]
    outs = refs[n_att + N_SAMPLE_IN:]
    attn_out = outs[:2 * N_GROUPS]
    sample_out = outs[2 * N_GROUPS:2 * N_GROUPS + N_SAMPLE_OUT]
    scratch = outs[2 * N_GROUPS + N_SAMPLE_OUT:]
    kv_bufs, sample_scratch = scratch[:2 * N_GROUPS], scratch[2 * N_GROUPS:]
    s = pl.program_id(0)

    for g, (first, count, n_sup, d, nq) in enumerate(plans):
        @pl.when((s >= first) & (s < first + count))
        def _(g=g, first=first, n_sup=n_sup, d=d, nq=nq):
            n0 = ((s - first) // d) % n_sup
            _attn_body(*attn_in[g * N_ATTN_IN:(g + 1) * N_ATTN_IN], attn_out[2 * g], attn_out[2 * g + 1],
                       kv_bufs[2 * g], kv_bufs[2 * g + 1], n0, nq)

    @pl.when(s < n_seq)
    def _():
        _sample_body(*sample_in, *sample_out, *sample_scratch, t_new)


def _attn_and_sample(qs, ks, vs, qbd, kvn, caches, state_pool, u, w_pool, pool_scale, nq_max):
    N, t_new, _ = u.shape
    rows = HEADS_PER_GROUP * t_new
    plans, in_specs, out_specs, out_shape, scratch, args = [], [], [], [], [], []
    first = 0
    for g in range(N_GROUPS):
        B, M, width = qs[g].shape
        d = width // GROUP_WIDTH
        nq = min(nq_max, M // SPAN)
        n_sup = M // (nq * SPAN)
        count = B * n_sup * d
        plans.append((first, count, n_sup, d, nq))

        def block_index(s, first=first, count=count, n_sup=n_sup, d=d):
            local = jnp.clip(s - first, 0, count - 1)
            return local // (n_sup * d), (local // d) % n_sup, local % d

        def cur_map(s, block_index=block_index):
            b, n, r = block_index(s)
            return b, n, r

        def prev_map(s, block_index=block_index, nq=nq):
            b, n, r = block_index(s)
            return b, jnp.maximum(n * nq - 1, 0), r

        cur = pl.BlockSpec((1, nq * SPAN, GROUP_WIDTH), cur_map)
        prev = pl.BlockSpec((1, SPAN, GROUP_WIDTH), prev_map)
        in_specs += [cur, cur, prev, cur, prev]
        args += [qs[g], ks[g], ks[g], vs[g], vs[g]]
        out_specs += [cur, cur]
        out_shape += [jax.ShapeDtypeStruct((B, M, width), BF16), jax.ShapeDtypeStruct((B, M, width), F32)]
        scratch += [pltpu.VMEM(((nq + 1) * SPAN, GROUP_WIDTH), BF16)] * 2
        first += count
    n_steps = max(first, N)

    per_n = lambda shape: pl.BlockSpec((1,) + tuple(shape),
                                       lambda s: (jnp.minimum(s, N - 1),) + (0,) * len(shape))
    cache_specs = [per_n(c.shape[1:]) for c in caches]
    in_specs += ([per_n(qbd.shape[1:]), per_n(kvn.shape[1:])] + cache_specs
                 + [per_n(state_pool.shape[1:]), per_n(u.shape[1:]), _full(w_pool.shape), _full((1, POOL_WIDTH))])
    args += [qbd, kvn, *caches, state_pool, u, w_pool, pool_scale]
    out_specs += ([per_n((t_new, GROUP_WIDTH)), per_n((t_new, POOL_WIDTH))] + cache_specs
                  + [per_n(state_pool.shape[1:])])
    out_shape += ([jax.ShapeDtypeStruct((N, t_new, GROUP_WIDTH), BF16),
                   jax.ShapeDtypeStruct((N, t_new, POOL_WIDTH), BF16)]
                  + [jax.ShapeDtypeStruct(c.shape, F32) for c in caches]
                  + [jax.ShapeDtypeStruct(state_pool.shape, F32)])
    scratch += [pltpu.VMEM((POOL_STATE_LEN + t_new + 5, POOL_WIDTH), F32),
                pltpu.VMEM((rows, POOL_WIDTH), F32),
                pltpu.VMEM((LANE, GROUP_WIDTH), F32)]
    outs = pl.pallas_call(
        functools.partial(_attn_sample_kernel, plans=tuple(plans), n_seq=N, t_new=t_new),
        grid=(n_steps,),
        in_specs=in_specs,
        out_specs=out_specs,
        out_shape=out_shape,
        scratch_shapes=scratch,
        compiler_params=_cparams(1),
        name="attn_and_sample",
    )(*args)
    return [(outs[2 * g], outs[2 * g + 1]) for g in range(N_GROUPS)], outs[2 * N_GROUPS:]


def _route(logits):
    lane = lax.broadcasted_iota(jnp.int32, logits.shape, 1).astype(F32)
    is_grp = (lane >= GROUP_LANE0) & (lane < GROUP_LANE0 + MOE_GROUPS)
    gl = jnp.where(is_grp, logits, NEG)
    gmax = jnp.max(gl, axis=-1, keepdims=True)
    gidx = jnp.min(jnp.where(gl == gmax, lane, BIG_LANE), axis=-1, keepdims=True) - GROUP_LANE0
    gsum = jnp.sum(jnp.where(is_grp, jnp.exp(gl - gmax), 0.0), axis=-1, keepdims=True)
    grp_w = 1.0 / gsum
    lo = gidx * EXPERTS_PER_GROUP
    in_grp = (lane >= lo) & (lane < lo + EXPERTS_PER_GROUP)
    el = jnp.where(in_grp, logits, NEG)
    v1 = jnp.max(el, axis=-1, keepdims=True)
    i1 = jnp.min(jnp.where(el == v1, lane, BIG_LANE), axis=-1, keepdims=True)
    el2 = jnp.where(lane == i1, NEG, el)
    v2 = jnp.max(el2, axis=-1, keepdims=True)
    i2 = jnp.min(jnp.where(el2 == v2, lane, BIG_LANE), axis=-1, keepdims=True)
    e = jnp.exp(v2 - v1)
    w1 = grp_w / (1.0 + e)
    w2 = grp_w * e / (1.0 + e)
    return lane, i1, i2, w1, w2


def _merge_kernel(*refs, n_groups):
    x_ref, mod_ref, g1_ref, g2_ref = refs[0:4]
    n_attn = 2 * n_groups if n_groups > 1 else 1
    attn_refs = refs[4:4 + n_attn]
    rest = refs[4 + n_attn:]
    pooled_ref, wg_ref, wa_ref, wp_ref, wo_ref, wrh_ref, wrc_ref, br_ref = rest[:8]
    x1_ref, h2_ref, route_ref, counts_ref, route_t_ref = rest[8:13]
    scratch_refs = rest[13:]
    stage_refs = scratch_refs[:n_attn] if n_groups > 1 else ()
    carry_ref = scratch_refs[-1]

    x = x_ref[0]
    tm = x.shape[0]
    if n_groups > 1:
        vals = []
        for idx, ref in enumerate(attn_refs):
            d = ATTN_DILATIONS[idx % n_groups]
            if d == 1:
                vals.append(ref[0].astype(F32))
                continue
            stage_ref = stage_refs[idx]
            chunks = GROUP_WIDTH // LANE
            for r in range(d):
                for c in range(chunks):
                    stage_ref[c, pl.ds(r, tm // d, stride=d), :] = (
                        ref[0, :, r * GROUP_WIDTH + c * LANE:r * GROUP_WIDTH + (c + 1) * LANE].astype(F32))
            vals.append(jnp.concatenate([stage_ref[c] for c in range(chunks)], axis=1))
        os_, ls = vals[:n_groups], vals[n_groups:]
        lmax = functools.reduce(jnp.maximum, ls)
        es = [jnp.exp(l - lmax) for l in ls]
        attn = sum(e * o for e, o in zip(es, os_)) / sum(es)
    else:
        attn = attn_refs[0][0]
    a = jnp.dot(attn.astype(BF16), wa_ref[...], preferred_element_type=F32)
    p = jnp.dot(pooled_ref[0], wp_ref[...], preferred_element_type=F32)
    h = _norm_mod(x, g1_ref[...], mod_ref[0, 1], mod_ref[0, 0]).astype(BF16)
    gates = jnp.dot(h, wg_ref[...], preferred_element_type=F32)
    D = x.shape[1]
    merged = jax.nn.sigmoid(gates[:, :D]) * a + jax.nn.sigmoid(gates[:, D:]) * p
    y = jnp.dot(merged.astype(BF16), wo_ref[...], preferred_element_type=F32)
    x1 = x + mod_ref[0, 2] * y
    x1_ref[0] = x1
    h2 = _norm_mod(x1, g2_ref[...], mod_ref[0, 4], mod_ref[0, 3])
    h2_hi = h2.astype(BF16)
    h2_lo = (h2 - h2_hi.astype(F32)).astype(BF16)
    hi_terms = jnp.dot(h2_hi, wrc_ref[...], preferred_element_type=F32)
    logits = (hi_terms[:, :ROUTER_LANES] + jnp.dot(h2_lo, wrh_ref[...], preferred_element_type=F32)
              + hi_terms[:, ROUTER_LANES:]) + br_ref[...]
    lane, i1, i2, w1, w2 = _route(logits)
    h2_ref[0] = h2

    @pl.when((pl.program_id(0) == 0) & (pl.program_id(1) == 0))
    def _():
        carry_ref[...] = jnp.zeros(carry_ref.shape, F32)

    hit = ((lane == i1) | (lane == i2)).astype(BF16)
    ltri = (lax.broadcasted_iota(jnp.int32, (tm, tm), 0) >= lax.broadcasted_iota(jnp.int32, (tm, tm), 1))
    prefix = jnp.dot(ltri.astype(BF16), hit, preferred_element_type=F32) + carry_ref[...]
    rank1 = jnp.sum(jnp.where(lane == i1, prefix, 0.0), axis=-1, keepdims=True) - 1.0
    rank2 = jnp.sum(jnp.where(lane == i2, prefix, 0.0), axis=-1, keepdims=True) - 1.0
    carry_ref[...] = prefix[tm - 1:tm, :]
    counts_ref[...] = jnp.broadcast_to(prefix[tm - 1:tm, :], counts_ref.shape)
    cols = (i1, i2, w1, w2, rank1, rank2)
    route = jnp.zeros(logits.shape, F32)
    for c, col in enumerate(cols):
        route = jnp.where(lane == float(c), col, route)
    route_ref[0] = route
    route_t_ref[...] = route.T[0:SUBLANE, :]


def _merge(x, mod, g1, g2, attn_inputs, pooled, w_gates, w_attn_out, w_pool_out, w_o, wr_hi, wr_hilo, b_r, tm):
    B, S, D = x.shape
    R = mod.shape[2]
    n_groups = len(attn_inputs) // 2 if len(attn_inputs) > 1 else 1
    tok = lambda width: pl.BlockSpec((1, tm, width), lambda b, i: (b, i, 0))
    if R == 1:
        mod_spec = pl.BlockSpec((1, N_COND, 1, D), lambda b, i: (b, 0, 0, 0))
    else:
        mod_spec = pl.BlockSpec((1, N_COND, tm, D), lambda b, i: (b, 0, i, 0))
    weights = (w_gates, w_attn_out, w_pool_out, w_o, wr_hi, wr_hilo, b_r)
    if n_groups > 1:
        attn_specs = [pl.BlockSpec((1, tm // (a.shape[2] // GROUP_WIDTH), a.shape[2]), lambda b, i: (b, i, 0))
                      for a in attn_inputs]
        scratch = [pltpu.VMEM((GROUP_WIDTH // LANE, tm, LANE), F32)] * len(attn_inputs)
    else:
        attn_specs = [tok(GROUP_WIDTH)]
        scratch = []
    n_t = S // tm
    out_specs = [tok(D), tok(D), tok(ROUTER_LANES), _full((SUBLANE, ROUTER_LANES)),
                 pl.BlockSpec((SUBLANE, tm), lambda b, i: (0, b * n_t + i))]
    out_shape = [jax.ShapeDtypeStruct((B, S, D), F32),
                 jax.ShapeDtypeStruct((B, S, D), F32),
                 jax.ShapeDtypeStruct((B, S, ROUTER_LANES), F32),
                 jax.ShapeDtypeStruct((SUBLANE, ROUTER_LANES), F32),
                 jax.ShapeDtypeStruct((SUBLANE, B * S), F32)]
    scratch = scratch + [pltpu.VMEM((1, ROUTER_LANES), F32)]
    return pl.pallas_call(
        functools.partial(_merge_kernel, n_groups=n_groups),
        grid=(B, S // tm),
        in_specs=[tok(D), mod_spec, _full((1, D)), _full((1, D))]
                 + attn_specs + [tok(POOL_WIDTH)]
                 + [_full(w.shape) for w in weights],
        out_specs=out_specs,
        out_shape=out_shape,
        scratch_shapes=scratch,
        compiler_params=_cparams(2),
        name="merge",
    )(x, mod, g1, g2, *attn_inputs, pooled, *weights)


EXPERT_TILE = 512
PAD_CHUNK = 32


def _positions_kernel(base_ref, route_ref, pos_ref):
    cols = route_ref[...].astype(jnp.int32)
    start = jnp.zeros_like(cols)
    for e in range(N_EXPERTS):
        start = jnp.where(cols == e, base_ref[e], start)
    pos_ref[...] = start + pltpu.roll(cols, SUBLANE - 4, 0)


def _positions(route_t, base):
    rows, T = route_t.shape
    grid_spec = pltpu.PrefetchScalarGridSpec(
        num_scalar_prefetch=1, grid=(1,),
        in_specs=[pl.BlockSpec((rows, T), lambda i, base: (0, 0))],
        out_specs=pl.BlockSpec((rows, T), lambda i, base: (0, 0)))
    pos = pl.pallas_call(
        _positions_kernel,
        grid_spec=grid_spec,
        out_shape=jax.ShapeDtypeStruct((rows, T), jnp.int32),
        compiler_params=_cparams(1),
        name="moe_positions",
    )(base.astype(jnp.int32), route_t)
    return pos[0:2]


def _routing_tables(routes, counts, tile):
    cnts = [c[0, :N_EXPERTS].astype(jnp.int32) for c in counts]
    cnt = sum(cnts)
    padded = ((cnt + tile - 1) // tile) * tile
    ends = jnp.cumsum(padded)
    base = ends - padded
    poss = []
    for route in routes:
        poss.append(_positions(route, base))
        base = base + cnts[len(poss) - 1]
    n_pairs = 2 * sum(r.shape[1] for r in routes)
    n_tiles = -(-n_pairs // tile) + N_EXPERTS
    starts = jnp.arange(n_tiles, dtype=jnp.int32) * tile
    n_used = ends[-1] // tile
    tile_expert = jnp.sum(starts[:, None] >= ends[None, :], axis=1).astype(jnp.int32)
    last = jnp.take(tile_expert, n_used - 1)
    tile_expert = jnp.where(jnp.arange(n_tiles) < n_used, tile_expert, last)
    pads = jnp.stack([ends - padded + cnts[0], ends], axis=1).reshape(-1).astype(jnp.int32)
    return poss, tile_expert, n_used.reshape(1).astype(jnp.int32), pads, n_tiles


def _pos_steps(pos, tm):
    steps = pos.shape[1] // tm
    return jnp.transpose(pos.reshape(2, steps, tm), (1, 0, 2)).reshape(steps, 1, 2 * tm)


def _row_copy(src_ref, src_row, dst_ref, dst_row, sem):
    return pltpu.make_async_copy(src_ref.at[pl.ds(src_row, 1)], dst_ref.at[pl.ds(dst_row, 1)], sem)


def _dispatch_kernel(pads_ref, pos_ref, h_ref, *rest, tm, tile, first):
    xs_ref, zero_ref, sem = rest[-3:]

    def zero_fill():
        zero_ref[...] = jnp.zeros(zero_ref.shape, zero_ref.dtype)
        n_tiles = xs_ref.shape[0] // tile
        first_unused = pads_ref[2 * N_EXPERTS - 1] // tile

        def tile_copy(j):
            return pltpu.make_async_copy(zero_ref, xs_ref.at[pl.ds(pl.multiple_of(j * tile, tile), tile)], sem)

        def fill_tile(j, c):
            tile_copy(j).start()
            return c

        def drain_tile(j, c):
            tile_copy(j).wait()
            return c

        lax.fori_loop(first_unused, n_tiles, fill_tile, 0)
        lax.fori_loop(first_unused, n_tiles, drain_tile, 0)

        def chunk_copy(c):
            rows = pl.ds(pl.multiple_of(c * PAD_CHUNK, PAD_CHUNK), PAD_CHUNK)
            return pltpu.make_async_copy(zero_ref.at[pl.ds(0, PAD_CHUNK)], xs_ref.at[rows], sem)

        def for_each_pad_piece(on_row, on_chunk):
            def per_expert(e, carry):
                lo, hi = pads_ref[2 * e], pads_ref[2 * e + 1]
                first_chunk = (lo + PAD_CHUNK - 1) // PAD_CHUNK
                lax.fori_loop(lo, jnp.minimum(first_chunk * PAD_CHUNK, hi), on_row, 0)
                lax.fori_loop(first_chunk, hi // PAD_CHUNK, on_chunk, 0)
                return carry

            lax.fori_loop(0, N_EXPERTS, per_expert, 0)

        def start_row(r, c):
            _row_copy(zero_ref, 0, xs_ref, r, sem).start()
            return c

        def wait_row(r, c):
            _row_copy(zero_ref, 0, xs_ref, r, sem).wait()
            return c

        def start_chunk(c, carry):
            chunk_copy(c).start()
            return carry

        def wait_chunk(c, carry):
            chunk_copy(c).wait()
            return carry

        for_each_pad_piece(start_row, start_chunk)
        for_each_pad_piece(wait_row, wait_chunk)

    if first:
        pl.when(pl.program_id(0) == 0)(zero_fill)

    def issue(k, carry):
        for j in range(SUBLANE):
            i = k * SUBLANE + j
            for prio, p in enumerate((pos_ref[0, 0, i], pos_ref[0, 0, tm + i])):
                pltpu.make_async_copy(h_ref.at[k, pl.ds(j, 1)], xs_ref.at[pl.ds(p, 1)], sem).start(priority=prio)
        return carry

    lax.fori_loop(0, tm // SUBLANE, issue, 0)
    for _ in range(2):
        pltpu.make_async_copy(h_ref, h_ref, sem).wait()


def _dispatch(h2, pos, pads, n_rows, tm, tile, xs_prev=None):
    T, width = h2.shape
    in_specs = [pl.BlockSpec((1, 1, 2 * tm), lambda i, pads: (i, 0, 0), memory_space=pltpu.SMEM),
                pl.BlockSpec((tm // SUBLANE, SUBLANE, width), lambda i, pads: (i, 0, 0))]
    args = [pads, _pos_steps(pos, tm), h2.reshape(T // SUBLANE, SUBLANE, width)]
    aliases = {}
    if xs_prev is not None:
        in_specs.append(pl.BlockSpec(memory_space=pl.ANY))
        args.append(xs_prev)
        aliases = {3: 0}
    grid_spec = pltpu.PrefetchScalarGridSpec(
        num_scalar_prefetch=1,
        grid=(T // tm,),
        in_specs=in_specs,
        out_specs=pl.BlockSpec(memory_space=pl.ANY),
        scratch_shapes=[pltpu.VMEM((tile, width), h2.dtype), pltpu.SemaphoreType.DMA(())],
    )
    return pl.pallas_call(
        functools.partial(_dispatch_kernel, tm=tm, tile=tile, first=xs_prev is None),
        grid_spec=grid_spec,
        out_shape=jax.ShapeDtypeStruct((n_rows, width), h2.dtype),
        input_output_aliases=aliases,
        compiler_params=_cparams(1),
        name="moe_dispatch",
    )(*args)


def _experts_kernel(te_ref, nu_ref, xs_ref, wgu_ref, wd_ref, ys_ref, wgu_bf_ref, wd_bf_ref, *, d_expert):
    j = pl.program_id(0)

    @pl.when(j >= nu_ref[0])
    def _():
        ys_ref[...] = jnp.zeros(ys_ref.shape, F32)

    @pl.when((j == 0) | (te_ref[j] != te_ref[jnp.maximum(j - 1, 0)]))
    def _():
        wgu_bf_ref[...] = wgu_ref[0].astype(BF16)
        wd_bf_ref[...] = wd_ref[0].astype(BF16)

    @pl.when(j < nu_ref[0])
    def _():
        gu = jnp.dot(xs_ref[...].astype(BF16), wgu_bf_ref[...], preferred_element_type=F32)
        act = _silu(gu[:, :d_expert]) * gu[:, d_expert:]
        ys_ref[...] = jnp.dot(act.astype(BF16), wd_bf_ref[...], preferred_element_type=F32)


def _experts(xs, tile_expert, n_used, w_gate_up, w_down, tile):
    n_rows, width = xs.shape
    _, D, two_f = w_gate_up.shape
    grid_spec = pltpu.PrefetchScalarGridSpec(
        num_scalar_prefetch=2,
        grid=(n_rows // tile,),
        in_specs=[pl.BlockSpec((tile, width), lambda j, te, nu: (jnp.minimum(j, nu[0] - 1), 0)),
                  pl.BlockSpec((1, D, two_f), lambda j, te, nu: (te[j], 0, 0)),
                  pl.BlockSpec((1, two_f // 2, D), lambda j, te, nu: (te[j], 0, 0))],
        out_specs=pl.BlockSpec((tile, D), lambda j, te, nu: (j, 0)),
        scratch_shapes=[pltpu.VMEM((D, two_f), BF16), pltpu.VMEM((two_f // 2, D), BF16)],
    )
    return pl.pallas_call(
        functools.partial(_experts_kernel, d_expert=two_f // 2),
        grid_spec=grid_spec,
        out_shape=jax.ShapeDtypeStruct((n_rows, D), F32),
        compiler_params=_cparams(1),
        name="moe_experts",
    )(tile_expert, n_used, xs, w_gate_up, w_down)


def _combine_kernel(pos_ref, pos_next_ref, ys_ref, route_ref, x1_ref, mod_ref, gf_ref, y_ref,
                    ya_ref, yb_ref, sems, *, tm):
    step = pl.program_id(0) * pl.num_programs(1) + pl.program_id(1)
    n_steps = pl.num_programs(0) * pl.num_programs(1)
    slot = step % 2

    def start_gathers(p_ref, s):
        def issue(k, carry):
            for j in range(SUBLANE):
                i = k * SUBLANE + j
                for prio, (buf, p) in enumerate(((ya_ref, p_ref[0, 0, i]), (yb_ref, p_ref[0, 0, tm + i]))):
                    pltpu.make_async_copy(ys_ref.at[pl.ds(p, 1)], buf.at[s, k, pl.ds(j, 1)],
                                          sems.at[s]).start(priority=prio)
            return carry

        lax.fori_loop(0, tm // SUBLANE, issue, 0)

    @pl.when(step == 0)
    def _():
        start_gathers(pos_ref, 0)

    @pl.when(step + 1 < n_steps)
    def _():
        start_gathers(pos_next_ref, 1 - slot)

    for buf in (ya_ref, yb_ref):
        pltpu.make_async_copy(buf.at[slot], buf.at[slot], sems.at[slot]).wait()
    route = route_ref[0]
    lane = lax.broadcasted_iota(jnp.int32, route.shape, 1)
    w1 = jnp.sum(jnp.where(lane == 2, route, 0.0), axis=-1, keepdims=True)
    w2 = jnp.sum(jnp.where(lane == 3, route, 0.0), axis=-1, keepdims=True)
    ya = ya_ref[slot].reshape(tm, ya_ref.shape[-1])
    yb = yb_ref[slot].reshape(tm, yb_ref.shape[-1])
    x2 = x1_ref[0] + mod_ref[0, 5] * (w1 * ya + w2 * yb)
    var = jnp.mean(x2 * x2, axis=-1, keepdims=True)
    y_ref[0] = x2 * lax.rsqrt(var + EPS) * gf_ref[...]


def _combine(ys, pos, route, x1, mod, gf, tm):
    B, S, D = x1.shape
    n_t = S // tm
    tok = lambda width: pl.BlockSpec((1, tm, width), lambda b, i: (b, i, 0))
    last = B * n_t - 1
    pos_spec = lambda ahead: pl.BlockSpec(
        (1, 1, 2 * tm), lambda b, i: (jnp.minimum(b * n_t + i + ahead, last), 0, 0), memory_space=pltpu.SMEM)
    pos_steps = _pos_steps(pos, tm)
    if mod.shape[2] == 1:
        mod_spec = pl.BlockSpec((1, N_COND, 1, D), lambda b, i: (b, 0, 0, 0))
    else:
        mod_spec = pl.BlockSpec((1, N_COND, tm, D), lambda b, i: (b, 0, i, 0))
    return pl.pallas_call(
        functools.partial(_combine_kernel, tm=tm),
        grid=(B, n_t),
        in_specs=[pos_spec(0), pos_spec(1),
                  pl.BlockSpec(memory_space=pl.ANY),
                  tok(ROUTER_LANES), tok(D),
                  mod_spec,
                  _full((1, D))],
        out_specs=tok(D),
        out_shape=jax.ShapeDtypeStruct((B, S, D), F32),
        scratch_shapes=[pltpu.VMEM((2, tm // SUBLANE, SUBLANE, D), F32)] * 2
                       + [pltpu.SemaphoreType.DMA((2,))],
        compiler_params=_cparams(2),
        name="moe_combine",
    )(pos_steps, pos_steps, ys, route, x1, mod, gf)


def _rope_tables(pos):
    half = HEAD_DIM // 2
    inv = ROPE_THETA ** (-jnp.arange(half, dtype=F32) * 2.0 / HEAD_DIM)
    ang = pos.astype(F32)[:, None] * inv[None, :]
    cos, sin = jnp.cos(ang), jnp.sin(ang)
    reps = LANE // HEAD_DIM
    return (jnp.tile(jnp.concatenate([cos, cos], axis=-1), (1, reps)),
            jnp.tile(jnp.concatenate([-sin, sin], axis=-1), (1, reps)))


def kernel(x_prompt, x_sample, cache_kv_w128, cache_kv_w512, cache_kv_w2048, state_pool, c_prompt, c_sample, norm1_g, w_ada, b_ada, w_in, w_attn_out, w_pool, pool_scale, w_pool_out, w_o, norm2_g, w_grp, b_grp, w_exp_router, b_exp_router, w_gate_up, w_down, final_norm_g):
    B, S, D = x_prompt.shape
    N, T, _ = x_sample.shape
    depth = norm1_g.shape[0]
    assert depth == 1, "single trunk layer"
    tm = min(TOKEN_TILE, S)
    assert S % tm == 0 and all(S % (SPAN * d) == 0 for d in ATTN_DILATIONS)

    n_qkvu = 3 * ATTN_WIDTH + POOL_WIDTH
    w_qkvu = w_in[0, :, :n_qkvu].astype(BF16)
    w_gates = w_in[0, :, n_qkvu:].astype(BF16)
    wa, wpo, wo = w_attn_out[0].astype(BF16), w_pool_out[0].astype(BF16), w_o[0].astype(BF16)
    wpool = w_pool[0].astype(BF16)
    pscale = pool_scale[0].reshape(1, POOL_WIDTH)
    w_r = jnp.concatenate([w_exp_router[0], w_grp[0]], axis=1)
    w_r = jnp.pad(w_r, ((0, 0), (0, ROUTER_LANES - w_r.shape[1])))
    wr_hi = w_r.astype(BF16)
    wr_hilo = jnp.concatenate([wr_hi, (w_r - wr_hi.astype(F32)).astype(BF16)], axis=1)
    b_r = jnp.pad(jnp.concatenate([b_exp_router[0], b_grp[0]]), (0, ROUTER_LANES - N_EXPERTS - MOE_GROUPS))
    b_r = b_r.reshape(1, ROUTER_LANES)
    wgu, wd = w_gate_up[0], w_down[0]
    g1, g2, gf = norm1_g[0].reshape(1, D), norm2_g[0].reshape(1, D), final_norm_g.reshape(1, D)

    c_rows = jnp.concatenate([jnp.repeat(c_sample, T, axis=0), c_prompt], axis=0)
    mod = _ada(c_rows, w_ada[0], b_ada[0])
    mod_p = jnp.transpose(mod[:, N * T:], (1, 0, 2)).reshape(B, N_COND, 1, D)
    mod_s = mod[None]

    cos_p, sin_p = _rope_tables(jnp.arange(S, dtype=jnp.int32))
    outs = _in_proj_prompt(x_prompt, mod_p, g1, w_qkvu, cos_p, sin_p, wpool, pscale, tm)
    qs, ks, vs = outs[0:3], outs[3:6], outs[6:9]
    pooled, kv0, kv1, kv2, ptail = outs[9:14]
    pool_prompt = ptail[:, POOL_HALO - POOL_STATE_LEN:, :][None]

    TS = N * T
    pos_s = PAST_LEN + jnp.arange(T, dtype=jnp.int32)
    cos_s, sin_s = _rope_tables(pos_s)
    cos_s, sin_s = jnp.tile(cos_s, (N, 1)), jnp.tile(sin_s, (N, 1))
    xs = x_sample.reshape(1, TS, D)
    q_s, kvn, u_s = _in_proj_sample(xs, mod_s, g1, w_qkvu, cos_s, sin_s)
    eye = jnp.eye(HEADS_PER_GROUP, dtype=BF16)
    qbd = jnp.einsum('ntghe,hk->ngthke', q_s.reshape(N, T, N_GROUPS, HEADS_PER_GROUP, HEAD_DIM), eye)
    qbd = qbd.reshape(N, N_GROUPS, T * HEADS_PER_GROUP, GROUP_WIDTH)
    caches = [jnp.transpose(c[0], (0, 2, 3, 4, 1)).reshape(N, 2, GROUP_WIDTH, c.shape[2])
              for c in (cache_kv_w128, cache_kv_w512, cache_kv_w2048)]

    attn_parts, (attn_s, pooled_s, ko0, ko1, ko2, pool_s) = _attn_and_sample(
        qs, ks, vs, qbd, kvn.reshape(N, T, 2 * ATTN_WIDTH), caches, state_pool[0],
        u_s.reshape(N, T, POOL_WIDTH), wpool, pscale, nq_max=ATTN_BLOCKS_PER_STEP)
    attn_inputs = [o for o, _ in attn_parts] + [l for _, l in attn_parts]
    x1, h2, route, counts, route_t = _merge(x_prompt, mod_p, g1, g2, attn_inputs, pooled, w_gates, wa, wpo, wo,
                                            wr_hi, wr_hilo, b_r, tm)
    x1s, h2s, route_s, counts_s, route_ts = _merge(xs, mod_s, g1, g2, [attn_s.reshape(1, TS, GROUP_WIDTH)],
                                                   pooled_s.reshape(1, TS, POOL_WIDTH), w_gates, wa, wpo, wo,
                                                   wr_hi, wr_hilo, b_r, TS)

    (pos, pos_s), tile_expert, n_used, pads, n_tiles = _routing_tables(
        [route_t, route_ts], [counts, counts_s], EXPERT_TILE)
    tm_rows = min(ROW_MOVE_TILE, S)
    sorted_rows = _dispatch(h2.reshape(B * S, D), pos, pads, n_tiles * EXPERT_TILE, tm_rows, EXPERT_TILE)
    sorted_rows = _dispatch(h2s.reshape(TS, D), pos_s, pads, n_tiles * EXPERT_TILE, TS, EXPERT_TILE,
                            xs_prev=sorted_rows)
    ys = _experts(sorted_rows, tile_expert, n_used, wgu, wd, EXPERT_TILE)
    y_prompt = _combine(ys, pos, route, x1, mod_p, gf, tm=tm)
    y_sample = _combine(ys, pos_s, route_s, x1s, mod_s, gf, tm=TS).reshape(N, T, D)

    def kv_shape(a):
        a = a.reshape(a.shape[0], 2, HEADS_PER_GROUP, HEAD_DIM, a.shape[3])
        return jnp.transpose(a, (0, 4, 1, 2, 3))[None]

    return (y_prompt, y_sample, kv_shape(kv0), kv_shape(kv1), kv_shape(kv2), pool_prompt,
            kv_shape(ko0), kv_shape(ko1), kv_shape(ko2), pool_s[None])
```
